```python
import math
import jax, jax.numpy as jnp
from jax import lax
import numpy as np

D_MODEL = 1024
BATCH = 8
SEQ = 4096
DEPTH = 1

GRID_W = 64
CTX_LEN = 256
N_HEADS = 8
HEAD_DIM = 64
ATTN_WIDTH = N_HEADS * HEAD_DIM
WIN_H_MAX = 8
WIN_W = 16
Q_COL_BLOCK = 16
BAND_W = 32
ROPE_BASE = 10000.0
SSM_WIDTH = 512
SSM_GROUP = 16
SSM_GROUPS = SSM_WIDTH // SSM_GROUP
SSM_STATE = 64
DT_MIN = 0.001
DT_MAX = 0.1
FFN_HIDDEN = ((8 * D_MODEL // 3 + 255) // 256) * 256
IN_COLS = 3 * ATTN_WIDTH + SSM_WIDTH + 2 * D_MODEL
N_MOD = 6
NORM_EPS = 1e-6
NEG_BIG = -1e30

kernel_name = 'hybrid_natten_s5_dit_block'


def rmsnorm(x, g):
    xf = x.astype(jnp.float32)
    xf = xf * lax.rsqrt(jnp.mean(xf * xf, axis=-1, keepdims=True) + NORM_EPS)
    return xf.astype(x.dtype) * g


def modulate(h, shift, scale):
    return h * (1.0 + scale) + shift


def heads(t):
    return t.reshape(t.shape[0], t.shape[1], N_HEADS, HEAD_DIM)


def axial_rope(x):
    L = x.shape[1]
    t = jnp.arange(L, dtype=jnp.int32)
    n_freq = HEAD_DIM // 4
    inv_freq = ROPE_BASE ** (-jnp.arange(n_freq, dtype=jnp.float32) / n_freq)

    def rotate(xa, pos):
        ang = pos.astype(jnp.float32)[:, None] * inv_freq[None, :]
        cos = jnp.cos(ang)[None, :, None, :].astype(x.dtype)
        sin = jnp.sin(ang)[None, :, None, :].astype(x.dtype)
        x1, x2 = xa[..., :n_freq], xa[..., n_freq:]
        return jnp.concatenate([x1 * cos - x2 * sin, x2 * cos + x1 * sin], axis=-1)

    half = HEAD_DIM // 2
    return jnp.concatenate([rotate(x[..., :half], t // GRID_W), rotate(x[..., half:], t % GRID_W)], axis=-1)


def neighbourhood_attention(q, k, v, k_ctx, v_ctx, rpb):
    B, L, H, d = q.shape
    rows = L // GRID_W
    kh = min(WIN_H_MAX, rows)
    n_cb = GRID_W // Q_COL_BLOCK
    qg = q.reshape(B, rows, n_cb, Q_COL_BLOCK, H, d)
    kg = k.reshape(B, rows, GRID_W, H, d)
    vg = v.reshape(B, rows, GRID_W, H, d)
    qcol = np.arange(GRID_W).reshape(n_cb, Q_COL_BLOCK)
    col_start = np.clip(qcol - WIN_W // 2, 0, GRID_W - WIN_W)
    band_start = np.clip(np.arange(n_cb) * Q_COL_BLOCK - WIN_W // 2, 0, GRID_W - BAND_W)
    band_idx = band_start[:, None] + np.arange(BAND_W)[None, :]
    kcol = band_idx[:, None, :]
    in_win = (kcol >= col_start[..., None]) & (kcol < col_start[..., None] + WIN_W)
    dc_idx = np.clip(kcol - qcol[..., None], -(WIN_W - 1), WIN_W - 1) + WIN_W - 1
    scale = HEAD_DIM ** -0.5
    n_loc = kh * BAND_W

    def one_row(r):
        rs = jnp.clip(r - kh // 2, 0, rows - kh)
        k_band = lax.dynamic_slice_in_dim(kg, rs, kh, axis=1)[:, :, band_idx]
        v_band = lax.dynamic_slice_in_dim(vg, rs, kh, axis=1)[:, :, band_idx]
        q_r = lax.dynamic_index_in_dim(qg, r, axis=1, keepdims=False)
        s_loc = jnp.einsum('bcqhd,bkcnhd->bhcqkn', q_r, k_band).astype(jnp.float32) * scale
        dr_idx = rs + jnp.arange(kh) - r + WIN_H_MAX - 1
        bias = rpb[:, dr_idx][:, :, dc_idx]
        bias = jnp.transpose(bias, (0, 2, 3, 1, 4)).astype(jnp.float32)
        s_loc = jnp.where(in_win[:, :, None, :], s_loc + bias, NEG_BIG)
        s_ctx = jnp.einsum('bcqhd,bmhd->bhcqm', q_r, k_ctx).astype(jnp.float32) * scale
        s = jnp.concatenate([s_loc.reshape(B, H, n_cb, Q_COL_BLOCK, n_loc), s_ctx], axis=-1)
        p = jax.nn.softmax(s, axis=-1).astype(q.dtype)
        p_loc = p[..., :n_loc].reshape(B, H, n_cb, Q_COL_BLOCK, kh, BAND_W)
        p_ctx = p[..., n_loc:]
        o = (jnp.einsum('bhcqkn,bkcnhd->bcqhd', p_loc, v_band)
             + jnp.einsum('bhcqm,bmhd->bcqhd', p_ctx, v_ctx))
        return o.reshape(B, GRID_W, H, d)

    out = lax.map(one_row, jnp.arange(rows, dtype=jnp.int32))
    return jnp.moveaxis(out, 0, 1).reshape(B, L, H * d)


def context_attention(q, k, v):
    s = jnp.einsum('bqhd,bkhd->bhqk', q, k).astype(jnp.float32) * (HEAD_DIM ** -0.5)
    p = jax.nn.softmax(s, axis=-1).astype(q.dtype)
    return jnp.einsum('bhqk,bkhd->bqhd', p, v).reshape(q.shape[0], q.shape[1], ATTN_WIDTH)


def s5_discretise(lam_re, lam_im, log_dt, b_re, b_im):
    dt = jnp.exp(log_dt)[:, None]
    mag = jnp.exp(lam_re * dt)
    ang = lam_im * dt
    lb_re, lb_im = mag * jnp.cos(ang), mag * jnp.sin(ang)
    den = lam_re * lam_re + lam_im * lam_im
    nr, ni = lb_re - 1.0, lb_im
    f_re = (nr * lam_re + ni * lam_im) / den
    f_im = (ni * lam_re - nr * lam_im) / den
    bb_re = f_re[..., None] * b_re - f_im[..., None] * b_im
    bb_im = f_re[..., None] * b_im + f_im[..., None] * b_re
    return lb_re, lb_im, bb_re, bb_im


def _linear_combine(e1, e2):
    a1r, a1i, b1r, b1i = e1
    a2r, a2i, b2r, b2i = e2
    return (a2r * a1r - a2i * a1i, a2r * a1i + a2i * a1r,
            a2r * b1r - a2i * b1i + b2r, a2r * b1i + a2i * b1r + b2i)


def diag_scan(lb_re, lb_im, bu_re, bu_im):
    L = bu_re.shape[1]
    a_re = jnp.broadcast_to(lb_re, (1, L) + lb_re.shape)
    a_im = jnp.broadcast_to(lb_im, (1, L) + lb_im.shape)
    return lax.associative_scan(_linear_combine, (a_re, a_im, bu_re, bu_im), axis=1)


def s5_direction(u_ctx, u_lat, lam_re, lam_im, log_dt, b_re, b_im, c_re, c_im, reverse, need_ctx):
    lb_re, lb_im, bb_re, bb_im = s5_discretise(lam_re, lam_im, log_dt, b_re, b_im)

    def drive(u):
        u = jnp.flip(u, axis=1) if reverse else u
        return (jnp.einsum('blgh,gph->blgp', u, bb_re), jnp.einsum('blgh,gph->blgp', u, bb_im))

    def readout(hr, hi):
        y = jnp.einsum('blgp,ghp->blgh', hr, c_re) - jnp.einsum('blgp,ghp->blgh', hi, c_im)
        return jnp.flip(y, axis=1) if reverse else y

    ur, ui = drive(u_ctx)
    _, _, hc_re, hc_im = diag_scan(lb_re, lb_im, ur, ui)
    h0_re, h0_im = hc_re[:, -1:], hc_im[:, -1:]
    ur, ui = drive(u_lat)
    ac_re, ac_im, hl_re, hl_im = diag_scan(lb_re, lb_im, ur, ui)
    hl_re = hl_re + ac_re * h0_re - ac_im * h0_im
    hl_im = hl_im + ac_re * h0_im + ac_im * h0_re
    y_ctx = readout(hc_re, hc_im) if need_ctx else None
    return y_ctx, readout(hl_re, hl_im)


def s5_branch(u_ctx, u_lat, lam_re, lam_im, log_dt, b_re, b_im, c_re, c_im, d_skip, w_glu, need_ctx):
    B, Lc, _ = u_ctx.shape
    L = u_lat.shape[1]
    ucg = u_ctx.reshape(B, Lc, SSM_GROUPS, SSM_GROUP)
    ulg = u_lat.reshape(B, L, SSM_GROUPS, SSM_GROUP)
    fwd = s5_direction(ucg, ulg, lam_re[0], lam_im[0], log_dt[0], b_re[0], b_im[0], c_re[0], c_im[0], False, need_ctx)
    bwd = s5_direction(ucg, ulg, lam_re[1], lam_im[1], log_dt[1], b_re[1], b_im[1], c_re[1], c_im[1], True, need_ctx)

    def finish(y_f, y_b, u):
        y = jax.nn.gelu((y_f + y_b).reshape(u.shape) + d_skip * u)
        val, gate = jnp.split(y @ w_glu, 2, axis=-1)
        return val * jax.nn.sigmoid(gate)

    y_lat = finish(fwd[1], bwd[1], u_lat)
    y_ctx = finish(fwd[0], bwd[0], u_ctx) if need_ctx else None
    return y_ctx, y_lat


def token_mixer(n_ctx, n_lat, w_in, rpb, lam_re, lam_im, log_dt, b_re, b_im, c_re, c_im, d_skip, w_glu,
                w_branch_attn, w_branch_ssm, w_out, need_ctx):
    cuts = [ATTN_WIDTH, 2 * ATTN_WIDTH, 3 * ATTN_WIDTH, 3 * ATTN_WIDTH + SSM_WIDTH,
            3 * ATTN_WIDTH + SSM_WIDTH + D_MODEL]
    qc, kc, vc, uc, gac, gsc = jnp.split(n_ctx @ w_in, cuts, axis=-1)
    ql, kl, vl, ul, gal, gsl = jnp.split(n_lat @ w_in, cuts, axis=-1)
    kc_h, vc_h = heads(kc), heads(vc)
    a_lat = neighbourhood_attention(axial_rope(heads(ql)), axial_rope(heads(kl)), heads(vl), kc_h, vc_h, rpb)
    s_ctx, s_lat = s5_branch(uc, ul, lam_re, lam_im, log_dt, b_re, b_im, c_re, c_im, d_skip, w_glu, need_ctx)

    def merge(a, s, ga, gs):
        return (jax.nn.sigmoid(ga) * (a @ w_branch_attn) + jax.nn.sigmoid(gs) * (s @ w_branch_ssm)) @ w_out

    out_lat = merge(a_lat, s_lat, gal, gsl)
    out_ctx = merge(context_attention(heads(qc), kc_h, vc_h), s_ctx, gac, gsc) if need_ctx else None
    return out_ctx, out_lat


def swiglu(h, w_ffn_in, w_ffn_out):
    a, b = jnp.split(h @ w_ffn_in, 2, axis=-1)
    return (jax.nn.silu(a) * b) @ w_ffn_out


def setup_inputs(seed: int = 0) -> dict:
    key = jax.random.key(seed)
    ks = jax.random.split(key, 26)
    f32 = jnp.float32

    def nrm(k, shape, s):
        return s * jax.random.normal(k, shape, f32)

    G, P, HG = SSM_GROUPS, SSM_STATE, SSM_GROUP
    n_idx = jnp.arange(P, dtype=f32)
    return {
        'x': nrm(ks[0], (BATCH, SEQ, D_MODEL), 1.0),
        'c': nrm(ks[1], (BATCH, D_MODEL), 1.0),
        'ctx': nrm(ks[2], (BATCH, CTX_LEN, D_MODEL), 1.0),
        'c_ctx': nrm(ks[3], (D_MODEL,), 1.0),
        'w_mod': nrm(ks[4], (DEPTH, D_MODEL, N_MOD * D_MODEL), D_MODEL ** -0.5),
        'b_mod': nrm(ks[5], (DEPTH, N_MOD * D_MODEL), 0.01),
        'attn_norm_g': 1.0 + nrm(ks[6], (DEPTH, D_MODEL), 0.01),
        'ffn_norm_g': 1.0 + nrm(ks[7], (DEPTH, D_MODEL), 0.01),
        'w_in': nrm(ks[8], (DEPTH, D_MODEL, IN_COLS), D_MODEL ** -0.5),
        'rel_pos_bias': nrm(ks[9], (DEPTH, N_HEADS, 2 * WIN_H_MAX - 1, 2 * WIN_W - 1), 0.1),
        'ssm_lambda_re': -0.5 + nrm(ks[10], (DEPTH, 2, G, P), 0.01),
        'ssm_lambda_im': math.pi * n_idx + nrm(ks[11], (DEPTH, 2, G, P), 0.01),
        'ssm_log_dt': jax.random.uniform(ks[12], (DEPTH, 2, G), f32, math.log(DT_MIN), math.log(DT_MAX)),
        'ssm_b_re': nrm(ks[13], (DEPTH, 2, G, P, HG), (2 * HG) ** -0.5),
        'ssm_b_im': nrm(ks[14], (DEPTH, 2, G, P, HG), (2 * HG) ** -0.5),
        'ssm_c_re': nrm(ks[15], (DEPTH, 2, G, HG, P), P ** -0.5),
        'ssm_c_im': nrm(ks[16], (DEPTH, 2, G, HG, P), P ** -0.5),
        'ssm_d': nrm(ks[17], (DEPTH, SSM_WIDTH), 1.0),
        'w_glu': nrm(ks[18], (DEPTH, SSM_WIDTH, 2 * SSM_WIDTH), SSM_WIDTH ** -0.5),
        'w_branch_attn': nrm(ks[19], (DEPTH, ATTN_WIDTH, D_MODEL), ATTN_WIDTH ** -0.5),
        'w_branch_ssm': nrm(ks[20], (DEPTH, SSM_WIDTH, D_MODEL), SSM_WIDTH ** -0.5),
        'w_out': nrm(ks[21], (DEPTH, D_MODEL, D_MODEL), D_MODEL ** -0.5),
        'w_ffn_in': nrm(ks[22], (DEPTH, D_MODEL, 2 * FFN_HIDDEN), D_MODEL ** -0.5),
        'w_ffn_out': nrm(ks[23], (DEPTH, FFN_HIDDEN, D_MODEL), FFN_HIDDEN ** -0.5),
        'final_norm_g': 1.0 + nrm(ks[24], (D_MODEL,), 0.01),
    }


def reference(x, c, ctx, c_ctx, w_mod, b_mod, attn_norm_g, ffn_norm_g, w_in, rel_pos_bias,
              ssm_lambda_re, ssm_lambda_im, ssm_log_dt, ssm_b_re, ssm_b_im, ssm_c_re, ssm_c_im, ssm_d,
              w_glu, w_branch_attn, w_branch_ssm, w_out, w_ffn_in, w_ffn_out, final_norm_g):
    h = x
    hc = ctx
    for layer in range(DEPTH):
        need_ctx = layer < DEPTH - 1
        sh1, sc1, g1, sh2, sc2, g2 = jnp.split((jax.nn.silu(c) @ w_mod[layer] + b_mod[layer])[:, None, :], N_MOD, axis=-1)
        csh1, csc1, cg1, csh2, csc2, cg2 = jnp.split(jax.nn.silu(c_ctx) @ w_mod[layer] + b_mod[layer], N_MOD, axis=-1)
        n_lat = modulate(rmsnorm(h, attn_norm_g[layer]), sh1, sc1)
        n_ctx = modulate(rmsnorm(hc, attn_norm_g[layer]), csh1, csc1)
        mix_ctx, mix_lat = token_mixer(n_ctx, n_lat, w_in[layer], rel_pos_bias[layer],
                                       ssm_lambda_re[layer], ssm_lambda_im[layer], ssm_log_dt[layer],
                                       ssm_b_re[layer], ssm_b_im[layer], ssm_c_re[layer], ssm_c_im[layer],
                                       ssm_d[layer], w_glu[layer], w_branch_attn[layer], w_branch_ssm[layer],
                                       w_out[layer], need_ctx)
        h = h + g1 * mix_lat
        h = h + g2 * swiglu(modulate(rmsnorm(h, ffn_norm_g[layer]), sh2, sc2), w_ffn_in[layer], w_ffn_out[layer])
        if need_ctx:
            hc = hc + cg1 * mix_ctx
            hc = hc + cg2 * swiglu(modulate(rmsnorm(hc, ffn_norm_g[layer]), csh2, csc2), w_ffn_in[layer], w_ffn_out[layer])
    return rmsnorm(h, final_norm_g)
```

```python
import functools
import math

import numpy as np
import jax
import jax.numpy as jnp
from jax import lax
from jax.experimental import pallas as pl
from jax.experimental.pallas import tpu as pltpu

F32 = jnp.float32
BF16 = jnp.bfloat16

D_MODEL = 1024
BATCH = 8
SEQ = 4096
GRID_W = 64
GRID_ROWS = SEQ // GRID_W
CTX_LEN = 256
N_HEADS = 8
HEAD_DIM = 64
ATTN_WIDTH = N_HEADS * HEAD_DIM
WIN_H = 8
WIN_W = 16
ROPE_BASE = 10000.0
SSM_WIDTH = 512
SSM_GROUP = 16
SSM_GROUPS = SSM_WIDTH // SSM_GROUP
SSM_STATE = 64
FFN_HIDDEN = 2816
IN_COLS = 3 * ATTN_WIDTH + SSM_WIDTH + 2 * D_MODEL
N_MOD = 6
NORM_EPS = 1e-6
NEG_BIG = -1e30

LANES = 128
CHUNK = 16
N_CHUNKS = SEQ // CHUNK
N_CTX_CHUNKS = CTX_LEN // CHUNK
CHUNK_COLS = CHUNK * SSM_GROUP
MOD_ROWS = 16
CTX_MOD_ROW = BATCH
VMEM_LIMIT = 56 * 1024 * 1024

Q_ROWS_PER_STEP = 8
Q_BLOCK = Q_ROWS_PER_STEP * GRID_W
K_ROWS_PER_STEP = 16
K_BLOCK = K_ROWS_PER_STEP * GRID_W
N_QBLOCKS = GRID_ROWS // Q_ROWS_PER_STEP


def _rms_modulate(x, g, shift, scale):
    xn = x * lax.rsqrt(jnp.mean(x * x, axis=-1, keepdims=True) + NORM_EPS)
    return (xn * g) * (1.0 + scale) + shift


def _mod_kernel(c_ref, w_ref, b_ref, o_ref):
    c = c_ref[...]
    s = c * jax.nn.sigmoid(c)
    o_ref[...] = jnp.dot(s, w_ref[...], preferred_element_type=F32) + b_ref[...]


def _modulation(c_rows, w_mod, b_mod):
    n = N_MOD * D_MODEL
    tn = 1536
    return pl.pallas_call(
        _mod_kernel,
        grid=(n // tn,),
        in_specs=[pl.BlockSpec((MOD_ROWS, D_MODEL), lambda j: (0, 0)),
                  pl.BlockSpec((D_MODEL, tn), lambda j: (0, j)),
                  pl.BlockSpec((1, tn), lambda j: (0, j))],
        out_specs=pl.BlockSpec((MOD_ROWS, tn), lambda j: (0, j)),
        out_shape=jax.ShapeDtypeStruct((MOD_ROWS, n), F32),
        name="modulation",
    )(c_rows, w_mod, b_mod.reshape(1, n))


def _rope_tables():
    n_freq = HEAD_DIM // 4
    inv_freq = ROPE_BASE ** (-np.arange(n_freq, dtype=np.float64) / n_freq)
    t = np.arange(SEQ)
    lane = np.arange(LANES)
    d = lane % HEAD_DIM
    use_col = (d // (HEAD_DIM // 2)) == 1
    w = d % (HEAD_DIM // 2)
    first = w < n_freq
    pos = np.where(use_col[None, :], (t % GRID_W)[:, None], (t // GRID_W)[:, None]).astype(np.float64)
    ang = pos * inv_freq[w % n_freq][None, :]
    cos = np.cos(ang)
    sin = np.sin(ang)
    sin_a = np.where(first[None, :], -sin, 0.0)
    sin_b = np.where(first[None, :], 0.0, sin)
    return (jnp.asarray(cos, F32), jnp.asarray(sin_a, F32), jnp.asarray(sin_b, F32))


def _rope_store(r, cos, sin_a, sin_b, out_ref):
    for j in range(ATTN_WIDTH // LANES):
        xs = r[:, j * LANES:(j + 1) * LANES]
        rot = (xs * cos + pltpu.roll(xs, LANES - HEAD_DIM // 4, 1) * sin_a
               + pltpu.roll(xs, HEAD_DIM // 4, 1) * sin_b)
        out_ref[0, :, j * LANES:(j + 1) * LANES] = rot.astype(BF16)


def _inproj_kernel(x_ref, mod_ref, g_ref, w_ref, cos_ref, sa_ref, sb_ref,
                   q_ref, k_ref, v_ref, u_ref, ga_ref, gs_ref):
    x = x_ref[0]
    shift = mod_ref[0, :, 0:D_MODEL]
    scale = mod_ref[0, :, D_MODEL:2 * D_MODEL]
    nb = _rms_modulate(x, g_ref[...], shift, scale).astype(BF16)
    cos = cos_ref[...]
    sin_a = sa_ref[...]
    sin_b = sb_ref[...]
    aw = ATTN_WIDTH
    q = jnp.dot(nb, w_ref[:, 0:aw], preferred_element_type=F32)
    _rope_store(q, cos, sin_a, sin_b, q_ref)
    k = jnp.dot(nb, w_ref[:, aw:2 * aw], preferred_element_type=F32)
    _rope_store(k, cos, sin_a, sin_b, k_ref)
    v_ref[0] = jnp.dot(nb, w_ref[:, 2 * aw:3 * aw], preferred_element_type=F32).astype(BF16)
    c0 = 3 * aw
    u_ref[0] = jnp.dot(nb, w_ref[:, c0:c0 + SSM_WIDTH], preferred_element_type=F32).astype(BF16)
    c1 = c0 + SSM_WIDTH
    ga = jnp.dot(nb, w_ref[:, c1:c1 + D_MODEL], preferred_element_type=F32)
    ga_ref[0] = jax.nn.sigmoid(ga).astype(BF16)
    c2 = c1 + D_MODEL
    gs = jnp.dot(nb, w_ref[:, c2:c2 + D_MODEL], preferred_element_type=F32)
    gs_ref[0] = jax.nn.sigmoid(gs).astype(BF16)


def _input_projection(x, mod3, norm_g, w_in_bf16, rope):
    tm = 512
    cos, sin_a, sin_b = rope
    tok = lambda width: pl.BlockSpec((1, tm, width), lambda i, b: (b, i, 0))
    tab = pl.BlockSpec((tm, LANES), lambda i, b: (i, 0))
    out = lambda width: jax.ShapeDtypeStruct((BATCH, SEQ, width), BF16)
    return pl.pallas_call(
        _inproj_kernel,
        grid=(SEQ // tm, BATCH),
        in_specs=[tok(D_MODEL),
                  pl.BlockSpec((1, 1, N_MOD * D_MODEL), lambda i, b: (b, 0, 0)),
                  pl.BlockSpec((1, D_MODEL), lambda i, b: (0, 0)),
                  pl.BlockSpec((D_MODEL, IN_COLS), lambda i, b: (0, 0)),
                  tab, tab, tab],
        out_specs=[tok(ATTN_WIDTH), tok(ATTN_WIDTH), tok(ATTN_WIDTH), tok(SSM_WIDTH),
                   tok(D_MODEL), tok(D_MODEL)],
        out_shape=[out(ATTN_WIDTH), out(ATTN_WIDTH), out(ATTN_WIDTH), out(SSM_WIDTH),
                   out(D_MODEL), out(D_MODEL)],
        compiler_params=pltpu.CompilerParams(
            dimension_semantics=("arbitrary", "arbitrary"), vmem_limit_bytes=VMEM_LIMIT),
        name="input_projection",
    )(x, mod3, norm_g, w_in_bf16, cos, sin_a, sin_b)


def _ctx_proj_kernel(x_ref, mod_ref, g_ref, wk_ref, wv_ref, wu_ref, k_ref, v_ref, u_ref):
    x = x_ref[0]
    shift = mod_ref[0, :, 0:D_MODEL]
    scale = mod_ref[0, :, D_MODEL:2 * D_MODEL]
    nb = _rms_modulate(x, g_ref[...], shift, scale).astype(BF16)
    k_ref[0] = jnp.dot(nb, wk_ref[...], preferred_element_type=F32).astype(BF16)
    v_ref[0] = jnp.dot(nb, wv_ref[...], preferred_element_type=F32).astype(BF16)
    u_ref[0] = jnp.dot(nb, wu_ref[...], preferred_element_type=F32).astype(BF16)


def _context_projection(ctx, mod3, norm_g, w_in_bf16):
    aw = ATTN_WIDTH
    tok = lambda width: pl.BlockSpec((1, CTX_LEN, width), lambda b: (b, 0, 0))
    wcol = lambda j: pl.BlockSpec((D_MODEL, aw), lambda b: (0, j))
    out = jax.ShapeDtypeStruct((BATCH, CTX_LEN, aw), BF16)
    return pl.pallas_call(
        _ctx_proj_kernel,
        grid=(BATCH,),
        in_specs=[tok(D_MODEL),
                  pl.BlockSpec((1, 1, N_MOD * D_MODEL), lambda b: (CTX_MOD_ROW, 0, 0)),
                  pl.BlockSpec((1, D_MODEL), lambda b: (0, 0)),
                  wcol(1), wcol(2), wcol(3)],
        out_specs=[tok(aw), tok(aw), tok(aw)],
        out_shape=[out, out, out],
        name="context_projection",
    )(ctx, mod3, norm_g, w_in_bf16, w_in_bf16, w_in_bf16)


def _window_start_rows(r):
    return min(max(r - WIN_H // 2, 0), GRID_ROWS - WIN_H)


def _key_block_row(jb):
    return min(max(Q_ROWS_PER_STEP * jb - WIN_H // 2, 0), GRID_ROWS - K_ROWS_PER_STEP)


def _attn_block(q_ref, k_ref, v_ref, kc_ref, vc_ref, bias_ref, o_ref,
                s_ref, sc_ref, p_ref, pc_ref, l_ref, acc_ref, *, jb, key_start):
    key_row0 = _key_block_row(jb)
    offs = []
    deltas = []
    for i in range(Q_ROWS_PER_STEP):
        r = Q_ROWS_PER_STEP * jb + i
        rs = _window_start_rows(r)
        offs.append(rs - key_row0)
        deltas.append(r - rs)
    lane = lax.broadcasted_iota(jnp.int32, (1, LANES), 1)
    left = lane < HEAD_DIM
    q2 = q_ref[0]
    kblk = k_ref[0, pl.ds(key_start, K_BLOCK), :]
    vblk = v_ref[0, pl.ds(key_start, K_BLOCK), :]
    kc = kc_ref[0]
    vc = vc_ref[0]
    nt_dims = (((1,), (1,)), ((), ()))
    n_ktiles = K_BLOCK // LANES
    for e in range(2):
        qm = jnp.where(left if e == 0 else jnp.logical_not(left), q2, jnp.zeros_like(q2))
        s_ref[...] = lax.dot_general(qm, kblk, nt_dims, preferred_element_type=F32)
        sc_ref[...] = lax.dot_general(qm, kc, nt_dims, preferred_element_type=F32)
        for i in range(Q_ROWS_PER_STEP):
            rows = slice(GRID_W * i, GRID_W * (i + 1))
            par = offs[i] % 2
            t0 = offs[i] // 2
            n_tiles = WIN_H // 2 + par
            tiles = []
            for xt in range(n_tiles):
                t = t0 + xt
                dr0 = 2 * xt - par - deltas[i]
                st = s_ref[rows, t * LANES:(t + 1) * LANES] + bias_ref[0, e, dr0 + WIN_H]
                if par and xt == 0:
                    st = jnp.where(left, NEG_BIG, st)
                if par and xt == n_tiles - 1:
                    st = jnp.where(left, st, NEG_BIG)
                tiles.append(st)
            c_tiles = [sc_ref[rows, 0:LANES], sc_ref[rows, LANES:2 * LANES]]
            mt = c_tiles[0]
            for st in tiles + c_tiles[1:]:
                mt = jnp.maximum(mt, st)
            m = jnp.max(mt, axis=1, keepdims=True)
            lt = None
            for xt in range(n_tiles):
                t = t0 + xt
                pt = jnp.exp(tiles[xt] - m)
                lt = pt if lt is None else lt + pt
                p_ref[rows, t * LANES:(t + 1) * LANES] = pt.astype(BF16)
            for t in range(n_ktiles):
                if not (t0 <= t < t0 + n_tiles):
                    p_ref[rows, t * LANES:(t + 1) * LANES] = jnp.zeros((GRID_W, LANES), BF16)
            for ci in range(2):
                pt = jnp.exp(c_tiles[ci] - m)
                lt = lt + pt
                pc_ref[rows, ci * LANES:(ci + 1) * LANES] = pt.astype(BF16)
            l = jnp.sum(lt, axis=1, keepdims=True)
            l_ref[rows, :] = jnp.broadcast_to(1.0 / l, (GRID_W, LANES))
        o = (jnp.dot(p_ref[...], vblk, preferred_element_type=F32)
             + jnp.dot(pc_ref[...], vc, preferred_element_type=F32)) * l_ref[...]
        if e == 0:
            acc_ref[...] = o
        else:
            o_ref[0] = jnp.where(left, acc_ref[...], o).astype(BF16)


def _attn_kernel(q_ref, k_ref, v_ref, kc_ref, vc_ref, bias_ref, o_ref, *scratch):
    jb = pl.program_id(2)
    args = (q_ref, k_ref, v_ref, kc_ref, vc_ref, bias_ref, o_ref) + scratch
    last = N_QBLOCKS - 1

    @pl.when(jb == 0)
    def _():
        _attn_block(*args, jb=0, key_start=_key_block_row(0) * GRID_W)

    @pl.when(jnp.logical_and(jb > 0, jb < last))
    def _():
        start = pl.multiple_of((Q_ROWS_PER_STEP * jb - WIN_H // 2) * GRID_W, 256)
        _attn_block(*args, jb=1, key_start=start)

    @pl.when(jb == last)
    def _():
        _attn_block(*args, jb=last, key_start=_key_block_row(last) * GRID_W)


def _bias_tables(rpb):
    qcol = np.arange(GRID_W)
    kcol = np.arange(GRID_W)
    col_start = np.clip(qcol - WIN_W // 2, 0, GRID_W - WIN_W)
    in_win = (kcol[None, :] >= col_start[:, None]) & (kcol[None, :] < col_start[:, None] + WIN_W)
    dc_idx = np.clip(kcol[None, :] - qcol[:, None], -(WIN_W - 1), WIN_W - 1) + WIN_W - 1
    toe = rpb[:, :, dc_idx]
    toe = jnp.where(in_win[None, None], toe, NEG_BIG)
    neg = jnp.full((N_HEADS, 1, GRID_W, GRID_W), NEG_BIG, F32)
    ext = jnp.concatenate([neg, toe, neg], axis=1)
    pair = jnp.concatenate([ext[:, 0:16], ext[:, 1:17]], axis=-1)
    return pair.reshape(N_HEADS // 2, 2, 16, GRID_W, LANES)


def _attention(q, k, v, kc, vc, bias):
    qspec = pl.BlockSpec((1, Q_BLOCK, LANES), lambda hp, b, j: (b, j, hp))
    kspec = pl.BlockSpec((1, SEQ, LANES), lambda hp, b, j: (b, 0, hp))
    cspec = pl.BlockSpec((1, CTX_LEN, LANES), lambda hp, b, j: (b, 0, hp))
    bspec = pl.BlockSpec((1, 2, 16, GRID_W, LANES), lambda hp, b, j: (hp, 0, 0, 0, 0))
    return pl.pallas_call(
        _attn_kernel,
        grid=(N_HEADS // 2, BATCH, N_QBLOCKS),
        in_specs=[qspec, kspec, kspec, cspec, cspec, bspec],
        out_specs=qspec,
        out_shape=jax.ShapeDtypeStruct((BATCH, SEQ, ATTN_WIDTH), BF16),
        scratch_shapes=[pltpu.VMEM((Q_BLOCK, K_BLOCK), F32),
                        pltpu.VMEM((Q_BLOCK, CTX_LEN), F32),
                        pltpu.VMEM((Q_BLOCK, K_BLOCK), BF16),
                        pltpu.VMEM((Q_BLOCK, CTX_LEN), BF16),
                        pltpu.VMEM((Q_BLOCK, LANES), F32),
                        pltpu.VMEM((Q_BLOCK, LANES), F32)],
        compiler_params=pltpu.CompilerParams(
            dimension_semantics=("arbitrary", "arbitrary", "arbitrary"),
            vmem_limit_bytes=VMEM_LIMIT),
        name="attention",
    )(q, k, v, kc, vc, bias)


def _s5_matrices(lam_re, lam_im, log_dt, b_re, b_im, c_re, c_im):
    dt = jnp.exp(log_dt)[..., None]
    mag = jnp.exp(lam_re * dt)
    ang = lam_im * dt
    lb_re, lb_im = mag * jnp.cos(ang), mag * jnp.sin(ang)
    den = lam_re * lam_re + lam_im * lam_im
    nr, ni = lb_re - 1.0, lb_im
    f_re = (nr * lam_re + ni * lam_im) / den
    f_im = (ni * lam_re - nr * lam_im) / den
    bb_re = f_re[..., None] * b_re - f_im[..., None] * b_im
    bb_im = f_re[..., None] * b_im + f_im[..., None] * b_re
    kk = jnp.arange(CHUNK + 1, dtype=F32)[:, None, None, None]
    pmag = jnp.exp(lam_re * dt * kk)
    pang = lam_im * dt * kk
    pw_re, pw_im = pmag * jnp.cos(pang), pmag * jnp.sin(pang)

    hi = lax.Precision.HIGHEST
    x_re = pw_re[:, :, :, None, :] * c_re[None] - pw_im[:, :, :, None, :] * c_im[None]
    x_im = pw_re[:, :, :, None, :] * c_im[None] + pw_im[:, :, :, None, :] * c_re[None]
    kern = (jnp.einsum('kdgop,dgph->kdgoh', x_re, bb_re, precision=hi)
            - jnp.einsum('kdgop,dgph->kdgoh', x_im, bb_im, precision=hi))
    i_idx = np.arange(CHUNK)[None, :]
    j_idx = np.arange(CHUNK)[:, None]
    lag_f = np.clip(i_idx - j_idx, 0, CHUNK - 1)
    lag_b = np.clip(j_idx - i_idx, 0, CHUNK - 1)
    kf = kern[:CHUNK, 0][lag_f]
    kb = kern[:CHUNK, 1][lag_b]
    mf = jnp.asarray(i_idx >= j_idx, F32)[:, :, None, None, None]
    mb = jnp.asarray(j_idx >= i_idx, F32)[:, :, None, None, None]
    m_intra = kf * mf + kb * mb
    m_intra = jnp.transpose(m_intra, (2, 0, 4, 1, 3)).reshape(SSM_GROUPS, CHUNK_COLS, CHUNK_COLS)

    def state_cols(d, expo):
        pr = pw_re[expo, d]
        pi = pw_im[expo, d]
        sr = pr[..., None] * bb_re[d][None] - pi[..., None] * bb_im[d][None]
        si = pr[..., None] * bb_im[d][None] + pi[..., None] * bb_re[d][None]
        to_rows = lambda a: jnp.transpose(a, (1, 0, 3, 2)).reshape(SSM_GROUPS, CHUNK_COLS, SSM_STATE)
        return to_rows(sr), to_rows(si)

    fr, fi = state_cols(0, np.arange(CHUNK - 1, -1, -1))
    br, bi = state_cols(1, np.arange(CHUNK))
    m_state = jnp.concatenate([fr, br, fi, bi], axis=-1)

    def out_rows(d, expo):
        xr = x_re[expo, d]
        xi = x_im[expo, d]
        to_cols = lambda a: jnp.transpose(a, (1, 3, 0, 2)).reshape(SSM_GROUPS, SSM_STATE, CHUNK_COLS)
        return to_cols(xr), to_cols(-xi)

    ofr, ofi = out_rows(0, np.arange(1, CHUNK + 1))
    obr, obi = out_rows(1, np.arange(CHUNK, 0, -1))
    m_out = jnp.concatenate([ofr, obr, ofi, obi], axis=1)

    a_re = jnp.concatenate([pw_re[CHUNK, 0], pw_re[CHUNK, 1]], axis=-1)
    a_im = jnp.concatenate([pw_im[CHUNK, 0], pw_im[CHUNK, 1]], axis=-1)
    a16 = jnp.concatenate([a_re, a_im], axis=-1)
    a16 = jnp.broadcast_to(a16[:, None, :], (SSM_GROUPS, 8, 2 * LANES))
    return m_intra.astype(BF16), m_state.astype(BF16), m_out.astype(BF16), a16


def _s5_kernel(ul_ref, uc_ref, ms_ref, mi_ref, mo_ref, a_ref, y_ref, s_ref, sc_ref, hp_ref, *, gb):
    for gi in range(gb):
        s_ref[gi] = jnp.dot(ul_ref[gi], ms_ref[gi], preferred_element_type=F32)
        sc_ref[gi] = jnp.dot(uc_ref[gi], ms_ref[gi], preferred_element_type=F32)
    lane = lax.broadcasted_iota(jnp.int32, (BATCH, LANES), 1)
    fwd = lane < SSM_STATE
    half = SSM_STATE

    def advance(gi, h_re, h_im, row_f, row_b, src):
        rf = src[gi, pl.ds(row_f, BATCH), :]
        rb = src[gi, pl.ds(row_b, BATCH), :]
        s_re = jnp.where(fwd, rf[:, :LANES], rb[:, :LANES])
        s_im = jnp.where(fwd, rf[:, LANES:], rb[:, LANES:])
        a_re = a_ref[gi, :, 0:LANES]
        a_im = a_ref[gi, :, LANES:2 * LANES]
        n_re = a_re * h_re - a_im * h_im + s_re
        n_im = a_re * h_im + a_im * h_re + s_im
        return n_re, n_im

    def ctx_step(t, carry):
        row_f = pl.multiple_of(t * BATCH, BATCH)
        row_b = pl.multiple_of((N_CTX_CHUNKS - 1 - t) * BATCH, BATCH)
        return tuple(advance(gi, carry[gi][0], carry[gi][1], row_f, row_b, sc_ref) for gi in range(gb))

    def lat_step(t, carry):
        row_f = pl.multiple_of(t * BATCH, BATCH)
        row_b = pl.multiple_of((N_CHUNKS - 1 - t) * BATCH, BATCH)
        out = []
        for gi in range(gb):
            h_re, h_im = carry[gi]
            hp_ref[gi, pl.ds(row_f, BATCH), 0:half] = h_re[:, 0:half]
            hp_ref[gi, pl.ds(row_b, BATCH), half:2 * half] = h_re[:, half:]
            hp_ref[gi, pl.ds(row_f, BATCH), 2 * half:3 * half] = h_im[:, 0:half]
            hp_ref[gi, pl.ds(row_b, BATCH), 3 * half:4 * half] = h_im[:, half:]
            out.append(advance(gi, h_re, h_im, row_f, row_b, s_ref))
        return tuple(out)

    zero = jnp.zeros((BATCH, LANES), F32)
    carry = tuple((zero, zero) for _ in range(gb))
    carry = lax.fori_loop(0, N_CTX_CHUNKS, ctx_step, carry)
    lax.fori_loop(0, N_CHUNKS, lat_step, carry)
    for gi in range(gb):
        y = (jnp.dot(ul_ref[gi], mi_ref[gi], preferred_element_type=F32)
             + jnp.dot(hp_ref[gi].astype(BF16), mo_ref[gi], preferred_element_type=F32))
        y_ref[gi] = y.astype(BF16)


def _s5_scan(u_lat_t, u_ctx_t, m_intra, m_state, m_out, a16):
    gb = 2
    rows = N_CHUNKS * BATCH
    crows = N_CTX_CHUNKS * BATCH
    grp = lambda r, c: pl.BlockSpec((gb, r, c), lambda g: (g, 0, 0))
    return pl.pallas_call(
        functools.partial(_s5_kernel, gb=gb),
        grid=(SSM_GROUPS // gb,),
        in_specs=[grp(rows, CHUNK_COLS), grp(crows, CHUNK_COLS), grp(CHUNK_COLS, CHUNK_COLS),
                  grp(CHUNK_COLS, CHUNK_COLS), grp(CHUNK_COLS, CHUNK_COLS), grp(8, 2 * LANES)],
        out_specs=grp(rows, CHUNK_COLS),
        out_shape=jax.ShapeDtypeStruct((SSM_GROUPS, rows, CHUNK_COLS), BF16),
        scratch_shapes=[pltpu.VMEM((gb, rows, CHUNK_COLS), F32),
                        pltpu.VMEM((gb, crows, CHUNK_COLS), F32),
                        pltpu.VMEM((gb, rows, CHUNK_COLS), F32)],
        compiler_params=pltpu.CompilerParams(
            dimension_semantics=("arbitrary",), vmem_limit_bytes=VMEM_LIMIT),
        name="s5_scan",
    )(u_lat_t, u_ctx_t, m_state, m_intra, m_out, a16)


def _to_chunk_major(u, n_chunks):
    t = u.reshape(BATCH, n_chunks, CHUNK, SSM_GROUPS, SSM_GROUP)
    t = jnp.transpose(t, (3, 1, 0, 2, 4))
    return t.reshape(SSM_GROUPS, n_chunks * BATCH, CHUNK_COLS)


def _from_chunk_major(y):
    t = y.reshape(SSM_GROUPS, N_CHUNKS, BATCH, CHUNK, SSM_GROUP)
    t = jnp.transpose(t, (2, 1, 3, 0, 4))
    return t.reshape(BATCH, SEQ, SSM_WIDTH)


FFN_TILE = 256


def _post_kernel(x_ref, a_ref, y_ref, u_ref, ga_ref, gs_ref, mod_ref, d_ref, fg_ref, og_ref,
                 wglu_ref, wba_ref, wbs_ref, wout_ref, wfi_ref, wfo_ref, o_ref, h1_ref, acc_ref):
    dm = D_MODEL
    g1 = mod_ref[0, :, 2 * dm:3 * dm]
    sh2 = mod_ref[0, :, 3 * dm:4 * dm]
    sc2 = mod_ref[0, :, 4 * dm:5 * dm]
    g2 = mod_ref[0, :, 5 * dm:6 * dm]
    u = u_ref[0].astype(F32)
    sp = jax.nn.gelu(y_ref[0].astype(F32) + d_ref[...] * u).astype(BF16)
    vg = jnp.dot(sp, wglu_ref[...], preferred_element_type=F32)
    s = (vg[:, :SSM_WIDTH] * jax.nn.sigmoid(vg[:, SSM_WIDTH:])).astype(BF16)
    merged = (ga_ref[0].astype(F32) * jnp.dot(a_ref[0], wba_ref[...], preferred_element_type=F32)
              + gs_ref[0].astype(F32) * jnp.dot(s, wbs_ref[...], preferred_element_type=F32))
    mix = jnp.dot(merged.astype(BF16), wout_ref[...], preferred_element_type=F32)
    h1 = x_ref[0] + g1 * mix
    h1_ref[...] = h1
    n2 = _rms_modulate(h1, fg_ref[...], sh2, sc2).astype(BF16)
    for c in range(FFN_HIDDEN // FFN_TILE):
        lo = c * FFN_TILE
        fa = jnp.dot(n2, wfi_ref[:, lo:lo + FFN_TILE], preferred_element_type=F32)
        fb = jnp.dot(n2, wfi_ref[:, FFN_HIDDEN + lo:FFN_HIDDEN + lo + FFN_TILE],
                     preferred_element_type=F32)
        act = (fa * jax.nn.sigmoid(fa) * fb).astype(BF16)
        part = jnp.dot(act, wfo_ref[lo:lo + FFN_TILE, :], preferred_element_type=F32)
        if c == 0:
            acc_ref[...] = part
        else:
            acc_ref[...] += part
    h2 = h1_ref[...] + g2 * acc_ref[...]
    o_ref[0] = (h2 * lax.rsqrt(jnp.mean(h2 * h2, axis=-1, keepdims=True) + NORM_EPS)) * og_ref[...]


def _post(x, a, y, u, ga, gs, mod3, d_skip, ffn_g, fin_g, wglu, wba, wbs, wout, wfi, wfo):
    tm = 512
    tok = lambda width: pl.BlockSpec((1, tm, width), lambda b, i: (b, i, 0))
    const = lambda r, c: pl.BlockSpec((r, c), lambda b, i: (0, 0), pipeline_mode=pl.Buffered(1))
    return pl.pallas_call(
        _post_kernel,
        grid=(BATCH, SEQ // tm),
        in_specs=[tok(D_MODEL), tok(ATTN_WIDTH), tok(SSM_WIDTH), tok(SSM_WIDTH),
                  tok(D_MODEL), tok(D_MODEL),
                  pl.BlockSpec((1, 1, N_MOD * D_MODEL), lambda b, i: (b, 0, 0)),
                  const(1, SSM_WIDTH), const(1, D_MODEL), const(1, D_MODEL),
                  const(SSM_WIDTH, 2 * SSM_WIDTH), const(ATTN_WIDTH, D_MODEL),
                  const(SSM_WIDTH, D_MODEL), const(D_MODEL, D_MODEL),
                  const(D_MODEL, 2 * FFN_HIDDEN), const(FFN_HIDDEN, D_MODEL)],
        out_specs=tok(D_MODEL),
        out_shape=jax.ShapeDtypeStruct((BATCH, SEQ, D_MODEL), F32),
        scratch_shapes=[pltpu.VMEM((tm, D_MODEL), F32), pltpu.VMEM((tm, D_MODEL), F32)],
        compiler_params=pltpu.CompilerParams(
            dimension_semantics=("arbitrary", "arbitrary"), vmem_limit_bytes=VMEM_LIMIT),
        name="post",
    )(x, a, y, u, ga, gs, mod3, d_skip, ffn_g, fin_g, wglu, wba, wbs, wout, wfi, wfo)


def kernel(x, c, ctx, c_ctx, w_mod, b_mod, attn_norm_g, ffn_norm_g, w_in, rel_pos_bias,
           ssm_lambda_re, ssm_lambda_im, ssm_log_dt, ssm_b_re, ssm_b_im, ssm_c_re, ssm_c_im, ssm_d,
           w_glu, w_branch_attn, w_branch_ssm, w_out, w_ffn_in, w_ffn_out, final_norm_g):
    assert x.shape == (BATCH, SEQ, D_MODEL) and w_mod.shape[0] == 1
    c_rows = jnp.concatenate(
        [c, c_ctx[None, :], jnp.zeros((MOD_ROWS - BATCH - 1, D_MODEL), F32)], axis=0)
    mod3 = _modulation(c_rows, w_mod[0], b_mod[0]).reshape(MOD_ROWS, 1, N_MOD * D_MODEL)

    col_scale = jnp.concatenate([jnp.full((ATTN_WIDTH,), HEAD_DIM ** -0.5, F32),
                                 jnp.ones((IN_COLS - ATTN_WIDTH,), F32)])
    w_in_bf16 = (w_in[0] * col_scale[None, :]).astype(BF16)
    norm_g = attn_norm_g[0].reshape(1, D_MODEL)

    q, k, v, u, ga, gs = _input_projection(x, mod3, norm_g, w_in_bf16, _rope_tables())
    kc, vc, uc = _context_projection(ctx, mod3, norm_g, w_in_bf16)

    attn = _attention(q, k, v, kc, vc, _bias_tables(rel_pos_bias[0]))

    m_intra, m_state, m_out, a16 = _s5_matrices(
        ssm_lambda_re[0], ssm_lambda_im[0], ssm_log_dt[0], ssm_b_re[0], ssm_b_im[0],
        ssm_c_re[0], ssm_c_im[0])
    y_t = _s5_scan(_to_chunk_major(u, N_CHUNKS), _to_chunk_major(uc, N_CTX_CHUNKS),
                   m_intra, m_state, m_out, a16)
    y = _from_chunk_major(y_t)

    return _post(x, attn, y, u, ga, gs, mod3,
                 ssm_d[0].reshape(1, SSM_WIDTH), ffn_norm_g[0].reshape(1, D_MODEL),
                 final_norm_g.reshape(1, D_MODEL),
                 w_glu[0].astype(BF16), w_branch_attn[0].astype(BF16), w_branch_ssm[0].astype(BF16),
                 w_out[0].astype(BF16), w_ffn_in[0].astype(BF16), w_ffn_out[0].astype(BF16))
```

```python
import functools
import math

import numpy as np
import jax
import jax.numpy as jnp
from jax import lax
from jax.experimental import pallas as pl
from jax.experimental.pallas import tpu as pltpu

F32 = jnp.float32
BF16 = jnp.bfloat16

D_MODEL = 1024
BATCH = 8
SEQ = 4096
GRID_W = 64
GRID_ROWS = SEQ // GRID_W
CTX_LEN = 256
N_HEADS = 8
HEAD_DIM = 64
ATTN_WIDTH = N_HEADS * HEAD_DIM
WIN_H = 8
WIN_W = 16
ROPE_BASE = 10000.0
SSM_WIDTH = 512
SSM_GROUP = 16
SSM_GROUPS = SSM_WIDTH // SSM_GROUP
SSM_STATE = 64
FFN_HIDDEN = 2816
IN_COLS = 3 * ATTN_WIDTH + SSM_WIDTH + 2 * D_MODEL
N_MOD = 6
NORM_EPS = 1e-6
NEG_BIG = -1e30

LANES = 128
CHUNK = 16
N_CHUNKS = SEQ // CHUNK
N_CTX_CHUNKS = CTX_LEN // CHUNK
CHUNK_COLS = CHUNK * SSM_GROUP
MOD_ROWS = 16
CTX_MOD_ROW = BATCH
VMEM_LIMIT = 56 * 1024 * 1024

Q_ROWS_PER_STEP = 8
Q_BLOCK = Q_ROWS_PER_STEP * GRID_W
K_ROWS_PER_STEP = 16
K_BLOCK = K_ROWS_PER_STEP * GRID_W
N_QBLOCKS = GRID_ROWS // Q_ROWS_PER_STEP


def _rms_modulate(x, g, shift, scale):
    xn = x * lax.rsqrt(jnp.mean(x * x, axis=-1, keepdims=True) + NORM_EPS)
    return (xn * g) * (1.0 + scale) + shift


def _block_transpose8(vs, lane):
    for shift in (64, 32, 16):
        keep = (lane & (2 * shift - 1)) < shift
        dist = shift // SSM_GROUP
        out = list(vs)
        for a in range(8):
            if a & dist:
                continue
            b = a + dist
            out[a] = jnp.where(keep, vs[a], pltpu.roll(vs[b], shift, 1))
            out[b] = jnp.where(keep, pltpu.roll(vs[a], LANES - shift, 1), vs[b])
        vs = out
    return vs


def _chunk_perm():
    n = CHUNK * CHUNK
    r = np.arange(n)
    m = np.zeros((n, n), np.float32)
    m[r, (r % CHUNK) * CHUNK + r // CHUNK] = 1.0
    return jnp.asarray(m, BF16)


def _store_chunk_layout(u16, perm_ref, out_ref, n_groups16):
    lane = lax.broadcasted_iota(jnp.int32, (1, LANES), 1)
    n = CHUNK * CHUNK
    for hf in range(n_groups16):
        r = jnp.dot(perm_ref[...], u16[hf * n:(hf + 1) * n, :], preferred_element_type=F32)
        for v in range(SSM_WIDTH // LANES):
            for jh in range(2):
                vs = [r[CHUNK * (8 * jh + jp):CHUNK * (8 * jh + jp + 1), v * LANES:(v + 1) * LANES]
                      for jp in range(8)]
                outs = _block_transpose8(vs, lane)
                for gi in range(8):
                    out_ref[8 * v + gi, 0, hf * CHUNK:(hf + 1) * CHUNK, jh * LANES:(jh + 1) * LANES] = (
                        outs[gi].astype(BF16))


def _load_chunk_layout(yt_ref, perm_t_ref, r_ref, n_groups16):
    lane = lax.broadcasted_iota(jnp.int32, (1, LANES), 1)
    parts = []
    for hf in range(n_groups16):
        for v in range(SSM_WIDTH // LANES):
            for jh in range(2):
                vs = [yt_ref[8 * v + gi, 0, hf * CHUNK:(hf + 1) * CHUNK,
                             jh * LANES:(jh + 1) * LANES].astype(F32) for gi in range(8)]
                outs = _block_transpose8(vs, lane)
                for jp in range(8):
                    r_ref[CHUNK * (8 * jh + jp):CHUNK * (8 * jh + jp + 1), v * LANES:(v + 1) * LANES] = outs[jp]
        parts.append(jnp.dot(perm_t_ref[...], r_ref[...].astype(BF16), preferred_element_type=F32))
    return jnp.concatenate(parts, axis=0)


def _mod_kernel(c_ref, w_ref, b_ref, o_ref):
    c = c_ref[...]
    s = c * jax.nn.sigmoid(c)
    o_ref[...] = jnp.dot(s, w_ref[...], preferred_element_type=F32) + b_ref[...]


def _modulation(c_rows, w_mod, b_mod):
    n = N_MOD * D_MODEL
    tn = 1536
    return pl.pallas_call(
        _mod_kernel,
        grid=(n // tn,),
        in_specs=[pl.BlockSpec((MOD_ROWS, D_MODEL), lambda j: (0, 0)),
                  pl.BlockSpec((D_MODEL, tn), lambda j: (0, j)),
                  pl.BlockSpec((1, tn), lambda j: (0, j))],
        out_specs=pl.BlockSpec((MOD_ROWS, tn), lambda j: (0, j)),
        out_shape=jax.ShapeDtypeStruct((MOD_ROWS, n), F32),
        name="modulation",
    )(c_rows, w_mod, b_mod.reshape(1, n))


def _rope_tables():
    n_freq = HEAD_DIM // 4
    inv_freq = ROPE_BASE ** (-np.arange(n_freq, dtype=np.float64) / n_freq)
    t = np.arange(SEQ)
    lane = np.arange(LANES)
    d = lane % HEAD_DIM
    use_col = (d // (HEAD_DIM // 2)) == 1
    w = d % (HEAD_DIM // 2)
    first = w < n_freq
    pos = np.where(use_col[None, :], (t % GRID_W)[:, None], (t // GRID_W)[:, None]).astype(np.float64)
    ang = pos * inv_freq[w % n_freq][None, :]
    cos = np.cos(ang)
    sin = np.sin(ang)
    sin_a = np.where(first[None, :], -sin, 0.0)
    sin_b = np.where(first[None, :], 0.0, sin)
    return (jnp.asarray(cos, F32), jnp.asarray(sin_a, F32), jnp.asarray(sin_b, F32))


def _rope_store(r, cos, sin_a, sin_b, out_ref):
    for j in range(ATTN_WIDTH // LANES):
        xs = r[:, j * LANES:(j + 1) * LANES]
        rot = (xs * cos + pltpu.roll(xs, LANES - HEAD_DIM // 4, 1) * sin_a
               + pltpu.roll(xs, HEAD_DIM // 4, 1) * sin_b)
        out_ref[0, :, j * LANES:(j + 1) * LANES] = rot.astype(BF16)


def _inproj_kernel(x_ref, mod_ref, g_ref, w_ref, cos_ref, sa_ref, sb_ref, perm_ref,
                   q_ref, k_ref, v_ref, u_ref, ut_ref, ga_ref, gs_ref):
    x = x_ref[0]
    shift = mod_ref[0, :, 0:D_MODEL]
    scale = mod_ref[0, :, D_MODEL:2 * D_MODEL]
    nb = _rms_modulate(x, g_ref[...], shift, scale).astype(BF16)
    cos = cos_ref[...]
    sin_a = sa_ref[...]
    sin_b = sb_ref[...]
    aw = ATTN_WIDTH
    q = jnp.dot(nb, w_ref[:, 0:aw], preferred_element_type=F32)
    _rope_store(q, cos, sin_a, sin_b, q_ref)
    k = jnp.dot(nb, w_ref[:, aw:2 * aw], preferred_element_type=F32)
    _rope_store(k, cos, sin_a, sin_b, k_ref)
    v_ref[0] = jnp.dot(nb, w_ref[:, 2 * aw:3 * aw], preferred_element_type=F32).astype(BF16)
    c0 = 3 * aw
    u16 = jnp.dot(nb, w_ref[:, c0:c0 + SSM_WIDTH], preferred_element_type=F32).astype(BF16)
    u_ref[0] = u16
    _store_chunk_layout(u16, perm_ref, ut_ref, x.shape[0] // (CHUNK * CHUNK))
    c1 = c0 + SSM_WIDTH
    ga = jnp.dot(nb, w_ref[:, c1:c1 + D_MODEL], preferred_element_type=F32)
    ga_ref[0] = jax.nn.sigmoid(ga).astype(BF16)
    c2 = c1 + D_MODEL
    gs = jnp.dot(nb, w_ref[:, c2:c2 + D_MODEL], preferred_element_type=F32)
    gs_ref[0] = jax.nn.sigmoid(gs).astype(BF16)


def _input_projection(x, mod3, norm_g, w_in_bf16, rope, perm):
    tm = 512
    n = CHUNK * CHUNK
    cos, sin_a, sin_b = rope
    tok = lambda width: pl.BlockSpec((1, tm, width), lambda i, b: (b, i, 0))
    tab = pl.BlockSpec((tm, LANES), lambda i, b: (i, 0))
    out = lambda width: jax.ShapeDtypeStruct((BATCH, SEQ, width), BF16)
    return pl.pallas_call(
        _inproj_kernel,
        grid=(SEQ // tm, BATCH),
        in_specs=[tok(D_MODEL),
                  pl.BlockSpec((1, 1, N_MOD * D_MODEL), lambda i, b: (b, 0, 0)),
                  pl.BlockSpec((1, D_MODEL), lambda i, b: (0, 0)),
                  pl.BlockSpec((D_MODEL, IN_COLS), lambda i, b: (0, 0)),
                  tab, tab, tab,
                  pl.BlockSpec((n, n), lambda i, b: (0, 0))],
        out_specs=[tok(ATTN_WIDTH), tok(ATTN_WIDTH), tok(ATTN_WIDTH), tok(SSM_WIDTH),
                   pl.BlockSpec((SSM_GROUPS, 1, tm // CHUNK, CHUNK_COLS), lambda i, b: (0, b, i, 0)),
                   tok(D_MODEL), tok(D_MODEL)],
        out_shape=[out(ATTN_WIDTH), out(ATTN_WIDTH), out(ATTN_WIDTH), out(SSM_WIDTH),
                   jax.ShapeDtypeStruct((SSM_GROUPS, BATCH, N_CHUNKS, CHUNK_COLS), BF16),
                   out(D_MODEL), out(D_MODEL)],
        compiler_params=pltpu.CompilerParams(
            dimension_semantics=("arbitrary", "arbitrary"), vmem_limit_bytes=VMEM_LIMIT),
        name="input_projection",
    )(x, mod3, norm_g, w_in_bf16, cos, sin_a, sin_b, perm)


def _ctx_proj_kernel(x_ref, mod_ref, g_ref, wk_ref, wv_ref, wu_ref, perm_ref, k_ref, v_ref, ut_ref):
    x = x_ref[0]
    shift = mod_ref[0, :, 0:D_MODEL]
    scale = mod_ref[0, :, D_MODEL:2 * D_MODEL]
    nb = _rms_modulate(x, g_ref[...], shift, scale).astype(BF16)
    k_ref[0] = jnp.dot(nb, wk_ref[...], preferred_element_type=F32).astype(BF16)
    v_ref[0] = jnp.dot(nb, wv_ref[...], preferred_element_type=F32).astype(BF16)
    u16 = jnp.dot(nb, wu_ref[...], preferred_element_type=F32).astype(BF16)
    _store_chunk_layout(u16, perm_ref, ut_ref, 1)


def _context_projection(ctx, mod3, norm_g, w_in_bf16, perm):
    n = CHUNK * CHUNK
    aw = ATTN_WIDTH
    tok = lambda width: pl.BlockSpec((1, CTX_LEN, width), lambda b: (b, 0, 0))
    wcol = lambda j: pl.BlockSpec((D_MODEL, aw), lambda b: (0, j))
    out = jax.ShapeDtypeStruct((BATCH, CTX_LEN, aw), BF16)
    return pl.pallas_call(
        _ctx_proj_kernel,
        grid=(BATCH,),
        in_specs=[tok(D_MODEL),
                  pl.BlockSpec((1, 1, N_MOD * D_MODEL), lambda b: (CTX_MOD_ROW, 0, 0)),
                  pl.BlockSpec((1, D_MODEL), lambda b: (0, 0)),
                  wcol(1), wcol(2), wcol(3),
                  pl.BlockSpec((n, n), lambda b: (0, 0))],
        out_specs=[tok(aw), tok(aw),
                   pl.BlockSpec((SSM_GROUPS, 1, N_CTX_CHUNKS, CHUNK_COLS), lambda b: (0, b, 0, 0))],
        out_shape=[out, out,
                   jax.ShapeDtypeStruct((SSM_GROUPS, BATCH, N_CTX_CHUNKS, CHUNK_COLS), BF16)],
        name="context_projection",
    )(ctx, mod3, norm_g, w_in_bf16, w_in_bf16, w_in_bf16, perm)


def _window_start_rows(r):
    return min(max(r - WIN_H // 2, 0), GRID_ROWS - WIN_H)


def _key_block_row(jb):
    return min(max(Q_ROWS_PER_STEP * jb - WIN_H // 2, 0), GRID_ROWS - K_ROWS_PER_STEP)


def _attn_block(q_ref, k_ref, v_ref, kc_ref, vc_ref, bias_ref, o_ref,
                s_ref, sc_ref, p_ref, pc_ref, l_ref, acc_ref, *, jb, key_start):
    key_row0 = _key_block_row(jb)
    offs = []
    deltas = []
    for i in range(Q_ROWS_PER_STEP):
        r = Q_ROWS_PER_STEP * jb + i
        rs = _window_start_rows(r)
        offs.append(rs - key_row0)
        deltas.append(r - rs)
    lane = lax.broadcasted_iota(jnp.int32, (1, LANES), 1)
    left = lane < HEAD_DIM
    q2 = q_ref[0]
    kblk = k_ref[0, pl.ds(key_start, K_BLOCK), :]
    vblk = v_ref[0, pl.ds(key_start, K_BLOCK), :]
    kc = kc_ref[0]
    vc = vc_ref[0]
    nt_dims = (((1,), (1,)), ((), ()))
    n_ktiles = K_BLOCK // LANES
    for e in range(2):
        qm = jnp.where(left if e == 0 else jnp.logical_not(left), q2, jnp.zeros_like(q2))
        s_ref[...] = lax.dot_general(qm, kblk, nt_dims, preferred_element_type=F32)
        sc_ref[...] = lax.dot_general(qm, kc, nt_dims, preferred_element_type=F32)
        for i in range(Q_ROWS_PER_STEP):
            rows = slice(GRID_W * i, GRID_W * (i + 1))
            par = offs[i] % 2
            t0 = offs[i] // 2
            n_tiles = WIN_H // 2 + par
            tiles = []
            for xt in range(n_tiles):
                t = t0 + xt
                dr0 = 2 * xt - par - deltas[i]
                st = s_ref[rows, t * LANES:(t + 1) * LANES] + bias_ref[0, e, dr0 + WIN_H]
                if par and xt == 0:
                    st = jnp.where(left, NEG_BIG, st)
                if par and xt == n_tiles - 1:
                    st = jnp.where(left, st, NEG_BIG)
                tiles.append(st)
            c_tiles = [sc_ref[rows, 0:LANES], sc_ref[rows, LANES:2 * LANES]]
            mt = c_tiles[0]
            for st in tiles + c_tiles[1:]:
                mt = jnp.maximum(mt, st)
            m = jnp.max(mt, axis=1, keepdims=True)
            lt = None
            for xt in range(n_tiles):
                t = t0 + xt
                pt = jnp.exp(tiles[xt] - m)
                lt = pt if lt is None else lt + pt
                p_ref[rows, t * LANES:(t + 1) * LANES] = pt.astype(BF16)
            for t in range(n_ktiles):
                if not (t0 <= t < t0 + n_tiles):
                    p_ref[rows, t * LANES:(t + 1) * LANES] = jnp.zeros((GRID_W, LANES), BF16)
            for ci in range(2):
                pt = jnp.exp(c_tiles[ci] - m)
                lt = lt + pt
                pc_ref[rows, ci * LANES:(ci + 1) * LANES] = pt.astype(BF16)
            l = jnp.sum(lt, axis=1, keepdims=True)
            l_ref[rows, :] = jnp.broadcast_to(1.0 / l, (GRID_W, LANES))
        o = (jnp.dot(p_ref[...], vblk, preferred_element_type=F32)
             + jnp.dot(pc_ref[...], vc, preferred_element_type=F32)) * l_ref[...]
        if e == 0:
            acc_ref[...] = o
        else:
            o_ref[0] = jnp.where(left, acc_ref[...], o).astype(BF16)


def _attn_kernel(q_ref, k_ref, v_ref, kc_ref, vc_ref, bias_ref, o_ref, *scratch):
    jb = pl.program_id(2)
    args = (q_ref, k_ref, v_ref, kc_ref, vc_ref, bias_ref, o_ref) + scratch
    last = N_QBLOCKS - 1

    @pl.when(jb == 0)
    def _():
        _attn_block(*args, jb=0, key_start=_key_block_row(0) * GRID_W)

    @pl.when(jnp.logical_and(jb > 0, jb < last))
    def _():
        start = pl.multiple_of((Q_ROWS_PER_STEP * jb - WIN_H // 2) * GRID_W, 256)
        _attn_block(*args, jb=1, key_start=start)

    @pl.when(jb == last)
    def _():
        _attn_block(*args, jb=last, key_start=_key_block_row(last) * GRID_W)


def _bias_tables(rpb):
    qcol = np.arange(GRID_W)
    kcol = np.arange(GRID_W)
    col_start = np.clip(qcol - WIN_W // 2, 0, GRID_W - WIN_W)
    in_win = (kcol[None, :] >= col_start[:, None]) & (kcol[None, :] < col_start[:, None] + WIN_W)
    dc_idx = np.clip(kcol[None, :] - qcol[:, None], -(WIN_W - 1), WIN_W - 1) + WIN_W - 1
    toe = rpb[:, :, dc_idx]
    toe = jnp.where(in_win[None, None], toe, NEG_BIG)
    neg = jnp.full((N_HEADS, 1, GRID_W, GRID_W), NEG_BIG, F32)
    ext = jnp.concatenate([neg, toe, neg], axis=1)
    pair = jnp.concatenate([ext[:, 0:16], ext[:, 1:17]], axis=-1)
    return pair.reshape(N_HEADS // 2, 2, 16, GRID_W, LANES)


def _attention(q, k, v, kc, vc, bias):
    qspec = pl.BlockSpec((1, Q_BLOCK, LANES), lambda hp, b, j: (b, j, hp))
    kspec = pl.BlockSpec((1, SEQ, LANES), lambda hp, b, j: (b, 0, hp))
    cspec = pl.BlockSpec((1, CTX_LEN, LANES), lambda hp, b, j: (b, 0, hp))
    bspec = pl.BlockSpec((1, 2, 16, GRID_W, LANES), lambda hp, b, j: (hp, 0, 0, 0, 0))
    return pl.pallas_call(
        _attn_kernel,
        grid=(N_HEADS // 2, BATCH, N_QBLOCKS),
        in_specs=[qspec, kspec, kspec, cspec, cspec, bspec],
        out_specs=qspec,
        out_shape=jax.ShapeDtypeStruct((BATCH, SEQ, ATTN_WIDTH), BF16),
        scratch_shapes=[pltpu.VMEM((Q_BLOCK, K_BLOCK), F32),
                        pltpu.VMEM((Q_BLOCK, CTX_LEN), F32),
                        pltpu.VMEM((Q_BLOCK, K_BLOCK), BF16),
                        pltpu.VMEM((Q_BLOCK, CTX_LEN), BF16),
                        pltpu.VMEM((Q_BLOCK, LANES), F32),
                        pltpu.VMEM((Q_BLOCK, LANES), F32)],
        compiler_params=pltpu.CompilerParams(
            dimension_semantics=("arbitrary", "arbitrary", "arbitrary"),
            vmem_limit_bytes=VMEM_LIMIT),
        name="attention",
    )(q, k, v, kc, vc, bias)


def _s5_matrices(lam_re, lam_im, log_dt, b_re, b_im, c_re, c_im):
    dt = jnp.exp(log_dt)[..., None]
    mag = jnp.exp(lam_re * dt)
    ang = lam_im * dt
    lb_re, lb_im = mag * jnp.cos(ang), mag * jnp.sin(ang)
    den = lam_re * lam_re + lam_im * lam_im
    nr, ni = lb_re - 1.0, lb_im
    f_re = (nr * lam_re + ni * lam_im) / den
    f_im = (ni * lam_re - nr * lam_im) / den
    bb_re = f_re[..., None] * b_re - f_im[..., None] * b_im
    bb_im = f_re[..., None] * b_im + f_im[..., None] * b_re
    kk = jnp.arange(CHUNK + 1, dtype=F32)[:, None, None, None]
    pmag = jnp.exp(lam_re * dt * kk)
    pang = lam_im * dt * kk
    pw_re, pw_im = pmag * jnp.cos(pang), pmag * jnp.sin(pang)

    hi = lax.Precision.HIGHEST
    x_re = pw_re[:, :, :, None, :] * c_re[None] - pw_im[:, :, :, None, :] * c_im[None]
    x_im = pw_re[:, :, :, None, :] * c_im[None] + pw_im[:, :, :, None, :] * c_re[None]
    kern = (jnp.einsum('kdgop,dgph->kdgoh', x_re, bb_re, precision=hi)
            - jnp.einsum('kdgop,dgph->kdgoh', x_im, bb_im, precision=hi))
    i_idx = np.arange(CHUNK)[None, :]
    j_idx = np.arange(CHUNK)[:, None]
    lag_f = np.clip(i_idx - j_idx, 0, CHUNK - 1)
    lag_b = np.clip(j_idx - i_idx, 0, CHUNK - 1)
    kf = kern[:CHUNK, 0][lag_f]
    kb = kern[:CHUNK, 1][lag_b]
    mf = jnp.asarray(i_idx >= j_idx, F32)[:, :, None, None, None]
    mb = jnp.asarray(j_idx >= i_idx, F32)[:, :, None, None, None]
    m_intra = kf * mf + kb * mb
    m_intra = jnp.transpose(m_intra, (2, 0, 4, 1, 3)).reshape(SSM_GROUPS, CHUNK_COLS, CHUNK_COLS)

    def state_cols(d, expo):
        pr = pw_re[expo, d]
        pi = pw_im[expo, d]
        sr = pr[..., None] * bb_re[d][None] - pi[..., None] * bb_im[d][None]
        si = pr[..., None] * bb_im[d][None] + pi[..., None] * bb_re[d][None]
        to_rows = lambda a: jnp.transpose(a, (1, 0, 3, 2)).reshape(SSM_GROUPS, CHUNK_COLS, SSM_STATE)
        return to_rows(sr), to_rows(si)

    fr, fi = state_cols(0, np.arange(CHUNK - 1, -1, -1))
    br, bi = state_cols(1, np.arange(CHUNK))
    m_state = jnp.concatenate([fr, br, fi, bi], axis=-1)

    def out_rows(d, expo):
        xr = x_re[expo, d]
        xi = x_im[expo, d]
        to_cols = lambda a: jnp.transpose(a, (1, 3, 0, 2)).reshape(SSM_GROUPS, SSM_STATE, CHUNK_COLS)
        return to_cols(xr), to_cols(-xi)

    ofr, ofi = out_rows(0, np.arange(1, CHUNK + 1))
    obr, obi = out_rows(1, np.arange(CHUNK, 0, -1))
    m_out = jnp.concatenate([ofr, obr, ofi, obi], axis=1)

    a_re = jnp.concatenate([pw_re[CHUNK, 0], pw_re[CHUNK, 1]], axis=-1)
    a_im = jnp.concatenate([pw_im[CHUNK, 0], pw_im[CHUNK, 1]], axis=-1)
    a16 = jnp.concatenate([a_re, a_im], axis=-1)
    a16 = jnp.broadcast_to(a16[:, None, :], (SSM_GROUPS, 8, 2 * LANES))
    return m_intra.astype(BF16), m_state.astype(BF16), m_out.astype(BF16), a16


def _s5_kernel(ul_ref, uc_ref, ms_ref, mi_ref, mo_ref, a_ref, y_ref, s_ref, sc_ref, hp_ref, *, gb):
    for gi in range(gb):
        for b in range(BATCH):
            sb = jnp.dot(ul_ref[gi, b], ms_ref[gi], preferred_element_type=F32)
            s_ref[gi, 0, pl.ds(b, N_CHUNKS, stride=BATCH), :] = sb[:, :LANES]
            s_ref[gi, 1, pl.ds(b, N_CHUNKS, stride=BATCH), :] = sb[:, LANES:]
            cb = jnp.dot(uc_ref[gi, b], ms_ref[gi], preferred_element_type=F32)
            sc_ref[gi, 0, pl.ds(b, N_CTX_CHUNKS, stride=BATCH), :] = cb[:, :LANES]
            sc_ref[gi, 1, pl.ds(b, N_CTX_CHUNKS, stride=BATCH), :] = cb[:, LANES:]
    lane = lax.broadcasted_iota(jnp.int32, (BATCH, LANES), 1)
    fwd = lane < SSM_STATE
    half = SSM_STATE

    def advance(gi, h_re, h_im, row_f, row_b, src):
        s_re = jnp.where(fwd, src[gi, 0, pl.ds(row_f, BATCH), :], src[gi, 0, pl.ds(row_b, BATCH), :])
        s_im = jnp.where(fwd, src[gi, 1, pl.ds(row_f, BATCH), :], src[gi, 1, pl.ds(row_b, BATCH), :])
        a_re = a_ref[gi, :, 0:LANES]
        a_im = a_ref[gi, :, LANES:2 * LANES]
        n_re = a_re * h_re - a_im * h_im + s_re
        n_im = a_re * h_im + a_im * h_re + s_im
        return n_re, n_im

    def ctx_step(t, carry):
        row_f = pl.multiple_of(t * BATCH, BATCH)
        row_b = pl.multiple_of((N_CTX_CHUNKS - 1 - t) * BATCH, BATCH)
        return tuple(advance(gi, carry[gi][0], carry[gi][1], row_f, row_b, sc_ref) for gi in range(gb))

    def lat_step(t, carry):
        row_f = pl.multiple_of(t * BATCH, BATCH)
        row_b = pl.multiple_of((N_CHUNKS - 1 - t) * BATCH, BATCH)
        out = []
        for gi in range(gb):
            h_re, h_im = carry[gi]
            hp_ref[gi, 0, pl.ds(row_f, BATCH), 0:half] = h_re[:, 0:half]
            hp_ref[gi, 0, pl.ds(row_b, BATCH), half:2 * half] = h_re[:, half:]
            hp_ref[gi, 1, pl.ds(row_f, BATCH), 0:half] = h_im[:, 0:half]
            hp_ref[gi, 1, pl.ds(row_b, BATCH), half:2 * half] = h_im[:, half:]
            out.append(advance(gi, h_re, h_im, row_f, row_b, s_ref))
        return tuple(out)

    zero = jnp.zeros((BATCH, LANES), F32)
    carry = tuple((zero, zero) for _ in range(gb))
    carry = lax.fori_loop(0, N_CTX_CHUNKS, ctx_step, carry)
    lax.fori_loop(0, N_CHUNKS, lat_step, carry)
    for gi in range(gb):
        for b in range(BATCH):
            hb_re = hp_ref[gi, 0, pl.ds(b, N_CHUNKS, stride=BATCH), :].astype(BF16)
            hb_im = hp_ref[gi, 1, pl.ds(b, N_CHUNKS, stride=BATCH), :].astype(BF16)
            y = (jnp.dot(ul_ref[gi, b], mi_ref[gi], preferred_element_type=F32)
                 + jnp.dot(hb_re, mo_ref[gi, 0:LANES, :], preferred_element_type=F32)
                 + jnp.dot(hb_im, mo_ref[gi, LANES:2 * LANES, :], preferred_element_type=F32))
            y_ref[gi, b] = y.astype(BF16)


def _s5_scan(u_lat_t, u_ctx_t, m_intra, m_state, m_out, a16):
    gb = 2
    rows = N_CHUNKS * BATCH
    crows = N_CTX_CHUNKS * BATCH
    grp = lambda r, c: pl.BlockSpec((gb, r, c), lambda g: (g, 0, 0))
    tok = lambda n: pl.BlockSpec((gb, BATCH, n, CHUNK_COLS), lambda g: (g, 0, 0, 0))
    return pl.pallas_call(
        functools.partial(_s5_kernel, gb=gb),
        grid=(SSM_GROUPS // gb,),
        in_specs=[tok(N_CHUNKS), tok(N_CTX_CHUNKS), grp(CHUNK_COLS, CHUNK_COLS),
                  grp(CHUNK_COLS, CHUNK_COLS), grp(CHUNK_COLS, CHUNK_COLS), grp(8, 2 * LANES)],
        out_specs=tok(N_CHUNKS),
        out_shape=jax.ShapeDtypeStruct((SSM_GROUPS, BATCH, N_CHUNKS, CHUNK_COLS), BF16),
        scratch_shapes=[pltpu.VMEM((gb, 2, rows, LANES), F32),
                        pltpu.VMEM((gb, 2, crows, LANES), F32),
                        pltpu.VMEM((gb, 2, rows, LANES), F32)],
        compiler_params=pltpu.CompilerParams(
            dimension_semantics=("arbitrary",), vmem_limit_bytes=VMEM_LIMIT),
        name="s5_scan",
    )(u_lat_t, u_ctx_t, m_state, m_intra, m_out, a16)


FFN_TILE = 256


def _post_kernel(x_ref, a_ref, yt_ref, u_ref, ga_ref, gs_ref, mod_ref, d_ref, fg_ref, og_ref, permt_ref,
                 wglu_ref, wba_ref, wbs_ref, wout_ref, wfi_ref, wfo_ref, o_ref, h1_ref, acc_ref, r_ref):
    dm = D_MODEL
    g1 = mod_ref[0, :, 2 * dm:3 * dm]
    sh2 = mod_ref[0, :, 3 * dm:4 * dm]
    sc2 = mod_ref[0, :, 4 * dm:5 * dm]
    g2 = mod_ref[0, :, 5 * dm:6 * dm]
    u = u_ref[0].astype(F32)
    y = _load_chunk_layout(yt_ref, permt_ref, r_ref, u.shape[0] // (CHUNK * CHUNK))
    sp = jax.nn.gelu(y + d_ref[...] * u).astype(BF16)
    vg = jnp.dot(sp, wglu_ref[...], preferred_element_type=F32)
    s = (vg[:, :SSM_WIDTH] * jax.nn.sigmoid(vg[:, SSM_WIDTH:])).astype(BF16)
    merged = (ga_ref[0].astype(F32) * jnp.dot(a_ref[0], wba_ref[...], preferred_element_type=F32)
              + gs_ref[0].astype(F32) * jnp.dot(s, wbs_ref[...], preferred_element_type=F32))
    mix = jnp.dot(merged.astype(BF16), wout_ref[...], preferred_element_type=F32)
    h1 = x_ref[0] + g1 * mix
    h1_ref[...] = h1
    n2 = _rms_modulate(h1, fg_ref[...], sh2, sc2).astype(BF16)
    for c in range(FFN_HIDDEN // FFN_TILE):
        lo = c * FFN_TILE
        fa = jnp.dot(n2, wfi_ref[:, lo:lo + FFN_TILE], preferred_element_type=F32)
        fb = jnp.dot(n2, wfi_ref[:, FFN_HIDDEN + lo:FFN_HIDDEN + lo + FFN_TILE],
                     preferred_element_type=F32)
        act = (fa * jax.nn.sigmoid(fa) * fb).astype(BF16)
        part = jnp.dot(act, wfo_ref[lo:lo + FFN_TILE, :], preferred_element_type=F32)
        if c == 0:
            acc_ref[...] = part
        else:
            acc_ref[...] += part
    h2 = h1_ref[...] + g2 * acc_ref[...]
    o_ref[0] = (h2 * lax.rsqrt(jnp.mean(h2 * h2, axis=-1, keepdims=True) + NORM_EPS)) * og_ref[...]


def _post(x, a, y_t, u, ga, gs, mod3, d_skip, ffn_g, fin_g, perm_t, wglu, wba, wbs, wout, wfi, wfo):
    tm = 512
    n = CHUNK * CHUNK
    tok = lambda width: pl.BlockSpec((1, tm, width), lambda b, i: (b, i, 0))
    const = lambda r, c: pl.BlockSpec((r, c), lambda b, i: (0, 0), pipeline_mode=pl.Buffered(1))
    return pl.pallas_call(
        _post_kernel,
        grid=(BATCH, SEQ // tm),
        in_specs=[tok(D_MODEL), tok(ATTN_WIDTH),
                  pl.BlockSpec((SSM_GROUPS, 1, tm // CHUNK, CHUNK_COLS), lambda b, i: (0, b, i, 0)),
                  tok(SSM_WIDTH), tok(D_MODEL), tok(D_MODEL),
                  pl.BlockSpec((1, 1, N_MOD * D_MODEL), lambda b, i: (b, 0, 0)),
                  const(1, SSM_WIDTH), const(1, D_MODEL), const(1, D_MODEL), const(n, n),
                  const(SSM_WIDTH, 2 * SSM_WIDTH), const(ATTN_WIDTH, D_MODEL),
                  const(SSM_WIDTH, D_MODEL), const(D_MODEL, D_MODEL),
                  const(D_MODEL, 2 * FFN_HIDDEN), const(FFN_HIDDEN, D_MODEL)],
        out_specs=tok(D_MODEL),
        out_shape=jax.ShapeDtypeStruct((BATCH, SEQ, D_MODEL), F32),
        scratch_shapes=[pltpu.VMEM((tm, D_MODEL), F32), pltpu.VMEM((tm, D_MODEL), F32),
                        pltpu.VMEM((n, SSM_WIDTH), F32)],
        compiler_params=pltpu.CompilerParams(
            dimension_semantics=("arbitrary", "arbitrary"), vmem_limit_bytes=VMEM_LIMIT),
        name="post",
    )(x, a, y_t, u, ga, gs, mod3, d_skip, ffn_g, fin_g, perm_t, wglu, wba, wbs, wout, wfi, wfo)


def kernel(x, c, ctx, c_ctx, w_mod, b_mod, attn_norm_g, ffn_norm_g, w_in, rel_pos_bias,
           ssm_lambda_re, ssm_lambda_im, ssm_log_dt, ssm_b_re, ssm_b_im, ssm_c_re, ssm_c_im, ssm_d,
           w_glu, w_branch_attn, w_branch_ssm, w_out, w_ffn_in, w_ffn_out, final_norm_g):
    assert x.shape == (BATCH, SEQ, D_MODEL) and w_mod.shape[0] == 1
    c_rows = jnp.concatenate(
        [c, c_ctx[None, :], jnp.zeros((MOD_ROWS - BATCH - 1, D_MODEL), F32)], axis=0)
    mod3 = _modulation(c_rows, w_mod[0], b_mod[0]).reshape(MOD_ROWS, 1, N_MOD * D_MODEL)

    col_scale = jnp.concatenate([jnp.full((ATTN_WIDTH,), HEAD_DIM ** -0.5, F32),
                                 jnp.ones((IN_COLS - ATTN_WIDTH,), F32)])
    w_in_bf16 = (w_in[0] * col_scale[None, :]).astype(BF16)
    norm_g = attn_norm_g[0].reshape(1, D_MODEL)

    perm = _chunk_perm()
    q, k, v, u, u_t, ga, gs = _input_projection(x, mod3, norm_g, w_in_bf16, _rope_tables(), perm)
    kc, vc, uc_t = _context_projection(ctx, mod3, norm_g, w_in_bf16, perm)

    attn = _attention(q, k, v, kc, vc, _bias_tables(rel_pos_bias[0]))

    m_intra, m_state, m_out, a16 = _s5_matrices(
        ssm_lambda_re[0], ssm_lambda_im[0], ssm_log_dt[0], ssm_b_re[0], ssm_b_im[0],
        ssm_c_re[0], ssm_c_im[0])
    y_t = _s5_scan(u_t, uc_t, m_intra, m_state, m_out, a16)

    return _post(x, attn, y_t, u, ga, gs, mod3,
                 ssm_d[0].reshape(1, SSM_WIDTH), ffn_norm_g[0].reshape(1, D_MODEL),
                 final_norm_g.reshape(1, D_MODEL), perm,
                 w_glu[0].astype(BF16), w_branch_attn[0].astype(BF16), w_branch_ssm[0].astype(BF16),
                 w_out[0].astype(BF16), w_ffn_in[0].astype(BF16), w_ffn_out[0].astype(BF16))
```

```python
import functools
import math

import numpy as np
import jax
import jax.numpy as jnp
from jax import lax
from jax.experimental import pallas as pl
from jax.experimental.pallas import tpu as pltpu

F32 = jnp.float32
BF16 = jnp.bfloat16

D_MODEL = 1024
BATCH = 8
SEQ = 4096
GRID_W = 64
GRID_ROWS = SEQ // GRID_W
CTX_LEN = 256
N_HEADS = 8
HEAD_DIM = 64
ATTN_WIDTH = N_HEADS * HEAD_DIM
WIN_H = 8
WIN_W = 16
ROPE_BASE = 10000.0
SSM_WIDTH = 512
SSM_GROUP = 16
SSM_GROUPS = SSM_WIDTH // SSM_GROUP
SSM_STATE = 64
FFN_HIDDEN = 2816
IN_COLS = 3 * ATTN_WIDTH + SSM_WIDTH + 2 * D_MODEL
N_MOD = 6
NORM_EPS = 1e-6
NEG_BIG = -1e30

LANES = 128
CHUNK = 16
N_CHUNKS = SEQ // CHUNK
N_CTX_CHUNKS = CTX_LEN // CHUNK
CHUNK_COLS = CHUNK * SSM_GROUP
MOD_ROWS = 16
CTX_MOD_ROW = BATCH
VMEM_LIMIT = 56 * 1024 * 1024

Q_ROWS_PER_STEP = 4
Q_BLOCK = Q_ROWS_PER_STEP * GRID_W
K_ROWS_PER_STEP = Q_ROWS_PER_STEP + WIN_H
K_BLOCK = K_ROWS_PER_STEP * GRID_W
N_QBLOCKS = GRID_ROWS // Q_ROWS_PER_STEP


def _rms_modulate(x, g, shift, scale):
    xn = x * lax.rsqrt(jnp.mean(x * x, axis=-1, keepdims=True) + NORM_EPS)
    return (xn * g) * (1.0 + scale) + shift


def _block_transpose8(vs, lane):
    for shift in (64, 32, 16):
        keep = (lane & (2 * shift - 1)) < shift
        dist = shift // SSM_GROUP
        out = list(vs)
        for a in range(8):
            if a & dist:
                continue
            b = a + dist
            out[a] = jnp.where(keep, vs[a], pltpu.roll(vs[b], shift, 1))
            out[b] = jnp.where(keep, pltpu.roll(vs[a], LANES - shift, 1), vs[b])
        vs = out
    return vs


def _chunk_perm():
    n = CHUNK * CHUNK
    r = np.arange(n)
    m = np.zeros((n, n), np.float32)
    m[r, (r % CHUNK) * CHUNK + r // CHUNK] = 1.0
    return jnp.asarray(m, BF16)


def _store_chunk_layout(u16, perm_ref, out_ref, n_groups16):
    lane = lax.broadcasted_iota(jnp.int32, (1, LANES), 1)
    n = CHUNK * CHUNK
    for hf in range(n_groups16):
        r = jnp.dot(perm_ref[...], u16[hf * n:(hf + 1) * n, :], preferred_element_type=F32)
        for v in range(SSM_WIDTH // LANES):
            for jh in range(2):
                vs = [r[CHUNK * (8 * jh + jp):CHUNK * (8 * jh + jp + 1), v * LANES:(v + 1) * LANES]
                      for jp in range(8)]
                outs = _block_transpose8(vs, lane)
                for gi in range(8):
                    out_ref[8 * v + gi, 0, hf * CHUNK:(hf + 1) * CHUNK, jh * LANES:(jh + 1) * LANES] = (
                        outs[gi].astype(BF16))


def _load_chunk_layout(yt_ref, perm_t_ref, r_ref, n_groups16):
    lane = lax.broadcasted_iota(jnp.int32, (1, LANES), 1)
    parts = []
    for hf in range(n_groups16):
        for v in range(SSM_WIDTH // LANES):
            for jh in range(2):
                vs = [yt_ref[8 * v + gi, 0, hf * CHUNK:(hf + 1) * CHUNK,
                             jh * LANES:(jh + 1) * LANES].astype(F32) for gi in range(8)]
                outs = _block_transpose8(vs, lane)
                for jp in range(8):
                    r_ref[CHUNK * (8 * jh + jp):CHUNK * (8 * jh + jp + 1), v * LANES:(v + 1) * LANES] = outs[jp]
        parts.append(jnp.dot(perm_t_ref[...], r_ref[...].astype(BF16), preferred_element_type=F32))
    return jnp.concatenate(parts, axis=0)


def _mod_kernel(c_ref, w_ref, b_ref, o_ref):
    c = c_ref[...]
    s = c * jax.nn.sigmoid(c)
    o_ref[...] = jnp.dot(s, w_ref[...], preferred_element_type=F32) + b_ref[...]


def _modulation(c_rows, w_mod, b_mod):
    n = N_MOD * D_MODEL
    tn = 1536
    return pl.pallas_call(
        _mod_kernel,
        grid=(n // tn,),
        in_specs=[pl.BlockSpec((MOD_ROWS, D_MODEL), lambda j: (0, 0)),
                  pl.BlockSpec((D_MODEL, tn), lambda j: (0, j)),
                  pl.BlockSpec((1, tn), lambda j: (0, j))],
        out_specs=pl.BlockSpec((MOD_ROWS, tn), lambda j: (0, j)),
        out_shape=jax.ShapeDtypeStruct((MOD_ROWS, n), F32),
        name="modulation",
    )(c_rows, w_mod, b_mod.reshape(1, n))


def _rope_tables():
    n_freq = HEAD_DIM // 4
    inv_freq = ROPE_BASE ** (-np.arange(n_freq, dtype=np.float64) / n_freq)
    t = np.arange(SEQ)
    lane = np.arange(LANES)
    d = lane % HEAD_DIM
    use_col = (d // (HEAD_DIM // 2)) == 1
    w = d % (HEAD_DIM // 2)
    first = w < n_freq
    pos = np.where(use_col[None, :], (t % GRID_W)[:, None], (t // GRID_W)[:, None]).astype(np.float64)
    ang = pos * inv_freq[w % n_freq][None, :]
    cos = np.cos(ang)
    sin = np.sin(ang)
    sin_a = np.where(first[None, :], -sin, 0.0)
    sin_b = np.where(first[None, :], 0.0, sin)
    return (jnp.asarray(cos, F32), jnp.asarray(sin_a, F32), jnp.asarray(sin_b, F32))


def _rope_store(r, cos, sin_a, sin_b, out_ref):
    for j in range(ATTN_WIDTH // LANES):
        xs = r[:, j * LANES:(j + 1) * LANES]
        rot = (xs * cos + pltpu.roll(xs, LANES - HEAD_DIM // 4, 1) * sin_a
               + pltpu.roll(xs, HEAD_DIM // 4, 1) * sin_b)
        out_ref[0, :, j * LANES:(j + 1) * LANES] = rot.astype(BF16)


def _inproj_kernel(x_ref, mod_ref, g_ref, w_ref, cos_ref, sa_ref, sb_ref, perm_ref,
                   q_ref, k_ref, v_ref, u_ref, ut_ref, ga_ref, gs_ref):
    x = x_ref[0]
    shift = mod_ref[0, :, 0:D_MODEL]
    scale = mod_ref[0, :, D_MODEL:2 * D_MODEL]
    nb = _rms_modulate(x, g_ref[...], shift, scale).astype(BF16)
    cos = cos_ref[...]
    sin_a = sa_ref[...]
    sin_b = sb_ref[...]
    aw = ATTN_WIDTH
    q = jnp.dot(nb, w_ref[:, 0:aw], preferred_element_type=F32)
    _rope_store(q, cos, sin_a, sin_b, q_ref)
    k = jnp.dot(nb, w_ref[:, aw:2 * aw], preferred_element_type=F32)
    _rope_store(k, cos, sin_a, sin_b, k_ref)
    v_ref[0] = jnp.dot(nb, w_ref[:, 2 * aw:3 * aw], preferred_element_type=F32).astype(BF16)
    c0 = 3 * aw
    u16 = jnp.dot(nb, w_ref[:, c0:c0 + SSM_WIDTH], preferred_element_type=F32).astype(BF16)
    u_ref[0] = u16
    _store_chunk_layout(u16, perm_ref, ut_ref, x.shape[0] // (CHUNK * CHUNK))
    c1 = c0 + SSM_WIDTH
    ga = jnp.dot(nb, w_ref[:, c1:c1 + D_MODEL], preferred_element_type=F32)
    ga_ref[0] = jax.nn.sigmoid(ga).astype(BF16)
    c2 = c1 + D_MODEL
    gs = jnp.dot(nb, w_ref[:, c2:c2 + D_MODEL], preferred_element_type=F32)
    gs_ref[0] = jax.nn.sigmoid(gs).astype(BF16)


def _input_projection(x, mod3, norm_g, w_in_bf16, rope, perm):
    tm = 512
    n = CHUNK * CHUNK
    cos, sin_a, sin_b = rope
    tok = lambda width: pl.BlockSpec((1, tm, width), lambda i, b: (b, i, 0))
    tab = pl.BlockSpec((tm, LANES), lambda i, b: (i, 0))
    out = lambda width: jax.ShapeDtypeStruct((BATCH, SEQ, width), BF16)
    return pl.pallas_call(
        _inproj_kernel,
        grid=(SEQ // tm, BATCH),
        in_specs=[tok(D_MODEL),
                  pl.BlockSpec((1, 1, N_MOD * D_MODEL), lambda i, b: (b, 0, 0)),
                  pl.BlockSpec((1, D_MODEL), lambda i, b: (0, 0)),
                  pl.BlockSpec((D_MODEL, IN_COLS), lambda i, b: (0, 0)),
                  tab, tab, tab,
                  pl.BlockSpec((n, n), lambda i, b: (0, 0))],
        out_specs=[tok(ATTN_WIDTH), tok(ATTN_WIDTH), tok(ATTN_WIDTH), tok(SSM_WIDTH),
                   pl.BlockSpec((SSM_GROUPS, 1, tm // CHUNK, CHUNK_COLS), lambda i, b: (0, b, i, 0)),
                   tok(D_MODEL), tok(D_MODEL)],
        out_shape=[out(ATTN_WIDTH), out(ATTN_WIDTH), out(ATTN_WIDTH), out(SSM_WIDTH),
                   jax.ShapeDtypeStruct((SSM_GROUPS, BATCH, N_CHUNKS, CHUNK_COLS), BF16),
                   out(D_MODEL), out(D_MODEL)],
        compiler_params=pltpu.CompilerParams(
            dimension_semantics=("arbitrary", "arbitrary"), vmem_limit_bytes=VMEM_LIMIT),
        name="input_projection",
    )(x, mod3, norm_g, w_in_bf16, cos, sin_a, sin_b, perm)


def _ctx_proj_kernel(x_ref, mod_ref, g_ref, wk_ref, wv_ref, wu_ref, perm_ref, k_ref, v_ref, ut_ref):
    x = x_ref[0]
    shift = mod_ref[0, :, 0:D_MODEL]
    scale = mod_ref[0, :, D_MODEL:2 * D_MODEL]
    nb = _rms_modulate(x, g_ref[...], shift, scale).astype(BF16)
    k_ref[0] = jnp.dot(nb, wk_ref[...], preferred_element_type=F32).astype(BF16)
    v_ref[0] = jnp.dot(nb, wv_ref[...], preferred_element_type=F32).astype(BF16)
    u16 = jnp.dot(nb, wu_ref[...], preferred_element_type=F32).astype(BF16)
    _store_chunk_layout(u16, perm_ref, ut_ref, 1)


def _context_projection(ctx, mod3, norm_g, w_in_bf16, perm):
    n = CHUNK * CHUNK
    aw = ATTN_WIDTH
    tok = lambda width: pl.BlockSpec((1, CTX_LEN, width), lambda b: (b, 0, 0))
    wcol = lambda j: pl.BlockSpec((D_MODEL, aw), lambda b: (0, j))
    out = jax.ShapeDtypeStruct((BATCH, CTX_LEN, aw), BF16)
    return pl.pallas_call(
        _ctx_proj_kernel,
        grid=(BATCH,),
        in_specs=[tok(D_MODEL),
                  pl.BlockSpec((1, 1, N_MOD * D_MODEL), lambda b: (CTX_MOD_ROW, 0, 0)),
                  pl.BlockSpec((1, D_MODEL), lambda b: (0, 0)),
                  wcol(1), wcol(2), wcol(3),
                  pl.BlockSpec((n, n), lambda b: (0, 0))],
        out_specs=[tok(aw), tok(aw),
                   pl.BlockSpec((SSM_GROUPS, 1, N_CTX_CHUNKS, CHUNK_COLS), lambda b: (0, b, 0, 0))],
        out_shape=[out, out,
                   jax.ShapeDtypeStruct((SSM_GROUPS, BATCH, N_CTX_CHUNKS, CHUNK_COLS), BF16)],
        name="context_projection",
    )(ctx, mod3, norm_g, w_in_bf16, w_in_bf16, w_in_bf16, perm)


def _window_start_rows(r):
    return min(max(r - WIN_H // 2, 0), GRID_ROWS - WIN_H)


def _key_block_row(jb):
    return min(max(Q_ROWS_PER_STEP * jb - WIN_H // 2, 0), GRID_ROWS - K_ROWS_PER_STEP)


def _window_geometry(jb):
    key_row0 = _key_block_row(jb)
    offs, deltas = [], []
    for i in range(Q_ROWS_PER_STEP):
        r = Q_ROWS_PER_STEP * jb + i
        rs = _window_start_rows(r)
        offs.append(rs - key_row0)
        deltas.append(r - rs)
    return offs, deltas


def _attn_kernel(q_ref, k_ref, v_ref, kc_ref, vc_ref, bias_ref, o_ref,
                 s_ref, sc_ref, p_ref, pc_ref, l_ref, acc_ref):
    lane = lax.broadcasted_iota(jnp.int32, (1, LANES), 1)
    left = lane < HEAD_DIM
    nt_dims = (((1,), (1,)), ((), ()))
    n_ktiles = K_BLOCK // LANES
    last = N_QBLOCKS - 1
    max_key_start = (GRID_ROWS - K_ROWS_PER_STEP) * GRID_W

    def scores(e, q_start, key_start):
        q2 = q_ref[0, pl.ds(q_start, Q_BLOCK), :]
        qm = jnp.where(left if e == 0 else jnp.logical_not(left), q2, jnp.zeros_like(q2))
        kblk = k_ref[0, pl.ds(key_start, K_BLOCK), :]
        s_ref[e] = lax.dot_general(qm, kblk, nt_dims, preferred_element_type=F32)
        sc_ref[e] = lax.dot_general(qm, kc_ref[0], nt_dims, preferred_element_type=F32)

    def softmax(e, offs, deltas):
        for i in range(Q_ROWS_PER_STEP):
            rows = slice(GRID_W * i, GRID_W * (i + 1))
            par = offs[i] % 2
            t0 = offs[i] // 2
            n_tiles = WIN_H // 2 + par
            tiles = []
            for xt in range(n_tiles):
                t = t0 + xt
                dr0 = 2 * xt - par - deltas[i]
                st = s_ref[e, rows, t * LANES:(t + 1) * LANES] + bias_ref[0, e, dr0 + WIN_H]
                if par and xt == 0:
                    st = jnp.where(left, NEG_BIG, st)
                if par and xt == n_tiles - 1:
                    st = jnp.where(left, st, NEG_BIG)
                tiles.append(st)
            c_tiles = [sc_ref[e, rows, 0:LANES], sc_ref[e, rows, LANES:2 * LANES]]
            mt = c_tiles[0]
            for st in tiles + c_tiles[1:]:
                mt = jnp.maximum(mt, st)
            m = jnp.max(mt, axis=1, keepdims=True)
            lt = None
            for xt in range(n_tiles):
                t = t0 + xt
                pt = jnp.exp(tiles[xt] - m)
                lt = pt if lt is None else lt + pt
                p_ref[e, rows, t * LANES:(t + 1) * LANES] = pt.astype(BF16)
            for t in range(n_ktiles):
                if not (t0 <= t < t0 + n_tiles):
                    p_ref[e, rows, t * LANES:(t + 1) * LANES] = jnp.zeros((GRID_W, LANES), BF16)
            for ci in range(2):
                pt = jnp.exp(c_tiles[ci] - m)
                lt = lt + pt
                pc_ref[e, rows, ci * LANES:(ci + 1) * LANES] = pt.astype(BF16)
            l = jnp.sum(lt, axis=1, keepdims=True)
            l_ref[e, rows, :] = jnp.broadcast_to(1.0 / l, (GRID_W, LANES))

    def values(e, key_start):
        vblk = v_ref[0, pl.ds(key_start, K_BLOCK), :]
        return (jnp.dot(p_ref[e], vblk, preferred_element_type=F32)
                + jnp.dot(pc_ref[e], vc_ref[0], preferred_element_type=F32)) * l_ref[e]

    def block(jb_static, q_start, key_start, next_starts):
        offs, deltas = _window_geometry(jb_static)
        scores(1, q_start, key_start)
        softmax(0, offs, deltas)
        acc_ref[...] = values(0, key_start)
        if next_starts is not None:
            scores(0, *next_starts)
        softmax(1, offs, deltas)
        o1 = values(1, key_start)
        o_ref[0, pl.ds(q_start, Q_BLOCK), :] = jnp.where(left, acc_ref[...], o1).astype(BF16)

    def starts(jb):
        q_start = pl.multiple_of(jb * Q_BLOCK, Q_BLOCK)
        key_start = jnp.clip((jb * Q_ROWS_PER_STEP - WIN_H // 2) * GRID_W, 0, max_key_start)
        return q_start, pl.multiple_of(key_start, Q_BLOCK)

    scores(0, 0, 0)
    block(0, 0, 0, (Q_BLOCK, 0))

    def interior(jb, carry):
        q_start, key_start = starts(jb)
        block(1, q_start, key_start, starts(jb + 1))
        return carry

    lax.fori_loop(1, last, interior, 0)
    block(last, last * Q_BLOCK, max_key_start, None)


def _bias_tables(rpb):
    qcol = np.arange(GRID_W)
    kcol = np.arange(GRID_W)
    col_start = np.clip(qcol - WIN_W // 2, 0, GRID_W - WIN_W)
    in_win = (kcol[None, :] >= col_start[:, None]) & (kcol[None, :] < col_start[:, None] + WIN_W)
    dc_idx = np.clip(kcol[None, :] - qcol[:, None], -(WIN_W - 1), WIN_W - 1) + WIN_W - 1
    toe = rpb[:, :, dc_idx]
    toe = jnp.where(in_win[None, None], toe, NEG_BIG)
    neg = jnp.full((N_HEADS, 1, GRID_W, GRID_W), NEG_BIG, F32)
    ext = jnp.concatenate([neg, toe, neg], axis=1)
    pair = jnp.concatenate([ext[:, 0:16], ext[:, 1:17]], axis=-1)
    return pair.reshape(N_HEADS // 2, 2, 16, GRID_W, LANES)


def _attention(q, k, v, kc, vc, bias):
    kspec = pl.BlockSpec((1, SEQ, LANES), lambda hp, b: (b, 0, hp))
    cspec = pl.BlockSpec((1, CTX_LEN, LANES), lambda hp, b: (b, 0, hp))
    bspec = pl.BlockSpec((1, 2, 16, GRID_W, LANES), lambda hp, b: (hp, 0, 0, 0, 0))
    return pl.pallas_call(
        _attn_kernel,
        grid=(N_HEADS // 2, BATCH),
        in_specs=[kspec, kspec, kspec, cspec, cspec, bspec],
        out_specs=kspec,
        out_shape=jax.ShapeDtypeStruct((BATCH, SEQ, ATTN_WIDTH), BF16),
        scratch_shapes=[pltpu.VMEM((2, Q_BLOCK, K_BLOCK), F32),
                        pltpu.VMEM((2, Q_BLOCK, CTX_LEN), F32),
                        pltpu.VMEM((2, Q_BLOCK, K_BLOCK), BF16),
                        pltpu.VMEM((2, Q_BLOCK, CTX_LEN), BF16),
                        pltpu.VMEM((2, Q_BLOCK, LANES), F32),
                        pltpu.VMEM((Q_BLOCK, LANES), F32)],
        compiler_params=pltpu.CompilerParams(
            dimension_semantics=("arbitrary", "arbitrary"),
            vmem_limit_bytes=VMEM_LIMIT),
        name="attention",
    )(q, k, v, kc, vc, bias)


def _s5_matrices(lam_re, lam_im, log_dt, b_re, b_im, c_re, c_im):
    dt = jnp.exp(log_dt)[..., None]
    mag = jnp.exp(lam_re * dt)
    ang = lam_im * dt
    lb_re, lb_im = mag * jnp.cos(ang), mag * jnp.sin(ang)
    den = lam_re * lam_re + lam_im * lam_im
    nr, ni = lb_re - 1.0, lb_im
    f_re = (nr * lam_re + ni * lam_im) / den
    f_im = (ni * lam_re - nr * lam_im) / den
    bb_re = f_re[..., None] * b_re - f_im[..., None] * b_im
    bb_im = f_re[..., None] * b_im + f_im[..., None] * b_re
    kk = jnp.arange(CHUNK + 1, dtype=F32)[:, None, None, None]
    pmag = jnp.exp(lam_re * dt * kk)
    pang = lam_im * dt * kk
    pw_re, pw_im = pmag * jnp.cos(pang), pmag * jnp.sin(pang)

    hi = lax.Precision.HIGHEST
    x_re = pw_re[:, :, :, None, :] * c_re[None] - pw_im[:, :, :, None, :] * c_im[None]
    x_im = pw_re[:, :, :, None, :] * c_im[None] + pw_im[:, :, :, None, :] * c_re[None]
    kern = (jnp.einsum('kdgop,dgph->kdgoh', x_re, bb_re, precision=hi)
            - jnp.einsum('kdgop,dgph->kdgoh', x_im, bb_im, precision=hi))
    i_idx = np.arange(CHUNK)[None, :]
    j_idx = np.arange(CHUNK)[:, None]
    lag_f = np.clip(i_idx - j_idx, 0, CHUNK - 1)
    lag_b = np.clip(j_idx - i_idx, 0, CHUNK - 1)
    kf = kern[:CHUNK, 0][lag_f]
    kb = kern[:CHUNK, 1][lag_b]
    mf = jnp.asarray(i_idx >= j_idx, F32)[:, :, None, None, None]
    mb = jnp.asarray(j_idx >= i_idx, F32)[:, :, None, None, None]
    m_intra = kf * mf + kb * mb
    m_intra = jnp.transpose(m_intra, (2, 0, 4, 1, 3)).reshape(SSM_GROUPS, CHUNK_COLS, CHUNK_COLS)

    def state_cols(d, expo):
        pr = pw_re[expo, d]
        pi = pw_im[expo, d]
        sr = pr[..., None] * bb_re[d][None] - pi[..., None] * bb_im[d][None]
        si = pr[..., None] * bb_im[d][None] + pi[..., None] * bb_re[d][None]
        to_rows = lambda a: jnp.transpose(a, (1, 0, 3, 2)).reshape(SSM_GROUPS, CHUNK_COLS, SSM_STATE)
        return to_rows(sr), to_rows(si)

    fr, fi = state_cols(0, np.arange(CHUNK - 1, -1, -1))
    br, bi = state_cols(1, np.arange(CHUNK))
    m_state = jnp.concatenate([fr, br, fi, bi], axis=-1)

    def out_rows(d, expo):
        xr = x_re[expo, d]
        xi = x_im[expo, d]
        to_cols = lambda a: jnp.transpose(a, (1, 3, 0, 2)).reshape(SSM_GROUPS, SSM_STATE, CHUNK_COLS)
        return to_cols(xr), to_cols(-xi)

    ofr, ofi = out_rows(0, np.arange(1, CHUNK + 1))
    obr, obi = out_rows(1, np.arange(CHUNK, 0, -1))
    m_out = jnp.concatenate([ofr, obr, ofi, obi], axis=1)

    a_re = jnp.concatenate([pw_re[CHUNK, 0], pw_re[CHUNK, 1]], axis=-1)
    a_im = jnp.concatenate([pw_im[CHUNK, 0], pw_im[CHUNK, 1]], axis=-1)
    a16 = jnp.concatenate([a_re, a_im], axis=-1)
    a16 = jnp.broadcast_to(a16[:, None, :], (SSM_GROUPS, 8, 2 * LANES))
    return m_intra.astype(BF16), m_state.astype(BF16), m_out.astype(BF16), a16


def _s5_kernel(ul_ref, uc_ref, ms_ref, mi_ref, mo_ref, a_ref, y_ref, s_ref, sc_ref, hp_ref, *, gb):
    for gi in range(gb):
        for b in range(BATCH):
            sb = jnp.dot(ul_ref[gi, b], ms_ref[gi], preferred_element_type=F32)
            s_ref[gi, 0, pl.ds(b, N_CHUNKS, stride=BATCH), :] = sb[:, :LANES]
            s_ref[gi, 1, pl.ds(b, N_CHUNKS, stride=BATCH), :] = sb[:, LANES:]
            cb = jnp.dot(uc_ref[gi, b], ms_ref[gi], preferred_element_type=F32)
            sc_ref[gi, 0, pl.ds(b, N_CTX_CHUNKS, stride=BATCH), :] = cb[:, :LANES]
            sc_ref[gi, 1, pl.ds(b, N_CTX_CHUNKS, stride=BATCH), :] = cb[:, LANES:]
    lane = lax.broadcasted_iota(jnp.int32, (BATCH, LANES), 1)
    fwd = lane < SSM_STATE
    half = SSM_STATE

    def advance(gi, h_re, h_im, row_f, row_b, src):
        s_re = jnp.where(fwd, src[gi, 0, pl.ds(row_f, BATCH), :], src[gi, 0, pl.ds(row_b, BATCH), :])
        s_im = jnp.where(fwd, src[gi, 1, pl.ds(row_f, BATCH), :], src[gi, 1, pl.ds(row_b, BATCH), :])
        a_re = a_ref[gi, :, 0:LANES]
        a_im = a_ref[gi, :, LANES:2 * LANES]
        n_re = a_re * h_re - a_im * h_im + s_re
        n_im = a_re * h_im + a_im * h_re + s_im
        return n_re, n_im

    def ctx_step(t, carry):
        row_f = pl.multiple_of(t * BATCH, BATCH)
        row_b = pl.multiple_of((N_CTX_CHUNKS - 1 - t) * BATCH, BATCH)
        return tuple(advance(gi, carry[gi][0], carry[gi][1], row_f, row_b, sc_ref) for gi in range(gb))

    def lat_step(t, carry):
        row_f = pl.multiple_of(t * BATCH, BATCH)
        row_b = pl.multiple_of((N_CHUNKS - 1 - t) * BATCH, BATCH)
        out = []
        for gi in range(gb):
            h_re, h_im = carry[gi]
            hp_ref[gi, 0, pl.ds(row_f, BATCH), 0:half] = h_re[:, 0:half]
            hp_ref[gi, 0, pl.ds(row_b, BATCH), half:2 * half] = h_re[:, half:]
            hp_ref[gi, 1, pl.ds(row_f, BATCH), 0:half] = h_im[:, 0:half]
            hp_ref[gi, 1, pl.ds(row_b, BATCH), half:2 * half] = h_im[:, half:]
            out.append(advance(gi, h_re, h_im, row_f, row_b, s_ref))
        return tuple(out)

    zero = jnp.zeros((BATCH, LANES), F32)
    carry = tuple((zero, zero) for _ in range(gb))
    carry = lax.fori_loop(0, N_CTX_CHUNKS, ctx_step, carry)
    lax.fori_loop(0, N_CHUNKS, lat_step, carry)
    for gi in range(gb):
        for b in range(BATCH):
            hb_re = hp_ref[gi, 0, pl.ds(b, N_CHUNKS, stride=BATCH), :].astype(BF16)
            hb_im = hp_ref[gi, 1, pl.ds(b, N_CHUNKS, stride=BATCH), :].astype(BF16)
            y = (jnp.dot(ul_ref[gi, b], mi_ref[gi], preferred_element_type=F32)
                 + jnp.dot(hb_re, mo_ref[gi, 0:LANES, :], preferred_element_type=F32)
                 + jnp.dot(hb_im, mo_ref[gi, LANES:2 * LANES, :], preferred_element_type=F32))
            y_ref[gi, b] = y.astype(BF16)


def _s5_scan(u_lat_t, u_ctx_t, m_intra, m_state, m_out, a16):
    gb = 2
    rows = N_CHUNKS * BATCH
    crows = N_CTX_CHUNKS * BATCH
    grp = lambda r, c: pl.BlockSpec((gb, r, c), lambda g: (g, 0, 0))
    tok = lambda n: pl.BlockSpec((gb, BATCH, n, CHUNK_COLS), lambda g: (g, 0, 0, 0))
    return pl.pallas_call(
        functools.partial(_s5_kernel, gb=gb),
        grid=(SSM_GROUPS // gb,),
        in_specs=[tok(N_CHUNKS), tok(N_CTX_CHUNKS), grp(CHUNK_COLS, CHUNK_COLS),
                  grp(CHUNK_COLS, CHUNK_COLS), grp(CHUNK_COLS, CHUNK_COLS), grp(8, 2 * LANES)],
        out_specs=tok(N_CHUNKS),
        out_shape=jax.ShapeDtypeStruct((SSM_GROUPS, BATCH, N_CHUNKS, CHUNK_COLS), BF16),
        scratch_shapes=[pltpu.VMEM((gb, 2, rows, LANES), F32),
                        pltpu.VMEM((gb, 2, crows, LANES), F32),
                        pltpu.VMEM((gb, 2, rows, LANES), F32)],
        compiler_params=pltpu.CompilerParams(
            dimension_semantics=("arbitrary",), vmem_limit_bytes=VMEM_LIMIT),
        name="s5_scan",
    )(u_lat_t, u_ctx_t, m_state, m_intra, m_out, a16)


FFN_TILE = 256


def _post_kernel(x_ref, a_ref, yt_ref, u_ref, ga_ref, gs_ref, mod_ref, d_ref, fg_ref, og_ref, permt_ref,
                 wglu_ref, wba_ref, wbs_ref, wout_ref, wfi_ref, wfo_ref, o_ref, h1_ref, acc_ref, r_ref):
    dm = D_MODEL
    g1 = mod_ref[0, :, 2 * dm:3 * dm]
    sh2 = mod_ref[0, :, 3 * dm:4 * dm]
    sc2 = mod_ref[0, :, 4 * dm:5 * dm]
    g2 = mod_ref[0, :, 5 * dm:6 * dm]
    u = u_ref[0].astype(F32)
    y = _load_chunk_layout(yt_ref, permt_ref, r_ref, u.shape[0] // (CHUNK * CHUNK))
    sp = jax.nn.gelu(y + d_ref[...] * u).astype(BF16)
    vg = jnp.dot(sp, wglu_ref[...], preferred_element_type=F32)
    s = (vg[:, :SSM_WIDTH] * jax.nn.sigmoid(vg[:, SSM_WIDTH:])).astype(BF16)
    merged = (ga_ref[0].astype(F32) * jnp.dot(a_ref[0], wba_ref[...], preferred_element_type=F32)
              + gs_ref[0].astype(F32) * jnp.dot(s, wbs_ref[...], preferred_element_type=F32))
    mix = jnp.dot(merged.astype(BF16), wout_ref[...], preferred_element_type=F32)
    h1 = x_ref[0] + g1 * mix
    h1_ref[...] = h1
    n2 = _rms_modulate(h1, fg_ref[...], sh2, sc2).astype(BF16)
    for c in range(FFN_HIDDEN // FFN_TILE):
        lo = c * FFN_TILE
        fa = jnp.dot(n2, wfi_ref[:, lo:lo + FFN_TILE], preferred_element_type=F32)
        fb = jnp.dot(n2, wfi_ref[:, FFN_HIDDEN + lo:FFN_HIDDEN + lo + FFN_TILE],
                     preferred_element_type=F32)
        act = (fa * jax.nn.sigmoid(fa) * fb).astype(BF16)
        part = jnp.dot(act, wfo_ref[lo:lo + FFN_TILE, :], preferred_element_type=F32)
        if c == 0:
            acc_ref[...] = part
        else:
            acc_ref[...] += part
    h2 = h1_ref[...] + g2 * acc_ref[...]
    o_ref[0] = (h2 * lax.rsqrt(jnp.mean(h2 * h2, axis=-1, keepdims=True) + NORM_EPS)) * og_ref[...]


def _post(x, a, y_t, u, ga, gs, mod3, d_skip, ffn_g, fin_g, perm_t, wglu, wba, wbs, wout, wfi, wfo):
    tm = 512
    n = CHUNK * CHUNK
    tok = lambda width: pl.BlockSpec((1, tm, width), lambda b, i: (b, i, 0))
    const = lambda r, c: pl.BlockSpec((r, c), lambda b, i: (0, 0), pipeline_mode=pl.Buffered(1))
    return pl.pallas_call(
        _post_kernel,
        grid=(BATCH, SEQ // tm),
        in_specs=[tok(D_MODEL), tok(ATTN_WIDTH),
                  pl.BlockSpec((SSM_GROUPS, 1, tm // CHUNK, CHUNK_COLS), lambda b, i: (0, b, i, 0)),
                  tok(SSM_WIDTH), tok(D_MODEL), tok(D_MODEL),
                  pl.BlockSpec((1, 1, N_MOD * D_MODEL), lambda b, i: (b, 0, 0)),
                  const(1, SSM_WIDTH), const(1, D_MODEL), const(1, D_MODEL), const(n, n),
                  const(SSM_WIDTH, 2 * SSM_WIDTH), const(ATTN_WIDTH, D_MODEL),
                  const(SSM_WIDTH, D_MODEL), const(D_MODEL, D_MODEL),
                  const(D_MODEL, 2 * FFN_HIDDEN), const(FFN_HIDDEN, D_MODEL)],
        out_specs=tok(D_MODEL),
        out_shape=jax.ShapeDtypeStruct((BATCH, SEQ, D_MODEL), F32),
        scratch_shapes=[pltpu.VMEM((tm, D_MODEL), F32), pltpu.VMEM((tm, D_MODEL), F32),
                        pltpu.VMEM((n, SSM_WIDTH), F32)],
        compiler_params=pltpu.CompilerParams(
            dimension_semantics=("arbitrary", "arbitrary"), vmem_limit_bytes=VMEM_LIMIT),
        name="post",
    )(x, a, y_t, u, ga, gs, mod3, d_skip, ffn_g, fin_g, perm_t, wglu, wba, wbs, wout, wfi, wfo)


def kernel(x, c, ctx, c_ctx, w_mod, b_mod, attn_norm_g, ffn_norm_g, w_in, rel_pos_bias,
           ssm_lambda_re, ssm_lambda_im, ssm_log_dt, ssm_b_re, ssm_b_im, ssm_c_re, ssm_c_im, ssm_d,
           w_glu, w_branch_attn, w_branch_ssm, w_out, w_ffn_in, w_ffn_out, final_norm_g):
    assert x.shape == (BATCH, SEQ, D_MODEL) and w_mod.shape[0] == 1
    c_rows = jnp.concatenate(
        [c, c_ctx[None, :], jnp.zeros((MOD_ROWS - BATCH - 1, D_MODEL), F32)], axis=0)
    mod3 = _modulation(c_rows, w_mod[0], b_mod[0]).reshape(MOD_ROWS, 1, N_MOD * D_MODEL)

    col_scale = jnp.concatenate([jnp.full((ATTN_WIDTH,), HEAD_DIM ** -0.5, F32),
                                 jnp.ones((IN_COLS - ATTN_WIDTH,), F32)])
    w_in_bf16 = (w_in[0] * col_scale[None, :]).astype(BF16)
    norm_g = attn_norm_g[0].reshape(1, D_MODEL)

    perm = _chunk_perm()
    q, k, v, u, u_t, ga, gs = _input_projection(x, mod3, norm_g, w_in_bf16, _rope_tables(), perm)
    kc, vc, uc_t = _context_projection(ctx, mod3, norm_g, w_in_bf16, perm)

    attn = _attention(q, k, v, kc, vc, _bias_tables(rel_pos_bias[0]))

    m_intra, m_state, m_out, a16 = _s5_matrices(
        ssm_lambda_re[0], ssm_lambda_im[0], ssm_log_dt[0], ssm_b_re[0], ssm_b_im[0],
        ssm_c_re[0], ssm_c_im[0])
    y_t = _s5_scan(u_t, uc_t, m_intra, m_state, m_out, a16)

    return _post(x, attn, y_t, u, ga, gs, mod3,
                 ssm_d[0].reshape(1, SSM_WIDTH), ffn_norm_g[0].reshape(1, D_MODEL),
                 final_norm_g.reshape(1, D_MODEL), perm,
                 w_glu[0].astype(BF16), w_branch_attn[0].astype(BF16), w_branch_ssm[0].astype(BF16),
                 w_out[0].astype(BF16), w_ffn_in[0].astype(BF16), w_ffn_out[0].astype(BF16))
```

```python
import functools
import math

import numpy as np
import jax
import jax.numpy as jnp
from jax import lax
from jax.experimental import pallas as pl
from jax.experimental.pallas import tpu as pltpu

F32 = jnp.float32
BF16 = jnp.bfloat16

D_MODEL = 1024
BATCH = 8
SEQ = 4096
GRID_W = 64
GRID_ROWS = SEQ // GRID_W
CTX_LEN = 256
N_HEADS = 8
HEAD_DIM = 64
ATTN_WIDTH = N_HEADS * HEAD_DIM
WIN_H = 8
WIN_W = 16
ROPE_BASE = 10000.0
SSM_WIDTH = 512
SSM_GROUP = 16
SSM_GROUPS = SSM_WIDTH // SSM_GROUP
SSM_STATE = 64
FFN_HIDDEN = 2816
IN_COLS = 3 * ATTN_WIDTH + SSM_WIDTH + 2 * D_MODEL
N_MOD = 6
NORM_EPS = 1e-6
NEG_BIG = -1e30
LOG2E = math.log2(math.e)

LANES = 128
CHUNK = 16
N_CHUNKS = SEQ // CHUNK
N_CTX_CHUNKS = CTX_LEN // CHUNK
CHUNK_COLS = CHUNK * SSM_GROUP
MOD_ROWS = 16
CTX_MOD_ROW = BATCH
VMEM_LIMIT = 56 * 1024 * 1024

Q_ROWS_PER_STEP = 4
Q_BLOCK = Q_ROWS_PER_STEP * GRID_W
K_ROWS_PER_STEP = Q_ROWS_PER_STEP + WIN_H
K_BLOCK = K_ROWS_PER_STEP * GRID_W
N_QBLOCKS = GRID_ROWS // Q_ROWS_PER_STEP


def _rms_modulate(x, g, shift, scale):
    xn = x * lax.rsqrt(jnp.mean(x * x, axis=-1, keepdims=True) + NORM_EPS)
    return (xn * g) * (1.0 + scale) + shift


def _block_transpose8(vs, lane):
    for shift in (64, 32, 16):
        keep = (lane & (2 * shift - 1)) < shift
        dist = shift // SSM_GROUP
        out = list(vs)
        for a in range(8):
            if a & dist:
                continue
            b = a + dist
            out[a] = jnp.where(keep, vs[a], pltpu.roll(vs[b], shift, 1))
            out[b] = jnp.where(keep, pltpu.roll(vs[a], LANES - shift, 1), vs[b])
        vs = out
    return vs


def _chunk_perm():
    n = CHUNK * CHUNK
    r = np.arange(n)
    m = np.zeros((n, n), np.float32)
    m[r, (r % CHUNK) * CHUNK + r // CHUNK] = 1.0
    return jnp.asarray(m, BF16)


def _store_chunk_layout(u16, perm_ref, out_ref, n_groups16):
    lane = lax.broadcasted_iota(jnp.int32, (1, LANES), 1)
    n = CHUNK * CHUNK
    for hf in range(n_groups16):
        r = jnp.dot(perm_ref[...], u16[hf * n:(hf + 1) * n, :], preferred_element_type=F32)
        for v in range(SSM_WIDTH // LANES):
            for jh in range(2):
                vs = [r[CHUNK * (8 * jh + jp):CHUNK * (8 * jh + jp + 1), v * LANES:(v + 1) * LANES]
                      for jp in range(8)]
                outs = _block_transpose8(vs, lane)
                for gi in range(8):
                    out_ref[8 * v + gi, 0, hf * CHUNK:(hf + 1) * CHUNK, jh * LANES:(jh + 1) * LANES] = (
                        outs[gi].astype(BF16))


def _load_chunk_layout(yt_ref, perm_t_ref, r_ref, n_groups16):
    lane = lax.broadcasted_iota(jnp.int32, (1, LANES), 1)
    parts = []
    for hf in range(n_groups16):
        for v in range(SSM_WIDTH // LANES):
            for jh in range(2):
                vs = [yt_ref[8 * v + gi, 0, hf * CHUNK:(hf + 1) * CHUNK,
                             jh * LANES:(jh + 1) * LANES].astype(F32) for gi in range(8)]
                outs = _block_transpose8(vs, lane)
                for jp in range(8):
                    r_ref[CHUNK * (8 * jh + jp):CHUNK * (8 * jh + jp + 1), v * LANES:(v + 1) * LANES] = outs[jp]
        parts.append(jnp.dot(perm_t_ref[...], r_ref[...].astype(BF16), preferred_element_type=F32))
    return jnp.concatenate(parts, axis=0)


def _mod_kernel(c_ref, w_ref, b_ref, o_ref):
    c = c_ref[...]
    s = c * jax.nn.sigmoid(c)
    o_ref[...] = jnp.dot(s, w_ref[...], preferred_element_type=F32) + b_ref[...]


def _modulation(c_rows, w_mod, b_mod):
    n = N_MOD * D_MODEL
    tn = 1536
    return pl.pallas_call(
        _mod_kernel,
        grid=(n // tn,),
        in_specs=[pl.BlockSpec((MOD_ROWS, D_MODEL), lambda j: (0, 0)),
                  pl.BlockSpec((D_MODEL, tn), lambda j: (0, j)),
                  pl.BlockSpec((1, tn), lambda j: (0, j))],
        out_specs=pl.BlockSpec((MOD_ROWS, tn), lambda j: (0, j)),
        out_shape=jax.ShapeDtypeStruct((MOD_ROWS, n), F32),
        name="modulation",
    )(c_rows, w_mod, b_mod.reshape(1, n))


def _rope_tables():
    n_freq = HEAD_DIM // 4
    inv_freq = ROPE_BASE ** (-np.arange(n_freq, dtype=np.float64) / n_freq)
    t = np.arange(SEQ)
    lane = np.arange(LANES)
    d = lane % HEAD_DIM
    use_col = (d // (HEAD_DIM // 2)) == 1
    w = d % (HEAD_DIM // 2)
    first = w < n_freq
    pos = np.where(use_col[None, :], (t % GRID_W)[:, None], (t // GRID_W)[:, None]).astype(np.float64)
    ang = pos * inv_freq[w % n_freq][None, :]
    cos = np.cos(ang)
    sin = np.sin(ang)
    sin_a = np.where(first[None, :], -sin, 0.0)
    sin_b = np.where(first[None, :], 0.0, sin)
    return (jnp.asarray(cos, F32), jnp.asarray(sin_a, F32), jnp.asarray(sin_b, F32))


def _rope_store(r, cos, sin_a, sin_b, out_ref):
    for j in range(ATTN_WIDTH // LANES):
        xs = r[:, j * LANES:(j + 1) * LANES]
        rot = (xs * cos + pltpu.roll(xs, LANES - HEAD_DIM // 4, 1) * sin_a
               + pltpu.roll(xs, HEAD_DIM // 4, 1) * sin_b)
        out_ref[0, :, j * LANES:(j + 1) * LANES] = rot.astype(BF16)


def _inproj_kernel(x_ref, mod_ref, g_ref, w_ref, cos_ref, sa_ref, sb_ref, perm_ref,
                   q_ref, k_ref, v_ref, u_ref, ut_ref, ga_ref, gs_ref):
    x = x_ref[0]
    shift = mod_ref[0, :, 0:D_MODEL]
    scale = mod_ref[0, :, D_MODEL:2 * D_MODEL]
    nb = _rms_modulate(x, g_ref[...], shift, scale).astype(BF16)
    cos = cos_ref[...]
    sin_a = sa_ref[...]
    sin_b = sb_ref[...]
    aw = ATTN_WIDTH
    q = jnp.dot(nb, w_ref[:, 0:aw], preferred_element_type=F32)
    _rope_store(q, cos, sin_a, sin_b, q_ref)
    k = jnp.dot(nb, w_ref[:, aw:2 * aw], preferred_element_type=F32)
    _rope_store(k, cos, sin_a, sin_b, k_ref)
    v_ref[0] = jnp.dot(nb, w_ref[:, 2 * aw:3 * aw], preferred_element_type=F32).astype(BF16)
    c0 = 3 * aw
    u16 = jnp.dot(nb, w_ref[:, c0:c0 + SSM_WIDTH], preferred_element_type=F32).astype(BF16)
    u_ref[0] = u16
    _store_chunk_layout(u16, perm_ref, ut_ref, x.shape[0] // (CHUNK * CHUNK))
    c1 = c0 + SSM_WIDTH
    ga = jnp.dot(nb, w_ref[:, c1:c1 + D_MODEL], preferred_element_type=F32)
    ga_ref[0] = jax.nn.sigmoid(ga).astype(BF16)
    c2 = c1 + D_MODEL
    gs = jnp.dot(nb, w_ref[:, c2:c2 + D_MODEL], preferred_element_type=F32)
    gs_ref[0] = jax.nn.sigmoid(gs).astype(BF16)


def _input_projection(x, mod3, norm_g, w_in_bf16, rope, perm):
    tm = 512
    n = CHUNK * CHUNK
    cos, sin_a, sin_b = rope
    tok = lambda width: pl.BlockSpec((1, tm, width), lambda i, b: (b, i, 0))
    tab = pl.BlockSpec((tm, LANES), lambda i, b: (i, 0))
    out = lambda width: jax.ShapeDtypeStruct((BATCH, SEQ, width), BF16)
    return pl.pallas_call(
        _inproj_kernel,
        grid=(SEQ // tm, BATCH),
        in_specs=[tok(D_MODEL),
                  pl.BlockSpec((1, 1, N_MOD * D_MODEL), lambda i, b: (b, 0, 0)),
                  pl.BlockSpec((1, D_MODEL), lambda i, b: (0, 0)),
                  pl.BlockSpec((D_MODEL, IN_COLS), lambda i, b: (0, 0)),
                  tab, tab, tab,
                  pl.BlockSpec((n, n), lambda i, b: (0, 0))],
        out_specs=[tok(ATTN_WIDTH), tok(ATTN_WIDTH), tok(ATTN_WIDTH), tok(SSM_WIDTH),
                   pl.BlockSpec((SSM_GROUPS, 1, tm // CHUNK, CHUNK_COLS), lambda i, b: (0, b, i, 0)),
                   tok(D_MODEL), tok(D_MODEL)],
        out_shape=[out(ATTN_WIDTH), out(ATTN_WIDTH), out(ATTN_WIDTH), out(SSM_WIDTH),
                   jax.ShapeDtypeStruct((SSM_GROUPS, BATCH, N_CHUNKS, CHUNK_COLS), BF16),
                   out(D_MODEL), out(D_MODEL)],
        compiler_params=pltpu.CompilerParams(
            dimension_semantics=("arbitrary", "arbitrary"), vmem_limit_bytes=VMEM_LIMIT),
        name="input_projection",
    )(x, mod3, norm_g, w_in_bf16, cos, sin_a, sin_b, perm)


def _ctx_proj_kernel(x_ref, mod_ref, g_ref, wk_ref, wv_ref, wu_ref, perm_ref, k_ref, v_ref, ut_ref):
    x = x_ref[0]
    shift = mod_ref[0, :, 0:D_MODEL]
    scale = mod_ref[0, :, D_MODEL:2 * D_MODEL]
    nb = _rms_modulate(x, g_ref[...], shift, scale).astype(BF16)
    k_ref[0] = jnp.dot(nb, wk_ref[...], preferred_element_type=F32).astype(BF16)
    v_ref[0] = jnp.dot(nb, wv_ref[...], preferred_element_type=F32).astype(BF16)
    u16 = jnp.dot(nb, wu_ref[...], preferred_element_type=F32).astype(BF16)
    _store_chunk_layout(u16, perm_ref, ut_ref, 1)


def _context_projection(ctx, mod3, norm_g, w_in_bf16, perm):
    n = CHUNK * CHUNK
    aw = ATTN_WIDTH
    tok = lambda width: pl.BlockSpec((1, CTX_LEN, width), lambda b: (b, 0, 0))
    wcol = lambda j: pl.BlockSpec((D_MODEL, aw), lambda b: (0, j))
    out = jax.ShapeDtypeStruct((BATCH, CTX_LEN, aw), BF16)
    return pl.pallas_call(
        _ctx_proj_kernel,
        grid=(BATCH,),
        in_specs=[tok(D_MODEL),
                  pl.BlockSpec((1, 1, N_MOD * D_MODEL), lambda b: (CTX_MOD_ROW, 0, 0)),
                  pl.BlockSpec((1, D_MODEL), lambda b: (0, 0)),
                  wcol(1), wcol(2), wcol(3),
                  pl.BlockSpec((n, n), lambda b: (0, 0))],
        out_specs=[tok(aw), tok(aw),
                   pl.BlockSpec((SSM_GROUPS, 1, N_CTX_CHUNKS, CHUNK_COLS), lambda b: (0, b, 0, 0))],
        out_shape=[out, out,
                   jax.ShapeDtypeStruct((SSM_GROUPS, BATCH, N_CTX_CHUNKS, CHUNK_COLS), BF16)],
        name="context_projection",
    )(ctx, mod3, norm_g, w_in_bf16, w_in_bf16, w_in_bf16, perm)


def _window_start_rows(r):
    return min(max(r - WIN_H // 2, 0), GRID_ROWS - WIN_H)


def _key_block_row(jb):
    return min(max(Q_ROWS_PER_STEP * jb - WIN_H // 2, 0), GRID_ROWS - K_ROWS_PER_STEP)


def _window_geometry(jb):
    key_row0 = _key_block_row(jb)
    offs, deltas = [], []
    for i in range(Q_ROWS_PER_STEP):
        r = Q_ROWS_PER_STEP * jb + i
        rs = _window_start_rows(r)
        offs.append(rs - key_row0)
        deltas.append(r - rs)
    return offs, deltas


def _attn_kernel(q_ref, k_ref, v_ref, kc_ref, vc_ref, bias_ref, o_ref,
                 s_ref, sc_ref, p_ref, pc_ref, l_ref, acc_ref):
    lane = lax.broadcasted_iota(jnp.int32, (1, LANES), 1)
    left = lane < HEAD_DIM
    nt_dims = (((1,), (1,)), ((), ()))
    n_ktiles = K_BLOCK // LANES
    last = N_QBLOCKS - 1
    max_key_start = (GRID_ROWS - K_ROWS_PER_STEP) * GRID_W

    def scores(e, q_start, key_start):
        q2 = q_ref[0, pl.ds(q_start, Q_BLOCK), :]
        qm = jnp.where(left if e == 0 else jnp.logical_not(left), q2, jnp.zeros_like(q2))
        kblk = k_ref[0, pl.ds(key_start, K_BLOCK), :]
        s_ref[e] = lax.dot_general(qm, kblk, nt_dims, preferred_element_type=F32)
        sc_ref[e] = lax.dot_general(qm, kc_ref[0], nt_dims, preferred_element_type=F32)

    def softmax(e, offs, deltas):
        for i in range(Q_ROWS_PER_STEP):
            rows = slice(GRID_W * i, GRID_W * (i + 1))
            par = offs[i] % 2
            t0 = offs[i] // 2
            n_tiles = WIN_H // 2 + par
            tiles = []
            for xt in range(n_tiles):
                t = t0 + xt
                dr0 = 2 * xt - par - deltas[i]
                st = s_ref[e, rows, t * LANES:(t + 1) * LANES] + bias_ref[0, e, dr0 + WIN_H]
                if par and xt == 0:
                    st = jnp.where(left, NEG_BIG, st)
                if par and xt == n_tiles - 1:
                    st = jnp.where(left, st, NEG_BIG)
                tiles.append(st)
            c_tiles = [sc_ref[e, rows, 0:LANES], sc_ref[e, rows, LANES:2 * LANES]]
            mt = c_tiles[0]
            for st in tiles + c_tiles[1:]:
                mt = jnp.maximum(mt, st)
            m = jnp.max(mt, axis=1, keepdims=True)
            lt = None
            for xt in range(n_tiles):
                t = t0 + xt
                pt = jnp.exp2(tiles[xt] - m)
                lt = pt if lt is None else lt + pt
                p_ref[e, rows, t * LANES:(t + 1) * LANES] = pt.astype(BF16)
            for t in range(n_ktiles):
                if not (t0 <= t < t0 + n_tiles):
                    p_ref[e, rows, t * LANES:(t + 1) * LANES] = jnp.zeros((GRID_W, LANES), BF16)
            for ci in range(2):
                pt = jnp.exp2(c_tiles[ci] - m)
                lt = lt + pt
                pc_ref[e, rows, ci * LANES:(ci + 1) * LANES] = pt.astype(BF16)
            l = jnp.sum(lt, axis=1, keepdims=True)
            l_ref[e, rows, :] = jnp.broadcast_to(1.0 / l, (GRID_W, LANES))

    def values(e, key_start):
        vblk = v_ref[0, pl.ds(key_start, K_BLOCK), :]
        return (jnp.dot(p_ref[e], vblk, preferred_element_type=F32)
                + jnp.dot(pc_ref[e], vc_ref[0], preferred_element_type=F32)) * l_ref[e]

    def block(jb_static, q_start, key_start, next_starts):
        offs, deltas = _window_geometry(jb_static)
        scores(1, q_start, key_start)
        softmax(0, offs, deltas)
        acc_ref[...] = values(0, key_start)
        if next_starts is not None:
            scores(0, *next_starts)
        softmax(1, offs, deltas)
        o1 = values(1, key_start)
        o_ref[0, pl.ds(q_start, Q_BLOCK), :] = jnp.where(left, acc_ref[...], o1).astype(BF16)

    def starts(jb):
        q_start = pl.multiple_of(jb * Q_BLOCK, Q_BLOCK)
        key_start = jnp.clip((jb * Q_ROWS_PER_STEP - WIN_H // 2) * GRID_W, 0, max_key_start)
        return q_start, pl.multiple_of(key_start, Q_BLOCK)

    scores(0, 0, 0)
    block(0, 0, 0, (Q_BLOCK, 0))

    def interior(jb, carry):
        q_start, key_start = starts(jb)
        block(1, q_start, key_start, starts(jb + 1))
        return carry

    lax.fori_loop(1, last, interior, 0)
    block(last, last * Q_BLOCK, max_key_start, None)


def _bias_tables(rpb):
    qcol = np.arange(GRID_W)
    kcol = np.arange(GRID_W)
    col_start = np.clip(qcol - WIN_W // 2, 0, GRID_W - WIN_W)
    in_win = (kcol[None, :] >= col_start[:, None]) & (kcol[None, :] < col_start[:, None] + WIN_W)
    dc_idx = np.clip(kcol[None, :] - qcol[:, None], -(WIN_W - 1), WIN_W - 1) + WIN_W - 1
    sel = (np.arange(2 * WIN_W - 1)[:, None, None] == dc_idx[None]).astype(np.float32)
    toe = jnp.einsum('hdt,tck->hdck', rpb * LOG2E, jnp.asarray(sel),
                     precision=lax.Precision.HIGHEST)
    toe = jnp.where(in_win[None, None], toe, NEG_BIG)
    neg = jnp.full((N_HEADS, 1, GRID_W, GRID_W), NEG_BIG, F32)
    ext = jnp.concatenate([neg, toe, neg], axis=1)
    pair = jnp.concatenate([ext[:, 0:16], ext[:, 1:17]], axis=-1)
    return pair.reshape(N_HEADS // 2, 2, 16, GRID_W, LANES)


def _attention(q, k, v, kc, vc, bias):
    kspec = pl.BlockSpec((1, SEQ, LANES), lambda hp, b: (b, 0, hp))
    cspec = pl.BlockSpec((1, CTX_LEN, LANES), lambda hp, b: (b, 0, hp))
    bspec = pl.BlockSpec((1, 2, 16, GRID_W, LANES), lambda hp, b: (hp, 0, 0, 0, 0))
    return pl.pallas_call(
        _attn_kernel,
        grid=(N_HEADS // 2, BATCH),
        in_specs=[kspec, kspec, kspec, cspec, cspec, bspec],
        out_specs=kspec,
        out_shape=jax.ShapeDtypeStruct((BATCH, SEQ, ATTN_WIDTH), BF16),
        scratch_shapes=[pltpu.VMEM((2, Q_BLOCK, K_BLOCK), F32),
                        pltpu.VMEM((2, Q_BLOCK, CTX_LEN), F32),
                        pltpu.VMEM((2, Q_BLOCK, K_BLOCK), BF16),
                        pltpu.VMEM((2, Q_BLOCK, CTX_LEN), BF16),
                        pltpu.VMEM((2, Q_BLOCK, LANES), F32),
                        pltpu.VMEM((Q_BLOCK, LANES), F32)],
        compiler_params=pltpu.CompilerParams(
            dimension_semantics=("arbitrary", "arbitrary"),
            vmem_limit_bytes=VMEM_LIMIT),
        name="attention",
    )(q, k, v, kc, vc, bias)


def _rot256(a, b, s, lane):
    s %= 2 * LANES
    if s >= LANES:
        a, b, s = b, a, s - LANES
    if s == 0:
        return a, b
    ra = pltpu.roll(a, s, 1)
    rb = pltpu.roll(b, s, 1)
    keep = lane >= s
    return jnp.where(keep, ra, rb), jnp.where(keep, rb, ra)


def _s5_prep_kernel(par_ref, b_ref, c_ref, t16_ref, e_ref, mi_ref, ms_ref, mo_ref, a_ref):
    hi = lax.Precision.HIGHEST
    dot = lambda x, y: jnp.dot(x, y, preferred_element_type=F32, precision=hi)
    ns = SSM_STATE
    par = par_ref[0]
    lam_re, lam_im = par[:, 0:1], par[:, 1:2]
    dt = jnp.exp(par[:, 2:3])
    lane = lax.broadcasted_iota(jnp.int32, (1, LANES), 1)
    kf = lane.astype(F32)
    mag = jnp.exp((lam_re * dt) * kf)
    ang = (lam_im * dt) * kf
    pw_re, pw_im = mag * jnp.cos(ang), mag * jnp.sin(ang)
    lb_re = jnp.sum(jnp.where(lane == 1, pw_re, 0.0), axis=1, keepdims=True)
    lb_im = jnp.sum(jnp.where(lane == 1, pw_im, 0.0), axis=1, keepdims=True)
    den = lam_re * lam_re + lam_im * lam_im
    nr, ni = lb_re - 1.0, lb_im
    f_re = (nr * lam_re + ni * lam_im) / den
    f_im = (ni * lam_re - nr * lam_im) / den
    b_re, b_im = b_ref[0, :, 0:SSM_GROUP], b_ref[0, :, SSM_GROUP:2 * SSM_GROUP]
    bb_re = f_re * b_re - f_im * b_im
    bb_im = f_re * b_im + f_im * b_re
    t16 = t16_ref[...]
    bbt_re, bbt_im = dot(bb_re, t16), dot(bb_im, t16)
    ct_re = dot(c_ref[0, :, 0:SSM_GROUP], t16)
    ct_im = dot(c_ref[0, :, SSM_GROUP:2 * SSM_GROUP], t16)
    pw_at = lambda x: (dot(pw_re, e_ref[x]), dot(pw_im, e_ref[x]))
    id_re, id_im = pw_at(0)
    rev_re, rev_im = pw_at(1)
    p1_re, p1_im = pw_at(2)
    r16_re, r16_im = pw_at(3)
    f, b = slice(0, ns), slice(ns, 2 * ns)
    cmul = lambda ar, ai, br, bi: (ar * br - ai * bi, ar * bi + ai * br)

    sf_re, sf_im = cmul(rev_re[f], rev_im[f], bbt_re[f], bbt_im[f])
    sb_re, sb_im = cmul(id_re[b], id_im[b], bbt_re[b], bbt_im[b])
    ms_ref[0] = jnp.concatenate([sf_re, sb_re, sf_im, sb_im], axis=0).T.astype(BF16)

    of_re, of_im = cmul(p1_re[f], p1_im[f], ct_re[f], ct_im[f])
    ob_re, ob_im = cmul(r16_re[b], r16_im[b], ct_re[b], ct_im[b])
    mo_ref[0] = jnp.concatenate([of_re, ob_re, -of_im, -ob_im], axis=0).astype(BF16)

    xf_re, xf_im = cmul(id_re[f], id_im[f], ct_re[f], ct_im[f])
    xb_re, xb_im = cmul(rev_re[b], rev_im[b], ct_re[b], ct_im[b])
    btf = jnp.concatenate([bbt_re[f], bbt_im[f]], axis=0).T
    btb = jnp.concatenate([bbt_re[b], bbt_im[b]], axis=0).T
    g_f = dot(btf, jnp.concatenate([xf_re, -xf_im], axis=0))
    g_b = dot(btb, jnp.concatenate([xb_re, -xb_im], axis=0))
    for j in range(CHUNK):
        rows = slice(SSM_GROUP * j, SSM_GROUP * (j + 1))
        lo_col, hi_col = SSM_GROUP * j, SSM_GROUP * (j + 1)
        f_lo, f_hi = _rot256(g_f[rows, :LANES], g_f[rows, LANES:], lo_col, lane)
        b_lo, b_hi = _rot256(g_b[rows, :LANES], g_b[rows, LANES:], -SSM_GROUP * (CHUNK - 1 - j), lane)
        lo = jnp.where(lane >= lo_col, f_lo, 0.0) + jnp.where(lane < hi_col, b_lo, 0.0)
        up = jnp.where(lane + LANES >= lo_col, f_hi, 0.0) + jnp.where(lane + LANES < hi_col, b_hi, 0.0)
        mi_ref[0, rows, 0:LANES] = lo.astype(BF16)
        mi_ref[0, rows, LANES:2 * LANES] = up.astype(BF16)

    a16 = jnp.concatenate([pw_re.T[CHUNK:CHUNK + 1, :], pw_im.T[CHUNK:CHUNK + 1, :]], axis=1)
    a_ref[0] = jnp.broadcast_to(a16, (8, 2 * LANES))


def _s5_matrices(lam_re, lam_im, log_dt, b_re, b_im, c_re, c_im):
    g, p2 = SSM_GROUPS, 2 * SSM_STATE
    both = lambda a: jnp.transpose(a, (1, 0, 2)).reshape(g, p2)
    log_dt_rows = jnp.repeat(jnp.transpose(log_dt), SSM_STATE, axis=1)
    par = jnp.stack([both(lam_re), both(lam_im), log_dt_rows] + [jnp.zeros((g, p2), F32)] * 5, axis=-1)
    rows_b = lambda a: jnp.transpose(a, (1, 0, 2, 3)).reshape(g, p2, SSM_GROUP)
    rows_c = lambda a: jnp.transpose(a, (1, 0, 3, 2)).reshape(g, p2, SSM_GROUP)
    b_cat = jnp.concatenate([rows_b(b_re), rows_b(b_im)], axis=-1)
    c_cat = jnp.concatenate([rows_c(c_re), rows_c(c_im)], axis=-1)

    col = np.arange(CHUNK_COLS)
    tile16 = (col[None, :] % SSM_GROUP == np.arange(SSM_GROUP)[:, None]).astype(np.float32)
    pos = col // SSM_GROUP
    k_idx = np.arange(LANES)[:, None]
    expand = np.stack([k_idx == pos[None, :], k_idx == (CHUNK - 1 - pos)[None, :],
                       k_idx == (pos + 1)[None, :], k_idx == (CHUNK - pos)[None, :]]).astype(np.float32)

    mat = lambda: pl.BlockSpec((1, CHUNK_COLS, CHUNK_COLS), lambda i: (i, 0, 0))
    mat_shape = jax.ShapeDtypeStruct((g, CHUNK_COLS, CHUNK_COLS), BF16)
    return pl.pallas_call(
        _s5_prep_kernel,
        grid=(g,),
        in_specs=[pl.BlockSpec((1, p2, 8), lambda i: (i, 0, 0)),
                  pl.BlockSpec((1, p2, 2 * SSM_GROUP), lambda i: (i, 0, 0)),
                  pl.BlockSpec((1, p2, 2 * SSM_GROUP), lambda i: (i, 0, 0)),
                  pl.BlockSpec((SSM_GROUP, CHUNK_COLS), lambda i: (0, 0)),
                  pl.BlockSpec((4, LANES, CHUNK_COLS), lambda i: (0, 0, 0))],
        out_specs=[mat(), mat(), mat(), pl.BlockSpec((1, 8, 2 * LANES), lambda i: (i, 0, 0))],
        out_shape=[mat_shape, mat_shape, mat_shape, jax.ShapeDtypeStruct((g, 8, 2 * LANES), F32)],
        name="s5_prep",
    )(par, b_cat, c_cat, jnp.asarray(tile16), jnp.asarray(expand))


def _s5_kernel(ul_ref, uc_ref, ms_ref, mi_ref, mo_ref, a_ref, y_ref, s_ref, sc_ref, hp_ref, *, gb):
    for gi in range(gb):
        for b in range(BATCH):
            sb = jnp.dot(ul_ref[gi, b], ms_ref[gi], preferred_element_type=F32)
            s_ref[gi, 0, pl.ds(b, N_CHUNKS, stride=BATCH), :] = sb[:, :LANES]
            s_ref[gi, 1, pl.ds(b, N_CHUNKS, stride=BATCH), :] = sb[:, LANES:]
            cb = jnp.dot(uc_ref[gi, b], ms_ref[gi], preferred_element_type=F32)
            sc_ref[gi, 0, pl.ds(b, N_CTX_CHUNKS, stride=BATCH), :] = cb[:, :LANES]
            sc_ref[gi, 1, pl.ds(b, N_CTX_CHUNKS, stride=BATCH), :] = cb[:, LANES:]
    lane = lax.broadcasted_iota(jnp.int32, (BATCH, LANES), 1)
    fwd = lane < SSM_STATE
    half = SSM_STATE

    def advance(gi, h_re, h_im, row_f, row_b, src):
        s_re = jnp.where(fwd, src[gi, 0, pl.ds(row_f, BATCH), :], src[gi, 0, pl.ds(row_b, BATCH), :])
        s_im = jnp.where(fwd, src[gi, 1, pl.ds(row_f, BATCH), :], src[gi, 1, pl.ds(row_b, BATCH), :])
        a_re = a_ref[gi, :, 0:LANES]
        a_im = a_ref[gi, :, LANES:2 * LANES]
        n_re = a_re * h_re - a_im * h_im + s_re
        n_im = a_re * h_im + a_im * h_re + s_im
        return n_re, n_im

    def ctx_step(t, carry):
        row_f = pl.multiple_of(t * BATCH, BATCH)
        row_b = pl.multiple_of((N_CTX_CHUNKS - 1 - t) * BATCH, BATCH)
        return tuple(advance(gi, carry[gi][0], carry[gi][1], row_f, row_b, sc_ref) for gi in range(gb))

    def lat_step(t, carry):
        row_f = pl.multiple_of(t * BATCH, BATCH)
        row_b = pl.multiple_of((N_CHUNKS - 1 - t) * BATCH, BATCH)
        out = []
        for gi in range(gb):
            h_re, h_im = carry[gi]
            hp_ref[gi, 0, pl.ds(row_f, BATCH), 0:half] = h_re[:, 0:half]
            hp_ref[gi, 0, pl.ds(row_b, BATCH), half:2 * half] = h_re[:, half:]
            hp_ref[gi, 1, pl.ds(row_f, BATCH), 0:half] = h_im[:, 0:half]
            hp_ref[gi, 1, pl.ds(row_b, BATCH), half:2 * half] = h_im[:, half:]
            out.append(advance(gi, h_re, h_im, row_f, row_b, s_ref))
        return tuple(out)

    zero = jnp.zeros((BATCH, LANES), F32)
    carry = tuple((zero, zero) for _ in range(gb))
    carry = lax.fori_loop(0, N_CTX_CHUNKS, ctx_step, carry)
    lax.fori_loop(0, N_CHUNKS, lat_step, carry)
    for gi in range(gb):
        for b in range(BATCH):
            hb_re = hp_ref[gi, 0, pl.ds(b, N_CHUNKS, stride=BATCH), :].astype(BF16)
            hb_im = hp_ref[gi, 1, pl.ds(b, N_CHUNKS, stride=BATCH), :].astype(BF16)
            y = (jnp.dot(ul_ref[gi, b], mi_ref[gi], preferred_element_type=F32)
                 + jnp.dot(hb_re, mo_ref[gi, 0:LANES, :], preferred_element_type=F32)
                 + jnp.dot(hb_im, mo_ref[gi, LANES:2 * LANES, :], preferred_element_type=F32))
            y_ref[gi, b] = y.astype(BF16)


def _s5_scan(u_lat_t, u_ctx_t, m_intra, m_state, m_out, a16):
    gb = 2
    rows = N_CHUNKS * BATCH
    crows = N_CTX_CHUNKS * BATCH
    grp = lambda r, c: pl.BlockSpec((gb, r, c), lambda g: (g, 0, 0))
    tok = lambda n: pl.BlockSpec((gb, BATCH, n, CHUNK_COLS), lambda g: (g, 0, 0, 0))
    return pl.pallas_call(
        functools.partial(_s5_kernel, gb=gb),
        grid=(SSM_GROUPS // gb,),
        in_specs=[tok(N_CHUNKS), tok(N_CTX_CHUNKS), grp(CHUNK_COLS, CHUNK_COLS),
                  grp(CHUNK_COLS, CHUNK_COLS), grp(CHUNK_COLS, CHUNK_COLS), grp(8, 2 * LANES)],
        out_specs=tok(N_CHUNKS),
        out_shape=jax.ShapeDtypeStruct((SSM_GROUPS, BATCH, N_CHUNKS, CHUNK_COLS), BF16),
        scratch_shapes=[pltpu.VMEM((gb, 2, rows, LANES), F32),
                        pltpu.VMEM((gb, 2, crows, LANES), F32),
                        pltpu.VMEM((gb, 2, rows, LANES), F32)],
        compiler_params=pltpu.CompilerParams(
            dimension_semantics=("arbitrary",), vmem_limit_bytes=VMEM_LIMIT),
        name="s5_scan",
    )(u_lat_t, u_ctx_t, m_state, m_intra, m_out, a16)


FFN_TILE = 256


def _post_kernel(x_ref, a_ref, yt_ref, u_ref, ga_ref, gs_ref, mod_ref, d_ref, fg_ref, og_ref, permt_ref,
                 wglu_ref, wba_ref, wbs_ref, wout_ref, wfi_ref, wfo_ref, o_ref, h1_ref, acc_ref, r_ref):
    dm = D_MODEL
    g1 = mod_ref[0, :, 2 * dm:3 * dm]
    sh2 = mod_ref[0, :, 3 * dm:4 * dm]
    sc2 = mod_ref[0, :, 4 * dm:5 * dm]
    g2 = mod_ref[0, :, 5 * dm:6 * dm]
    u = u_ref[0].astype(F32)
    y = _load_chunk_layout(yt_ref, permt_ref, r_ref, u.shape[0] // (CHUNK * CHUNK))
    sp = jax.nn.gelu(y + d_ref[...] * u).astype(BF16)
    vg = jnp.dot(sp, wglu_ref[...], preferred_element_type=F32)
    s = (vg[:, :SSM_WIDTH] * jax.nn.sigmoid(vg[:, SSM_WIDTH:])).astype(BF16)
    merged = (ga_ref[0].astype(F32) * jnp.dot(a_ref[0], wba_ref[...], preferred_element_type=F32)
              + gs_ref[0].astype(F32) * jnp.dot(s, wbs_ref[...], preferred_element_type=F32))
    mix = jnp.dot(merged.astype(BF16), wout_ref[...], preferred_element_type=F32)
    h1 = x_ref[0] + g1 * mix
    h1_ref[...] = h1
    n2 = _rms_modulate(h1, fg_ref[...], sh2, sc2).astype(BF16)
    for c in range(FFN_HIDDEN // FFN_TILE):
        lo = c * FFN_TILE
        fa = jnp.dot(n2, wfi_ref[:, lo:lo + FFN_TILE], preferred_element_type=F32)
        fb = jnp.dot(n2, wfi_ref[:, FFN_HIDDEN + lo:FFN_HIDDEN + lo + FFN_TILE],
                     preferred_element_type=F32)
        act = (fa * jax.nn.sigmoid(fa) * fb).astype(BF16)
        part = jnp.dot(act, wfo_ref[lo:lo + FFN_TILE, :], preferred_element_type=F32)
        if c == 0:
            acc_ref[...] = part
        else:
            acc_ref[...] += part
    h2 = h1_ref[...] + g2 * acc_ref[...]
    o_ref[0] = (h2 * lax.rsqrt(jnp.mean(h2 * h2, axis=-1, keepdims=True) + NORM_EPS)) * og_ref[...]


def _post(x, a, y_t, u, ga, gs, mod3, d_skip, ffn_g, fin_g, perm_t, wglu, wba, wbs, wout, wfi, wfo):
    tm = 512
    n = CHUNK * CHUNK
    tok = lambda width: pl.BlockSpec((1, tm, width), lambda b, i: (b, i, 0))
    const = lambda r, c: pl.BlockSpec((r, c), lambda b, i: (0, 0), pipeline_mode=pl.Buffered(1))
    return pl.pallas_call(
        _post_kernel,
        grid=(BATCH, SEQ // tm),
        in_specs=[tok(D_MODEL), tok(ATTN_WIDTH),
                  pl.BlockSpec((SSM_GROUPS, 1, tm // CHUNK, CHUNK_COLS), lambda b, i: (0, b, i, 0)),
                  tok(SSM_WIDTH), tok(D_MODEL), tok(D_MODEL),
                  pl.BlockSpec((1, 1, N_MOD * D_MODEL), lambda b, i: (b, 0, 0)),
                  const(1, SSM_WIDTH), const(1, D_MODEL), const(1, D_MODEL), const(n, n),
                  const(SSM_WIDTH, 2 * SSM_WIDTH), const(ATTN_WIDTH, D_MODEL),
                  const(SSM_WIDTH, D_MODEL), const(D_MODEL, D_MODEL),
                  const(D_MODEL, 2 * FFN_HIDDEN), const(FFN_HIDDEN, D_MODEL)],
        out_specs=tok(D_MODEL),
        out_shape=jax.ShapeDtypeStruct((BATCH, SEQ, D_MODEL), F32),
        scratch_shapes=[pltpu.VMEM((tm, D_MODEL), F32), pltpu.VMEM((tm, D_MODEL), F32),
                        pltpu.VMEM((n, SSM_WIDTH), F32)],
        compiler_params=pltpu.CompilerParams(
            dimension_semantics=("arbitrary", "arbitrary"), vmem_limit_bytes=VMEM_LIMIT),
        name="post",
    )(x, a, y_t, u, ga, gs, mod3, d_skip, ffn_g, fin_g, perm_t, wglu, wba, wbs, wout, wfi, wfo)


def kernel(x, c, ctx, c_ctx, w_mod, b_mod, attn_norm_g, ffn_norm_g, w_in, rel_pos_bias,
           ssm_lambda_re, ssm_lambda_im, ssm_log_dt, ssm_b_re, ssm_b_im, ssm_c_re, ssm_c_im, ssm_d,
           w_glu, w_branch_attn, w_branch_ssm, w_out, w_ffn_in, w_ffn_out, final_norm_g):
    assert x.shape == (BATCH, SEQ, D_MODEL) and w_mod.shape[0] == 1
    c_rows = jnp.concatenate(
        [c, c_ctx[None, :], jnp.zeros((MOD_ROWS - BATCH - 1, D_MODEL), F32)], axis=0)
    mod3 = _modulation(c_rows, w_mod[0], b_mod[0]).reshape(MOD_ROWS, 1, N_MOD * D_MODEL)

    col_scale = jnp.concatenate([jnp.full((ATTN_WIDTH,), HEAD_DIM ** -0.5 * LOG2E, F32),
                                 jnp.ones((IN_COLS - ATTN_WIDTH,), F32)])
    w_in_bf16 = (w_in[0] * col_scale[None, :]).astype(BF16)
    norm_g = attn_norm_g[0].reshape(1, D_MODEL)

    perm = _chunk_perm()
    q, k, v, u, u_t, ga, gs = _input_projection(x, mod3, norm_g, w_in_bf16, _rope_tables(), perm)
    kc, vc, uc_t = _context_projection(ctx, mod3, norm_g, w_in_bf16, perm)

    attn = _attention(q, k, v, kc, vc, _bias_tables(rel_pos_bias[0]))

    m_intra, m_state, m_out, a16 = _s5_matrices(
        ssm_lambda_re[0], ssm_lambda_im[0], ssm_log_dt[0], ssm_b_re[0], ssm_b_im[0],
        ssm_c_re[0], ssm_c_im[0])
    y_t = _s5_scan(u_t, uc_t, m_intra, m_state, m_out, a16)

    return _post(x, attn, y_t, u, ga, gs, mod3,
                 ssm_d[0].reshape(1, SSM_WIDTH), ffn_norm_g[0].reshape(1, D_MODEL),
                 final_norm_g.reshape(1, D_MODEL), perm,
                 w_glu[0].astype(BF16), w_branch_attn[0].astype(BF16), w_branch_ssm[0].astype(BF16),
                 w_out[0].astype(BF16), w_ffn_in[0].astype(BF16), w_ffn_out[0].astype(BF16))
```

```python
import functools
import math

import numpy as np
import jax
import jax.numpy as jnp
from jax import lax
from jax.experimental import pallas as pl
from jax.experimental.pallas import tpu as pltpu

F32 = jnp.float32
BF16 = jnp.bfloat16

D_MODEL = 1024
BATCH = 8
SEQ = 4096
GRID_W = 64
GRID_ROWS = SEQ // GRID_W
CTX_LEN = 256
N_HEADS = 8
HEAD_DIM = 64
ATTN_WIDTH = N_HEADS * HEAD_DIM
WIN_H = 8
WIN_W = 16
ROPE_BASE = 10000.0
SSM_WIDTH = 512
SSM_GROUP = 16
SSM_GROUPS = SSM_WIDTH // SSM_GROUP
SSM_STATE = 64
FFN_HIDDEN = 2816
IN_COLS = 3 * ATTN_WIDTH + SSM_WIDTH + 2 * D_MODEL
N_MOD = 6
NORM_EPS = 1e-6
NEG_BIG = -1e30
LOG2E = math.log2(math.e)

LANES = 128
CHUNK = 16
N_CHUNKS = SEQ // CHUNK
N_CTX_CHUNKS = CTX_LEN // CHUNK
CHUNK_COLS = CHUNK * SSM_GROUP
MOD_ROWS = 16
CTX_MOD_ROW = BATCH
VMEM_LIMIT = 56 * 1024 * 1024

Q_ROWS_PER_STEP = 4
Q_BLOCK = Q_ROWS_PER_STEP * GRID_W
K_ROWS_PER_STEP = Q_ROWS_PER_STEP + WIN_H
K_BLOCK = K_ROWS_PER_STEP * GRID_W
N_QBLOCKS = GRID_ROWS // Q_ROWS_PER_STEP
KEY_GROUP = Q_BLOCK
SOFTMAX_ROWS = 32


def _rms_modulate(x, g, shift, scale):
    xn = x * lax.rsqrt(jnp.mean(x * x, axis=-1, keepdims=True) + NORM_EPS)
    return (xn * g) * (1.0 + scale) + shift


def _block_transpose8(vs, lane):
    for shift in (64, 32, 16):
        keep = (lane & (2 * shift - 1)) < shift
        dist = shift // SSM_GROUP
        out = list(vs)
        for a in range(8):
            if a & dist:
                continue
            b = a + dist
            out[a] = jnp.where(keep, vs[a], pltpu.roll(vs[b], shift, 1))
            out[b] = jnp.where(keep, pltpu.roll(vs[a], LANES - shift, 1), vs[b])
        vs = out
    return vs


def _chunk_perm():
    n = CHUNK * CHUNK
    r = np.arange(n)
    m = np.zeros((n, n), np.float32)
    m[r, (r % CHUNK) * CHUNK + r // CHUNK] = 1.0
    return jnp.asarray(m, BF16)


def _store_chunk_layout(u16, perm_ref, out_ref, n_groups16):
    lane = lax.broadcasted_iota(jnp.int32, (1, LANES), 1)
    n = CHUNK * CHUNK
    for hf in range(n_groups16):
        r = jnp.dot(perm_ref[...], u16[hf * n:(hf + 1) * n, :], preferred_element_type=F32)
        for v in range(SSM_WIDTH // LANES):
            for jh in range(2):
                vs = [r[CHUNK * (8 * jh + jp):CHUNK * (8 * jh + jp + 1), v * LANES:(v + 1) * LANES]
                      for jp in range(8)]
                outs = _block_transpose8(vs, lane)
                for gi in range(8):
                    out_ref[8 * v + gi, 0, hf * CHUNK:(hf + 1) * CHUNK, jh * LANES:(jh + 1) * LANES] = (
                        outs[gi].astype(BF16))


def _load_chunk_layout(yt_ref, perm_t_ref, r_ref, n_groups16):
    lane = lax.broadcasted_iota(jnp.int32, (1, LANES), 1)
    parts = []
    for hf in range(n_groups16):
        for v in range(SSM_WIDTH // LANES):
            for jh in range(2):
                vs = [yt_ref[8 * v + gi, 0, hf * CHUNK:(hf + 1) * CHUNK,
                             jh * LANES:(jh + 1) * LANES].astype(F32) for gi in range(8)]
                outs = _block_transpose8(vs, lane)
                for jp in range(8):
                    r_ref[CHUNK * (8 * jh + jp):CHUNK * (8 * jh + jp + 1), v * LANES:(v + 1) * LANES] = outs[jp]
        parts.append(jnp.dot(perm_t_ref[...], r_ref[...].astype(BF16), preferred_element_type=F32))
    return jnp.concatenate(parts, axis=0)


def _mod_kernel(c_ref, w_ref, b_ref, o_ref):
    c = c_ref[...]
    s = c * jax.nn.sigmoid(c)
    o_ref[...] = jnp.dot(s, w_ref[...], preferred_element_type=F32) + b_ref[...]


def _modulation(c_rows, w_mod, b_mod):
    n = N_MOD * D_MODEL
    tn = 1536
    return pl.pallas_call(
        _mod_kernel,
        grid=(n // tn,),
        in_specs=[pl.BlockSpec((MOD_ROWS, D_MODEL), lambda j: (0, 0)),
                  pl.BlockSpec((D_MODEL, tn), lambda j: (0, j)),
                  pl.BlockSpec((1, tn), lambda j: (0, j))],
        out_specs=pl.BlockSpec((MOD_ROWS, tn), lambda j: (0, j)),
        out_shape=jax.ShapeDtypeStruct((MOD_ROWS, n), F32),
        name="modulation",
    )(c_rows, w_mod, b_mod.reshape(1, n))


def _rope_tables():
    n_freq = HEAD_DIM // 4
    inv_freq = ROPE_BASE ** (-np.arange(n_freq, dtype=np.float64) / n_freq)
    t = np.arange(SEQ)
    lane = np.arange(LANES)
    d = lane % HEAD_DIM
    use_col = (d // (HEAD_DIM // 2)) == 1
    w = d % (HEAD_DIM // 2)
    first = w < n_freq
    pos = np.where(use_col[None, :], (t % GRID_W)[:, None], (t // GRID_W)[:, None]).astype(np.float64)
    ang = pos * inv_freq[w % n_freq][None, :]
    cos = np.cos(ang)
    sin = np.sin(ang)
    sin_a = np.where(first[None, :], -sin, 0.0)
    sin_b = np.where(first[None, :], 0.0, sin)
    return (jnp.asarray(cos, F32), jnp.asarray(sin_a, F32), jnp.asarray(sin_b, F32))


def _rope_store(r, cos, sin_a, sin_b, out_ref, transposed=False):
    for j in range(ATTN_WIDTH // LANES):
        xs = r[:, j * LANES:(j + 1) * LANES]
        rot = (xs * cos + pltpu.roll(xs, LANES - HEAD_DIM // 4, 1) * sin_a
               + pltpu.roll(xs, HEAD_DIM // 4, 1) * sin_b)
        if transposed:
            rot_t = rot.T.astype(BF16)
            for gi in range(r.shape[0] // KEY_GROUP):
                out_ref[0, gi, j * LANES:(j + 1) * LANES, :] = rot_t[:, gi * KEY_GROUP:(gi + 1) * KEY_GROUP]
        else:
            out_ref[0, :, j * LANES:(j + 1) * LANES] = rot.astype(BF16)


def _inproj_kernel(x_ref, mod_ref, g_ref, w_ref, cos_ref, sa_ref, sb_ref, perm_ref,
                   q_ref, k_ref, v_ref, u_ref, ut_ref, ga_ref, gs_ref):
    x = x_ref[0]
    shift = mod_ref[0, :, 0:D_MODEL]
    scale = mod_ref[0, :, D_MODEL:2 * D_MODEL]
    nb = _rms_modulate(x, g_ref[...], shift, scale).astype(BF16)
    cos = cos_ref[...]
    sin_a = sa_ref[...]
    sin_b = sb_ref[...]
    aw = ATTN_WIDTH
    q = jnp.dot(nb, w_ref[:, 0:aw], preferred_element_type=F32)
    _rope_store(q, cos, sin_a, sin_b, q_ref)
    k = jnp.dot(nb, w_ref[:, aw:2 * aw], preferred_element_type=F32)
    _rope_store(k, cos, sin_a, sin_b, k_ref, transposed=True)
    v_ref[0] = jnp.dot(nb, w_ref[:, 2 * aw:3 * aw], preferred_element_type=F32).astype(BF16)
    c0 = 3 * aw
    u16 = jnp.dot(nb, w_ref[:, c0:c0 + SSM_WIDTH], preferred_element_type=F32).astype(BF16)
    u_ref[0] = u16
    _store_chunk_layout(u16, perm_ref, ut_ref, x.shape[0] // (CHUNK * CHUNK))
    c1 = c0 + SSM_WIDTH
    ga = jnp.dot(nb, w_ref[:, c1:c1 + D_MODEL], preferred_element_type=F32)
    ga_ref[0] = jax.nn.sigmoid(ga).astype(BF16)
    c2 = c1 + D_MODEL
    gs = jnp.dot(nb, w_ref[:, c2:c2 + D_MODEL], preferred_element_type=F32)
    gs_ref[0] = jax.nn.sigmoid(gs).astype(BF16)


def _input_projection(x, mod3, norm_g, w_in_bf16, rope, perm):
    tm = 512
    n = CHUNK * CHUNK
    cos, sin_a, sin_b = rope
    tok = lambda width: pl.BlockSpec((1, tm, width), lambda i, b: (b, i, 0))
    tab = pl.BlockSpec((tm, LANES), lambda i, b: (i, 0))
    out = lambda width: jax.ShapeDtypeStruct((BATCH, SEQ, width), BF16)
    return pl.pallas_call(
        _inproj_kernel,
        grid=(SEQ // tm, BATCH),
        in_specs=[tok(D_MODEL),
                  pl.BlockSpec((1, 1, N_MOD * D_MODEL), lambda i, b: (b, 0, 0)),
                  pl.BlockSpec((1, D_MODEL), lambda i, b: (0, 0)),
                  pl.BlockSpec((D_MODEL, IN_COLS), lambda i, b: (0, 0)),
                  tab, tab, tab,
                  pl.BlockSpec((n, n), lambda i, b: (0, 0))],
        out_specs=[tok(ATTN_WIDTH),
                   pl.BlockSpec((1, tm // KEY_GROUP, ATTN_WIDTH, KEY_GROUP), lambda i, b: (b, i, 0, 0)),
                   tok(ATTN_WIDTH), tok(SSM_WIDTH),
                   pl.BlockSpec((SSM_GROUPS, 1, tm // CHUNK, CHUNK_COLS), lambda i, b: (0, b, i, 0)),
                   tok(D_MODEL), tok(D_MODEL)],
        out_shape=[out(ATTN_WIDTH),
                   jax.ShapeDtypeStruct((BATCH, SEQ // KEY_GROUP, ATTN_WIDTH, KEY_GROUP), BF16),
                   out(ATTN_WIDTH), out(SSM_WIDTH),
                   jax.ShapeDtypeStruct((SSM_GROUPS, BATCH, N_CHUNKS, CHUNK_COLS), BF16),
                   out(D_MODEL), out(D_MODEL)],
        compiler_params=pltpu.CompilerParams(
            dimension_semantics=("arbitrary", "arbitrary"), vmem_limit_bytes=VMEM_LIMIT),
        name="input_projection",
    )(x, mod3, norm_g, w_in_bf16, cos, sin_a, sin_b, perm)


def _ctx_proj_kernel(x_ref, mod_ref, g_ref, wk_ref, wv_ref, wu_ref, perm_ref, k_ref, v_ref, ut_ref):
    x = x_ref[0]
    shift = mod_ref[0, :, 0:D_MODEL]
    scale = mod_ref[0, :, D_MODEL:2 * D_MODEL]
    nb = _rms_modulate(x, g_ref[...], shift, scale).astype(BF16)
    k_ref[0] = jnp.dot(nb, wk_ref[...], preferred_element_type=F32).T.astype(BF16)
    v_ref[0] = jnp.dot(nb, wv_ref[...], preferred_element_type=F32).astype(BF16)
    u16 = jnp.dot(nb, wu_ref[...], preferred_element_type=F32).astype(BF16)
    _store_chunk_layout(u16, perm_ref, ut_ref, 1)


def _context_projection(ctx, mod3, norm_g, w_in_bf16, perm):
    n = CHUNK * CHUNK
    aw = ATTN_WIDTH
    tok = lambda width: pl.BlockSpec((1, CTX_LEN, width), lambda b: (b, 0, 0))
    wcol = lambda j: pl.BlockSpec((D_MODEL, aw), lambda b: (0, j))
    out = jax.ShapeDtypeStruct((BATCH, CTX_LEN, aw), BF16)
    return pl.pallas_call(
        _ctx_proj_kernel,
        grid=(BATCH,),
        in_specs=[tok(D_MODEL),
                  pl.BlockSpec((1, 1, N_MOD * D_MODEL), lambda b: (CTX_MOD_ROW, 0, 0)),
                  pl.BlockSpec((1, D_MODEL), lambda b: (0, 0)),
                  wcol(1), wcol(2), wcol(3),
                  pl.BlockSpec((n, n), lambda b: (0, 0))],
        out_specs=[pl.BlockSpec((1, aw, CTX_LEN), lambda b: (b, 0, 0)), tok(aw),
                   pl.BlockSpec((SSM_GROUPS, 1, N_CTX_CHUNKS, CHUNK_COLS), lambda b: (0, b, 0, 0))],
        out_shape=[jax.ShapeDtypeStruct((BATCH, aw, CTX_LEN), BF16), out,
                   jax.ShapeDtypeStruct((SSM_GROUPS, BATCH, N_CTX_CHUNKS, CHUNK_COLS), BF16)],
        name="context_projection",
    )(ctx, mod3, norm_g, w_in_bf16, w_in_bf16, w_in_bf16, perm)


def _window_start_rows(r):
    return min(max(r - WIN_H // 2, 0), GRID_ROWS - WIN_H)


def _key_block_row(jb):
    return min(max(Q_ROWS_PER_STEP * jb - WIN_H // 2, 0), GRID_ROWS - K_ROWS_PER_STEP)


def _window_geometry(jb):
    key_row0 = _key_block_row(jb)
    offs, deltas = [], []
    for i in range(Q_ROWS_PER_STEP):
        r = Q_ROWS_PER_STEP * jb + i
        rs = _window_start_rows(r)
        offs.append(rs - key_row0)
        deltas.append(r - rs)
    return offs, deltas


def _attn_kernel(q_ref, k_ref, v_ref, kc_ref, vc_ref, bias_ref, o_ref,
                 s_ref, sc_ref, p_ref, pc_ref, l_ref, acc_ref):
    lane = lax.broadcasted_iota(jnp.int32, (1, LANES), 1)
    left = lane < HEAD_DIM
    n_ktiles = K_BLOCK // LANES
    last = N_QBLOCKS - 1
    max_key_start = (GRID_ROWS - K_ROWS_PER_STEP) * GRID_W

    def scores(e, q_start, key_start):
        q2 = q_ref[0, pl.ds(q_start, Q_BLOCK), :]
        qm = jnp.where(left if e == 0 else jnp.logical_not(left), q2, jnp.zeros_like(q2))
        group0 = key_start // KEY_GROUP
        for gi in range(K_BLOCK // KEY_GROUP):
            s_ref[e, :, gi * KEY_GROUP:(gi + 1) * KEY_GROUP] = jnp.dot(
                qm, k_ref[0, group0 + gi], preferred_element_type=F32)
        sc_ref[e] = jnp.dot(qm, kc_ref[0], preferred_element_type=F32)

    def softmax(e, offs, deltas):
        for sub in range(Q_BLOCK // SOFTMAX_ROWS):
            i = sub * SOFTMAX_ROWS // GRID_W
            qcols = slice(sub * SOFTMAX_ROWS % GRID_W, sub * SOFTMAX_ROWS % GRID_W + SOFTMAX_ROWS)
            rows = slice(sub * SOFTMAX_ROWS, (sub + 1) * SOFTMAX_ROWS)
            par = offs[i] % 2
            t0 = offs[i] // 2
            n_tiles = WIN_H // 2 + par
            tiles = []
            for xt in range(n_tiles):
                t = t0 + xt
                dr0 = 2 * xt - par - deltas[i]
                st = s_ref[e, rows, t * LANES:(t + 1) * LANES] + bias_ref[0, e, dr0 + WIN_H, qcols, :]
                if par and xt == 0:
                    st = jnp.where(left, NEG_BIG, st)
                if par and xt == n_tiles - 1:
                    st = jnp.where(left, st, NEG_BIG)
                tiles.append(st)
            c_tiles = [sc_ref[e, rows, 0:LANES], sc_ref[e, rows, LANES:2 * LANES]]
            mt = c_tiles[0]
            for st in tiles + c_tiles[1:]:
                mt = jnp.maximum(mt, st)
            m = jnp.max(mt, axis=1, keepdims=True)
            lt = None
            for xt in range(n_tiles):
                t = t0 + xt
                pt = jnp.exp2(tiles[xt] - m)
                lt = pt if lt is None else lt + pt
                p_ref[e, rows, t * LANES:(t + 1) * LANES] = pt.astype(BF16)
            for t in range(n_ktiles):
                if not (t0 <= t < t0 + n_tiles):
                    p_ref[e, rows, t * LANES:(t + 1) * LANES] = jnp.zeros((SOFTMAX_ROWS, LANES), BF16)
            for ci in range(2):
                pt = jnp.exp2(c_tiles[ci] - m)
                lt = lt + pt
                pc_ref[e, rows, ci * LANES:(ci + 1) * LANES] = pt.astype(BF16)
            l = jnp.sum(lt, axis=1, keepdims=True)
            l_ref[e, rows, :] = jnp.broadcast_to(1.0 / l, (SOFTMAX_ROWS, LANES))

    def values(e, key_start):
        vblk = v_ref[0, pl.ds(key_start, K_BLOCK), :]
        return (jnp.dot(p_ref[e], vblk, preferred_element_type=F32)
                + jnp.dot(pc_ref[e], vc_ref[0], preferred_element_type=F32)) * l_ref[e]

    def block(jb_static, q_start, key_start, next_starts):
        offs, deltas = _window_geometry(jb_static)
        scores(1, q_start, key_start)
        softmax(0, offs, deltas)
        acc_ref[...] = values(0, key_start)
        if next_starts is not None:
            scores(0, *next_starts)
        softmax(1, offs, deltas)
        o1 = values(1, key_start)
        o_ref[0, pl.ds(q_start, Q_BLOCK), :] = jnp.where(left, acc_ref[...], o1).astype(BF16)

    def starts(jb):
        q_start = pl.multiple_of(jb * Q_BLOCK, Q_BLOCK)
        key_start = jnp.clip((jb * Q_ROWS_PER_STEP - WIN_H // 2) * GRID_W, 0, max_key_start)
        return q_start, pl.multiple_of(key_start, Q_BLOCK)

    scores(0, 0, 0)
    block(0, 0, 0, (Q_BLOCK, 0))

    def interior(jb, carry):
        q_start, key_start = starts(jb)
        block(1, q_start, key_start, starts(jb + 1))
        return carry

    lax.fori_loop(1, last, interior, 0)
    block(last, last * Q_BLOCK, max_key_start, None)


def _bias_tables(rpb):
    qcol = np.arange(GRID_W)
    kcol = np.arange(GRID_W)
    col_start = np.clip(qcol - WIN_W // 2, 0, GRID_W - WIN_W)
    in_win = (kcol[None, :] >= col_start[:, None]) & (kcol[None, :] < col_start[:, None] + WIN_W)
    dc_idx = np.clip(kcol[None, :] - qcol[:, None], -(WIN_W - 1), WIN_W - 1) + WIN_W - 1
    sel = (np.arange(2 * WIN_W - 1)[:, None, None] == dc_idx[None]).astype(np.float32)
    toe = jnp.einsum('hdt,tck->hdck', rpb * LOG2E, jnp.asarray(sel),
                     precision=lax.Precision.HIGHEST)
    toe = jnp.where(in_win[None, None], toe, NEG_BIG)
    neg = jnp.full((N_HEADS, 1, GRID_W, GRID_W), NEG_BIG, F32)
    ext = jnp.concatenate([neg, toe, neg], axis=1)
    pair = jnp.concatenate([ext[:, 0:16], ext[:, 1:17]], axis=-1)
    return pair.reshape(N_HEADS // 2, 2, 16, GRID_W, LANES)


def _attention(q, k, v, kc, vc, bias):
    kspec = pl.BlockSpec((1, SEQ, LANES), lambda hp, b: (b, 0, hp))
    ktspec = pl.BlockSpec((1, SEQ // KEY_GROUP, LANES, KEY_GROUP), lambda hp, b: (b, 0, hp, 0))
    cspec = pl.BlockSpec((1, CTX_LEN, LANES), lambda hp, b: (b, 0, hp))
    ctspec = pl.BlockSpec((1, LANES, CTX_LEN), lambda hp, b: (b, hp, 0))
    bspec = pl.BlockSpec((1, 2, 16, GRID_W, LANES), lambda hp, b: (hp, 0, 0, 0, 0))
    return pl.pallas_call(
        _attn_kernel,
        grid=(N_HEADS // 2, BATCH),
        in_specs=[kspec, ktspec, kspec, ctspec, cspec, bspec],
        out_specs=kspec,
        out_shape=jax.ShapeDtypeStruct((BATCH, SEQ, ATTN_WIDTH), BF16),
        scratch_shapes=[pltpu.VMEM((2, Q_BLOCK, K_BLOCK), F32),
                        pltpu.VMEM((2, Q_BLOCK, CTX_LEN), F32),
                        pltpu.VMEM((2, Q_BLOCK, K_BLOCK), BF16),
                        pltpu.VMEM((2, Q_BLOCK, CTX_LEN), BF16),
                        pltpu.VMEM((2, Q_BLOCK, LANES), F32),
                        pltpu.VMEM((Q_BLOCK, LANES), F32)],
        compiler_params=pltpu.CompilerParams(
            dimension_semantics=("arbitrary", "arbitrary"),
            vmem_limit_bytes=VMEM_LIMIT),
        name="attention",
    )(q, k, v, kc, vc, bias)


def _rot256(a, b, s, lane):
    s %= 2 * LANES
    if s >= LANES:
        a, b, s = b, a, s - LANES
    if s == 0:
        return a, b
    ra = pltpu.roll(a, s, 1)
    rb = pltpu.roll(b, s, 1)
    keep = lane >= s
    return jnp.where(keep, ra, rb), jnp.where(keep, rb, ra)


def _s5_prep_kernel(par_ref, b_ref, c_ref, t16_ref, e_ref, mi_ref, ms_ref, mo_ref, a_ref):
    hi = lax.Precision.HIGHEST
    dot = lambda x, y: jnp.dot(x, y, preferred_element_type=F32, precision=hi)
    ns = SSM_STATE
    par = par_ref[0]
    lam_re, lam_im = par[:, 0:1], par[:, 1:2]
    dt = jnp.exp(par[:, 2:3])
    lane = lax.broadcasted_iota(jnp.int32, (1, LANES), 1)
    kf = lane.astype(F32)
    mag = jnp.exp((lam_re * dt) * kf)
    ang = (lam_im * dt) * kf
    pw_re, pw_im = mag * jnp.cos(ang), mag * jnp.sin(ang)
    lb_re = jnp.sum(jnp.where(lane == 1, pw_re, 0.0), axis=1, keepdims=True)
    lb_im = jnp.sum(jnp.where(lane == 1, pw_im, 0.0), axis=1, keepdims=True)
    den = lam_re * lam_re + lam_im * lam_im
    nr, ni = lb_re - 1.0, lb_im
    f_re = (nr * lam_re + ni * lam_im) / den
    f_im = (ni * lam_re - nr * lam_im) / den
    b_re, b_im = b_ref[0, :, 0:SSM_GROUP], b_ref[0, :, SSM_GROUP:2 * SSM_GROUP]
    bb_re = f_re * b_re - f_im * b_im
    bb_im = f_re * b_im + f_im * b_re
    t16 = t16_ref[...]
    bbt_re, bbt_im = dot(bb_re, t16), dot(bb_im, t16)
    ct_re = dot(c_ref[0, :, 0:SSM_GROUP], t16)
    ct_im = dot(c_ref[0, :, SSM_GROUP:2 * SSM_GROUP], t16)
    pw_at = lambda x: (dot(pw_re, e_ref[x]), dot(pw_im, e_ref[x]))
    id_re, id_im = pw_at(0)
    rev_re, rev_im = pw_at(1)
    p1_re, p1_im = pw_at(2)
    r16_re, r16_im = pw_at(3)
    f, b = slice(0, ns), slice(ns, 2 * ns)
    cmul = lambda ar, ai, br, bi: (ar * br - ai * bi, ar * bi + ai * br)

    sf_re, sf_im = cmul(rev_re[f], rev_im[f], bbt_re[f], bbt_im[f])
    sb_re, sb_im = cmul(id_re[b], id_im[b], bbt_re[b], bbt_im[b])
    ms_ref[0] = jnp.concatenate([sf_re, sb_re, sf_im, sb_im], axis=0).T.astype(BF16)

    of_re, of_im = cmul(p1_re[f], p1_im[f], ct_re[f], ct_im[f])
    ob_re, ob_im = cmul(r16_re[b], r16_im[b], ct_re[b], ct_im[b])
    mo_ref[0] = jnp.concatenate([of_re, ob_re, -of_im, -ob_im], axis=0).astype(BF16)

    xf_re, xf_im = cmul(id_re[f], id_im[f], ct_re[f], ct_im[f])
    xb_re, xb_im = cmul(rev_re[b], rev_im[b], ct_re[b], ct_im[b])
    btf = jnp.concatenate([bbt_re[f], bbt_im[f]], axis=0).T
    btb = jnp.concatenate([bbt_re[b], bbt_im[b]], axis=0).T
    g_f = dot(btf, jnp.concatenate([xf_re, -xf_im], axis=0))
    g_b = dot(btb, jnp.concatenate([xb_re, -xb_im], axis=0))
    for j in range(CHUNK):
        rows = slice(SSM_GROUP * j, SSM_GROUP * (j + 1))
        lo_col, hi_col = SSM_GROUP * j, SSM_GROUP * (j + 1)
        f_lo, f_hi = _rot256(g_f[rows, :LANES], g_f[rows, LANES:], lo_col, lane)
        b_lo, b_hi = _rot256(g_b[rows, :LANES], g_b[rows, LANES:], -SSM_GROUP * (CHUNK - 1 - j), lane)
        lo = jnp.where(lane >= lo_col, f_lo, 0.0) + jnp.where(lane < hi_col, b_lo, 0.0)
        up = jnp.where(lane + LANES >= lo_col, f_hi, 0.0) + jnp.where(lane + LANES < hi_col, b_hi, 0.0)
        mi_ref[0, rows, 0:LANES] = lo.astype(BF16)
        mi_ref[0, rows, LANES:2 * LANES] = up.astype(BF16)

    a16 = jnp.concatenate([pw_re.T[CHUNK:CHUNK + 1, :], pw_im.T[CHUNK:CHUNK + 1, :]], axis=1)
    a_ref[0] = jnp.broadcast_to(a16, (8, 2 * LANES))


def _s5_matrices(lam_re, lam_im, log_dt, b_re, b_im, c_re, c_im):
    g, p2 = SSM_GROUPS, 2 * SSM_STATE
    both = lambda a: jnp.transpose(a, (1, 0, 2)).reshape(g, p2)
    log_dt_rows = jnp.repeat(jnp.transpose(log_dt), SSM_STATE, axis=1)
    par = jnp.stack([both(lam_re), both(lam_im), log_dt_rows] + [jnp.zeros((g, p2), F32)] * 5, axis=-1)
    rows_b = lambda a: jnp.transpose(a, (1, 0, 2, 3)).reshape(g, p2, SSM_GROUP)
    rows_c = lambda a: jnp.transpose(a, (1, 0, 3, 2)).reshape(g, p2, SSM_GROUP)
    b_cat = jnp.concatenate([rows_b(b_re), rows_b(b_im)], axis=-1)
    c_cat = jnp.concatenate([rows_c(c_re), rows_c(c_im)], axis=-1)

    col = np.arange(CHUNK_COLS)
    tile16 = (col[None, :] % SSM_GROUP == np.arange(SSM_GROUP)[:, None]).astype(np.float32)
    pos = col // SSM_GROUP
    k_idx = np.arange(LANES)[:, None]
    expand = np.stack([k_idx == pos[None, :], k_idx == (CHUNK - 1 - pos)[None, :],
                       k_idx == (pos + 1)[None, :], k_idx == (CHUNK - pos)[None, :]]).astype(np.float32)

    mat = lambda: pl.BlockSpec((1, CHUNK_COLS, CHUNK_COLS), lambda i: (i, 0, 0))
    mat_shape = jax.ShapeDtypeStruct((g, CHUNK_COLS, CHUNK_COLS), BF16)
    return pl.pallas_call(
        _s5_prep_kernel,
        grid=(g,),
        in_specs=[pl.BlockSpec((1, p2, 8), lambda i: (i, 0, 0)),
                  pl.BlockSpec((1, p2, 2 * SSM_GROUP), lambda i: (i, 0, 0)),
                  pl.BlockSpec((1, p2, 2 * SSM_GROUP), lambda i: (i, 0, 0)),
                  pl.BlockSpec((SSM_GROUP, CHUNK_COLS), lambda i: (0, 0)),
                  pl.BlockSpec((4, LANES, CHUNK_COLS), lambda i: (0, 0, 0))],
        out_specs=[mat(), mat(), mat(), pl.BlockSpec((1, 8, 2 * LANES), lambda i: (i, 0, 0))],
        out_shape=[mat_shape, mat_shape, mat_shape, jax.ShapeDtypeStruct((g, 8, 2 * LANES), F32)],
        name="s5_prep",
    )(par, b_cat, c_cat, jnp.asarray(tile16), jnp.asarray(expand))


def _s5_kernel(ul_ref, uc_ref, ms_ref, mi_ref, mo_ref, a_ref, y_ref, s_ref, sc_ref, hp_ref, *, gb):
    for gi in range(gb):
        for b in range(BATCH):
            sb = jnp.dot(ul_ref[gi, b], ms_ref[gi], preferred_element_type=F32)
            s_ref[gi, 0, pl.ds(b, N_CHUNKS, stride=BATCH), :] = sb[:, :LANES]
            s_ref[gi, 1, pl.ds(b, N_CHUNKS, stride=BATCH), :] = sb[:, LANES:]
            cb = jnp.dot(uc_ref[gi, b], ms_ref[gi], preferred_element_type=F32)
            sc_ref[gi, 0, pl.ds(b, N_CTX_CHUNKS, stride=BATCH), :] = cb[:, :LANES]
            sc_ref[gi, 1, pl.ds(b, N_CTX_CHUNKS, stride=BATCH), :] = cb[:, LANES:]
    lane = lax.broadcasted_iota(jnp.int32, (BATCH, LANES), 1)
    fwd = lane < SSM_STATE
    half = SSM_STATE

    def advance(gi, h_re, h_im, row_f, row_b, src):
        s_re = jnp.where(fwd, src[gi, 0, pl.ds(row_f, BATCH), :], src[gi, 0, pl.ds(row_b, BATCH), :])
        s_im = jnp.where(fwd, src[gi, 1, pl.ds(row_f, BATCH), :], src[gi, 1, pl.ds(row_b, BATCH), :])
        a_re = a_ref[gi, :, 0:LANES]
        a_im = a_ref[gi, :, LANES:2 * LANES]
        n_re = a_re * h_re - a_im * h_im + s_re
        n_im = a_re * h_im + a_im * h_re + s_im
        return n_re, n_im

    def ctx_step(t, carry):
        row_f = pl.multiple_of(t * BATCH, BATCH)
        row_b = pl.multiple_of((N_CTX_CHUNKS - 1 - t) * BATCH, BATCH)
        return tuple(advance(gi, carry[gi][0], carry[gi][1], row_f, row_b, sc_ref) for gi in range(gb))

    def lat_step(t, carry):
        row_f = pl.multiple_of(t * BATCH, BATCH)
        row_b = pl.multiple_of((N_CHUNKS - 1 - t) * BATCH, BATCH)
        out = []
        for gi in range(gb):
            h_re, h_im = carry[gi]
            hp_ref[gi, 0, pl.ds(row_f, BATCH), 0:half] = h_re[:, 0:half]
            hp_ref[gi, 0, pl.ds(row_b, BATCH), half:2 * half] = h_re[:, half:]
            hp_ref[gi, 1, pl.ds(row_f, BATCH), 0:half] = h_im[:, 0:half]
            hp_ref[gi, 1, pl.ds(row_b, BATCH), half:2 * half] = h_im[:, half:]
            out.append(advance(gi, h_re, h_im, row_f, row_b, s_ref))
        return tuple(out)

    zero = jnp.zeros((BATCH, LANES), F32)
    carry = tuple((zero, zero) for _ in range(gb))
    carry = lax.fori_loop(0, N_CTX_CHUNKS, ctx_step, carry)
    lax.fori_loop(0, N_CHUNKS, lat_step, carry)
    for gi in range(gb):
        for b in range(BATCH):
            hb_re = hp_ref[gi, 0, pl.ds(b, N_CHUNKS, stride=BATCH), :].astype(BF16)
            hb_im = hp_ref[gi, 1, pl.ds(b, N_CHUNKS, stride=BATCH), :].astype(BF16)
            y = (jnp.dot(ul_ref[gi, b], mi_ref[gi], preferred_element_type=F32)
                 + jnp.dot(hb_re, mo_ref[gi, 0:LANES, :], preferred_element_type=F32)
                 + jnp.dot(hb_im, mo_ref[gi, LANES:2 * LANES, :], preferred_element_type=F32))
            y_ref[gi, b] = y.astype(BF16)


def _s5_scan(u_lat_t, u_ctx_t, m_intra, m_state, m_out, a16):
    gb = 2
    rows = N_CHUNKS * BATCH
    crows = N_CTX_CHUNKS * BATCH
    grp = lambda r, c: pl.BlockSpec((gb, r, c), lambda g: (g, 0, 0))
    tok = lambda n: pl.BlockSpec((gb, BATCH, n, CHUNK_COLS), lambda g: (g, 0, 0, 0))
    return pl.pallas_call(
        functools.partial(_s5_kernel, gb=gb),
        grid=(SSM_GROUPS // gb,),
        in_specs=[tok(N_CHUNKS), tok(N_CTX_CHUNKS), grp(CHUNK_COLS, CHUNK_COLS),
                  grp(CHUNK_COLS, CHUNK_COLS), grp(CHUNK_COLS, CHUNK_COLS), grp(8, 2 * LANES)],
        out_specs=tok(N_CHUNKS),
        out_shape=jax.ShapeDtypeStruct((SSM_GROUPS, BATCH, N_CHUNKS, CHUNK_COLS), BF16),
        scratch_shapes=[pltpu.VMEM((gb, 2, rows, LANES), F32),
                        pltpu.VMEM((gb, 2, crows, LANES), F32),
                        pltpu.VMEM((gb, 2, rows, LANES), F32)],
        compiler_params=pltpu.CompilerParams(
            dimension_semantics=("arbitrary",), vmem_limit_bytes=VMEM_LIMIT),
        name="s5_scan",
    )(u_lat_t, u_ctx_t, m_state, m_intra, m_out, a16)


FFN_TILE = 256


def _post_kernel(x_ref, a_ref, yt_ref, u_ref, ga_ref, gs_ref, mod_ref, d_ref, fg_ref, og_ref, permt_ref,
                 wglu_ref, wba_ref, wbs_ref, wout_ref, wfi_ref, wfo_ref, o_ref, h1_ref, acc_ref, r_ref):
    dm = D_MODEL
    g1 = mod_ref[0, :, 2 * dm:3 * dm]
    sh2 = mod_ref[0, :, 3 * dm:4 * dm]
    sc2 = mod_ref[0, :, 4 * dm:5 * dm]
    g2 = mod_ref[0, :, 5 * dm:6 * dm]
    u = u_ref[0].astype(F32)
    y = _load_chunk_layout(yt_ref, permt_ref, r_ref, u.shape[0] // (CHUNK * CHUNK))
    sp = jax.nn.gelu(y + d_ref[...] * u).astype(BF16)
    vg = jnp.dot(sp, wglu_ref[...], preferred_element_type=F32)
    s = (vg[:, :SSM_WIDTH] * jax.nn.sigmoid(vg[:, SSM_WIDTH:])).astype(BF16)
    merged = (ga_ref[0].astype(F32) * jnp.dot(a_ref[0], wba_ref[...], preferred_element_type=F32)
              + gs_ref[0].astype(F32) * jnp.dot(s, wbs_ref[...], preferred_element_type=F32))
    mix = jnp.dot(merged.astype(BF16), wout_ref[...], preferred_element_type=F32)
    h1 = x_ref[0] + g1 * mix
    h1_ref[...] = h1
    n2 = _rms_modulate(h1, fg_ref[...], sh2, sc2).astype(BF16)
    for c in range(FFN_HIDDEN // FFN_TILE):
        lo = c * FFN_TILE
        fa = jnp.dot(n2, wfi_ref[:, lo:lo + FFN_TILE], preferred_element_type=F32)
        fb = jnp.dot(n2, wfi_ref[:, FFN_HIDDEN + lo:FFN_HIDDEN + lo + FFN_TILE],
                     preferred_element_type=F32)
        act = (fa * jax.nn.sigmoid(fa) * fb).astype(BF16)
        part = jnp.dot(act, wfo_ref[lo:lo + FFN_TILE, :], preferred_element_type=F32)
        if c == 0:
            acc_ref[...] = part
        else:
            acc_ref[...] += part
    h2 = h1_ref[...] + g2 * acc_ref[...]
    o_ref[0] = (h2 * lax.rsqrt(jnp.mean(h2 * h2, axis=-1, keepdims=True) + NORM_EPS)) * og_ref[...]


def _post(x, a, y_t, u, ga, gs, mod3, d_skip, ffn_g, fin_g, perm_t, wglu, wba, wbs, wout, wfi, wfo):
    tm = 512
    n = CHUNK * CHUNK
    tok = lambda width: pl.BlockSpec((1, tm, width), lambda b, i: (b, i, 0))
    const = lambda r, c: pl.BlockSpec((r, c), lambda b, i: (0, 0), pipeline_mode=pl.Buffered(1))
    return pl.pallas_call(
        _post_kernel,
        grid=(BATCH, SEQ // tm),
        in_specs=[tok(D_MODEL), tok(ATTN_WIDTH),
                  pl.BlockSpec((SSM_GROUPS, 1, tm // CHUNK, CHUNK_COLS), lambda b, i: (0, b, i, 0)),
                  tok(SSM_WIDTH), tok(D_MODEL), tok(D_MODEL),
                  pl.BlockSpec((1, 1, N_MOD * D_MODEL), lambda b, i: (b, 0, 0)),
                  const(1, SSM_WIDTH), const(1, D_MODEL), const(1, D_MODEL), const(n, n),
                  const(SSM_WIDTH, 2 * SSM_WIDTH), const(ATTN_WIDTH, D_MODEL),
                  const(SSM_WIDTH, D_MODEL), const(D_MODEL, D_MODEL),
                  const(D_MODEL, 2 * FFN_HIDDEN), const(FFN_HIDDEN, D_MODEL)],
        out_specs=tok(D_MODEL),
        out_shape=jax.ShapeDtypeStruct((BATCH, SEQ, D_MODEL), F32),
        scratch_shapes=[pltpu.VMEM((tm, D_MODEL), F32), pltpu.VMEM((tm, D_MODEL), F32),
                        pltpu.VMEM((n, SSM_WIDTH), F32)],
        compiler_params=pltpu.CompilerParams(
            dimension_semantics=("arbitrary", "arbitrary"), vmem_limit_bytes=VMEM_LIMIT),
        name="post",
    )(x, a, y_t, u, ga, gs, mod3, d_skip, ffn_g, fin_g, perm_t, wglu, wba, wbs, wout, wfi, wfo)


def kernel(x, c, ctx, c_ctx, w_mod, b_mod, attn_norm_g, ffn_norm_g, w_in, rel_pos_bias,
           ssm_lambda_re, ssm_lambda_im, ssm_log_dt, ssm_b_re, ssm_b_im, ssm_c_re, ssm_c_im, ssm_d,
           w_glu, w_branch_attn, w_branch_ssm, w_out, w_ffn_in, w_ffn_out, final_norm_g):
    assert x.shape == (BATCH, SEQ, D_MODEL) and w_mod.shape[0] == 1
    c_rows = jnp.concatenate(
        [c, c_ctx[None, :], jnp.zeros((MOD_ROWS - BATCH - 1, D_MODEL), F32)], axis=0)
    mod3 = _modulation(c_rows, w_mod[0], b_mod[0]).reshape(MOD_ROWS, 1, N_MOD * D_MODEL)

    col_scale = jnp.concatenate([jnp.full((ATTN_WIDTH,), HEAD_DIM ** -0.5 * LOG2E, F32),
                                 jnp.ones((IN_COLS - ATTN_WIDTH,), F32)])
    w_in_bf16 = (w_in[0] * col_scale[None, :]).astype(BF16)
    norm_g = attn_norm_g[0].reshape(1, D_MODEL)

    perm = _chunk_perm()
    q, k, v, u, u_t, ga, gs = _input_projection(x, mod3, norm_g, w_in_bf16, _rope_tables(), perm)
    kc, vc, uc_t = _context_projection(ctx, mod3, norm_g, w_in_bf16, perm)

    attn = _attention(q, k, v, kc, vc, _bias_tables(rel_pos_bias[0]))

    m_intra, m_state, m_out, a16 = _s5_matrices(
        ssm_lambda_re[0], ssm_lambda_im[0], ssm_log_dt[0], ssm_b_re[0], ssm_b_im[0],
        ssm_c_re[0], ssm_c_im[0])
    y_t = _s5_scan(u_t, uc_t, m_intra, m_state, m_out, a16)

    return _post(x, attn, y_t, u, ga, gs, mod3,
                 ssm_d[0].reshape(1, SSM_WIDTH), ffn_norm_g[0].reshape(1, D_MODEL),
                 final_norm_g.reshape(1, D_MODEL), perm,
                 w_glu[0].astype(BF16), w_branch_attn[0].astype(BF16), w_branch_ssm[0].astype(BF16),
                 w_out[0].astype(BF16), w_ffn_in[0].astype(BF16), w_ffn_out[0].astype(BF16))
```

```python
import functools
import math

import numpy as np
import jax
import jax.numpy as jnp
from jax import lax
from jax.experimental import pallas as pl
from jax.experimental.pallas import tpu as pltpu

F32 = jnp.float32
BF16 = jnp.bfloat16

D_MODEL = 1024
BATCH = 8
SEQ = 4096
GRID_W = 64
GRID_ROWS = SEQ // GRID_W
CTX_LEN = 256
N_HEADS = 8
HEAD_DIM = 64
ATTN_WIDTH = N_HEADS * HEAD_DIM
WIN_H = 8
WIN_W = 16
ROPE_BASE = 10000.0
SSM_WIDTH = 512
SSM_GROUP = 16
SSM_GROUPS = SSM_WIDTH // SSM_GROUP
SSM_STATE = 64
FFN_HIDDEN = 2816
IN_COLS = 3 * ATTN_WIDTH + SSM_WIDTH + 2 * D_MODEL
N_MOD = 6
NORM_EPS = 1e-6
NEG_BIG = -1e30
LOG2E = math.log2(math.e)

LANES = 128
CHUNK = 16
N_CHUNKS = SEQ // CHUNK
N_CTX_CHUNKS = CTX_LEN // CHUNK
CHUNK_COLS = CHUNK * SSM_GROUP
MOD_ROWS = 16
CTX_MOD_ROW = BATCH
VMEM_LIMIT = 56 * 1024 * 1024

Q_ROWS_PER_STEP = 4
Q_BLOCK = Q_ROWS_PER_STEP * GRID_W
K_ROWS_PER_STEP = Q_ROWS_PER_STEP + WIN_H
K_BLOCK = K_ROWS_PER_STEP * GRID_W
N_QBLOCKS = GRID_ROWS // Q_ROWS_PER_STEP
KEY_GROUP = Q_BLOCK
SOFTMAX_ROWS = 32


def _rms_modulate(x, g, shift, scale):
    xn = x * lax.rsqrt(jnp.mean(x * x, axis=-1, keepdims=True) + NORM_EPS)
    return (xn * g) * (1.0 + scale) + shift


def _block_transpose8(vs, lane):
    for shift in (64, 32, 16):
        keep = (lane & (2 * shift - 1)) < shift
        dist = shift // SSM_GROUP
        out = list(vs)
        for a in range(8):
            if a & dist:
                continue
            b = a + dist
            out[a] = jnp.where(keep, vs[a], pltpu.roll(vs[b], shift, 1))
            out[b] = jnp.where(keep, pltpu.roll(vs[a], LANES - shift, 1), vs[b])
        vs = out
    return vs


def _chunk_perm():
    n = CHUNK * CHUNK
    r = np.arange(n)
    m = np.zeros((n, n), np.float32)
    m[r, (r % CHUNK) * CHUNK + r // CHUNK] = 1.0
    return jnp.asarray(m, BF16)


def _store_chunk_layout(u16, perm_ref, out_ref, n_groups16):
    lane = lax.broadcasted_iota(jnp.int32, (1, LANES), 1)
    n = CHUNK * CHUNK
    for hf in range(n_groups16):
        r = jnp.dot(perm_ref[...], u16[hf * n:(hf + 1) * n, :], preferred_element_type=F32)
        for v in range(SSM_WIDTH // LANES):
            for jh in range(2):
                vs = [r[CHUNK * (8 * jh + jp):CHUNK * (8 * jh + jp + 1), v * LANES:(v + 1) * LANES]
                      for jp in range(8)]
                outs = _block_transpose8(vs, lane)
                for gi in range(8):
                    out_ref[8 * v + gi, 0, hf * CHUNK:(hf + 1) * CHUNK, jh * LANES:(jh + 1) * LANES] = (
                        outs[gi].astype(BF16))


def _load_chunk_layout(yt_ref, perm_t_ref, r_ref, hf):
    lane = lax.broadcasted_iota(jnp.int32, (1, LANES), 1)
    for v in range(SSM_WIDTH // LANES):
        for jh in range(2):
            vs = [yt_ref[8 * v + gi, 0, hf * CHUNK:(hf + 1) * CHUNK,
                         jh * LANES:(jh + 1) * LANES].astype(F32) for gi in range(8)]
            outs = _block_transpose8(vs, lane)
            for jp in range(8):
                r_ref[hf, CHUNK * (8 * jh + jp):CHUNK * (8 * jh + jp + 1), v * LANES:(v + 1) * LANES] = outs[jp]
    return jnp.dot(perm_t_ref[...], r_ref[hf].astype(BF16), preferred_element_type=F32)


def _mod_kernel(c_ref, w_ref, b_ref, o_ref):
    c = c_ref[...]
    s = c * jax.nn.sigmoid(c)
    o_ref[...] = jnp.dot(s, w_ref[...], preferred_element_type=F32) + b_ref[...]


def _modulation(c_rows, w_mod, b_mod):
    n = N_MOD * D_MODEL
    tn = 1536
    return pl.pallas_call(
        _mod_kernel,
        grid=(n // tn,),
        in_specs=[pl.BlockSpec((MOD_ROWS, D_MODEL), lambda j: (0, 0)),
                  pl.BlockSpec((D_MODEL, tn), lambda j: (0, j)),
                  pl.BlockSpec((1, tn), lambda j: (0, j))],
        out_specs=pl.BlockSpec((MOD_ROWS, tn), lambda j: (0, j)),
        out_shape=jax.ShapeDtypeStruct((MOD_ROWS, n), F32),
        name="modulation",
    )(c_rows, w_mod, b_mod.reshape(1, n))


def _rope_tables():
    n_freq = HEAD_DIM // 4
    inv_freq = ROPE_BASE ** (-np.arange(n_freq, dtype=np.float64) / n_freq)
    t = np.arange(SEQ)
    lane = np.arange(LANES)
    d = lane % HEAD_DIM
    use_col = (d // (HEAD_DIM // 2)) == 1
    w = d % (HEAD_DIM // 2)
    first = w < n_freq
    pos = np.where(use_col[None, :], (t % GRID_W)[:, None], (t // GRID_W)[:, None]).astype(np.float64)
    ang = pos * inv_freq[w % n_freq][None, :]
    cos = np.cos(ang)
    sin = np.sin(ang)
    sin_a = np.where(first[None, :], -sin, 0.0)
    sin_b = np.where(first[None, :], 0.0, sin)
    return (jnp.asarray(cos, F32), jnp.asarray(sin_a, F32), jnp.asarray(sin_b, F32))


def _rope_store(r, cos, sin_a, sin_b, out_ref, transposed=False):
    for j in range(ATTN_WIDTH // LANES):
        xs = r[:, j * LANES:(j + 1) * LANES]
        rot = (xs * cos + pltpu.roll(xs, LANES - HEAD_DIM // 4, 1) * sin_a
               + pltpu.roll(xs, HEAD_DIM // 4, 1) * sin_b)
        if transposed:
            rot_t = rot.T.astype(BF16)
            for gi in range(r.shape[0] // KEY_GROUP):
                out_ref[0, gi, j * LANES:(j + 1) * LANES, :] = rot_t[:, gi * KEY_GROUP:(gi + 1) * KEY_GROUP]
        else:
            out_ref[0, :, j * LANES:(j + 1) * LANES] = rot.astype(BF16)


def _inproj_kernel(x_ref, mod_ref, g_ref, w_ref, cos_ref, sa_ref, sb_ref, perm_ref,
                   q_ref, k_ref, v_ref, u_ref, ut_ref, ga_ref, gs_ref):
    x = x_ref[0]
    shift = mod_ref[0, :, 0:D_MODEL]
    scale = mod_ref[0, :, D_MODEL:2 * D_MODEL]
    nb = _rms_modulate(x, g_ref[...], shift, scale).astype(BF16)
    cos = cos_ref[...]
    sin_a = sa_ref[...]
    sin_b = sb_ref[...]
    aw = ATTN_WIDTH
    q = jnp.dot(nb, w_ref[:, 0:aw], preferred_element_type=F32)
    _rope_store(q, cos, sin_a, sin_b, q_ref)
    k = jnp.dot(nb, w_ref[:, aw:2 * aw], preferred_element_type=F32)
    _rope_store(k, cos, sin_a, sin_b, k_ref, transposed=True)
    v_ref[0] = jnp.dot(nb, w_ref[:, 2 * aw:3 * aw], preferred_element_type=F32).astype(BF16)
    c0 = 3 * aw
    u16 = jnp.dot(nb, w_ref[:, c0:c0 + SSM_WIDTH], preferred_element_type=F32).astype(BF16)
    u_ref[0] = u16
    _store_chunk_layout(u16, perm_ref, ut_ref, x.shape[0] // (CHUNK * CHUNK))
    c1 = c0 + SSM_WIDTH
    ga = jnp.dot(nb, w_ref[:, c1:c1 + D_MODEL], preferred_element_type=F32)
    ga_ref[0] = jax.nn.sigmoid(ga).astype(BF16)
    c2 = c1 + D_MODEL
    gs = jnp.dot(nb, w_ref[:, c2:c2 + D_MODEL], preferred_element_type=F32)
    gs_ref[0] = jax.nn.sigmoid(gs).astype(BF16)


def _input_projection(x, mod3, norm_g, w_in_bf16, rope, perm):
    tm = 512
    n = CHUNK * CHUNK
    cos, sin_a, sin_b = rope
    tok = lambda width: pl.BlockSpec((1, tm, width), lambda i, b: (b, i, 0))
    tab = pl.BlockSpec((tm, LANES), lambda i, b: (i, 0))
    out = lambda width: jax.ShapeDtypeStruct((BATCH, SEQ, width), BF16)
    return pl.pallas_call(
        _inproj_kernel,
        grid=(SEQ // tm, BATCH),
        in_specs=[tok(D_MODEL),
                  pl.BlockSpec((1, 1, N_MOD * D_MODEL), lambda i, b: (b, 0, 0)),
                  pl.BlockSpec((1, D_MODEL), lambda i, b: (0, 0)),
                  pl.BlockSpec((D_MODEL, IN_COLS), lambda i, b: (0, 0)),
                  tab, tab, tab,
                  pl.BlockSpec((n, n), lambda i, b: (0, 0))],
        out_specs=[tok(ATTN_WIDTH),
                   pl.BlockSpec((1, tm // KEY_GROUP, ATTN_WIDTH, KEY_GROUP), lambda i, b: (b, i, 0, 0)),
                   tok(ATTN_WIDTH), tok(SSM_WIDTH),
                   pl.BlockSpec((SSM_GROUPS, 1, tm // CHUNK, CHUNK_COLS), lambda i, b: (0, b, i, 0)),
                   tok(D_MODEL), tok(D_MODEL)],
        out_shape=[out(ATTN_WIDTH),
                   jax.ShapeDtypeStruct((BATCH, SEQ // KEY_GROUP, ATTN_WIDTH, KEY_GROUP), BF16),
                   out(ATTN_WIDTH), out(SSM_WIDTH),
                   jax.ShapeDtypeStruct((SSM_GROUPS, BATCH, N_CHUNKS, CHUNK_COLS), BF16),
                   out(D_MODEL), out(D_MODEL)],
        compiler_params=pltpu.CompilerParams(
            dimension_semantics=("arbitrary", "arbitrary"), vmem_limit_bytes=VMEM_LIMIT),
        name="input_projection",
    )(x, mod3, norm_g, w_in_bf16, cos, sin_a, sin_b, perm)


def _ctx_proj_kernel(x_ref, mod_ref, g_ref, wk_ref, wv_ref, wu_ref, perm_ref, k_ref, v_ref, ut_ref):
    x = x_ref[0]
    shift = mod_ref[0, :, 0:D_MODEL]
    scale = mod_ref[0, :, D_MODEL:2 * D_MODEL]
    nb = _rms_modulate(x, g_ref[...], shift, scale).astype(BF16)
    k_ref[0] = jnp.dot(nb, wk_ref[...], preferred_element_type=F32).T.astype(BF16)
    v_ref[0] = jnp.dot(nb, wv_ref[...], preferred_element_type=F32).astype(BF16)
    u16 = jnp.dot(nb, wu_ref[...], preferred_element_type=F32).astype(BF16)
    _store_chunk_layout(u16, perm_ref, ut_ref, 1)


def _context_projection(ctx, mod3, norm_g, w_in_bf16, perm):
    n = CHUNK * CHUNK
    aw = ATTN_WIDTH
    tok = lambda width: pl.BlockSpec((1, CTX_LEN, width), lambda b: (b, 0, 0))
    wcol = lambda j: pl.BlockSpec((D_MODEL, aw), lambda b: (0, j))
    out = jax.ShapeDtypeStruct((BATCH, CTX_LEN, aw), BF16)
    return pl.pallas_call(
        _ctx_proj_kernel,
        grid=(BATCH,),
        in_specs=[tok(D_MODEL),
                  pl.BlockSpec((1, 1, N_MOD * D_MODEL), lambda b: (CTX_MOD_ROW, 0, 0)),
                  pl.BlockSpec((1, D_MODEL), lambda b: (0, 0)),
                  wcol(1), wcol(2), wcol(3),
                  pl.BlockSpec((n, n), lambda b: (0, 0))],
        out_specs=[pl.BlockSpec((1, aw, CTX_LEN), lambda b: (b, 0, 0)), tok(aw),
                   pl.BlockSpec((SSM_GROUPS, 1, N_CTX_CHUNKS, CHUNK_COLS), lambda b: (0, b, 0, 0))],
        out_shape=[jax.ShapeDtypeStruct((BATCH, aw, CTX_LEN), BF16), out,
                   jax.ShapeDtypeStruct((SSM_GROUPS, BATCH, N_CTX_CHUNKS, CHUNK_COLS), BF16)],
        name="context_projection",
    )(ctx, mod3, norm_g, w_in_bf16, w_in_bf16, w_in_bf16, perm)


def _window_start_rows(r):
    return min(max(r - WIN_H // 2, 0), GRID_ROWS - WIN_H)


def _key_block_row(jb):
    return min(max(Q_ROWS_PER_STEP * jb - WIN_H // 2, 0), GRID_ROWS - K_ROWS_PER_STEP)


def _window_geometry(jb):
    key_row0 = _key_block_row(jb)
    offs, deltas = [], []
    for i in range(Q_ROWS_PER_STEP):
        r = Q_ROWS_PER_STEP * jb + i
        rs = _window_start_rows(r)
        offs.append(rs - key_row0)
        deltas.append(r - rs)
    return offs, deltas


def _attn_kernel(q_ref, k_ref, v_ref, kc_ref, vc_ref, bias_ref, o_ref,
                 s_ref, sc_ref, p_ref, pc_ref, l_ref, m_ref, acc_ref):
    lane = lax.broadcasted_iota(jnp.int32, (1, LANES), 1)
    left = lane < HEAD_DIM
    n_ktiles = K_BLOCK // LANES
    last = N_QBLOCKS - 1
    max_key_start = (GRID_ROWS - K_ROWS_PER_STEP) * GRID_W

    def scores(e, q_start, key_start):
        q2 = q_ref[0, pl.ds(q_start, Q_BLOCK), :]
        qm = jnp.where(left if e == 0 else jnp.logical_not(left), q2, jnp.zeros_like(q2))
        group0 = key_start // KEY_GROUP
        for gi in range(K_BLOCK // KEY_GROUP):
            s_ref[e, :, gi * KEY_GROUP:(gi + 1) * KEY_GROUP] = jnp.dot(
                qm, k_ref[0, group0 + gi], preferred_element_type=F32)
        sc_ref[e] = jnp.dot(qm, kc_ref[0], preferred_element_type=F32)

    def softmax(e, offs, deltas):
        def geometry(sub):
            i = sub * SOFTMAX_ROWS // GRID_W
            qcols = slice(sub * SOFTMAX_ROWS % GRID_W, sub * SOFTMAX_ROWS % GRID_W + SOFTMAX_ROWS)
            rows = slice(sub * SOFTMAX_ROWS, (sub + 1) * SOFTMAX_ROWS)
            par = offs[i] % 2
            return rows, qcols, par, offs[i] // 2, WIN_H // 2 + par, deltas[i]

        def tile(rows, qcols, par, t0, n_tiles, delta, xt):
            t = t0 + xt
            dr0 = 2 * xt - par - delta
            st = s_ref[e, rows, t * LANES:(t + 1) * LANES] + bias_ref[0, e, dr0 + WIN_H, qcols, :]
            if par and xt == 0:
                st = jnp.where(left, NEG_BIG, st)
            if par and xt == n_tiles - 1:
                st = jnp.where(left, st, NEG_BIG)
            return st

        n_sub = Q_BLOCK // SOFTMAX_ROWS
        for sub in range(n_sub):
            geo = geometry(sub)
            rows, n_tiles = geo[0], geo[4]
            mt = jnp.maximum(sc_ref[e, rows, 0:LANES], sc_ref[e, rows, LANES:2 * LANES])
            for xt in range(n_tiles):
                mt = jnp.maximum(mt, tile(*geo, xt))
            m_ref[e, rows, :] = jnp.broadcast_to(jnp.max(mt, axis=1, keepdims=True), (SOFTMAX_ROWS, LANES))
        for sub in range(n_sub):
            geo = geometry(sub)
            rows, t0, n_tiles = geo[0], geo[3], geo[4]
            m = m_ref[e, rows, :]
            lt = None
            for xt in range(n_tiles):
                t = t0 + xt
                pt = jnp.exp2(tile(*geo, xt) - m)
                lt = pt if lt is None else lt + pt
                p_ref[e, rows, t * LANES:(t + 1) * LANES] = pt.astype(BF16)
            for t in range(n_ktiles):
                if not (t0 <= t < t0 + n_tiles):
                    p_ref[e, rows, t * LANES:(t + 1) * LANES] = jnp.zeros((SOFTMAX_ROWS, LANES), BF16)
            for ci in range(2):
                pt = jnp.exp2(sc_ref[e, rows, ci * LANES:(ci + 1) * LANES] - m)
                lt = lt + pt
                pc_ref[e, rows, ci * LANES:(ci + 1) * LANES] = pt.astype(BF16)
            l = jnp.sum(lt, axis=1, keepdims=True)
            l_ref[e, rows, :] = jnp.broadcast_to(1.0 / l, (SOFTMAX_ROWS, LANES))

    def values(e, key_start):
        vblk = v_ref[0, pl.ds(key_start, K_BLOCK), :]
        return (jnp.dot(p_ref[e], vblk, preferred_element_type=F32)
                + jnp.dot(pc_ref[e], vc_ref[0], preferred_element_type=F32)) * l_ref[e]

    def block(jb_static, q_start, key_start, next_starts):
        offs, deltas = _window_geometry(jb_static)
        scores(1, q_start, key_start)
        softmax(0, offs, deltas)
        acc_ref[...] = values(0, key_start)
        if next_starts is not None:
            scores(0, *next_starts)
        softmax(1, offs, deltas)
        o1 = values(1, key_start)
        o_ref[0, pl.ds(q_start, Q_BLOCK), :] = jnp.where(left, acc_ref[...], o1).astype(BF16)

    def starts(jb):
        q_start = pl.multiple_of(jb * Q_BLOCK, Q_BLOCK)
        key_start = jnp.clip((jb * Q_ROWS_PER_STEP - WIN_H // 2) * GRID_W, 0, max_key_start)
        return q_start, pl.multiple_of(key_start, Q_BLOCK)

    scores(0, 0, 0)
    block(0, 0, 0, (Q_BLOCK, 0))

    def interior(jb, carry):
        q_start, key_start = starts(jb)
        block(1, q_start, key_start, starts(jb + 1))
        return carry

    lax.fori_loop(1, last, interior, 0)
    block(last, last * Q_BLOCK, max_key_start, None)


def _bias_tables(rpb):
    qcol = np.arange(GRID_W)
    kcol = np.arange(GRID_W)
    col_start = np.clip(qcol - WIN_W // 2, 0, GRID_W - WIN_W)
    in_win = (kcol[None, :] >= col_start[:, None]) & (kcol[None, :] < col_start[:, None] + WIN_W)
    dc_idx = np.clip(kcol[None, :] - qcol[:, None], -(WIN_W - 1), WIN_W - 1) + WIN_W - 1
    sel = (np.arange(2 * WIN_W - 1)[:, None, None] == dc_idx[None]).astype(np.float32)
    toe = jnp.einsum('hdt,tck->hdck', rpb * LOG2E, jnp.asarray(sel),
                     precision=lax.Precision.HIGHEST)
    toe = jnp.where(in_win[None, None], toe, NEG_BIG)
    neg = jnp.full((N_HEADS, 1, GRID_W, GRID_W), NEG_BIG, F32)
    ext = jnp.concatenate([neg, toe, neg], axis=1)
    pair = jnp.concatenate([ext[:, 0:16], ext[:, 1:17]], axis=-1)
    return pair.reshape(N_HEADS // 2, 2, 16, GRID_W, LANES)


def _attention(q, k, v, kc, vc, bias):
    kspec = pl.BlockSpec((1, SEQ, LANES), lambda hp, b: (b, 0, hp))
    ktspec = pl.BlockSpec((1, SEQ // KEY_GROUP, LANES, KEY_GROUP), lambda hp, b: (b, 0, hp, 0))
    cspec = pl.BlockSpec((1, CTX_LEN, LANES), lambda hp, b: (b, 0, hp))
    ctspec = pl.BlockSpec((1, LANES, CTX_LEN), lambda hp, b: (b, hp, 0))
    bspec = pl.BlockSpec((1, 2, 16, GRID_W, LANES), lambda hp, b: (hp, 0, 0, 0, 0))
    return pl.pallas_call(
        _attn_kernel,
        grid=(N_HEADS // 2, BATCH),
        in_specs=[kspec, ktspec, kspec, ctspec, cspec, bspec],
        out_specs=kspec,
        out_shape=jax.ShapeDtypeStruct((BATCH, SEQ, ATTN_WIDTH), BF16),
        scratch_shapes=[pltpu.VMEM((2, Q_BLOCK, K_BLOCK), F32),
                        pltpu.VMEM((2, Q_BLOCK, CTX_LEN), F32),
                        pltpu.VMEM((2, Q_BLOCK, K_BLOCK), BF16),
                        pltpu.VMEM((2, Q_BLOCK, CTX_LEN), BF16),
                        pltpu.VMEM((2, Q_BLOCK, LANES), F32),
                        pltpu.VMEM((2, Q_BLOCK, LANES), F32),
                        pltpu.VMEM((Q_BLOCK, LANES), F32)],
        compiler_params=pltpu.CompilerParams(
            dimension_semantics=("arbitrary", "arbitrary"),
            vmem_limit_bytes=VMEM_LIMIT),
        name="attention",
    )(q, k, v, kc, vc, bias)


def _rot256(a, b, s, lane):
    s %= 2 * LANES
    if s >= LANES:
        a, b, s = b, a, s - LANES
    if s == 0:
        return a, b
    ra = pltpu.roll(a, s, 1)
    rb = pltpu.roll(b, s, 1)
    keep = lane >= s
    return jnp.where(keep, ra, rb), jnp.where(keep, rb, ra)


def _s5_prep_kernel(par_ref, b_ref, c_ref, t16_ref, e_ref, mi_ref, ms_ref, mo_ref, a_ref):
    hi = lax.Precision.HIGHEST
    dot = lambda x, y: jnp.dot(x, y, preferred_element_type=F32, precision=hi)
    ns = SSM_STATE
    par = par_ref[0]
    lam_re, lam_im = par[:, 0:1], par[:, 1:2]
    dt = jnp.exp(par[:, 2:3])
    lane = lax.broadcasted_iota(jnp.int32, (1, LANES), 1)
    kf = lane.astype(F32)
    mag = jnp.exp((lam_re * dt) * kf)
    ang = (lam_im * dt) * kf
    pw_re, pw_im = mag * jnp.cos(ang), mag * jnp.sin(ang)
    lb_re = jnp.sum(jnp.where(lane == 1, pw_re, 0.0), axis=1, keepdims=True)
    lb_im = jnp.sum(jnp.where(lane == 1, pw_im, 0.0), axis=1, keepdims=True)
    den = lam_re * lam_re + lam_im * lam_im
    nr, ni = lb_re - 1.0, lb_im
    f_re = (nr * lam_re + ni * lam_im) / den
    f_im = (ni * lam_re - nr * lam_im) / den
    b_re, b_im = b_ref[0, :, 0:SSM_GROUP], b_ref[0, :, SSM_GROUP:2 * SSM_GROUP]
    bb_re = f_re * b_re - f_im * b_im
    bb_im = f_re * b_im + f_im * b_re
    t16 = t16_ref[...]
    bbt_re, bbt_im = dot(bb_re, t16), dot(bb_im, t16)
    ct_re = dot(c_ref[0, :, 0:SSM_GROUP], t16)
    ct_im = dot(c_ref[0, :, SSM_GROUP:2 * SSM_GROUP], t16)
    pw_at = lambda x: (dot(pw_re, e_ref[x]), dot(pw_im, e_ref[x]))
    id_re, id_im = pw_at(0)
    rev_re, rev_im = pw_at(1)
    p1_re, p1_im = pw_at(2)
    r16_re, r16_im = pw_at(3)
    f, b = slice(0, ns), slice(ns, 2 * ns)
    cmul = lambda ar, ai, br, bi: (ar * br - ai * bi, ar * bi + ai * br)

    sf_re, sf_im = cmul(rev_re[f], rev_im[f], bbt_re[f], bbt_im[f])
    sb_re, sb_im = cmul(id_re[b], id_im[b], bbt_re[b], bbt_im[b])
    ms_ref[0] = jnp.concatenate([sf_re, sb_re, sf_im, sb_im], axis=0).T.astype(BF16)

    of_re, of_im = cmul(p1_re[f], p1_im[f], ct_re[f], ct_im[f])
    ob_re, ob_im = cmul(r16_re[b], r16_im[b], ct_re[b], ct_im[b])
    mo_ref[0] = jnp.concatenate([of_re, ob_re, -of_im, -ob_im], axis=0).astype(BF16)

    xf_re, xf_im = cmul(id_re[f], id_im[f], ct_re[f], ct_im[f])
    xb_re, xb_im = cmul(rev_re[b], rev_im[b], ct_re[b], ct_im[b])
    btf = jnp.concatenate([bbt_re[f], bbt_im[f]], axis=0).T
    btb = jnp.concatenate([bbt_re[b], bbt_im[b]], axis=0).T
    g_f = dot(btf, jnp.concatenate([xf_re, -xf_im], axis=0))
    g_b = dot(btb, jnp.concatenate([xb_re, -xb_im], axis=0))
    for j in range(CHUNK):
        rows = slice(SSM_GROUP * j, SSM_GROUP * (j + 1))
        lo_col, hi_col = SSM_GROUP * j, SSM_GROUP * (j + 1)
        f_lo, f_hi = _rot256(g_f[rows, :LANES], g_f[rows, LANES:], lo_col, lane)
        b_lo, b_hi = _rot256(g_b[rows, :LANES], g_b[rows, LANES:], -SSM_GROUP * (CHUNK - 1 - j), lane)
        lo = jnp.where(lane >= lo_col, f_lo, 0.0) + jnp.where(lane < hi_col, b_lo, 0.0)
        up = jnp.where(lane + LANES >= lo_col, f_hi, 0.0) + jnp.where(lane + LANES < hi_col, b_hi, 0.0)
        mi_ref[0, rows, 0:LANES] = lo.astype(BF16)
        mi_ref[0, rows, LANES:2 * LANES] = up.astype(BF16)

    a16 = jnp.concatenate([pw_re.T[CHUNK:CHUNK + 1, :], pw_im.T[CHUNK:CHUNK + 1, :]], axis=1)
    a_ref[0] = jnp.broadcast_to(a16, (8, 2 * LANES))


def _s5_matrices(lam_re, lam_im, log_dt, b_re, b_im, c_re, c_im):
    g, p2 = SSM_GROUPS, 2 * SSM_STATE
    both = lambda a: jnp.transpose(a, (1, 0, 2)).reshape(g, p2)
    log_dt_rows = jnp.repeat(jnp.transpose(log_dt), SSM_STATE, axis=1)
    par = jnp.stack([both(lam_re), both(lam_im), log_dt_rows] + [jnp.zeros((g, p2), F32)] * 5, axis=-1)
    rows_b = lambda a: jnp.transpose(a, (1, 0, 2, 3)).reshape(g, p2, SSM_GROUP)
    rows_c = lambda a: jnp.transpose(a, (1, 0, 3, 2)).reshape(g, p2, SSM_GROUP)
    b_cat = jnp.concatenate([rows_b(b_re), rows_b(b_im)], axis=-1)
    c_cat = jnp.concatenate([rows_c(c_re), rows_c(c_im)], axis=-1)

    col = np.arange(CHUNK_COLS)
    tile16 = (col[None, :] % SSM_GROUP == np.arange(SSM_GROUP)[:, None]).astype(np.float32)
    pos = col // SSM_GROUP
    k_idx = np.arange(LANES)[:, None]
    expand = np.stack([k_idx == pos[None, :], k_idx == (CHUNK - 1 - pos)[None, :],
                       k_idx == (pos + 1)[None, :], k_idx == (CHUNK - pos)[None, :]]).astype(np.float32)

    mat = lambda: pl.BlockSpec((1, CHUNK_COLS, CHUNK_COLS), lambda i: (i, 0, 0))
    mat_shape = jax.ShapeDtypeStruct((g, CHUNK_COLS, CHUNK_COLS), BF16)
    return pl.pallas_call(
        _s5_prep_kernel,
        grid=(g,),
        in_specs=[pl.BlockSpec((1, p2, 8), lambda i: (i, 0, 0)),
                  pl.BlockSpec((1, p2, 2 * SSM_GROUP), lambda i: (i, 0, 0)),
                  pl.BlockSpec((1, p2, 2 * SSM_GROUP), lambda i: (i, 0, 0)),
                  pl.BlockSpec((SSM_GROUP, CHUNK_COLS), lambda i: (0, 0)),
                  pl.BlockSpec((4, LANES, CHUNK_COLS), lambda i: (0, 0, 0))],
        out_specs=[mat(), mat(), mat(), pl.BlockSpec((1, 8, 2 * LANES), lambda i: (i, 0, 0))],
        out_shape=[mat_shape, mat_shape, mat_shape, jax.ShapeDtypeStruct((g, 8, 2 * LANES), F32)],
        name="s5_prep",
    )(par, b_cat, c_cat, jnp.asarray(tile16), jnp.asarray(expand))


def _s5_kernel(ul_ref, uc_ref, ms_ref, mi_ref, mo_ref, a_ref, y_ref, s_ref, sc_ref, hp_ref, *, gb):
    for gi in range(gb):
        for b in range(BATCH):
            sb = jnp.dot(ul_ref[gi, b], ms_ref[gi], preferred_element_type=F32)
            s_ref[gi, 0, pl.ds(b, N_CHUNKS, stride=BATCH), :] = sb[:, :LANES]
            s_ref[gi, 1, pl.ds(b, N_CHUNKS, stride=BATCH), :] = sb[:, LANES:]
            cb = jnp.dot(uc_ref[gi, b], ms_ref[gi], preferred_element_type=F32)
            sc_ref[gi, 0, pl.ds(b, N_CTX_CHUNKS, stride=BATCH), :] = cb[:, :LANES]
            sc_ref[gi, 1, pl.ds(b, N_CTX_CHUNKS, stride=BATCH), :] = cb[:, LANES:]
    lane = lax.broadcasted_iota(jnp.int32, (BATCH, LANES), 1)
    fwd = lane < SSM_STATE
    half = SSM_STATE

    def advance(gi, h_re, h_im, row_f, row_b, src):
        s_re = jnp.where(fwd, src[gi, 0, pl.ds(row_f, BATCH), :], src[gi, 0, pl.ds(row_b, BATCH), :])
        s_im = jnp.where(fwd, src[gi, 1, pl.ds(row_f, BATCH), :], src[gi, 1, pl.ds(row_b, BATCH), :])
        a_re = a_ref[gi, :, 0:LANES]
        a_im = a_ref[gi, :, LANES:2 * LANES]
        n_re = a_re * h_re - a_im * h_im + s_re
        n_im = a_re * h_im + a_im * h_re + s_im
        return n_re, n_im

    def ctx_step(t, carry):
        row_f = pl.multiple_of(t * BATCH, BATCH)
        row_b = pl.multiple_of((N_CTX_CHUNKS - 1 - t) * BATCH, BATCH)
        return tuple(advance(gi, carry[gi][0], carry[gi][1], row_f, row_b, sc_ref) for gi in range(gb))

    def lat_step(t, carry):
        row_f = pl.multiple_of(t * BATCH, BATCH)
        row_b = pl.multiple_of((N_CHUNKS - 1 - t) * BATCH, BATCH)
        out = []
        for gi in range(gb):
            h_re, h_im = carry[gi]
            hp_ref[gi, 0, pl.ds(row_f, BATCH), 0:half] = h_re[:, 0:half]
            hp_ref[gi, 0, pl.ds(row_b, BATCH), half:2 * half] = h_re[:, half:]
            hp_ref[gi, 1, pl.ds(row_f, BATCH), 0:half] = h_im[:, 0:half]
            hp_ref[gi, 1, pl.ds(row_b, BATCH), half:2 * half] = h_im[:, half:]
            out.append(advance(gi, h_re, h_im, row_f, row_b, s_ref))
        return tuple(out)

    zero = jnp.zeros((BATCH, LANES), F32)
    carry = tuple((zero, zero) for _ in range(gb))
    carry = lax.fori_loop(0, N_CTX_CHUNKS, ctx_step, carry)
    lax.fori_loop(0, N_CHUNKS, lat_step, carry)
    for gi in range(gb):
        for b in range(BATCH):
            hb_re = hp_ref[gi, 0, pl.ds(b, N_CHUNKS, stride=BATCH), :].astype(BF16)
            hb_im = hp_ref[gi, 1, pl.ds(b, N_CHUNKS, stride=BATCH), :].astype(BF16)
            y = (jnp.dot(ul_ref[gi, b], mi_ref[gi], preferred_element_type=F32)
                 + jnp.dot(jnp.concatenate([hb_re, hb_im], axis=1), mo_ref[gi], preferred_element_type=F32))
            y_ref[gi, b] = y.astype(BF16)


def _s5_scan(u_lat_t, u_ctx_t, m_intra, m_state, m_out, a16):
    gb = 2
    rows = N_CHUNKS * BATCH
    crows = N_CTX_CHUNKS * BATCH
    grp = lambda r, c: pl.BlockSpec((gb, r, c), lambda g: (g, 0, 0))
    tok = lambda n: pl.BlockSpec((gb, BATCH, n, CHUNK_COLS), lambda g: (g, 0, 0, 0))
    return pl.pallas_call(
        functools.partial(_s5_kernel, gb=gb),
        grid=(SSM_GROUPS // gb,),
        in_specs=[tok(N_CHUNKS), tok(N_CTX_CHUNKS), grp(CHUNK_COLS, CHUNK_COLS),
                  grp(CHUNK_COLS, CHUNK_COLS), grp(CHUNK_COLS, CHUNK_COLS), grp(8, 2 * LANES)],
        out_specs=tok(N_CHUNKS),
        out_shape=jax.ShapeDtypeStruct((SSM_GROUPS, BATCH, N_CHUNKS, CHUNK_COLS), BF16),
        scratch_shapes=[pltpu.VMEM((gb, 2, rows, LANES), F32),
                        pltpu.VMEM((gb, 2, crows, LANES), F32),
                        pltpu.VMEM((gb, 2, rows, LANES), F32)],
        compiler_params=pltpu.CompilerParams(
            dimension_semantics=("arbitrary",), vmem_limit_bytes=VMEM_LIMIT),
        name="s5_scan",
    )(u_lat_t, u_ctx_t, m_state, m_intra, m_out, a16)


FFN_TILE = 256


def _post_kernel(x_ref, a_ref, yt_ref, u_ref, ga_ref, gs_ref, mod_ref, d_ref, fg_ref, og_ref, permt_ref,
                 wglu_ref, wba_ref, wbs_ref, wout_ref, wfi_ref, wfo_ref, o_ref, h1_ref, n2_ref, act_ref, r_ref):
    dm = D_MODEL
    g1 = mod_ref[0, :, 2 * dm:3 * dm]
    sh2 = mod_ref[0, :, 3 * dm:4 * dm]
    sc2 = mod_ref[0, :, 4 * dm:5 * dm]
    g2 = mod_ref[0, :, 5 * dm:6 * dm]
    half = CHUNK * CHUNK
    halves = [slice(h * half, (h + 1) * half) for h in range(x_ref.shape[1] // half)]
    dot = functools.partial(jnp.dot, preferred_element_type=F32)

    sp = []
    for h, rows in enumerate(halves):
        y = _load_chunk_layout(yt_ref, permt_ref, r_ref, h)
        sp.append(jax.nn.gelu(y + d_ref[...] * u_ref[0, rows, :].astype(F32)).astype(BF16))
    s = []
    for h, rows in enumerate(halves):
        vg = dot(sp[h], wglu_ref[...])
        s.append((vg[:, :SSM_WIDTH] * jax.nn.sigmoid(vg[:, SSM_WIDTH:])).astype(BF16))
    merged = []
    for h, rows in enumerate(halves):
        m = (ga_ref[0, rows, :].astype(F32) * dot(a_ref[0, rows, :], wba_ref[...])
             + gs_ref[0, rows, :].astype(F32) * dot(s[h], wbs_ref[...]))
        merged.append(m.astype(BF16))
    for h, rows in enumerate(halves):
        h1 = x_ref[0, rows, :] + g1 * dot(merged[h], wout_ref[...])
        h1_ref[rows, :] = h1
        n2_ref[rows, :] = _rms_modulate(h1, fg_ref[...], sh2, sc2).astype(BF16)
    for c in range(FFN_HIDDEN // FFN_TILE):
        lo = c * FFN_TILE
        for rows in halves:
            n2 = n2_ref[rows, :]
            fa = dot(n2, wfi_ref[:, lo:lo + FFN_TILE])
            fb = dot(n2, wfi_ref[:, FFN_HIDDEN + lo:FFN_HIDDEN + lo + FFN_TILE])
            act_ref[rows, lo:lo + FFN_TILE] = (fa * jax.nn.sigmoid(fa) * fb).astype(BF16)
    for rows in halves:
        h2 = h1_ref[rows, :] + g2 * dot(act_ref[rows, :], wfo_ref[...])
        o_ref[0, rows, :] = (h2 * lax.rsqrt(jnp.mean(h2 * h2, axis=-1, keepdims=True) + NORM_EPS)) * og_ref[...]


def _post(x, a, y_t, u, ga, gs, mod3, d_skip, ffn_g, fin_g, perm_t, wglu, wba, wbs, wout, wfi, wfo):
    tm = 512
    n = CHUNK * CHUNK
    tok = lambda width: pl.BlockSpec((1, tm, width), lambda b, i: (b, i, 0))
    const = lambda r, c: pl.BlockSpec((r, c), lambda b, i: (0, 0), pipeline_mode=pl.Buffered(1))
    return pl.pallas_call(
        _post_kernel,
        grid=(BATCH, SEQ // tm),
        in_specs=[tok(D_MODEL), tok(ATTN_WIDTH),
                  pl.BlockSpec((SSM_GROUPS, 1, tm // CHUNK, CHUNK_COLS), lambda b, i: (0, b, i, 0)),
                  tok(SSM_WIDTH), tok(D_MODEL), tok(D_MODEL),
                  pl.BlockSpec((1, 1, N_MOD * D_MODEL), lambda b, i: (b, 0, 0)),
                  const(1, SSM_WIDTH), const(1, D_MODEL), const(1, D_MODEL), const(n, n),
                  const(SSM_WIDTH, 2 * SSM_WIDTH), const(ATTN_WIDTH, D_MODEL),
                  const(SSM_WIDTH, D_MODEL), const(D_MODEL, D_MODEL),
                  const(D_MODEL, 2 * FFN_HIDDEN), const(FFN_HIDDEN, D_MODEL)],
        out_specs=tok(D_MODEL),
        out_shape=jax.ShapeDtypeStruct((BATCH, SEQ, D_MODEL), F32),
        scratch_shapes=[pltpu.VMEM((tm, D_MODEL), F32), pltpu.VMEM((tm, D_MODEL), BF16),
                        pltpu.VMEM((tm, FFN_HIDDEN), BF16), pltpu.VMEM((tm // n, n, SSM_WIDTH), F32)],
        compiler_params=pltpu.CompilerParams(
            dimension_semantics=("arbitrary", "arbitrary"), vmem_limit_bytes=VMEM_LIMIT),
        name="post",
    )(x, a, y_t, u, ga, gs, mod3, d_skip, ffn_g, fin_g, perm_t, wglu, wba, wbs, wout, wfi, wfo)


def kernel(x, c, ctx, c_ctx, w_mod, b_mod, attn_norm_g, ffn_norm_g, w_in, rel_pos_bias,
           ssm_lambda_re, ssm_lambda_im, ssm_log_dt, ssm_b_re, ssm_b_im, ssm_c_re, ssm_c_im, ssm_d,
           w_glu, w_branch_attn, w_branch_ssm, w_out, w_ffn_in, w_ffn_out, final_norm_g):
    assert x.shape == (BATCH, SEQ, D_MODEL) and w_mod.shape[0] == 1
    c_rows = jnp.concatenate(
        [c, c_ctx[None, :], jnp.zeros((MOD_ROWS - BATCH - 1, D_MODEL), F32)], axis=0)
    mod3 = _modulation(c_rows, w_mod[0], b_mod[0]).reshape(MOD_ROWS, 1, N_MOD * D_MODEL)

    col_scale = jnp.concatenate([jnp.full((ATTN_WIDTH,), HEAD_DIM ** -0.5 * LOG2E, F32),
                                 jnp.ones((IN_COLS - ATTN_WIDTH,), F32)])
    w_in_bf16 = (w_in[0] * col_scale[None, :]).astype(BF16)
    norm_g = attn_norm_g[0].reshape(1, D_MODEL)

    perm = _chunk_perm()
    q, k, v, u, u_t, ga, gs = _input_projection(x, mod3, norm_g, w_in_bf16, _rope_tables(), perm)
    kc, vc, uc_t = _context_projection(ctx, mod3, norm_g, w_in_bf16, perm)

    attn = _attention(q, k, v, kc, vc, _bias_tables(rel_pos_bias[0]))

    m_intra, m_state, m_out, a16 = _s5_matrices(
        ssm_lambda_re[0], ssm_lambda_im[0], ssm_log_dt[0], ssm_b_re[0], ssm_b_im[0],
        ssm_c_re[0], ssm_c_im[0])
    y_t = _s5_scan(u_t, uc_t, m_intra, m_state, m_out, a16)

    return _post(x, attn, y_t, u, ga, gs, mod3,
                 ssm_d[0].reshape(1, SSM_WIDTH), ffn_norm_g[0].reshape(1, D_MODEL),
                 final_norm_g.reshape(1, D_MODEL), perm,
                 w_glu[0].astype(BF16), w_branch_attn[0].astype(BF16), w_branch_ssm[0].astype(BF16),
                 w_out[0].astype(BF16), w_ffn_in[0].astype(BF16), w_ffn_out[0].astype(BF16))
```

```python
import functools
import math

import numpy as np
import jax
import jax.numpy as jnp
from jax import lax
from jax.experimental import pallas as pl
from jax.experimental.pallas import tpu as pltpu

F32 = jnp.float32
BF16 = jnp.bfloat16

D_MODEL = 1024
BATCH = 8
SEQ = 4096
GRID_W = 64
GRID_ROWS = SEQ // GRID_W
CTX_LEN = 256
N_HEADS = 8
HEAD_DIM = 64
ATTN_WIDTH = N_HEADS * HEAD_DIM
WIN_H = 8
WIN_W = 16
ROPE_BASE = 10000.0
SSM_WIDTH = 512
SSM_GROUP = 16
SSM_GROUPS = SSM_WIDTH // SSM_GROUP
SSM_STATE = 64
FFN_HIDDEN = 2816
IN_COLS = 3 * ATTN_WIDTH + SSM_WIDTH + 2 * D_MODEL
N_MOD = 6
NORM_EPS = 1e-6
NEG_BIG = -1e30
LOG2E = math.log2(math.e)

LANES = 128
CHUNK = 16
N_CHUNKS = SEQ // CHUNK
N_CTX_CHUNKS = CTX_LEN // CHUNK
CHUNK_COLS = CHUNK * SSM_GROUP
MOD_ROWS = 16
CTX_MOD_ROW = BATCH
VMEM_LIMIT = 56 * 1024 * 1024

Q_ROWS_PER_STEP = 4
Q_BLOCK = Q_ROWS_PER_STEP * GRID_W
K_ROWS_PER_STEP = Q_ROWS_PER_STEP + WIN_H
K_BLOCK = K_ROWS_PER_STEP * GRID_W
N_QBLOCKS = GRID_ROWS // Q_ROWS_PER_STEP
KEY_GROUP = Q_BLOCK
SOFTMAX_ROWS = 32


def _rms_modulate(x, g, shift, scale):
    xn = x * lax.rsqrt(jnp.mean(x * x, axis=-1, keepdims=True) + NORM_EPS)
    return (xn * g) * (1.0 + scale) + shift


def _block_transpose8(vs, lane):
    for shift in (64, 32, 16):
        keep = (lane & (2 * shift - 1)) < shift
        dist = shift // SSM_GROUP
        out = list(vs)
        for a in range(8):
            if a & dist:
                continue
            b = a + dist
            out[a] = jnp.where(keep, vs[a], pltpu.roll(vs[b], shift, 1))
            out[b] = jnp.where(keep, pltpu.roll(vs[a], LANES - shift, 1), vs[b])
        vs = out
    return vs


def _chunk_perm():
    n = CHUNK * CHUNK
    r = np.arange(n)
    m = np.zeros((n, n), np.float32)
    m[r, (r % CHUNK) * CHUNK + r // CHUNK] = 1.0
    return jnp.asarray(m, BF16)


def _store_chunk_layout(u16, perm_ref, out_ref, n_groups16, group0=0):
    lane = lax.broadcasted_iota(jnp.int32, (1, LANES), 1)
    n = CHUNK * CHUNK
    for hf in range(n_groups16):
        r = jnp.dot(perm_ref[...], u16[hf * n:(hf + 1) * n, :], preferred_element_type=F32)
        c0 = (group0 + hf) * CHUNK
        for v in range(SSM_WIDTH // LANES):
            for jh in range(2):
                vs = [r[CHUNK * (8 * jh + jp):CHUNK * (8 * jh + jp + 1), v * LANES:(v + 1) * LANES]
                      for jp in range(8)]
                outs = _block_transpose8(vs, lane)
                for gi in range(8):
                    out_ref[8 * v + gi, 0, c0:c0 + CHUNK, jh * LANES:(jh + 1) * LANES] = outs[gi].astype(BF16)


def _load_chunk_layout(yt_ref, perm_t_ref, r_ref, hf):
    lane = lax.broadcasted_iota(jnp.int32, (1, LANES), 1)
    for v in range(SSM_WIDTH // LANES):
        for jh in range(2):
            vs = [yt_ref[8 * v + gi, 0, hf * CHUNK:(hf + 1) * CHUNK,
                         jh * LANES:(jh + 1) * LANES].astype(F32) for gi in range(8)]
            outs = _block_transpose8(vs, lane)
            for jp in range(8):
                r_ref[hf, CHUNK * (8 * jh + jp):CHUNK * (8 * jh + jp + 1), v * LANES:(v + 1) * LANES] = outs[jp]
    return jnp.dot(perm_t_ref[...], r_ref[hf].astype(BF16), preferred_element_type=F32)


def _mod_kernel(c_ref, w_ref, b_ref, o_ref):
    c = c_ref[...]
    s = c * jax.nn.sigmoid(c)
    o_ref[...] = jnp.dot(s, w_ref[...], preferred_element_type=F32) + b_ref[...]


def _modulation(c_rows, w_mod, b_mod):
    n = N_MOD * D_MODEL
    tn = 1536
    return pl.pallas_call(
        _mod_kernel,
        grid=(n // tn,),
        in_specs=[pl.BlockSpec((MOD_ROWS, D_MODEL), lambda j: (0, 0)),
                  pl.BlockSpec((D_MODEL, tn), lambda j: (0, j)),
                  pl.BlockSpec((1, tn), lambda j: (0, j))],
        out_specs=pl.BlockSpec((MOD_ROWS, tn), lambda j: (0, j)),
        out_shape=jax.ShapeDtypeStruct((MOD_ROWS, n), F32),
        name="modulation",
    )(c_rows, w_mod, b_mod.reshape(1, n))


def _rope_tables():
    n_freq = HEAD_DIM // 4
    inv_freq = ROPE_BASE ** (-np.arange(n_freq, dtype=np.float64) / n_freq)
    t = np.arange(SEQ)
    lane = np.arange(LANES)
    d = lane % HEAD_DIM
    use_col = (d // (HEAD_DIM // 2)) == 1
    w = d % (HEAD_DIM // 2)
    first = w < n_freq
    pos = np.where(use_col[None, :], (t % GRID_W)[:, None], (t // GRID_W)[:, None]).astype(np.float64)
    ang = pos * inv_freq[w % n_freq][None, :]
    cos = np.cos(ang)
    sin = np.sin(ang)
    sin_a = np.where(first[None, :], -sin, 0.0)
    sin_b = np.where(first[None, :], 0.0, sin)
    return (jnp.asarray(cos, F32), jnp.asarray(sin_a, F32), jnp.asarray(sin_b, F32))


def _rope_store(r, cos, sin_a, sin_b, out_ref, row0, transposed=False):
    n_rows = r.shape[0]
    for j in range(ATTN_WIDTH // LANES):
        xs = r[:, j * LANES:(j + 1) * LANES]
        rot = (xs * cos + pltpu.roll(xs, LANES - HEAD_DIM // 4, 1) * sin_a
               + pltpu.roll(xs, HEAD_DIM // 4, 1) * sin_b)
        if transposed:
            rot_t = rot.T.astype(BF16)
            for gi in range(n_rows // KEY_GROUP):
                out_ref[0, row0 // KEY_GROUP + gi, j * LANES:(j + 1) * LANES, :] = (
                    rot_t[:, gi * KEY_GROUP:(gi + 1) * KEY_GROUP])
        else:
            out_ref[0, row0:row0 + n_rows, j * LANES:(j + 1) * LANES] = rot.astype(BF16)


INPROJ_SUBTILE = 512


def _inproj_kernel(x_ref, mod_ref, g_ref, w_ref, cos_ref, sa_ref, sb_ref, perm_ref,
                   q_ref, k_ref, v_ref, u_ref, ut_ref, ga_ref, gs_ref):
    shift = mod_ref[0, :, 0:D_MODEL]
    scale = mod_ref[0, :, D_MODEL:2 * D_MODEL]
    aw = ATTN_WIDTH
    dot = functools.partial(jnp.dot, preferred_element_type=F32)
    for row0 in range(0, x_ref.shape[1], INPROJ_SUBTILE):
        rows = slice(row0, row0 + INPROJ_SUBTILE)
        nb = _rms_modulate(x_ref[0, rows, :], g_ref[...], shift, scale).astype(BF16)
        cos = cos_ref[rows, :]
        sin_a = sa_ref[rows, :]
        sin_b = sb_ref[rows, :]
        _rope_store(dot(nb, w_ref[:, 0:aw]), cos, sin_a, sin_b, q_ref, row0)
        _rope_store(dot(nb, w_ref[:, aw:2 * aw]), cos, sin_a, sin_b, k_ref, row0, transposed=True)
        v_ref[0, rows, :] = dot(nb, w_ref[:, 2 * aw:3 * aw]).astype(BF16)
        c0 = 3 * aw
        u16 = dot(nb, w_ref[:, c0:c0 + SSM_WIDTH]).astype(BF16)
        u_ref[0, rows, :] = u16
        n16 = CHUNK * CHUNK
        _store_chunk_layout(u16, perm_ref, ut_ref, INPROJ_SUBTILE // n16, row0 // n16)
        c1 = c0 + SSM_WIDTH
        ga_ref[0, rows, :] = jax.nn.sigmoid(dot(nb, w_ref[:, c1:c1 + D_MODEL])).astype(BF16)
        c2 = c1 + D_MODEL
        gs_ref[0, rows, :] = jax.nn.sigmoid(dot(nb, w_ref[:, c2:c2 + D_MODEL])).astype(BF16)


def _input_projection(x, mod3, norm_g, w_in_bf16, rope, perm):
    tm = 2 * INPROJ_SUBTILE
    n = CHUNK * CHUNK
    cos, sin_a, sin_b = rope
    tok = lambda width: pl.BlockSpec((1, tm, width), lambda i, b: (b, i, 0))
    tab = pl.BlockSpec((tm, LANES), lambda i, b: (i, 0))
    out = lambda width: jax.ShapeDtypeStruct((BATCH, SEQ, width), BF16)
    return pl.pallas_call(
        _inproj_kernel,
        grid=(SEQ // tm, BATCH),
        in_specs=[tok(D_MODEL),
                  pl.BlockSpec((1, 1, N_MOD * D_MODEL), lambda i, b: (b, 0, 0)),
                  pl.BlockSpec((1, D_MODEL), lambda i, b: (0, 0)),
                  pl.BlockSpec((D_MODEL, IN_COLS), lambda i, b: (0, 0), pipeline_mode=pl.Buffered(1)),
                  tab, tab, tab,
                  pl.BlockSpec((n, n), lambda i, b: (0, 0), pipeline_mode=pl.Buffered(1))],
        out_specs=[tok(ATTN_WIDTH),
                   pl.BlockSpec((1, tm // KEY_GROUP, ATTN_WIDTH, KEY_GROUP), lambda i, b: (b, i, 0, 0)),
                   tok(ATTN_WIDTH), tok(SSM_WIDTH),
                   pl.BlockSpec((SSM_GROUPS, 1, tm // CHUNK, CHUNK_COLS), lambda i, b: (0, b, i, 0)),
                   tok(D_MODEL), tok(D_MODEL)],
        out_shape=[out(ATTN_WIDTH),
                   jax.ShapeDtypeStruct((BATCH, SEQ // KEY_GROUP, ATTN_WIDTH, KEY_GROUP), BF16),
                   out(ATTN_WIDTH), out(SSM_WIDTH),
                   jax.ShapeDtypeStruct((SSM_GROUPS, BATCH, N_CHUNKS, CHUNK_COLS), BF16),
                   out(D_MODEL), out(D_MODEL)],
        compiler_params=pltpu.CompilerParams(
            dimension_semantics=("arbitrary", "arbitrary"), vmem_limit_bytes=VMEM_LIMIT),
        name="input_projection",
    )(x, mod3, norm_g, w_in_bf16, cos, sin_a, sin_b, perm)


def _ctx_proj_kernel(x_ref, mod_ref, g_ref, wk_ref, wv_ref, wu_ref, perm_ref, k_ref, v_ref, ut_ref):
    x = x_ref[0]
    shift = mod_ref[0, :, 0:D_MODEL]
    scale = mod_ref[0, :, D_MODEL:2 * D_MODEL]
    nb = _rms_modulate(x, g_ref[...], shift, scale).astype(BF16)
    k_ref[0] = jnp.dot(nb, wk_ref[...], preferred_element_type=F32).T.astype(BF16)
    v_ref[0] = jnp.dot(nb, wv_ref[...], preferred_element_type=F32).astype(BF16)
    u16 = jnp.dot(nb, wu_ref[...], preferred_element_type=F32).astype(BF16)
    _store_chunk_layout(u16, perm_ref, ut_ref, 1)


def _context_projection(ctx, mod3, norm_g, w_in_bf16, perm):
    n = CHUNK * CHUNK
    aw = ATTN_WIDTH
    tok = lambda width: pl.BlockSpec((1, CTX_LEN, width), lambda b: (b, 0, 0))
    wcol = lambda j: pl.BlockSpec((D_MODEL, aw), lambda b: (0, j))
    out = jax.ShapeDtypeStruct((BATCH, CTX_LEN, aw), BF16)
    return pl.pallas_call(
        _ctx_proj_kernel,
        grid=(BATCH,),
        in_specs=[tok(D_MODEL),
                  pl.BlockSpec((1, 1, N_MOD * D_MODEL), lambda b: (CTX_MOD_ROW, 0, 0)),
                  pl.BlockSpec((1, D_MODEL), lambda b: (0, 0)),
                  wcol(1), wcol(2), wcol(3),
                  pl.BlockSpec((n, n), lambda b: (0, 0))],
        out_specs=[pl.BlockSpec((1, aw, CTX_LEN), lambda b: (b, 0, 0)), tok(aw),
                   pl.BlockSpec((SSM_GROUPS, 1, N_CTX_CHUNKS, CHUNK_COLS), lambda b: (0, b, 0, 0))],
        out_shape=[jax.ShapeDtypeStruct((BATCH, aw, CTX_LEN), BF16), out,
                   jax.ShapeDtypeStruct((SSM_GROUPS, BATCH, N_CTX_CHUNKS, CHUNK_COLS), BF16)],
        name="context_projection",
    )(ctx, mod3, norm_g, w_in_bf16, w_in_bf16, w_in_bf16, perm)


def _window_start_rows(r):
    return min(max(r - WIN_H // 2, 0), GRID_ROWS - WIN_H)


def _key_block_row(jb):
    return min(max(Q_ROWS_PER_STEP * jb - WIN_H // 2, 0), GRID_ROWS - K_ROWS_PER_STEP)


def _window_geometry(jb):
    key_row0 = _key_block_row(jb)
    offs, deltas = [], []
    for i in range(Q_ROWS_PER_STEP):
        r = Q_ROWS_PER_STEP * jb + i
        rs = _window_start_rows(r)
        offs.append(rs - key_row0)
        deltas.append(r - rs)
    return offs, deltas


def _attn_kernel(q_ref, k_ref, v_ref, kc_ref, vc_ref, bias_ref, o_ref,
                 s_ref, sc_ref, p_ref, pc_ref, l_ref, m_ref, acc_ref):
    lane = lax.broadcasted_iota(jnp.int32, (1, LANES), 1)
    left = lane < HEAD_DIM
    n_ktiles = K_BLOCK // LANES
    last = N_QBLOCKS - 1
    max_key_start = (GRID_ROWS - K_ROWS_PER_STEP) * GRID_W

    def scores(e, q_start, key_start):
        q2 = q_ref[0, pl.ds(q_start, Q_BLOCK), :]
        qm = jnp.where(left if e == 0 else jnp.logical_not(left), q2, jnp.zeros_like(q2))
        group0 = key_start // KEY_GROUP
        for gi in range(K_BLOCK // KEY_GROUP):
            s_ref[e, :, gi * KEY_GROUP:(gi + 1) * KEY_GROUP] = jnp.dot(
                qm, k_ref[0, group0 + gi], preferred_element_type=F32)
        sc_ref[e] = jnp.dot(qm, kc_ref[0], preferred_element_type=F32)

    def softmax(e, offs, deltas):
        def geometry(sub):
            i = sub * SOFTMAX_ROWS // GRID_W
            qcols = slice(sub * SOFTMAX_ROWS % GRID_W, sub * SOFTMAX_ROWS % GRID_W + SOFTMAX_ROWS)
            rows = slice(sub * SOFTMAX_ROWS, (sub + 1) * SOFTMAX_ROWS)
            par = offs[i] % 2
            return rows, qcols, par, offs[i] // 2, WIN_H // 2 + par, deltas[i]

        def tile(rows, qcols, par, t0, n_tiles, delta, xt):
            t = t0 + xt
            dr0 = 2 * xt - par - delta
            st = s_ref[e, rows, t * LANES:(t + 1) * LANES] + bias_ref[0, e, dr0 + WIN_H, qcols, :]
            if par and xt == 0:
                st = jnp.where(left, NEG_BIG, st)
            if par and xt == n_tiles - 1:
                st = jnp.where(left, st, NEG_BIG)
            return st

        n_sub = Q_BLOCK // SOFTMAX_ROWS
        for sub in range(n_sub):
            geo = geometry(sub)
            rows, n_tiles = geo[0], geo[4]
            mt = jnp.maximum(sc_ref[e, rows, 0:LANES], sc_ref[e, rows, LANES:2 * LANES])
            for xt in range(n_tiles):
                mt = jnp.maximum(mt, tile(*geo, xt))
            m_ref[e, rows, :] = jnp.broadcast_to(jnp.max(mt, axis=1, keepdims=True), (SOFTMAX_ROWS, LANES))
        for sub in range(n_sub):
            geo = geometry(sub)
            rows, t0, n_tiles = geo[0], geo[3], geo[4]
            m = m_ref[e, rows, :]
            lt = None
            for xt in range(n_tiles):
                t = t0 + xt
                pt = jnp.exp2(tile(*geo, xt) - m)
                lt = pt if lt is None else lt + pt
                p_ref[e, rows, t * LANES:(t + 1) * LANES] = pt.astype(BF16)
            for t in range(n_ktiles):
                if not (t0 <= t < t0 + n_tiles):
                    p_ref[e, rows, t * LANES:(t + 1) * LANES] = jnp.zeros((SOFTMAX_ROWS, LANES), BF16)
            for ci in range(2):
                pt = jnp.exp2(sc_ref[e, rows, ci * LANES:(ci + 1) * LANES] - m)
                lt = lt + pt
                pc_ref[e, rows, ci * LANES:(ci + 1) * LANES] = pt.astype(BF16)
            l = jnp.sum(lt, axis=1, keepdims=True)
            l_ref[e, rows, :] = jnp.broadcast_to(1.0 / l, (SOFTMAX_ROWS, LANES))

    def values(e, key_start):
        vblk = v_ref[0, pl.ds(key_start, K_BLOCK), :]
        return (jnp.dot(p_ref[e], vblk, preferred_element_type=F32)
                + jnp.dot(pc_ref[e], vc_ref[0], preferred_element_type=F32)) * l_ref[e]

    def block(jb_static, q_start, key_start, next_starts):
        offs, deltas = _window_geometry(jb_static)
        scores(1, q_start, key_start)
        softmax(0, offs, deltas)
        acc_ref[...] = values(0, key_start)
        if next_starts is not None:
            scores(0, *next_starts)
        softmax(1, offs, deltas)
        o1 = values(1, key_start)
        o_ref[0, pl.ds(q_start, Q_BLOCK), :] = jnp.where(left, acc_ref[...], o1).astype(BF16)

    def starts(jb):
        q_start = pl.multiple_of(jb * Q_BLOCK, Q_BLOCK)
        key_start = jnp.clip((jb * Q_ROWS_PER_STEP - WIN_H // 2) * GRID_W, 0, max_key_start)
        return q_start, pl.multiple_of(key_start, Q_BLOCK)

    scores(0, 0, 0)
    block(0, 0, 0, (Q_BLOCK, 0))

    blocks_per_trip = 2

    def interior(trip, carry):
        jb = 1 + blocks_per_trip * trip
        for d in range(blocks_per_trip):
            q_start, key_start = starts(jb + d)
            block(1, q_start, key_start, starts(jb + d + 1))
        return carry

    assert (last - 1) % blocks_per_trip == 0
    lax.fori_loop(0, (last - 1) // blocks_per_trip, interior, 0)
    block(last, last * Q_BLOCK, max_key_start, None)


def _bias_tables(rpb):
    qcol = np.arange(GRID_W)
    kcol = np.arange(GRID_W)
    col_start = np.clip(qcol - WIN_W // 2, 0, GRID_W - WIN_W)
    in_win = (kcol[None, :] >= col_start[:, None]) & (kcol[None, :] < col_start[:, None] + WIN_W)
    dc_idx = np.clip(kcol[None, :] - qcol[:, None], -(WIN_W - 1), WIN_W - 1) + WIN_W - 1
    sel = (np.arange(2 * WIN_W - 1)[:, None, None] == dc_idx[None]).astype(np.float32)
    toe = jnp.einsum('hdt,tck->hdck', rpb * LOG2E, jnp.asarray(sel),
                     precision=lax.Precision.HIGHEST)
    toe = jnp.where(in_win[None, None], toe, NEG_BIG)
    neg = jnp.full((N_HEADS, 1, GRID_W, GRID_W), NEG_BIG, F32)
    ext = jnp.concatenate([neg, toe, neg], axis=1)
    pair = jnp.concatenate([ext[:, 0:16], ext[:, 1:17]], axis=-1)
    return pair.reshape(N_HEADS // 2, 2, 16, GRID_W, LANES)


def _attention(q, k, v, kc, vc, bias):
    kspec = pl.BlockSpec((1, SEQ, LANES), lambda hp, b: (b, 0, hp))
    ktspec = pl.BlockSpec((1, SEQ // KEY_GROUP, LANES, KEY_GROUP), lambda hp, b: (b, 0, hp, 0))
    cspec = pl.BlockSpec((1, CTX_LEN, LANES), lambda hp, b: (b, 0, hp))
    ctspec = pl.BlockSpec((1, LANES, CTX_LEN), lambda hp, b: (b, hp, 0))
    bspec = pl.BlockSpec((1, 2, 16, GRID_W, LANES), lambda hp, b: (hp, 0, 0, 0, 0))
    return pl.pallas_call(
        _attn_kernel,
        grid=(N_HEADS // 2, BATCH),
        in_specs=[kspec, ktspec, kspec, ctspec, cspec, bspec],
        out_specs=kspec,
        out_shape=jax.ShapeDtypeStruct((BATCH, SEQ, ATTN_WIDTH), BF16),
        scratch_shapes=[pltpu.VMEM((2, Q_BLOCK, K_BLOCK), F32),
                        pltpu.VMEM((2, Q_BLOCK, CTX_LEN), F32),
                        pltpu.VMEM((2, Q_BLOCK, K_BLOCK), BF16),
                        pltpu.VMEM((2, Q_BLOCK, CTX_LEN), BF16),
                        pltpu.VMEM((2, Q_BLOCK, LANES), F32),
                        pltpu.VMEM((2, Q_BLOCK, LANES), F32),
                        pltpu.VMEM((Q_BLOCK, LANES), F32)],
        compiler_params=pltpu.CompilerParams(
            dimension_semantics=("arbitrary", "arbitrary"),
            vmem_limit_bytes=VMEM_LIMIT),
        name="attention",
    )(q, k, v, kc, vc, bias)


def _rot256(a, b, s, lane):
    s %= 2 * LANES
    if s >= LANES:
        a, b, s = b, a, s - LANES
    if s == 0:
        return a, b
    ra = pltpu.roll(a, s, 1)
    rb = pltpu.roll(b, s, 1)
    keep = lane >= s
    return jnp.where(keep, ra, rb), jnp.where(keep, rb, ra)


def _s5_prep_kernel(par_ref, b_ref, c_ref, t16_ref, e_ref, mi_ref, ms_ref, mo_ref, a_ref):
    dot = functools.partial(jnp.dot, preferred_element_type=F32)

    def split2(x):
        hi = x.astype(BF16)
        return hi, (x - hi.astype(F32)).astype(BF16)

    def split3(x):
        hi = x.astype(BF16)
        r1 = x - hi.astype(F32)
        mid = r1.astype(BF16)
        return hi, mid, (r1 - mid.astype(F32)).astype(BF16)

    def pick(x, onehot):
        hi, mid, lo = split3(x)
        return dot(hi, onehot) + dot(mid, onehot) + dot(lo, onehot)

    def dot_f32(a, b):
        ah, al = split2(a)
        bh, bl = split2(b)
        return dot(jnp.concatenate([ah, ah, al], axis=1), jnp.concatenate([bh, bl, bh], axis=0))

    ns = SSM_STATE
    par = par_ref[0].T
    lam_re, lam_im = par[:, 0:1], par[:, 1:2]
    dt = jnp.exp(par[:, 2:3])
    lane = lax.broadcasted_iota(jnp.int32, (1, LANES), 1)
    kf = lane.astype(F32)
    mag = jnp.exp((lam_re * dt) * kf)
    ang = (lam_im * dt) * kf
    pw_re, pw_im = mag * jnp.cos(ang), mag * jnp.sin(ang)
    lb_re = jnp.sum(jnp.where(lane == 1, pw_re, 0.0), axis=1, keepdims=True)
    lb_im = jnp.sum(jnp.where(lane == 1, pw_im, 0.0), axis=1, keepdims=True)
    den = lam_re * lam_re + lam_im * lam_im
    nr, ni = lb_re - 1.0, lb_im
    f_re = (nr * lam_re + ni * lam_im) / den
    f_im = (ni * lam_re - nr * lam_im) / den
    b_re, b_im = b_ref[0, :, 0:SSM_GROUP], b_ref[0, :, SSM_GROUP:2 * SSM_GROUP]
    bb_re = f_re * b_re - f_im * b_im
    bb_im = f_re * b_im + f_im * b_re
    t16 = t16_ref[...]
    bbt_re, bbt_im = pick(bb_re, t16), pick(bb_im, t16)
    ct_re = pick(c_ref[0, :, 0:SSM_GROUP], t16)
    ct_im = pick(c_ref[0, :, SSM_GROUP:2 * SSM_GROUP], t16)
    pw_at = lambda x: (pick(pw_re, e_ref[x]), pick(pw_im, e_ref[x]))
    id_re, id_im = pw_at(0)
    rev_re, rev_im = pw_at(1)
    p1_re, p1_im = pw_at(2)
    r16_re, r16_im = pw_at(3)
    f, b = slice(0, ns), slice(ns, 2 * ns)
    cmul = lambda ar, ai, br, bi: (ar * br - ai * bi, ar * bi + ai * br)

    sf_re, sf_im = cmul(rev_re[f], rev_im[f], bbt_re[f], bbt_im[f])
    sb_re, sb_im = cmul(id_re[b], id_im[b], bbt_re[b], bbt_im[b])
    ms_ref[0] = jnp.concatenate([sf_re, sb_re, sf_im, sb_im], axis=0).T.astype(BF16)

    of_re, of_im = cmul(p1_re[f], p1_im[f], ct_re[f], ct_im[f])
    ob_re, ob_im = cmul(r16_re[b], r16_im[b], ct_re[b], ct_im[b])
    mo_ref[0] = jnp.concatenate([of_re, ob_re, -of_im, -ob_im], axis=0).astype(BF16)

    xf_re, xf_im = cmul(id_re[f], id_im[f], ct_re[f], ct_im[f])
    xb_re, xb_im = cmul(rev_re[b], rev_im[b], ct_re[b], ct_im[b])
    btf = jnp.concatenate([bbt_re[f], bbt_im[f]], axis=0).T
    btb = jnp.concatenate([bbt_re[b], bbt_im[b]], axis=0).T
    g_f = dot_f32(btf, jnp.concatenate([xf_re, -xf_im], axis=0))
    g_b = dot_f32(btb, jnp.concatenate([xb_re, -xb_im], axis=0))
    for j in range(CHUNK):
        rows = slice(SSM_GROUP * j, SSM_GROUP * (j + 1))
        lo_col, hi_col = SSM_GROUP * j, SSM_GROUP * (j + 1)
        f_lo, f_hi = _rot256(g_f[rows, :LANES], g_f[rows, LANES:], lo_col, lane)
        b_lo, b_hi = _rot256(g_b[rows, :LANES], g_b[rows, LANES:], -SSM_GROUP * (CHUNK - 1 - j), lane)
        lo = jnp.where(lane >= lo_col, f_lo, 0.0) + jnp.where(lane < hi_col, b_lo, 0.0)
        up = jnp.where(lane + LANES >= lo_col, f_hi, 0.0) + jnp.where(lane + LANES < hi_col, b_hi, 0.0)
        mi_ref[0, rows, 0:LANES] = lo.astype(BF16)
        mi_ref[0, rows, LANES:2 * LANES] = up.astype(BF16)

    a16 = jnp.concatenate([pw_re.T[CHUNK:CHUNK + 1, :], pw_im.T[CHUNK:CHUNK + 1, :]], axis=1)
    a_ref[0] = jnp.broadcast_to(a16, (8, 2 * LANES))


def _s5_matrices(lam_re, lam_im, log_dt, b_re, b_im, c_re, c_im):
    g, p2 = SSM_GROUPS, 2 * SSM_STATE
    both = lambda a: jnp.transpose(a, (1, 0, 2)).reshape(g, p2)
    log_dt_rows = jnp.repeat(jnp.transpose(log_dt), SSM_STATE, axis=1)
    par = jnp.stack([both(lam_re), both(lam_im), log_dt_rows], axis=1)
    par = jnp.concatenate([par, jnp.zeros((g, LANES - 3, p2), F32)], axis=1)
    rows_b = lambda a: jnp.transpose(a, (1, 0, 2, 3)).reshape(g, p2, SSM_GROUP)
    rows_c = lambda a: jnp.transpose(a, (1, 0, 3, 2)).reshape(g, p2, SSM_GROUP)
    b_cat = jnp.concatenate([rows_b(b_re), rows_b(b_im)], axis=-1)
    c_cat = jnp.concatenate([rows_c(c_re), rows_c(c_im)], axis=-1)

    col = np.arange(CHUNK_COLS)
    tile16 = (col[None, :] % SSM_GROUP == np.arange(SSM_GROUP)[:, None]).astype(np.float32)
    pos = col // SSM_GROUP
    k_idx = np.arange(LANES)[:, None]
    expand = np.stack([k_idx == pos[None, :], k_idx == (CHUNK - 1 - pos)[None, :],
                       k_idx == (pos + 1)[None, :], k_idx == (CHUNK - pos)[None, :]]).astype(np.float32)

    mat = lambda: pl.BlockSpec((1, CHUNK_COLS, CHUNK_COLS), lambda i: (i, 0, 0))
    mat_shape = jax.ShapeDtypeStruct((g, CHUNK_COLS, CHUNK_COLS), BF16)
    return pl.pallas_call(
        _s5_prep_kernel,
        grid=(g,),
        in_specs=[pl.BlockSpec((1, LANES, p2), lambda i: (i, 0, 0)),
                  pl.BlockSpec((1, p2, 2 * SSM_GROUP), lambda i: (i, 0, 0)),
                  pl.BlockSpec((1, p2, 2 * SSM_GROUP), lambda i: (i, 0, 0)),
                  pl.BlockSpec((SSM_GROUP, CHUNK_COLS), lambda i: (0, 0)),
                  pl.BlockSpec((4, LANES, CHUNK_COLS), lambda i: (0, 0, 0))],
        out_specs=[mat(), mat(), mat(), pl.BlockSpec((1, 8, 2 * LANES), lambda i: (i, 0, 0))],
        out_shape=[mat_shape, mat_shape, mat_shape, jax.ShapeDtypeStruct((g, 8, 2 * LANES), F32)],
        name="s5_prep",
    )(par, b_cat, c_cat, jnp.asarray(tile16, BF16), jnp.asarray(expand, BF16))


def _s5_kernel(ul_ref, uc_ref, ms_ref, mi_ref, mo_ref, a_ref, y_ref, s_ref, sc_ref, hp_ref, *, gb):
    for gi in range(gb):
        for b in range(BATCH):
            sb = jnp.dot(ul_ref[gi, b], ms_ref[gi], preferred_element_type=F32)
            s_ref[gi, 0, pl.ds(b, N_CHUNKS, stride=BATCH), :] = sb[:, :LANES]
            s_ref[gi, 1, pl.ds(b, N_CHUNKS, stride=BATCH), :] = sb[:, LANES:]
            cb = jnp.dot(uc_ref[gi, b], ms_ref[gi], preferred_element_type=F32)
            sc_ref[gi, 0, pl.ds(b, N_CTX_CHUNKS, stride=BATCH), :] = cb[:, :LANES]
            sc_ref[gi, 1, pl.ds(b, N_CTX_CHUNKS, stride=BATCH), :] = cb[:, LANES:]
    lane = lax.broadcasted_iota(jnp.int32, (BATCH, LANES), 1)
    fwd = lane < SSM_STATE
    half = SSM_STATE

    def advance(gi, h_re, h_im, row_f, row_b, src):
        s_re = jnp.where(fwd, src[gi, 0, pl.ds(row_f, BATCH), :], src[gi, 0, pl.ds(row_b, BATCH), :])
        s_im = jnp.where(fwd, src[gi, 1, pl.ds(row_f, BATCH), :], src[gi, 1, pl.ds(row_b, BATCH), :])
        a_re = a_ref[gi, :, 0:LANES]
        a_im = a_ref[gi, :, LANES:2 * LANES]
        n_re = a_re * h_re - a_im * h_im + s_re
        n_im = a_re * h_im + a_im * h_re + s_im
        return n_re, n_im

    def ctx_step(t, carry):
        row_f = pl.multiple_of(t * BATCH, BATCH)
        row_b = pl.multiple_of((N_CTX_CHUNKS - 1 - t) * BATCH, BATCH)
        return tuple(advance(gi, carry[gi][0], carry[gi][1], row_f, row_b, sc_ref) for gi in range(gb))

    def lat_step(t, carry):
        row_f = pl.multiple_of(t * BATCH, BATCH)
        row_b = pl.multiple_of((N_CHUNKS - 1 - t) * BATCH, BATCH)
        out = []
        for gi in range(gb):
            h_re, h_im = carry[gi]
            hp_ref[gi, 0, pl.ds(row_f, BATCH), 0:half] = h_re[:, 0:half]
            hp_ref[gi, 0, pl.ds(row_b, BATCH), half:2 * half] = h_re[:, half:]
            hp_ref[gi, 1, pl.ds(row_f, BATCH), 0:half] = h_im[:, 0:half]
            hp_ref[gi, 1, pl.ds(row_b, BATCH), half:2 * half] = h_im[:, half:]
            out.append(advance(gi, h_re, h_im, row_f, row_b, s_ref))
        return tuple(out)

    zero = jnp.zeros((BATCH, LANES), F32)
    carry = tuple((zero, zero) for _ in range(gb))
    carry = lax.fori_loop(0, N_CTX_CHUNKS, ctx_step, carry)
    lax.fori_loop(0, N_CHUNKS, lat_step, carry)
    for gi in range(gb):
        for b in range(BATCH):
            hb_re = hp_ref[gi, 0, pl.ds(b, N_CHUNKS, stride=BATCH), :].astype(BF16)
            hb_im = hp_ref[gi, 1, pl.ds(b, N_CHUNKS, stride=BATCH), :].astype(BF16)
            y = (jnp.dot(ul_ref[gi, b], mi_ref[gi], preferred_element_type=F32)
                 + jnp.dot(jnp.concatenate([hb_re, hb_im], axis=1), mo_ref[gi], preferred_element_type=F32))
            y_ref[gi, b] = y.astype(BF16)


def _s5_scan(u_lat_t, u_ctx_t, m_intra, m_state, m_out, a16):
    gb = 2
    rows = N_CHUNKS * BATCH
    crows = N_CTX_CHUNKS * BATCH
    grp = lambda r, c: pl.BlockSpec((gb, r, c), lambda g: (g, 0, 0))
    tok = lambda n: pl.BlockSpec((gb, BATCH, n, CHUNK_COLS), lambda g: (g, 0, 0, 0))
    return pl.pallas_call(
        functools.partial(_s5_kernel, gb=gb),
        grid=(SSM_GROUPS // gb,),
        in_specs=[tok(N_CHUNKS), tok(N_CTX_CHUNKS), grp(CHUNK_COLS, CHUNK_COLS),
                  grp(CHUNK_COLS, CHUNK_COLS), grp(CHUNK_COLS, CHUNK_COLS), grp(8, 2 * LANES)],
        out_specs=tok(N_CHUNKS),
        out_shape=jax.ShapeDtypeStruct((SSM_GROUPS, BATCH, N_CHUNKS, CHUNK_COLS), BF16),
        scratch_shapes=[pltpu.VMEM((gb, 2, rows, LANES), F32),
                        pltpu.VMEM((gb, 2, crows, LANES), F32),
                        pltpu.VMEM((gb, 2, rows, LANES), F32)],
        compiler_params=pltpu.CompilerParams(
            dimension_semantics=("arbitrary",), vmem_limit_bytes=VMEM_LIMIT),
        name="s5_scan",
    )(u_lat_t, u_ctx_t, m_state, m_intra, m_out, a16)


FFN_TILE = 256


def _post_kernel(x_ref, a_ref, yt_ref, u_ref, ga_ref, gs_ref, mod_ref, d_ref, fg_ref, og_ref, permt_ref,
                 wglu_ref, wba_ref, wbs_ref, wout_ref, wfi_ref, wfo_ref, o_ref, h1_ref, n2_ref, act_ref, r_ref):
    dm = D_MODEL
    g1 = mod_ref[0, :, 2 * dm:3 * dm]
    sh2 = mod_ref[0, :, 3 * dm:4 * dm]
    sc2 = mod_ref[0, :, 4 * dm:5 * dm]
    g2 = mod_ref[0, :, 5 * dm:6 * dm]
    half = CHUNK * CHUNK
    halves = [slice(h * half, (h + 1) * half) for h in range(x_ref.shape[1] // half)]
    dot = functools.partial(jnp.dot, preferred_element_type=F32)

    sp = []
    for h, rows in enumerate(halves):
        y = _load_chunk_layout(yt_ref, permt_ref, r_ref, h)
        sp.append(jax.nn.gelu(y + d_ref[...] * u_ref[0, rows, :].astype(F32)).astype(BF16))
    s = []
    for h, rows in enumerate(halves):
        vg = dot(sp[h], wglu_ref[...])
        s.append((vg[:, :SSM_WIDTH] * jax.nn.sigmoid(vg[:, SSM_WIDTH:])).astype(BF16))
    merged = []
    for h, rows in enumerate(halves):
        m = (ga_ref[0, rows, :].astype(F32) * dot(a_ref[0, rows, :], wba_ref[...])
             + gs_ref[0, rows, :].astype(F32) * dot(s[h], wbs_ref[...]))
        merged.append(m.astype(BF16))
    for h, rows in enumerate(halves):
        h1 = x_ref[0, rows, :] + g1 * dot(merged[h], wout_ref[...])
        h1_ref[rows, :] = h1
        n2_ref[rows, :] = _rms_modulate(h1, fg_ref[...], sh2, sc2).astype(BF16)
    for c in range(FFN_HIDDEN // FFN_TILE):
        lo = c * FFN_TILE
        for rows in halves:
            n2 = n2_ref[rows, :]
            fa = dot(n2, wfi_ref[:, lo:lo + FFN_TILE])
            fb = dot(n2, wfi_ref[:, FFN_HIDDEN + lo:FFN_HIDDEN + lo + FFN_TILE])
            act_ref[rows, lo:lo + FFN_TILE] = (fa * jax.nn.sigmoid(fa) * fb).astype(BF16)
    for rows in halves:
        h2 = h1_ref[rows, :] + g2 * dot(act_ref[rows, :], wfo_ref[...])
        o_ref[0, rows, :] = (h2 * lax.rsqrt(jnp.mean(h2 * h2, axis=-1, keepdims=True) + NORM_EPS)) * og_ref[...]


def _post(x, a, y_t, u, ga, gs, mod3, d_skip, ffn_g, fin_g, perm_t, wglu, wba, wbs, wout, wfi, wfo):
    tm = 512
    n = CHUNK * CHUNK
    tok = lambda width: pl.BlockSpec((1, tm, width), lambda b, i: (b, i, 0))
    const = lambda r, c: pl.BlockSpec((r, c), lambda b, i: (0, 0), pipeline_mode=pl.Buffered(1))
    return pl.pallas_call(
        _post_kernel,
        grid=(BATCH, SEQ // tm),
        in_specs=[tok(D_MODEL), tok(ATTN_WIDTH),
                  pl.BlockSpec((SSM_GROUPS, 1, tm // CHUNK, CHUNK_COLS), lambda b, i: (0, b, i, 0)),
                  tok(SSM_WIDTH), tok(D_MODEL), tok(D_MODEL),
                  pl.BlockSpec((1, 1, N_MOD * D_MODEL), lambda b, i: (b, 0, 0)),
                  const(1, SSM_WIDTH), const(1, D_MODEL), const(1, D_MODEL), const(n, n),
                  const(SSM_WIDTH, 2 * SSM_WIDTH), const(ATTN_WIDTH, D_MODEL),
                  const(SSM_WIDTH, D_MODEL), const(D_MODEL, D_MODEL),
                  const(D_MODEL, 2 * FFN_HIDDEN), const(FFN_HIDDEN, D_MODEL)],
        out_specs=tok(D_MODEL),
        out_shape=jax.ShapeDtypeStruct((BATCH, SEQ, D_MODEL), F32),
        scratch_shapes=[pltpu.VMEM((tm, D_MODEL), F32), pltpu.VMEM((tm, D_MODEL), BF16),
                        pltpu.VMEM((tm, FFN_HIDDEN), BF16), pltpu.VMEM((tm // n, n, SSM_WIDTH), F32)],
        compiler_params=pltpu.CompilerParams(
            dimension_semantics=("arbitrary", "arbitrary"), vmem_limit_bytes=VMEM_LIMIT),
        name="post",
    )(x, a, y_t, u, ga, gs, mod3, d_skip, ffn_g, fin_g, perm_t, wglu, wba, wbs, wout, wfi, wfo)


def kernel(x, c, ctx, c_ctx, w_mod, b_mod, attn_norm_g, ffn_norm_g, w_in, rel_pos_bias,
           ssm_lambda_re, ssm_lambda_im, ssm_log_dt, ssm_b_re, ssm_b_im, ssm_c_re, ssm_c_im, ssm_d,
           w_glu, w_branch_attn, w_branch_ssm, w_out, w_ffn_in, w_ffn_out, final_norm_g):
    assert x.shape == (BATCH, SEQ, D_MODEL) and w_mod.shape[0] == 1
    c_rows = jnp.concatenate(
        [c, c_ctx[None, :], jnp.zeros((MOD_ROWS - BATCH - 1, D_MODEL), F32)], axis=0)
    mod3 = _modulation(c_rows, w_mod[0], b_mod[0]).reshape(MOD_ROWS, 1, N_MOD * D_MODEL)

    col_scale = jnp.concatenate([jnp.full((ATTN_WIDTH,), HEAD_DIM ** -0.5 * LOG2E, F32),
                                 jnp.ones((IN_COLS - ATTN_WIDTH,), F32)])
    w_in_bf16 = (w_in[0] * col_scale[None, :]).astype(BF16)
    norm_g = attn_norm_g[0].reshape(1, D_MODEL)

    perm = _chunk_perm()
    q, k, v, u, u_t, ga, gs = _input_projection(x, mod3, norm_g, w_in_bf16, _rope_tables(), perm)
    kc, vc, uc_t = _context_projection(ctx, mod3, norm_g, w_in_bf16, perm)

    attn = _attention(q, k, v, kc, vc, _bias_tables(rel_pos_bias[0]))

    m_intra, m_state, m_out, a16 = _s5_matrices(
        ssm_lambda_re[0], ssm_lambda_im[0], ssm_log_dt[0], ssm_b_re[0], ssm_b_im[0],
        ssm_c_re[0], ssm_c_im[0])
    y_t = _s5_scan(u_t, uc_t, m_intra, m_state, m_out, a16)

    return _post(x, attn, y_t, u, ga, gs, mod3,
                 ssm_d[0].reshape(1, SSM_WIDTH), ffn_norm_g[0].reshape(1, D_MODEL),
                 final_norm_g.reshape(1, D_MODEL), perm,
                 w_glu[0].astype(BF16), w_branch_attn[0].astype(BF16), w_branch_ssm[0].astype(BF16),
                 w_out[0].astype(BF16), w_ffn_in[0].astype(BF16), w_ffn_out[0].astype(BF16))
```

```python
import functools
import math

import numpy as np
import jax
import jax.numpy as jnp
from jax import lax
from jax.experimental import pallas as pl
from jax.experimental.pallas import tpu as pltpu

F32 = jnp.float32
BF16 = jnp.bfloat16

D_MODEL = 1024
BATCH = 8
SEQ = 4096
GRID_W = 64
GRID_ROWS = SEQ // GRID_W
CTX_LEN = 256
N_HEADS = 8
HEAD_DIM = 64
ATTN_WIDTH = N_HEADS * HEAD_DIM
WIN_H = 8
WIN_W = 16
ROPE_BASE = 10000.0
SSM_WIDTH = 512
SSM_GROUP = 16
SSM_GROUPS = SSM_WIDTH // SSM_GROUP
SSM_STATE = 64
FFN_HIDDEN = 2816
IN_COLS = 3 * ATTN_WIDTH + SSM_WIDTH + 2 * D_MODEL
N_MOD = 6
NORM_EPS = 1e-6
NEG_BIG = -1e30
LOG2E = math.log2(math.e)

LANES = 128
CHUNK = 16
N_CHUNKS = SEQ // CHUNK
N_CTX_CHUNKS = CTX_LEN // CHUNK
CHUNK_COLS = CHUNK * SSM_GROUP
MOD_ROWS = 16
CTX_MOD_ROW = BATCH
VMEM_LIMIT = 56 * 1024 * 1024

Q_ROWS_PER_STEP = 4
Q_BLOCK = Q_ROWS_PER_STEP * GRID_W
K_ROWS_PER_STEP = Q_ROWS_PER_STEP + WIN_H
K_BLOCK = K_ROWS_PER_STEP * GRID_W
N_QBLOCKS = GRID_ROWS // Q_ROWS_PER_STEP
KEY_GROUP = Q_BLOCK
SOFTMAX_ROWS = 32


def _rms_modulate(x, g, shift, scale):
    xn = x * lax.rsqrt(jnp.mean(x * x, axis=-1, keepdims=True) + NORM_EPS)
    return (xn * g) * (1.0 + scale) + shift


def _block_transpose8(vs, lane):
    for shift in (64, 32, 16):
        keep = (lane & (2 * shift - 1)) < shift
        dist = shift // SSM_GROUP
        out = list(vs)
        for a in range(8):
            if a & dist:
                continue
            b = a + dist
            out[a] = jnp.where(keep, vs[a], pltpu.roll(vs[b], shift, 1))
            out[b] = jnp.where(keep, pltpu.roll(vs[a], LANES - shift, 1), vs[b])
        vs = out
    return vs


def _chunk_perm():
    n = CHUNK * CHUNK
    r = np.arange(n)
    m = np.zeros((n, n), np.float32)
    m[r, (r % CHUNK) * CHUNK + r // CHUNK] = 1.0
    return jnp.asarray(m, BF16)


def _store_chunk_layout(u16, perm_ref, out_ref, n_groups16, group0=0):
    lane = lax.broadcasted_iota(jnp.int32, (1, LANES), 1)
    n = CHUNK * CHUNK
    for hf in range(n_groups16):
        r = jnp.dot(perm_ref[...], u16[hf * n:(hf + 1) * n, :], preferred_element_type=F32)
        c0 = (group0 + hf) * CHUNK
        for v in range(SSM_WIDTH // LANES):
            for jh in range(2):
                vs = [r[CHUNK * (8 * jh + jp):CHUNK * (8 * jh + jp + 1), v * LANES:(v + 1) * LANES]
                      for jp in range(8)]
                outs = _block_transpose8(vs, lane)
                for gi in range(8):
                    out_ref[8 * v + gi, 0, c0:c0 + CHUNK, jh * LANES:(jh + 1) * LANES] = outs[gi].astype(BF16)


def _load_chunk_layout(yt_ref, perm_t_ref, r_ref, hf):
    lane = lax.broadcasted_iota(jnp.int32, (1, LANES), 1)
    for v in range(SSM_WIDTH // LANES):
        for jh in range(2):
            vs = [yt_ref[8 * v + gi, 0, hf * CHUNK:(hf + 1) * CHUNK,
                         jh * LANES:(jh + 1) * LANES].astype(F32) for gi in range(8)]
            outs = _block_transpose8(vs, lane)
            for jp in range(8):
                r_ref[hf, CHUNK * (8 * jh + jp):CHUNK * (8 * jh + jp + 1), v * LANES:(v + 1) * LANES] = outs[jp]
    return jnp.dot(perm_t_ref[...], r_ref[hf].astype(BF16), preferred_element_type=F32)


def _mod_kernel(c_ref, w_ref, b_ref, o_ref):
    c = c_ref[...]
    s = c * jax.nn.sigmoid(c)
    o_ref[...] = jnp.dot(s, w_ref[...], preferred_element_type=F32) + b_ref[...]


def _modulation(c_rows, w_mod, b_mod):
    n = N_MOD * D_MODEL
    tn = 1536
    return pl.pallas_call(
        _mod_kernel,
        grid=(n // tn,),
        in_specs=[pl.BlockSpec((MOD_ROWS, D_MODEL), lambda j: (0, 0)),
                  pl.BlockSpec((D_MODEL, tn), lambda j: (0, j)),
                  pl.BlockSpec((1, tn), lambda j: (0, j))],
        out_specs=pl.BlockSpec((MOD_ROWS, tn), lambda j: (0, j)),
        out_shape=jax.ShapeDtypeStruct((MOD_ROWS, n), F32),
        name="modulation",
    )(c_rows, w_mod, b_mod.reshape(1, n))


def _rope_tables():
    n_freq = HEAD_DIM // 4
    inv_freq = ROPE_BASE ** (-np.arange(n_freq, dtype=np.float64) / n_freq)
    t = np.arange(SEQ)
    lane = np.arange(LANES)
    d = lane % HEAD_DIM
    use_col = (d // (HEAD_DIM // 2)) == 1
    w = d % (HEAD_DIM // 2)
    first = w < n_freq
    pos = np.where(use_col[None, :], (t % GRID_W)[:, None], (t // GRID_W)[:, None]).astype(np.float64)
    ang = pos * inv_freq[w % n_freq][None, :]
    cos = np.cos(ang)
    sin = np.sin(ang)
    sin_a = np.where(first[None, :], -sin, 0.0)
    sin_b = np.where(first[None, :], 0.0, sin)
    return (jnp.asarray(cos, F32), jnp.asarray(sin_a, F32), jnp.asarray(sin_b, F32))


def _rope_store(r, cos, sin_a, sin_b, out_ref, row0, transposed=False):
    n_rows = r.shape[0]
    for j in range(ATTN_WIDTH // LANES):
        xs = r[:, j * LANES:(j + 1) * LANES]
        rot = (xs * cos + pltpu.roll(xs, LANES - HEAD_DIM // 4, 1) * sin_a
               + pltpu.roll(xs, HEAD_DIM // 4, 1) * sin_b)
        if transposed:
            rot_t = rot.T.astype(BF16)
            for gi in range(n_rows // KEY_GROUP):
                out_ref[0, row0 // KEY_GROUP + gi, j * LANES:(j + 1) * LANES, :] = (
                    rot_t[:, gi * KEY_GROUP:(gi + 1) * KEY_GROUP])
        else:
            out_ref[0, row0:row0 + n_rows, j * LANES:(j + 1) * LANES] = rot.astype(BF16)


INPROJ_SUBTILE = 512


def _inproj_kernel(x_ref, mod_ref, g_ref, w_ref, cos_ref, sa_ref, sb_ref, perm_ref,
                   q_ref, k_ref, v_ref, u_ref, ut_ref, ga_ref, gs_ref):
    shift = mod_ref[0, :, 0:D_MODEL]
    scale = mod_ref[0, :, D_MODEL:2 * D_MODEL]
    aw = ATTN_WIDTH
    dot = functools.partial(jnp.dot, preferred_element_type=F32)
    for row0 in range(0, x_ref.shape[1], INPROJ_SUBTILE):
        rows = slice(row0, row0 + INPROJ_SUBTILE)
        nb = _rms_modulate(x_ref[0, rows, :], g_ref[...], shift, scale).astype(BF16)
        cos = cos_ref[rows, :]
        sin_a = sa_ref[rows, :]
        sin_b = sb_ref[rows, :]
        _rope_store(dot(nb, w_ref[:, 0:aw]), cos, sin_a, sin_b, q_ref, row0)
        _rope_store(dot(nb, w_ref[:, aw:2 * aw]), cos, sin_a, sin_b, k_ref, row0, transposed=True)
        v_ref[0, rows, :] = dot(nb, w_ref[:, 2 * aw:3 * aw]).astype(BF16)
        c0 = 3 * aw
        u16 = dot(nb, w_ref[:, c0:c0 + SSM_WIDTH]).astype(BF16)
        u_ref[0, rows, :] = u16
        n16 = CHUNK * CHUNK
        _store_chunk_layout(u16, perm_ref, ut_ref, INPROJ_SUBTILE // n16, row0 // n16)
        c1 = c0 + SSM_WIDTH
        ga_ref[0, rows, :] = jax.nn.sigmoid(dot(nb, w_ref[:, c1:c1 + D_MODEL])).astype(BF16)
        c2 = c1 + D_MODEL
        gs_ref[0, rows, :] = jax.nn.sigmoid(dot(nb, w_ref[:, c2:c2 + D_MODEL])).astype(BF16)


def _input_projection(x, mod3, norm_g, w_in_bf16, rope, perm):
    tm = 2 * INPROJ_SUBTILE
    n = CHUNK * CHUNK
    cos, sin_a, sin_b = rope
    tok = lambda width: pl.BlockSpec((1, tm, width), lambda i, b: (b, i, 0))
    tab = pl.BlockSpec((tm, LANES), lambda i, b: (i, 0))
    out = lambda width: jax.ShapeDtypeStruct((BATCH, SEQ, width), BF16)
    return pl.pallas_call(
        _inproj_kernel,
        grid=(SEQ // tm, BATCH),
        in_specs=[tok(D_MODEL),
                  pl.BlockSpec((1, 1, N_MOD * D_MODEL), lambda i, b: (b, 0, 0)),
                  pl.BlockSpec((1, D_MODEL), lambda i, b: (0, 0)),
                  pl.BlockSpec((D_MODEL, IN_COLS), lambda i, b: (0, 0), pipeline_mode=pl.Buffered(1)),
                  tab, tab, tab,
                  pl.BlockSpec((n, n), lambda i, b: (0, 0), pipeline_mode=pl.Buffered(1))],
        out_specs=[tok(ATTN_WIDTH),
                   pl.BlockSpec((1, tm // KEY_GROUP, ATTN_WIDTH, KEY_GROUP), lambda i, b: (b, i, 0, 0)),
                   tok(ATTN_WIDTH), tok(SSM_WIDTH),
                   pl.BlockSpec((SSM_GROUPS, 1, tm // CHUNK, CHUNK_COLS), lambda i, b: (0, b, i, 0)),
                   tok(D_MODEL), tok(D_MODEL)],
        out_shape=[out(ATTN_WIDTH),
                   jax.ShapeDtypeStruct((BATCH, SEQ // KEY_GROUP, ATTN_WIDTH, KEY_GROUP), BF16),
                   out(ATTN_WIDTH), out(SSM_WIDTH),
                   jax.ShapeDtypeStruct((SSM_GROUPS, BATCH, N_CHUNKS, CHUNK_COLS), BF16),
                   out(D_MODEL), out(D_MODEL)],
        compiler_params=pltpu.CompilerParams(
            dimension_semantics=("arbitrary", "arbitrary"), vmem_limit_bytes=VMEM_LIMIT),
        name="input_projection",
    )(x, mod3, norm_g, w_in_bf16, cos, sin_a, sin_b, perm)


def _ctx_proj_kernel(x_ref, mod_ref, g_ref, wk_ref, wv_ref, wu_ref, perm_ref, k_ref, v_ref, ut_ref):
    x = x_ref[0]
    shift = mod_ref[0, :, 0:D_MODEL]
    scale = mod_ref[0, :, D_MODEL:2 * D_MODEL]
    nb = _rms_modulate(x, g_ref[...], shift, scale).astype(BF16)
    k_ref[0] = jnp.dot(nb, wk_ref[...], preferred_element_type=F32).T.astype(BF16)
    v_ref[0] = jnp.dot(nb, wv_ref[...], preferred_element_type=F32).astype(BF16)
    u16 = jnp.dot(nb, wu_ref[...], preferred_element_type=F32).astype(BF16)
    _store_chunk_layout(u16, perm_ref, ut_ref, 1)


def _context_projection(ctx, mod3, norm_g, w_in_bf16, perm):
    n = CHUNK * CHUNK
    aw = ATTN_WIDTH
    tok = lambda width: pl.BlockSpec((1, CTX_LEN, width), lambda b: (b, 0, 0))
    wcol = lambda j: pl.BlockSpec((D_MODEL, aw), lambda b: (0, j))
    out = jax.ShapeDtypeStruct((BATCH, CTX_LEN, aw), BF16)
    return pl.pallas_call(
        _ctx_proj_kernel,
        grid=(BATCH,),
        in_specs=[tok(D_MODEL),
                  pl.BlockSpec((1, 1, N_MOD * D_MODEL), lambda b: (CTX_MOD_ROW, 0, 0)),
                  pl.BlockSpec((1, D_MODEL), lambda b: (0, 0)),
                  wcol(1), wcol(2), wcol(3),
                  pl.BlockSpec((n, n), lambda b: (0, 0))],
        out_specs=[pl.BlockSpec((1, aw, CTX_LEN), lambda b: (b, 0, 0)), tok(aw),
                   pl.BlockSpec((SSM_GROUPS, 1, N_CTX_CHUNKS, CHUNK_COLS), lambda b: (0, b, 0, 0))],
        out_shape=[jax.ShapeDtypeStruct((BATCH, aw, CTX_LEN), BF16), out,
                   jax.ShapeDtypeStruct((SSM_GROUPS, BATCH, N_CTX_CHUNKS, CHUNK_COLS), BF16)],
        name="context_projection",
    )(ctx, mod3, norm_g, w_in_bf16, w_in_bf16, w_in_bf16, perm)


def _window_start_rows(r):
    return min(max(r - WIN_H // 2, 0), GRID_ROWS - WIN_H)


def _key_block_row(jb):
    return min(max(Q_ROWS_PER_STEP * jb - WIN_H // 2, 0), GRID_ROWS - K_ROWS_PER_STEP)


def _window_geometry(jb):
    key_row0 = _key_block_row(jb)
    offs, deltas = [], []
    for i in range(Q_ROWS_PER_STEP):
        r = Q_ROWS_PER_STEP * jb + i
        rs = _window_start_rows(r)
        offs.append(rs - key_row0)
        deltas.append(r - rs)
    return offs, deltas


def _attn_kernel(q_ref, k_ref, v_ref, kc_ref, vc_ref, bias_ref, o_ref,
                 s_ref, sc_ref, p_ref, pc_ref, l_ref, m_ref, acc_ref):
    lane = lax.broadcasted_iota(jnp.int32, (1, LANES), 1)
    left = lane < HEAD_DIM
    n_ktiles = K_BLOCK // LANES
    last = N_QBLOCKS - 1
    max_key_start = (GRID_ROWS - K_ROWS_PER_STEP) * GRID_W

    def scores(e, q_start, key_start):
        q2 = q_ref[0, pl.ds(q_start, Q_BLOCK), :]
        qm = jnp.where(left if e == 0 else jnp.logical_not(left), q2, jnp.zeros_like(q2))
        group0 = key_start // KEY_GROUP
        for gi in range(K_BLOCK // KEY_GROUP):
            s_ref[e, :, gi * KEY_GROUP:(gi + 1) * KEY_GROUP] = jnp.dot(
                qm, k_ref[0, group0 + gi], preferred_element_type=F32)
        sc_ref[e] = jnp.dot(qm, kc_ref[0], preferred_element_type=F32)

    def softmax(e, offs, deltas):
        def geometry(sub):
            i = sub * SOFTMAX_ROWS // GRID_W
            qcols = slice(sub * SOFTMAX_ROWS % GRID_W, sub * SOFTMAX_ROWS % GRID_W + SOFTMAX_ROWS)
            rows = slice(sub * SOFTMAX_ROWS, (sub + 1) * SOFTMAX_ROWS)
            par = offs[i] % 2
            return rows, qcols, par, offs[i] // 2, WIN_H // 2 + par, deltas[i]

        def tile(rows, qcols, par, t0, n_tiles, delta, xt):
            t = t0 + xt
            dr0 = 2 * xt - par - delta
            st = s_ref[e, rows, t * LANES:(t + 1) * LANES] + bias_ref[0, e, dr0 + WIN_H, qcols, :]
            if par and xt == 0:
                st = jnp.where(left, NEG_BIG, st)
            if par and xt == n_tiles - 1:
                st = jnp.where(left, st, NEG_BIG)
            return st

        n_sub = Q_BLOCK // SOFTMAX_ROWS
        for sub in range(n_sub):
            geo = geometry(sub)
            rows, n_tiles = geo[0], geo[4]
            mt = jnp.maximum(sc_ref[e, rows, 0:LANES], sc_ref[e, rows, LANES:2 * LANES])
            for xt in range(n_tiles):
                mt = jnp.maximum(mt, tile(*geo, xt))
            m_ref[e, rows, :] = jnp.broadcast_to(jnp.max(mt, axis=1, keepdims=True), (SOFTMAX_ROWS, LANES))
        for sub in range(n_sub):
            geo = geometry(sub)
            rows, t0, n_tiles = geo[0], geo[3], geo[4]
            m = m_ref[e, rows, :]
            lt = None
            for xt in range(n_tiles):
                t = t0 + xt
                pt = jnp.exp2(tile(*geo, xt) - m)
                lt = pt if lt is None else lt + pt
                p_ref[e, rows, t * LANES:(t + 1) * LANES] = pt.astype(BF16)
            for t in range(n_ktiles):
                if not (t0 <= t < t0 + n_tiles):
                    p_ref[e, rows, t * LANES:(t + 1) * LANES] = jnp.zeros((SOFTMAX_ROWS, LANES), BF16)
            for ci in range(2):
                pt = jnp.exp2(sc_ref[e, rows, ci * LANES:(ci + 1) * LANES] - m)
                lt = lt + pt
                pc_ref[e, rows, ci * LANES:(ci + 1) * LANES] = pt.astype(BF16)
            l = jnp.sum(lt, axis=1, keepdims=True)
            l_ref[e, rows, :] = jnp.broadcast_to(1.0 / l, (SOFTMAX_ROWS, LANES))

    def values(e, key_start):
        vblk = v_ref[0, pl.ds(key_start, K_BLOCK), :]
        return (jnp.dot(p_ref[e], vblk, preferred_element_type=F32)
                + jnp.dot(pc_ref[e], vc_ref[0], preferred_element_type=F32)) * l_ref[e]

    def finish(q_start, key_start):
        o1 = values(1, key_start)
        o_ref[0, pl.ds(q_start, Q_BLOCK), :] = jnp.where(left, acc_ref[...], o1).astype(BF16)

    def block(jb_static, cur, nxt, prev):
        offs, deltas = _window_geometry(jb_static)
        scores(1, *cur)
        softmax(0, offs, deltas)
        acc_ref[...] = values(0, cur[1])
        if nxt is not None:
            scores(0, *nxt)
        softmax(1, offs, deltas)
        finish(*cur)

    def starts(jb):
        q_start = pl.multiple_of(jb * Q_BLOCK, Q_BLOCK)
        key_start = jnp.clip((jb * Q_ROWS_PER_STEP - WIN_H // 2) * GRID_W, 0, max_key_start)
        return q_start, pl.multiple_of(key_start, Q_BLOCK)

    scores(0, 0, 0)
    block(0, (0, 0), (Q_BLOCK, 0), None)

    blocks_per_trip = 2

    def interior(trip, carry):
        jb = 1 + blocks_per_trip * trip
        for d in range(blocks_per_trip):
            block(1, starts(jb + d), starts(jb + d + 1), starts(jb + d - 1))
        return carry

    assert (last - 1) % blocks_per_trip == 0
    lax.fori_loop(0, (last - 1) // blocks_per_trip, interior, 0)
    prev_key_start = min(((last - 1) * Q_ROWS_PER_STEP - WIN_H // 2) * GRID_W, max_key_start)
    block(last, (last * Q_BLOCK, max_key_start), None, ((last - 1) * Q_BLOCK, prev_key_start))


def _bias_tables(rpb):
    qcol = np.arange(GRID_W)
    kcol = np.arange(GRID_W)
    col_start = np.clip(qcol - WIN_W // 2, 0, GRID_W - WIN_W)
    in_win = (kcol[None, :] >= col_start[:, None]) & (kcol[None, :] < col_start[:, None] + WIN_W)
    dc_idx = np.clip(kcol[None, :] - qcol[:, None], -(WIN_W - 1), WIN_W - 1) + WIN_W - 1
    sel = (np.arange(2 * WIN_W - 1)[:, None, None] == dc_idx[None]).astype(np.float32)
    toe = jnp.einsum('hdt,tck->hdck', rpb * LOG2E, jnp.asarray(sel),
                     precision=lax.Precision.HIGHEST)
    toe = jnp.where(in_win[None, None], toe, NEG_BIG)
    neg = jnp.full((N_HEADS, 1, GRID_W, GRID_W), NEG_BIG, F32)
    ext = jnp.concatenate([neg, toe, neg], axis=1)
    pair = jnp.concatenate([ext[:, 0:16], ext[:, 1:17]], axis=-1)
    return pair.reshape(N_HEADS // 2, 2, 16, GRID_W, LANES)


def _attention(q, k, v, kc, vc, bias):
    kspec = pl.BlockSpec((1, SEQ, LANES), lambda hp, b: (b, 0, hp))
    ktspec = pl.BlockSpec((1, SEQ // KEY_GROUP, LANES, KEY_GROUP), lambda hp, b: (b, 0, hp, 0))
    cspec = pl.BlockSpec((1, CTX_LEN, LANES), lambda hp, b: (b, 0, hp))
    ctspec = pl.BlockSpec((1, LANES, CTX_LEN), lambda hp, b: (b, hp, 0))
    bspec = pl.BlockSpec((1, 2, 16, GRID_W, LANES), lambda hp, b: (hp, 0, 0, 0, 0))
    return pl.pallas_call(
        _attn_kernel,
        grid=(N_HEADS // 2, BATCH),
        in_specs=[kspec, ktspec, kspec, ctspec, cspec, bspec],
        out_specs=kspec,
        out_shape=jax.ShapeDtypeStruct((BATCH, SEQ, ATTN_WIDTH), BF16),
        scratch_shapes=[pltpu.VMEM((2, Q_BLOCK, K_BLOCK), F32),
                        pltpu.VMEM((2, Q_BLOCK, CTX_LEN), F32),
                        pltpu.VMEM((2, Q_BLOCK, K_BLOCK), BF16),
                        pltpu.VMEM((2, Q_BLOCK, CTX_LEN), BF16),
                        pltpu.VMEM((2, Q_BLOCK, LANES), F32),
                        pltpu.VMEM((2, Q_BLOCK, LANES), F32),
                        pltpu.VMEM((Q_BLOCK, LANES), F32)],
        compiler_params=pltpu.CompilerParams(
            dimension_semantics=("arbitrary", "arbitrary"),
            vmem_limit_bytes=VMEM_LIMIT),
        name="attention",
    )(q, k, v, kc, vc, bias)


def _rot256(a, b, s, lane):
    s %= 2 * LANES
    if s >= LANES:
        a, b, s = b, a, s - LANES
    if s == 0:
        return a, b
    ra = pltpu.roll(a, s, 1)
    rb = pltpu.roll(b, s, 1)
    keep = lane >= s
    return jnp.where(keep, ra, rb), jnp.where(keep, rb, ra)


S5_PREP_GROUPS = 2


def _s5_prep_kernel(*refs):
    for gi in range(S5_PREP_GROUPS):
        _s5_prep_group(gi, *refs)


def _s5_prep_group(gi, par_ref, b_ref, c_ref, t16_ref, e_ref, mi_ref, ms_ref, mo_ref, a_ref):
    dot = functools.partial(jnp.dot, preferred_element_type=F32)

    def split2(x):
        hi = x.astype(BF16)
        return hi, (x - hi.astype(F32)).astype(BF16)

    def split3(x):
        hi = x.astype(BF16)
        r1 = x - hi.astype(F32)
        mid = r1.astype(BF16)
        return hi, mid, (r1 - mid.astype(F32)).astype(BF16)

    def pick(x, onehot):
        hi, mid, lo = split3(x)
        return dot(hi, onehot) + dot(mid, onehot) + dot(lo, onehot)

    def dot_f32(a, b):
        ah, al = split2(a)
        bh, bl = split2(b)
        return dot(jnp.concatenate([ah, ah, al], axis=1), jnp.concatenate([bh, bl, bh], axis=0))

    ns = SSM_STATE
    par = par_ref[gi].T
    lam_re, lam_im = par[:, 0:1], par[:, 1:2]
    dt = jnp.exp(par[:, 2:3])
    lane = lax.broadcasted_iota(jnp.int32, (1, LANES), 1)
    kf = lane.astype(F32)
    mag = jnp.exp((lam_re * dt) * kf)
    ang = (lam_im * dt) * kf
    pw_re, pw_im = mag * jnp.cos(ang), mag * jnp.sin(ang)
    lb_re = jnp.sum(jnp.where(lane == 1, pw_re, 0.0), axis=1, keepdims=True)
    lb_im = jnp.sum(jnp.where(lane == 1, pw_im, 0.0), axis=1, keepdims=True)
    den = lam_re * lam_re + lam_im * lam_im
    nr, ni = lb_re - 1.0, lb_im
    f_re = (nr * lam_re + ni * lam_im) / den
    f_im = (ni * lam_re - nr * lam_im) / den
    b_re, b_im = b_ref[gi, :, 0:SSM_GROUP], b_ref[gi, :, SSM_GROUP:2 * SSM_GROUP]
    bb_re = f_re * b_re - f_im * b_im
    bb_im = f_re * b_im + f_im * b_re
    t16 = t16_ref[...]
    bbt_re, bbt_im = pick(bb_re, t16), pick(bb_im, t16)
    ct_re = pick(c_ref[gi, :, 0:SSM_GROUP], t16)
    ct_im = pick(c_ref[gi, :, SSM_GROUP:2 * SSM_GROUP], t16)
    pw_at = lambda x: (pick(pw_re, e_ref[x]), pick(pw_im, e_ref[x]))
    id_re, id_im = pw_at(0)
    rev_re, rev_im = pw_at(1)
    p1_re, p1_im = pw_at(2)
    r16_re, r16_im = pw_at(3)
    f, b = slice(0, ns), slice(ns, 2 * ns)
    cmul = lambda ar, ai, br, bi: (ar * br - ai * bi, ar * bi + ai * br)

    sf_re, sf_im = cmul(rev_re[f], rev_im[f], bbt_re[f], bbt_im[f])
    sb_re, sb_im = cmul(id_re[b], id_im[b], bbt_re[b], bbt_im[b])
    ms_ref[gi] = jnp.concatenate([sf_re, sb_re, sf_im, sb_im], axis=0).T.astype(BF16)

    of_re, of_im = cmul(p1_re[f], p1_im[f], ct_re[f], ct_im[f])
    ob_re, ob_im = cmul(r16_re[b], r16_im[b], ct_re[b], ct_im[b])
    mo_ref[gi] = jnp.concatenate([of_re, ob_re, -of_im, -ob_im], axis=0).astype(BF16)

    xf_re, xf_im = cmul(id_re[f], id_im[f], ct_re[f], ct_im[f])
    xb_re, xb_im = cmul(rev_re[b], rev_im[b], ct_re[b], ct_im[b])
    btf = jnp.concatenate([bbt_re[f], bbt_im[f]], axis=0).T
    btb = jnp.concatenate([bbt_re[b], bbt_im[b]], axis=0).T
    g_f = dot_f32(btf, jnp.concatenate([xf_re, -xf_im], axis=0))
    g_b = dot_f32(btb, jnp.concatenate([xb_re, -xb_im], axis=0))
    for j in range(CHUNK):
        rows = slice(SSM_GROUP * j, SSM_GROUP * (j + 1))
        lo_col, hi_col = SSM_GROUP * j, SSM_GROUP * (j + 1)
        f_lo, f_hi = _rot256(g_f[rows, :LANES], g_f[rows, LANES:], lo_col, lane)
        b_lo, b_hi = _rot256(g_b[rows, :LANES], g_b[rows, LANES:], -SSM_GROUP * (CHUNK - 1 - j), lane)
        lo = jnp.where(lane >= lo_col, f_lo, 0.0) + jnp.where(lane < hi_col, b_lo, 0.0)
        up = jnp.where(lane + LANES >= lo_col, f_hi, 0.0) + jnp.where(lane + LANES < hi_col, b_hi, 0.0)
        mi_ref[gi, rows, 0:LANES] = lo.astype(BF16)
        mi_ref[gi, rows, LANES:2 * LANES] = up.astype(BF16)

    a16 = jnp.concatenate([pw_re.T[CHUNK:CHUNK + 1, :], pw_im.T[CHUNK:CHUNK + 1, :]], axis=1)
    a_ref[gi] = jnp.broadcast_to(a16, (8, 2 * LANES))


def _s5_matrices(lam_re, lam_im, log_dt, b_re, b_im, c_re, c_im):
    g, p2 = SSM_GROUPS, 2 * SSM_STATE
    both = lambda a: jnp.transpose(a, (1, 0, 2)).reshape(g, p2)
    log_dt_rows = jnp.repeat(jnp.transpose(log_dt), SSM_STATE, axis=1)
    par = jnp.stack([both(lam_re), both(lam_im), log_dt_rows], axis=1)
    par = jnp.concatenate([par, jnp.zeros((g, LANES - 3, p2), F32)], axis=1)
    rows_b = lambda a: jnp.transpose(a, (1, 0, 2, 3)).reshape(g, p2, SSM_GROUP)
    rows_c = lambda a: jnp.transpose(a, (1, 0, 3, 2)).reshape(g, p2, SSM_GROUP)
    b_cat = jnp.concatenate([rows_b(b_re), rows_b(b_im)], axis=-1)
    c_cat = jnp.concatenate([rows_c(c_re), rows_c(c_im)], axis=-1)

    col = np.arange(CHUNK_COLS)
    tile16 = (col[None, :] % SSM_GROUP == np.arange(SSM_GROUP)[:, None]).astype(np.float32)
    pos = col // SSM_GROUP
    k_idx = np.arange(LANES)[:, None]
    expand = np.stack([k_idx == pos[None, :], k_idx == (CHUNK - 1 - pos)[None, :],
                       k_idx == (pos + 1)[None, :], k_idx == (CHUNK - pos)[None, :]]).astype(np.float32)

    gp = S5_PREP_GROUPS
    mat = lambda: pl.BlockSpec((gp, CHUNK_COLS, CHUNK_COLS), lambda i: (i, 0, 0))
    mat_shape = jax.ShapeDtypeStruct((g, CHUNK_COLS, CHUNK_COLS), BF16)
    return pl.pallas_call(
        _s5_prep_kernel,
        grid=(g // gp,),
        in_specs=[pl.BlockSpec((gp, LANES, p2), lambda i: (i, 0, 0)),
                  pl.BlockSpec((gp, p2, 2 * SSM_GROUP), lambda i: (i, 0, 0)),
                  pl.BlockSpec((gp, p2, 2 * SSM_GROUP), lambda i: (i, 0, 0)),
                  pl.BlockSpec((SSM_GROUP, CHUNK_COLS), lambda i: (0, 0)),
                  pl.BlockSpec((4, LANES, CHUNK_COLS), lambda i: (0, 0, 0))],
        out_specs=[mat(), mat(), mat(), pl.BlockSpec((gp, 8, 2 * LANES), lambda i: (i, 0, 0))],
        out_shape=[mat_shape, mat_shape, mat_shape, jax.ShapeDtypeStruct((g, 8, 2 * LANES), F32)],
        name="s5_prep",
    )(par, b_cat, c_cat, jnp.asarray(tile16, BF16), jnp.asarray(expand, BF16))


def _s5_kernel(ul_ref, uc_ref, ms_ref, mi_ref, mo_ref, a_ref, y_ref, s_ref, sc_ref, hp_ref, *, gb):
    for gi in range(gb):
        for b in range(BATCH):
            sb = jnp.dot(ul_ref[gi, b], ms_ref[gi], preferred_element_type=F32)
            s_ref[gi, 0, pl.ds(b, N_CHUNKS, stride=BATCH), :] = sb[:, :LANES]
            s_ref[gi, 1, pl.ds(b, N_CHUNKS, stride=BATCH), :] = sb[:, LANES:]
            cb = jnp.dot(uc_ref[gi, b], ms_ref[gi], preferred_element_type=F32)
            sc_ref[gi, 0, pl.ds(b, N_CTX_CHUNKS, stride=BATCH), :] = cb[:, :LANES]
            sc_ref[gi, 1, pl.ds(b, N_CTX_CHUNKS, stride=BATCH), :] = cb[:, LANES:]
    lane = lax.broadcasted_iota(jnp.int32, (BATCH, LANES), 1)
    fwd = lane < SSM_STATE
    half = SSM_STATE

    def advance(gi, h_re, h_im, row_f, row_b, src):
        s_re = jnp.where(fwd, src[gi, 0, pl.ds(row_f, BATCH), :], src[gi, 0, pl.ds(row_b, BATCH), :])
        s_im = jnp.where(fwd, src[gi, 1, pl.ds(row_f, BATCH), :], src[gi, 1, pl.ds(row_b, BATCH), :])
        a_re = a_ref[gi, :, 0:LANES]
        a_im = a_ref[gi, :, LANES:2 * LANES]
        n_re = a_re * h_re - a_im * h_im + s_re
        n_im = a_re * h_im + a_im * h_re + s_im
        return n_re, n_im

    def ctx_step(t, carry):
        row_f = pl.multiple_of(t * BATCH, BATCH)
        row_b = pl.multiple_of((N_CTX_CHUNKS - 1 - t) * BATCH, BATCH)
        return tuple(advance(gi, carry[gi][0], carry[gi][1], row_f, row_b, sc_ref) for gi in range(gb))

    def lat_step(t, carry):
        row_f = pl.multiple_of(t * BATCH, BATCH)
        row_b = pl.multiple_of((N_CHUNKS - 1 - t) * BATCH, BATCH)
        out = []
        for gi in range(gb):
            h_re, h_im = carry[gi]
            hp_ref[gi, 0, pl.ds(row_f, BATCH), 0:half] = h_re[:, 0:half]
            hp_ref[gi, 0, pl.ds(row_b, BATCH), half:2 * half] = h_re[:, half:]
            hp_ref[gi, 1, pl.ds(row_f, BATCH), 0:half] = h_im[:, 0:half]
            hp_ref[gi, 1, pl.ds(row_b, BATCH), half:2 * half] = h_im[:, half:]
            out.append(advance(gi, h_re, h_im, row_f, row_b, s_ref))
        return tuple(out)

    zero = jnp.zeros((BATCH, LANES), F32)
    carry = tuple((zero, zero) for _ in range(gb))
    carry = lax.fori_loop(0, N_CTX_CHUNKS, ctx_step, carry)
    lax.fori_loop(0, N_CHUNKS, lat_step, carry)
    for gi in range(gb):
        for b in range(BATCH):
            hb_re = hp_ref[gi, 0, pl.ds(b, N_CHUNKS, stride=BATCH), :].astype(BF16)
            hb_im = hp_ref[gi, 1, pl.ds(b, N_CHUNKS, stride=BATCH), :].astype(BF16)
            y = (jnp.dot(ul_ref[gi, b], mi_ref[gi], preferred_element_type=F32)
                 + jnp.dot(jnp.concatenate([hb_re, hb_im], axis=1), mo_ref[gi], preferred_element_type=F32))
            y_ref[gi, b] = y.astype(BF16)


def _s5_scan(u_lat_t, u_ctx_t, m_intra, m_state, m_out, a16):
    gb = 2
    rows = N_CHUNKS * BATCH
    crows = N_CTX_CHUNKS * BATCH
    grp = lambda r, c: pl.BlockSpec((gb, r, c), lambda g: (g, 0, 0))
    tok = lambda n: pl.BlockSpec((gb, BATCH, n, CHUNK_COLS), lambda g: (g, 0, 0, 0))
    return pl.pallas_call(
        functools.partial(_s5_kernel, gb=gb),
        grid=(SSM_GROUPS // gb,),
        in_specs=[tok(N_CHUNKS), tok(N_CTX_CHUNKS), grp(CHUNK_COLS, CHUNK_COLS),
                  grp(CHUNK_COLS, CHUNK_COLS), grp(CHUNK_COLS, CHUNK_COLS), grp(8, 2 * LANES)],
        out_specs=tok(N_CHUNKS),
        out_shape=jax.ShapeDtypeStruct((SSM_GROUPS, BATCH, N_CHUNKS, CHUNK_COLS), BF16),
        scratch_shapes=[pltpu.VMEM((gb, 2, rows, LANES), F32),
                        pltpu.VMEM((gb, 2, crows, LANES), F32),
                        pltpu.VMEM((gb, 2, rows, LANES), F32)],
        compiler_params=pltpu.CompilerParams(
            dimension_semantics=("arbitrary",), vmem_limit_bytes=VMEM_LIMIT),
        name="s5_scan",
    )(u_lat_t, u_ctx_t, m_state, m_intra, m_out, a16)


FFN_TILE = 256


def _post_kernel(x_ref, a_ref, yt_ref, u_ref, ga_ref, gs_ref, mod_ref, d_ref, fg_ref, og_ref, permt_ref,
                 wglu_ref, wba_ref, wbs_ref, wout_ref, wfi_ref, wfo_ref, o_ref, h1_ref, n2_ref, act_ref, r_ref):
    dm = D_MODEL
    g1 = mod_ref[0, :, 2 * dm:3 * dm]
    sh2 = mod_ref[0, :, 3 * dm:4 * dm]
    sc2 = mod_ref[0, :, 4 * dm:5 * dm]
    g2 = mod_ref[0, :, 5 * dm:6 * dm]
    half = CHUNK * CHUNK
    halves = [slice(h * half, (h + 1) * half) for h in range(x_ref.shape[1] // half)]
    dot = functools.partial(jnp.dot, preferred_element_type=F32)

    sp = []
    for h, rows in enumerate(halves):
        y = _load_chunk_layout(yt_ref, permt_ref, r_ref, h)
        sp.append(jax.nn.gelu(y + d_ref[...] * u_ref[0, rows, :].astype(F32)).astype(BF16))
    s = []
    for h, rows in enumerate(halves):
        vg = dot(sp[h], wglu_ref[...])
        s.append((vg[:, :SSM_WIDTH] * jax.nn.sigmoid(vg[:, SSM_WIDTH:])).astype(BF16))
    merged = []
    for h, rows in enumerate(halves):
        m = (ga_ref[0, rows, :].astype(F32) * dot(a_ref[0, rows, :], wba_ref[...])
             + gs_ref[0, rows, :].astype(F32) * dot(s[h], wbs_ref[...]))
        merged.append(m.astype(BF16))
    for h, rows in enumerate(halves):
        h1 = x_ref[0, rows, :] + g1 * dot(merged[h], wout_ref[...])
        h1_ref[rows, :] = h1
        n2_ref[rows, :] = _rms_modulate(h1, fg_ref[...], sh2, sc2).astype(BF16)
    for c in range(FFN_HIDDEN // FFN_TILE):
        lo = c * FFN_TILE
        for rows in halves:
            n2 = n2_ref[rows, :]
            fa = dot(n2, wfi_ref[:, lo:lo + FFN_TILE])
            fb = dot(n2, wfi_ref[:, FFN_HIDDEN + lo:FFN_HIDDEN + lo + FFN_TILE])
            act_ref[rows, lo:lo + FFN_TILE] = (fa * jax.nn.sigmoid(fa) * fb).astype(BF16)
    for rows in halves:
        h2 = h1_ref[rows, :] + g2 * dot(act_ref[rows, :], wfo_ref[...])
        o_ref[0, rows, :] = (h2 * lax.rsqrt(jnp.mean(h2 * h2, axis=-1, keepdims=True) + NORM_EPS)) * og_ref[...]


def _post(x, a, y_t, u, ga, gs, mod3, d_skip, ffn_g, fin_g, perm_t, wglu, wba, wbs, wout, wfi, wfo):
    tm = 512
    n = CHUNK * CHUNK
    tok = lambda width: pl.BlockSpec((1, tm, width), lambda b, i: (b, i, 0))
    const = lambda r, c: pl.BlockSpec((r, c), lambda b, i: (0, 0), pipeline_mode=pl.Buffered(1))
    return pl.pallas_call(
        _post_kernel,
        grid=(BATCH, SEQ // tm),
        in_specs=[tok(D_MODEL), tok(ATTN_WIDTH),
                  pl.BlockSpec((SSM_GROUPS, 1, tm // CHUNK, CHUNK_COLS), lambda b, i: (0, b, i, 0)),
                  tok(SSM_WIDTH), tok(D_MODEL), tok(D_MODEL),
                  pl.BlockSpec((1, 1, N_MOD * D_MODEL), lambda b, i: (b, 0, 0)),
                  const(1, SSM_WIDTH), const(1, D_MODEL), const(1, D_MODEL), const(n, n),
                  const(SSM_WIDTH, 2 * SSM_WIDTH), const(ATTN_WIDTH, D_MODEL),
                  const(SSM_WIDTH, D_MODEL), const(D_MODEL, D_MODEL),
                  const(D_MODEL, 2 * FFN_HIDDEN), const(FFN_HIDDEN, D_MODEL)],
        out_specs=tok(D_MODEL),
        out_shape=jax.ShapeDtypeStruct((BATCH, SEQ, D_MODEL), F32),
        scratch_shapes=[pltpu.VMEM((tm, D_MODEL), F32), pltpu.VMEM((tm, D_MODEL), BF16),
                        pltpu.VMEM((tm, FFN_HIDDEN), BF16), pltpu.VMEM((tm // n, n, SSM_WIDTH), F32)],
        compiler_params=pltpu.CompilerParams(
            dimension_semantics=("arbitrary", "arbitrary"), vmem_limit_bytes=VMEM_LIMIT),
        name="post",
    )(x, a, y_t, u, ga, gs, mod3, d_skip, ffn_g, fin_g, perm_t, wglu, wba, wbs, wout, wfi, wfo)


def kernel(x, c, ctx, c_ctx, w_mod, b_mod, attn_norm_g, ffn_norm_g, w_in, rel_pos_bias,
           ssm_lambda_re, ssm_lambda_im, ssm_log_dt, ssm_b_re, ssm_b_im, ssm_c_re, ssm_c_im, ssm_d,
           w_glu, w_branch_attn, w_branch_ssm, w_out, w_ffn_in, w_ffn_out, final_norm_g):
    assert x.shape == (BATCH, SEQ, D_MODEL) and w_mod.shape[0] == 1
    c_rows = jnp.concatenate(
        [c, c_ctx[None, :], jnp.zeros((MOD_ROWS - BATCH - 1, D_MODEL), F32)], axis=0)
    mod3 = _modulation(c_rows, w_mod[0], b_mod[0]).reshape(MOD_ROWS, 1, N_MOD * D_MODEL)

    col_scale = jnp.concatenate([jnp.full((ATTN_WIDTH,), HEAD_DIM ** -0.5 * LOG2E, F32),
                                 jnp.ones((IN_COLS - ATTN_WIDTH,), F32)])
    w_in_bf16 = (w_in[0] * col_scale[None, :]).astype(BF16)
    norm_g = attn_norm_g[0].reshape(1, D_MODEL)

    perm = _chunk_perm()
    q, k, v, u, u_t, ga, gs = _input_projection(x, mod3, norm_g, w_in_bf16, _rope_tables(), perm)
    kc, vc, uc_t = _context_projection(ctx, mod3, norm_g, w_in_bf16, perm)

    attn = _attention(q, k, v, kc, vc, _bias_tables(rel_pos_bias[0]))

    m_intra, m_state, m_out, a16 = _s5_matrices(
        ssm_lambda_re[0], ssm_lambda_im[0], ssm_log_dt[0], ssm_b_re[0], ssm_b_im[0],
        ssm_c_re[0], ssm_c_im[0])
    y_t = _s5_scan(u_t, uc_t, m_intra, m_state, m_out, a16)

    return _post(x, attn, y_t, u, ga, gs, mod3,
                 ssm_d[0].reshape(1, SSM_WIDTH), ffn_norm_g[0].reshape(1, D_MODEL),
                 final_norm_g.reshape(1, D_MODEL), perm,
                 w_glu[0].astype(BF16), w_branch_attn[0].astype(BF16), w_branch_ssm[0].astype(BF16),
                 w_out[0].astype(BF16), w_ffn_in[0].astype(BF16), w_ffn_out[0].astype(BF16))
```

```python
import functools
import math

import numpy as np
import jax
import jax.numpy as jnp
from jax import lax
from jax.experimental import pallas as pl
from jax.experimental.pallas import tpu as pltpu

F32 = jnp.float32
BF16 = jnp.bfloat16

D_MODEL = 1024
BATCH = 8
SEQ = 4096
GRID_W = 64
GRID_ROWS = SEQ // GRID_W
CTX_LEN = 256
N_HEADS = 8
HEAD_DIM = 64
ATTN_WIDTH = N_HEADS * HEAD_DIM
WIN_H = 8
WIN_W = 16
ROPE_BASE = 10000.0
SSM_WIDTH = 512
SSM_GROUP = 16
SSM_GROUPS = SSM_WIDTH // SSM_GROUP
SSM_STATE = 64
FFN_HIDDEN = 2816
IN_COLS = 3 * ATTN_WIDTH + SSM_WIDTH + 2 * D_MODEL
N_MOD = 6
NORM_EPS = 1e-6
NEG_BIG = -1e30
LOG2E = math.log2(math.e)

LANES = 128
CHUNK = 16
N_CHUNKS = SEQ // CHUNK
N_CTX_CHUNKS = CTX_LEN // CHUNK
CHUNK_COLS = CHUNK * SSM_GROUP
MOD_ROWS = 16
CTX_MOD_ROW = BATCH
VMEM_LIMIT = 56 * 1024 * 1024

Q_ROWS_PER_STEP = 4
Q_BLOCK = Q_ROWS_PER_STEP * GRID_W
K_ROWS_PER_STEP = Q_ROWS_PER_STEP + WIN_H
K_BLOCK = K_ROWS_PER_STEP * GRID_W
N_QBLOCKS = GRID_ROWS // Q_ROWS_PER_STEP
KEY_GROUP = Q_BLOCK
SOFTMAX_ROWS = 32
N_CHAINS = 2


def _rms_modulate(x, g, shift, scale):
    xn = x * lax.rsqrt(jnp.mean(x * x, axis=-1, keepdims=True) + NORM_EPS)
    return (xn * g) * (1.0 + scale) + shift


def _block_transpose8(vs, lane):
    for shift in (64, 32, 16):
        keep = (lane & (2 * shift - 1)) < shift
        dist = shift // SSM_GROUP
        out = list(vs)
        for a in range(8):
            if a & dist:
                continue
            b = a + dist
            out[a] = jnp.where(keep, vs[a], pltpu.roll(vs[b], shift, 1))
            out[b] = jnp.where(keep, pltpu.roll(vs[a], LANES - shift, 1), vs[b])
        vs = out
    return vs


def _chunk_perm():
    n = CHUNK * CHUNK
    r = np.arange(n)
    m = np.zeros((n, n), np.float32)
    m[r, (r % CHUNK) * CHUNK + r // CHUNK] = 1.0
    return jnp.asarray(m, BF16)


def _store_chunk_layout(u16, perm_ref, out_ref, n_groups16, group0=0):
    lane = lax.broadcasted_iota(jnp.int32, (1, LANES), 1)
    n = CHUNK * CHUNK
    for hf in range(n_groups16):
        r = jnp.dot(perm_ref[...], u16[hf * n:(hf + 1) * n, :], preferred_element_type=F32)
        c0 = (group0 + hf) * CHUNK
        for v in range(SSM_WIDTH // LANES):
            for jh in range(2):
                vs = [r[CHUNK * (8 * jh + jp):CHUNK * (8 * jh + jp + 1), v * LANES:(v + 1) * LANES]
                      for jp in range(8)]
                outs = _block_transpose8(vs, lane)
                for gi in range(8):
                    out_ref[8 * v + gi, 0, c0:c0 + CHUNK, jh * LANES:(jh + 1) * LANES] = outs[gi].astype(BF16)


def _load_chunk_layout(yt_ref, perm_t_ref, r_ref, hf):
    lane = lax.broadcasted_iota(jnp.int32, (1, LANES), 1)
    for v in range(SSM_WIDTH // LANES):
        for jh in range(2):
            vs = [yt_ref[8 * v + gi, 0, hf * CHUNK:(hf + 1) * CHUNK,
                         jh * LANES:(jh + 1) * LANES].astype(F32) for gi in range(8)]
            outs = _block_transpose8(vs, lane)
            for jp in range(8):
                r_ref[hf, CHUNK * (8 * jh + jp):CHUNK * (8 * jh + jp + 1), v * LANES:(v + 1) * LANES] = outs[jp]
    return jnp.dot(perm_t_ref[...], r_ref[hf].astype(BF16), preferred_element_type=F32)


def _mod_kernel(c_ref, w_ref, b_ref, o_ref):
    c = c_ref[...]
    s = c * jax.nn.sigmoid(c)
    o_ref[...] = jnp.dot(s, w_ref[...], preferred_element_type=F32) + b_ref[...]


def _modulation(c_rows, w_mod, b_mod):
    n = N_MOD * D_MODEL
    tn = 1536
    return pl.pallas_call(
        _mod_kernel,
        grid=(n // tn,),
        in_specs=[pl.BlockSpec((MOD_ROWS, D_MODEL), lambda j: (0, 0)),
                  pl.BlockSpec((D_MODEL, tn), lambda j: (0, j)),
                  pl.BlockSpec((1, tn), lambda j: (0, j))],
        out_specs=pl.BlockSpec((MOD_ROWS, tn), lambda j: (0, j)),
        out_shape=jax.ShapeDtypeStruct((MOD_ROWS, n), F32),
        name="modulation",
    )(c_rows, w_mod, b_mod.reshape(1, n))


def _rope_tables():
    n_freq = HEAD_DIM // 4
    inv_freq = ROPE_BASE ** (-np.arange(n_freq, dtype=np.float64) / n_freq)
    t = np.arange(SEQ)
    lane = np.arange(LANES)
    d = lane % HEAD_DIM
    use_col = (d // (HEAD_DIM // 2)) == 1
    w = d % (HEAD_DIM // 2)
    first = w < n_freq
    pos = np.where(use_col[None, :], (t % GRID_W)[:, None], (t // GRID_W)[:, None]).astype(np.float64)
    ang = pos * inv_freq[w % n_freq][None, :]
    cos = np.cos(ang)
    sin = np.sin(ang)
    sin_a = np.where(first[None, :], -sin, 0.0)
    sin_b = np.where(first[None, :], 0.0, sin)
    return (jnp.asarray(cos, F32), jnp.asarray(sin_a, F32), jnp.asarray(sin_b, F32))


def _rope_store(r, cos, sin_a, sin_b, out_ref, row0, transposed=False):
    n_rows = r.shape[0]
    for j in range(ATTN_WIDTH // LANES):
        xs = r[:, j * LANES:(j + 1) * LANES]
        rot = (xs * cos + pltpu.roll(xs, LANES - HEAD_DIM // 4, 1) * sin_a
               + pltpu.roll(xs, HEAD_DIM // 4, 1) * sin_b)
        if transposed:
            rot_t = rot.T.astype(BF16)
            for gi in range(n_rows // KEY_GROUP):
                out_ref[0, row0 // KEY_GROUP + gi, j * LANES:(j + 1) * LANES, :] = (
                    rot_t[:, gi * KEY_GROUP:(gi + 1) * KEY_GROUP])
        else:
            out_ref[0, row0:row0 + n_rows, j * LANES:(j + 1) * LANES] = rot.astype(BF16)


INPROJ_SUBTILE = 512


def _inproj_kernel(x_ref, mod_ref, g_ref, w_ref, cos_ref, sa_ref, sb_ref, perm_ref,
                   q_ref, k_ref, v_ref, u_ref, ut_ref, ga_ref, gs_ref):
    shift = mod_ref[0, :, 0:D_MODEL]
    scale = mod_ref[0, :, D_MODEL:2 * D_MODEL]
    aw = ATTN_WIDTH
    dot = functools.partial(jnp.dot, preferred_element_type=F32)
    for row0 in range(0, x_ref.shape[1], INPROJ_SUBTILE):
        rows = slice(row0, row0 + INPROJ_SUBTILE)
        nb = _rms_modulate(x_ref[0, rows, :], g_ref[...], shift, scale).astype(BF16)
        cos = cos_ref[rows, :]
        sin_a = sa_ref[rows, :]
        sin_b = sb_ref[rows, :]
        _rope_store(dot(nb, w_ref[:, 0:aw]), cos, sin_a, sin_b, q_ref, row0)
        _rope_store(dot(nb, w_ref[:, aw:2 * aw]), cos, sin_a, sin_b, k_ref, row0, transposed=True)
        v_ref[0, rows, :] = dot(nb, w_ref[:, 2 * aw:3 * aw]).astype(BF16)
        c0 = 3 * aw
        u16 = dot(nb, w_ref[:, c0:c0 + SSM_WIDTH]).astype(BF16)
        u_ref[0, rows, :] = u16
        n16 = CHUNK * CHUNK
        _store_chunk_layout(u16, perm_ref, ut_ref, INPROJ_SUBTILE // n16, row0 // n16)
        c1 = c0 + SSM_WIDTH
        ga_ref[0, rows, :] = jax.nn.sigmoid(dot(nb, w_ref[:, c1:c1 + D_MODEL])).astype(BF16)
        c2 = c1 + D_MODEL
        gs_ref[0, rows, :] = jax.nn.sigmoid(dot(nb, w_ref[:, c2:c2 + D_MODEL])).astype(BF16)


def _input_projection(x, mod3, norm_g, w_in_bf16, rope, perm):
    tm = 2 * INPROJ_SUBTILE
    n = CHUNK * CHUNK
    cos, sin_a, sin_b = rope
    tok = lambda width: pl.BlockSpec((1, tm, width), lambda i, b: (b, i, 0))
    tab = pl.BlockSpec((tm, LANES), lambda i, b: (i, 0))
    out = lambda width: jax.ShapeDtypeStruct((BATCH, SEQ, width), BF16)
    return pl.pallas_call(
        _inproj_kernel,
        grid=(SEQ // tm, BATCH),
        in_specs=[tok(D_MODEL),
                  pl.BlockSpec((1, 1, N_MOD * D_MODEL), lambda i, b: (b, 0, 0)),
                  pl.BlockSpec((1, D_MODEL), lambda i, b: (0, 0)),
                  pl.BlockSpec((D_MODEL, IN_COLS), lambda i, b: (0, 0), pipeline_mode=pl.Buffered(1)),
                  tab, tab, tab,
                  pl.BlockSpec((n, n), lambda i, b: (0, 0), pipeline_mode=pl.Buffered(1))],
        out_specs=[tok(ATTN_WIDTH),
                   pl.BlockSpec((1, tm // KEY_GROUP, ATTN_WIDTH, KEY_GROUP), lambda i, b: (b, i, 0, 0)),
                   tok(ATTN_WIDTH), tok(SSM_WIDTH),
                   pl.BlockSpec((SSM_GROUPS, 1, tm // CHUNK, CHUNK_COLS), lambda i, b: (0, b, i, 0)),
                   tok(D_MODEL), tok(D_MODEL)],
        out_shape=[out(ATTN_WIDTH),
                   jax.ShapeDtypeStruct((BATCH, SEQ // KEY_GROUP, ATTN_WIDTH, KEY_GROUP), BF16),
                   out(ATTN_WIDTH), out(SSM_WIDTH),
                   jax.ShapeDtypeStruct((SSM_GROUPS, BATCH, N_CHUNKS, CHUNK_COLS), BF16),
                   out(D_MODEL), out(D_MODEL)],
        compiler_params=pltpu.CompilerParams(
            dimension_semantics=("arbitrary", "arbitrary"), vmem_limit_bytes=VMEM_LIMIT),
        name="input_projection",
    )(x, mod3, norm_g, w_in_bf16, cos, sin_a, sin_b, perm)


def _ctx_proj_kernel(x_ref, mod_ref, g_ref, wk_ref, wv_ref, wu_ref, perm_ref, k_ref, v_ref, ut_ref):
    x = x_ref[0]
    shift = mod_ref[0, :, 0:D_MODEL]
    scale = mod_ref[0, :, D_MODEL:2 * D_MODEL]
    nb = _rms_modulate(x, g_ref[...], shift, scale).astype(BF16)
    k_ref[0] = jnp.dot(nb, wk_ref[...], preferred_element_type=F32).T.astype(BF16)
    v_ref[0] = jnp.dot(nb, wv_ref[...], preferred_element_type=F32).astype(BF16)
    u16 = jnp.dot(nb, wu_ref[...], preferred_element_type=F32).astype(BF16)
    _store_chunk_layout(u16, perm_ref, ut_ref, 1)


def _context_projection(ctx, mod3, norm_g, w_in_bf16, perm):
    n = CHUNK * CHUNK
    aw = ATTN_WIDTH
    tok = lambda width: pl.BlockSpec((1, CTX_LEN, width), lambda b: (b, 0, 0))
    wcol = lambda j: pl.BlockSpec((D_MODEL, aw), lambda b: (0, j))
    out = jax.ShapeDtypeStruct((BATCH, CTX_LEN, aw), BF16)
    return pl.pallas_call(
        _ctx_proj_kernel,
        grid=(BATCH,),
        in_specs=[tok(D_MODEL),
                  pl.BlockSpec((1, 1, N_MOD * D_MODEL), lambda b: (CTX_MOD_ROW, 0, 0)),
                  pl.BlockSpec((1, D_MODEL), lambda b: (0, 0)),
                  wcol(1), wcol(2), wcol(3),
                  pl.BlockSpec((n, n), lambda b: (0, 0))],
        out_specs=[pl.BlockSpec((1, aw, CTX_LEN), lambda b: (b, 0, 0)), tok(aw),
                   pl.BlockSpec((SSM_GROUPS, 1, N_CTX_CHUNKS, CHUNK_COLS), lambda b: (0, b, 0, 0))],
        out_shape=[jax.ShapeDtypeStruct((BATCH, aw, CTX_LEN), BF16), out,
                   jax.ShapeDtypeStruct((SSM_GROUPS, BATCH, N_CTX_CHUNKS, CHUNK_COLS), BF16)],
        name="context_projection",
    )(ctx, mod3, norm_g, w_in_bf16, w_in_bf16, w_in_bf16, perm)


def _window_start_rows(r):
    return min(max(r - WIN_H // 2, 0), GRID_ROWS - WIN_H)


def _key_block_row(jb):
    return min(max(Q_ROWS_PER_STEP * jb - WIN_H // 2, 0), GRID_ROWS - K_ROWS_PER_STEP)


def _window_geometry(jb):
    key_row0 = _key_block_row(jb)
    offs, deltas = [], []
    for i in range(Q_ROWS_PER_STEP):
        r = Q_ROWS_PER_STEP * jb + i
        rs = _window_start_rows(r)
        offs.append(rs - key_row0)
        deltas.append(r - rs)
    return offs, deltas


def _attn_kernel(q_ref, k_ref, v_ref, kc_ref, vc_ref, bias_ref, o_ref,
                 s_ref, sc_ref, p_ref, pc_ref, l_ref, m_ref, acc_ref):
    lane = lax.broadcasted_iota(jnp.int32, (1, LANES), 1)
    left = lane < HEAD_DIM
    n_ktiles = K_BLOCK // LANES
    last = N_QBLOCKS - 1
    max_key_start = (GRID_ROWS - K_ROWS_PER_STEP) * GRID_W

    def scores(sl, e, q_start, key_start):
        q2 = q_ref[0, pl.ds(q_start, Q_BLOCK), :]
        qm = jnp.where(left if e == 0 else jnp.logical_not(left), q2, jnp.zeros_like(q2))
        group0 = key_start // KEY_GROUP
        for gi in range(K_BLOCK // KEY_GROUP):
            s_ref[sl, :, gi * KEY_GROUP:(gi + 1) * KEY_GROUP] = jnp.dot(
                qm, k_ref[0, group0 + gi], preferred_element_type=F32)
        sc_ref[sl] = jnp.dot(qm, kc_ref[0], preferred_element_type=F32)

    def softmax(sl, e, offs, deltas):
        def geometry(sub):
            i = sub * SOFTMAX_ROWS // GRID_W
            qcols = slice(sub * SOFTMAX_ROWS % GRID_W, sub * SOFTMAX_ROWS % GRID_W + SOFTMAX_ROWS)
            rows = slice(sub * SOFTMAX_ROWS, (sub + 1) * SOFTMAX_ROWS)
            par = offs[i] % 2
            return rows, qcols, par, offs[i] // 2, WIN_H // 2 + par, deltas[i]

        def tile(rows, qcols, par, t0, n_tiles, delta, xt):
            t = t0 + xt
            dr0 = 2 * xt - par - delta
            st = s_ref[sl, rows, t * LANES:(t + 1) * LANES] + bias_ref[0, e, dr0 + WIN_H, qcols, :]
            if par and xt == 0:
                st = jnp.where(left, NEG_BIG, st)
            if par and xt == n_tiles - 1:
                st = jnp.where(left, st, NEG_BIG)
            return st

        n_sub = Q_BLOCK // SOFTMAX_ROWS
        for sub in range(n_sub):
            geo = geometry(sub)
            rows, n_tiles = geo[0], geo[4]
            mt = jnp.maximum(sc_ref[sl, rows, 0:LANES], sc_ref[sl, rows, LANES:2 * LANES])
            for xt in range(n_tiles):
                mt = jnp.maximum(mt, tile(*geo, xt))
            m_ref[sl, rows, :] = jnp.broadcast_to(jnp.max(mt, axis=1, keepdims=True), (SOFTMAX_ROWS, LANES))
        for sub in range(n_sub):
            geo = geometry(sub)
            rows, t0, n_tiles = geo[0], geo[3], geo[4]
            m = m_ref[sl, rows, :]
            lt = None
            for xt in range(n_tiles):
                t = t0 + xt
                pt = jnp.exp2(tile(*geo, xt) - m)
                lt = pt if lt is None else lt + pt
                p_ref[sl, rows, t * LANES:(t + 1) * LANES] = pt.astype(BF16)
            for t in range(n_ktiles):
                if not (t0 <= t < t0 + n_tiles):
                    p_ref[sl, rows, t * LANES:(t + 1) * LANES] = jnp.zeros((SOFTMAX_ROWS, LANES), BF16)
            for ci in range(2):
                pt = jnp.exp2(sc_ref[sl, rows, ci * LANES:(ci + 1) * LANES] - m)
                lt = lt + pt
                pc_ref[sl, rows, ci * LANES:(ci + 1) * LANES] = pt.astype(BF16)
            l = jnp.sum(lt, axis=1, keepdims=True)
            l_ref[sl, rows, :] = jnp.broadcast_to(1.0 / l, (SOFTMAX_ROWS, LANES))

    def values(sl, key_start):
        vblk = v_ref[0, pl.ds(key_start, K_BLOCK), :]
        return (jnp.dot(p_ref[sl], vblk, preferred_element_type=F32)
                + jnp.dot(pc_ref[sl], vc_ref[0], preferred_element_type=F32)) * l_ref[sl]

    def pair_block(variants, curs, nxts):
        geos = [_window_geometry(jb_static) for jb_static in variants]
        for c in range(N_CHAINS):
            scores(2 * c + 1, 1, *curs[c])
        for c in range(N_CHAINS):
            softmax(2 * c, 0, *geos[c])
        for c in range(N_CHAINS):
            acc_ref[c] = values(2 * c, curs[c][1])
        for c in range(N_CHAINS):
            if nxts[c] is not None:
                scores(2 * c, 0, *nxts[c])
        for c in range(N_CHAINS):
            softmax(2 * c + 1, 1, *geos[c])
        for c in range(N_CHAINS):
            o1 = values(2 * c + 1, curs[c][1])
            o_ref[0, pl.ds(curs[c][0], Q_BLOCK), :] = jnp.where(left, acc_ref[c], o1).astype(BF16)

    def starts(jb):
        q_start = pl.multiple_of(jb * Q_BLOCK, Q_BLOCK)
        key_start = jnp.clip((jb * Q_ROWS_PER_STEP - WIN_H // 2) * GRID_W, 0, max_key_start)
        return q_start, pl.multiple_of(key_start, Q_BLOCK)

    def static_starts(jb):
        return jb * Q_BLOCK, min(max((jb * Q_ROWS_PER_STEP - WIN_H // 2) * GRID_W, 0), max_key_start)

    half = N_QBLOCKS // N_CHAINS
    scores(0, 0, *static_starts(0))
    scores(2, 0, *static_starts(half))
    pair_block((0, 1), (static_starts(0), static_starts(half)), (static_starts(1), static_starts(half + 1)))

    def interior(jb, carry):
        pair_block((1, 1), (starts(jb), starts(half + jb)), (starts(jb + 1), starts(half + jb + 1)))
        return carry

    lax.fori_loop(1, half - 1, interior, 0)
    pair_block((1, last), (static_starts(half - 1), static_starts(last)), (None, None))


def _bias_tables(rpb):
    qcol = np.arange(GRID_W)
    kcol = np.arange(GRID_W)
    col_start = np.clip(qcol - WIN_W // 2, 0, GRID_W - WIN_W)
    in_win = (kcol[None, :] >= col_start[:, None]) & (kcol[None, :] < col_start[:, None] + WIN_W)
    dc_idx = np.clip(kcol[None, :] - qcol[:, None], -(WIN_W - 1), WIN_W - 1) + WIN_W - 1
    sel = (np.arange(2 * WIN_W - 1)[:, None, None] == dc_idx[None]).astype(np.float32)
    toe = jnp.einsum('hdt,tck->hdck', rpb * LOG2E, jnp.asarray(sel),
                     precision=lax.Precision.HIGHEST)
    toe = jnp.where(in_win[None, None], toe, NEG_BIG)
    neg = jnp.full((N_HEADS, 1, GRID_W, GRID_W), NEG_BIG, F32)
    ext = jnp.concatenate([neg, toe, neg], axis=1)
    pair = jnp.concatenate([ext[:, 0:16], ext[:, 1:17]], axis=-1)
    return pair.reshape(N_HEADS // 2, 2, 16, GRID_W, LANES)


def _attention(q, k, v, kc, vc, bias):
    n_slots = 2 * N_CHAINS
    kspec = pl.BlockSpec((1, SEQ, LANES), lambda hp, b: (b, 0, hp))
    ktspec =pl.BlockSpec((1, SEQ // KEY_GROUP, LANES, KEY_GROUP), lambda hp, b: (b, 0, hp, 0))
    cspec = pl.BlockSpec((1, CTX_LEN, LANES), lambda hp, b: (b, 0, hp))
    ctspec = pl.BlockSpec((1, LANES, CTX_LEN), lambda hp, b: (b, hp, 0))
    bspec = pl.BlockSpec((1, 2, 16, GRID_W, LANES), lambda hp, b: (hp, 0, 0, 0, 0))
    return pl.pallas_call(
        _attn_kernel,
        grid=(N_HEADS // 2, BATCH),
        in_specs=[kspec, ktspec, kspec, ctspec, cspec, bspec],
        out_specs=kspec,
        out_shape=jax.ShapeDtypeStruct((BATCH, SEQ, ATTN_WIDTH), BF16),
        scratch_shapes=[pltpu.VMEM((n_slots, Q_BLOCK, K_BLOCK), F32),
                        pltpu.VMEM((n_slots, Q_BLOCK, CTX_LEN), F32),
                        pltpu.VMEM((n_slots, Q_BLOCK, K_BLOCK), BF16),
                        pltpu.VMEM((n_slots, Q_BLOCK, CTX_LEN), BF16),
                        pltpu.VMEM((n_slots, Q_BLOCK, LANES), F32),
                        pltpu.VMEM((n_slots, Q_BLOCK, LANES), F32),
                        pltpu.VMEM((N_CHAINS, Q_BLOCK, LANES), F32)],
        compiler_params=pltpu.CompilerParams(
            dimension_semantics=("arbitrary", "arbitrary"),
            vmem_limit_bytes=VMEM_LIMIT),
        name="attention",
    )(q, k, v, kc, vc, bias)


def _rot256(a, b, s, lane):
    s %= 2 * LANES
    if s >= LANES:
        a, b, s = b, a, s - LANES
    if s == 0:
        return a, b
    ra = pltpu.roll(a, s, 1)
    rb = pltpu.roll(b, s, 1)
    keep = lane >= s
    return jnp.where(keep, ra, rb), jnp.where(keep, rb, ra)


S5_PREP_GROUPS = 2


def _s5_prep_kernel(*refs):
    for gi in range(S5_PREP_GROUPS):
        _s5_prep_group(gi, *refs)


def _s5_prep_group(gi, par_ref, b_ref, c_ref, t16_ref, e_ref, mi_ref, ms_ref, mo_ref, a_ref):
    dot = functools.partial(jnp.dot, preferred_element_type=F32)

    def split2(x):
        hi = x.astype(BF16)
        return hi, (x - hi.astype(F32)).astype(BF16)

    def split3(x):
        hi = x.astype(BF16)
        r1 = x - hi.astype(F32)
        mid = r1.astype(BF16)
        return hi, mid, (r1 - mid.astype(F32)).astype(BF16)

    def pick(x, onehot):
        hi, mid, lo = split3(x)
        return dot(hi, onehot) + dot(mid, onehot) + dot(lo, onehot)

    def dot_f32(a, b):
        ah, al = split2(a)
        bh, bl = split2(b)
        return dot(jnp.concatenate([ah, ah, al], axis=1), jnp.concatenate([bh, bl, bh], axis=0))

    ns = SSM_STATE
    par = par_ref[gi].T
    lam_re, lam_im = par[:, 0:1], par[:, 1:2]
    dt = jnp.exp(par[:, 2:3])
    lane = lax.broadcasted_iota(jnp.int32, (1, LANES), 1)
    kf = lane.astype(F32)
    mag = jnp.exp((lam_re * dt) * kf)
    ang = (lam_im * dt) * kf
    pw_re, pw_im = mag * jnp.cos(ang), mag * jnp.sin(ang)
    lb_re = jnp.sum(jnp.where(lane == 1, pw_re, 0.0), axis=1, keepdims=True)
    lb_im = jnp.sum(jnp.where(lane == 1, pw_im, 0.0), axis=1, keepdims=True)
    den = lam_re * lam_re + lam_im * lam_im
    nr, ni = lb_re - 1.0, lb_im
    f_re = (nr * lam_re + ni * lam_im) / den
    f_im = (ni * lam_re - nr * lam_im) / den
    b_re, b_im = b_ref[gi, :, 0:SSM_GROUP], b_ref[gi, :, SSM_GROUP:2 * SSM_GROUP]
    bb_re = f_re * b_re - f_im * b_im
    bb_im = f_re * b_im + f_im * b_re
    t16 = t16_ref[...]
    bbt_re, bbt_im = pick(bb_re, t16), pick(bb_im, t16)
    ct_re = pick(c_ref[gi, :, 0:SSM_GROUP], t16)
    ct_im = pick(c_ref[gi, :, SSM_GROUP:2 * SSM_GROUP], t16)
    pw_at = lambda x: (pick(pw_re, e_ref[x]), pick(pw_im, e_ref[x]))
    id_re, id_im = pw_at(0)
    rev_re, rev_im = pw_at(1)
    p1_re, p1_im = pw_at(2)
    r16_re, r16_im = pw_at(3)
    f, b = slice(0, ns), slice(ns, 2 * ns)
    cmul = lambda ar, ai, br, bi: (ar * br - ai * bi, ar * bi + ai * br)

    sf_re, sf_im = cmul(rev_re[f], rev_im[f], bbt_re[f], bbt_im[f])
    sb_re, sb_im = cmul(id_re[b], id_im[b], bbt_re[b], bbt_im[b])
    ms_ref[gi] = jnp.concatenate([sf_re, sb_re, sf_im, sb_im], axis=0).T.astype(BF16)

    of_re, of_im = cmul(p1_re[f], p1_im[f], ct_re[f], ct_im[f])
    ob_re, ob_im = cmul(r16_re[b], r16_im[b], ct_re[b], ct_im[b])
    mo_ref[gi] = jnp.concatenate([of_re, ob_re, -of_im, -ob_im], axis=0).astype(BF16)

    xf_re, xf_im = cmul(id_re[f], id_im[f], ct_re[f], ct_im[f])
    xb_re, xb_im = cmul(rev_re[b], rev_im[b], ct_re[b], ct_im[b])
    btf = jnp.concatenate([bbt_re[f], bbt_im[f]], axis=0).T
    btb = jnp.concatenate([bbt_re[b], bbt_im[b]], axis=0).T
    g_f = dot_f32(btf, jnp.concatenate([xf_re, -xf_im], axis=0))
    g_b = dot_f32(btb, jnp.concatenate([xb_re, -xb_im], axis=0))
    for j in range(CHUNK):
        rows = slice(SSM_GROUP * j, SSM_GROUP * (j + 1))
        lo_col, hi_col = SSM_GROUP * j, SSM_GROUP * (j + 1)
        f_lo, f_hi = _rot256(g_f[rows, :LANES], g_f[rows, LANES:], lo_col, lane)
        b_lo, b_hi = _rot256(g_b[rows, :LANES], g_b[rows, LANES:], -SSM_GROUP * (CHUNK - 1 - j), lane)
        lo = jnp.where(lane >= lo_col, f_lo, 0.0) + jnp.where(lane < hi_col, b_lo, 0.0)
        up = jnp.where(lane + LANES >= lo_col, f_hi, 0.0) + jnp.where(lane + LANES < hi_col, b_hi, 0.0)
        mi_ref[gi, rows, 0:LANES] = lo.astype(BF16)
        mi_ref[gi, rows, LANES:2 * LANES] = up.astype(BF16)

    a16 = jnp.concatenate([pw_re.T[CHUNK:CHUNK + 1, :], pw_im.T[CHUNK:CHUNK + 1, :]], axis=1)
    a_ref[gi] = jnp.broadcast_to(a16, (8, 2 * LANES))


def _s5_matrices(lam_re, lam_im, log_dt, b_re, b_im, c_re, c_im):
    g, p2 = SSM_GROUPS, 2 * SSM_STATE
    both = lambda a: jnp.transpose(a, (1, 0, 2)).reshape(g, p2)
    log_dt_rows = jnp.repeat(jnp.transpose(log_dt), SSM_STATE, axis=1)
    par = jnp.stack([both(lam_re), both(lam_im), log_dt_rows], axis=1)
    par = jnp.concatenate([par, jnp.zeros((g, LANES - 3, p2), F32)], axis=1)
    rows_b = lambda a: jnp.transpose(a, (1, 0, 2, 3)).reshape(g, p2, SSM_GROUP)
    rows_c = lambda a: jnp.transpose(a, (1, 0, 3, 2)).reshape(g, p2, SSM_GROUP)
    b_cat = jnp.concatenate([rows_b(b_re), rows_b(b_im)], axis=-1)
    c_cat = jnp.concatenate([rows_c(c_re), rows_c(c_im)], axis=-1)

    col = np.arange(CHUNK_COLS)
    tile16 = (col[None, :] % SSM_GROUP == np.arange(SSM_GROUP)[:, None]).astype(np.float32)
    pos = col // SSM_GROUP
    k_idx = np.arange(LANES)[:, None]
    expand = np.stack([k_idx == pos[None, :], k_idx == (CHUNK - 1 - pos)[None, :],
                       k_idx == (pos + 1)[None, :], k_idx == (CHUNK - pos)[None, :]]).astype(np.float32)

    gp = S5_PREP_GROUPS
    mat = lambda: pl.BlockSpec((gp, CHUNK_COLS, CHUNK_COLS), lambda i: (i, 0, 0))
    mat_shape = jax.ShapeDtypeStruct((g, CHUNK_COLS, CHUNK_COLS), BF16)
    return pl.pallas_call(
        _s5_prep_kernel,
        grid=(g // gp,),
        in_specs=[pl.BlockSpec((gp, LANES, p2), lambda i: (i, 0, 0)),
                  pl.BlockSpec((gp, p2, 2 * SSM_GROUP), lambda i: (i, 0, 0)),
                  pl.BlockSpec((gp, p2, 2 * SSM_GROUP), lambda i: (i, 0, 0)),
                  pl.BlockSpec((SSM_GROUP, CHUNK_COLS), lambda i: (0, 0)),
                  pl.BlockSpec((4, LANES, CHUNK_COLS), lambda i: (0, 0, 0))],
        out_specs=[mat(), mat(), mat(), pl.BlockSpec((gp, 8, 2 * LANES), lambda i: (i, 0, 0))],
        out_shape=[mat_shape, mat_shape, mat_shape, jax.ShapeDtypeStruct((g, 8, 2 * LANES), F32)],
        name="s5_prep",
    )(par, b_cat, c_cat, jnp.asarray(tile16, BF16), jnp.asarray(expand, BF16))


def _s5_kernel(ul_ref, uc_ref, ms_ref, mi_ref, mo_ref, a_ref, y_ref, s_ref, sc_ref, hp_ref, *, gb):
    for gi in range(gb):
        for b in range(BATCH):
            sb = jnp.dot(ul_ref[gi, b], ms_ref[gi], preferred_element_type=F32)
            s_ref[gi, 0, pl.ds(b, N_CHUNKS, stride=BATCH), :] = sb[:, :LANES]
            s_ref[gi, 1, pl.ds(b, N_CHUNKS, stride=BATCH), :] = sb[:, LANES:]
            cb = jnp.dot(uc_ref[gi, b], ms_ref[gi], preferred_element_type=F32)
            sc_ref[gi, 0, pl.ds(b, N_CTX_CHUNKS, stride=BATCH), :] = cb[:, :LANES]
            sc_ref[gi, 1, pl.ds(b, N_CTX_CHUNKS, stride=BATCH), :] = cb[:, LANES:]
    lane = lax.broadcasted_iota(jnp.int32, (BATCH, LANES), 1)
    fwd = lane < SSM_STATE
    half = SSM_STATE

    def advance(gi, h_re, h_im, row_f, row_b, src):
        s_re = jnp.where(fwd, src[gi, 0, pl.ds(row_f, BATCH), :], src[gi, 0, pl.ds(row_b, BATCH), :])
        s_im = jnp.where(fwd, src[gi, 1, pl.ds(row_f, BATCH), :], src[gi, 1, pl.ds(row_b, BATCH), :])
        a_re = a_ref[gi, :, 0:LANES]
        a_im = a_ref[gi, :, LANES:2 * LANES]
        n_re = a_re * h_re - a_im * h_im + s_re
        n_im = a_re * h_im + a_im * h_re + s_im
        return n_re, n_im

    def ctx_step(t, carry):
        row_f = pl.multiple_of(t * BATCH, BATCH)
        row_b = pl.multiple_of((N_CTX_CHUNKS - 1 - t) * BATCH, BATCH)
        return tuple(advance(gi, carry[gi][0], carry[gi][1], row_f, row_b, sc_ref) for gi in range(gb))

    def lat_step(t, carry):
        row_f = pl.multiple_of(t * BATCH, BATCH)
        row_b = pl.multiple_of((N_CHUNKS - 1 - t) * BATCH, BATCH)
        out = []
        for gi in range(gb):
            h_re, h_im = carry[gi]
            hp_ref[gi, 0, pl.ds(row_f, BATCH), 0:half] = h_re[:, 0:half]
            hp_ref[gi, 0, pl.ds(row_b, BATCH), half:2 * half] = h_re[:, half:]
            hp_ref[gi, 1, pl.ds(row_f, BATCH), 0:half] = h_im[:, 0:half]
            hp_ref[gi, 1, pl.ds(row_b, BATCH), half:2 * half] = h_im[:, half:]
            out.append(advance(gi, h_re, h_im, row_f, row_b, s_ref))
        return tuple(out)

    zero = jnp.zeros((BATCH, LANES), F32)
    carry = tuple((zero, zero) for _ in range(gb))
    carry = lax.fori_loop(0, N_CTX_CHUNKS, ctx_step, carry)
    lax.fori_loop(0, N_CHUNKS, lat_step, carry)
    for gi in range(gb):
        for b in range(BATCH):
            hb_re = hp_ref[gi, 0, pl.ds(b, N_CHUNKS, stride=BATCH), :].astype(BF16)
            hb_im = hp_ref[gi, 1, pl.ds(b, N_CHUNKS, stride=BATCH), :].astype(BF16)
            y = (jnp.dot(ul_ref[gi, b], mi_ref[gi], preferred_element_type=F32)
                 + jnp.dot(jnp.concatenate([hb_re, hb_im], axis=1), mo_ref[gi], preferred_element_type=F32))
            y_ref[gi, b] = y.astype(BF16)


def _s5_scan(u_lat_t, u_ctx_t, m_intra, m_state, m_out, a16):
    gb = 4
    rows = N_CHUNKS * BATCH
    crows = N_CTX_CHUNKS * BATCH
    grp = lambda r, c: pl.BlockSpec((gb, r, c), lambda g: (g, 0, 0))
    tok = lambda n: pl.BlockSpec((gb, BATCH, n, CHUNK_COLS), lambda g: (g, 0, 0, 0))
    return pl.pallas_call(
        functools.partial(_s5_kernel, gb=gb),
        grid=(SSM_GROUPS // gb,),
        in_specs=[tok(N_CHUNKS), tok(N_CTX_CHUNKS), grp(CHUNK_COLS, CHUNK_COLS),
                  grp(CHUNK_COLS, CHUNK_COLS), grp(CHUNK_COLS, CHUNK_COLS), grp(8, 2 * LANES)],
        out_specs=tok(N_CHUNKS),
        out_shape=jax.ShapeDtypeStruct((SSM_GROUPS, BATCH, N_CHUNKS, CHUNK_COLS), BF16),
        scratch_shapes=[pltpu.VMEM((gb, 2, rows, LANES), F32),
                        pltpu.VMEM((gb, 2, crows, LANES), F32),
                        pltpu.VMEM((gb, 2, rows, LANES), F32)],
        compiler_params=pltpu.CompilerParams(
            dimension_semantics=("arbitrary",), vmem_limit_bytes=VMEM_LIMIT),
        name="s5_scan",
    )(u_lat_t, u_ctx_t, m_state, m_intra, m_out, a16)


FFN_TILE = 256


def _post_kernel(x_ref, a_ref, yt_ref, u_ref, ga_ref, gs_ref, mod_ref, d_ref, fg_ref, og_ref, permt_ref,
                 wglu_ref, wba_ref, wbs_ref, wout_ref, wfi_ref, wfo_ref, o_ref, h1_ref, n2_ref, act_ref, r_ref):
    dm = D_MODEL
    g1 = mod_ref[0, :, 2 * dm:3 * dm]
    sh2 = mod_ref[0, :, 3 * dm:4 * dm]
    sc2 = mod_ref[0, :, 4 * dm:5 * dm]
    g2 = mod_ref[0, :, 5 * dm:6 * dm]
    half = CHUNK * CHUNK
    halves = [slice(h * half, (h + 1) * half) for h in range(x_ref.shape[1] // half)]
    dot = functools.partial(jnp.dot, preferred_element_type=F32)

    sp = []
    for h, rows in enumerate(halves):
        y = _load_chunk_layout(yt_ref, permt_ref, r_ref, h)
        sp.append(jax.nn.gelu(y + d_ref[...] * u_ref[0, rows, :].astype(F32)).astype(BF16))
    s = []
    for h, rows in enumerate(halves):
        vg = dot(sp[h], wglu_ref[...])
        s.append((vg[:, :SSM_WIDTH] * jax.nn.sigmoid(vg[:, SSM_WIDTH:])).astype(BF16))
    merged = []
    for h, rows in enumerate(halves):
        m = (ga_ref[0, rows, :].astype(F32) * dot(a_ref[0, rows, :], wba_ref[...])
             + gs_ref[0, rows, :].astype(F32) * dot(s[h], wbs_ref[...]))
        merged.append(m.astype(BF16))
    for h, rows in enumerate(halves):
        h1 = x_ref[0, rows, :] + g1 * dot(merged[h], wout_ref[...])
        h1_ref[rows, :] = h1
        n2_ref[rows, :] = _rms_modulate(h1, fg_ref[...], sh2, sc2).astype(BF16)
    for c in range(FFN_HIDDEN // FFN_TILE):
        lo = c * FFN_TILE
        for rows in halves:
            n2 = n2_ref[rows, :]
            fa = dot(n2, wfi_ref[:, lo:lo + FFN_TILE])
            fb = dot(n2, wfi_ref[:, FFN_HIDDEN + lo:FFN_HIDDEN + lo + FFN_TILE])
            act_ref[rows, lo:lo + FFN_TILE] = (fa * jax.nn.sigmoid(fa) * fb).astype(BF16)
    for rows in halves:
        h2 = h1_ref[rows, :] + g2 * dot(act_ref[rows, :], wfo_ref[...])
        o_ref[0, rows, :] = (h2 * lax.rsqrt(jnp.mean(h2 * h2, axis=-1, keepdims=True) + NORM_EPS)) * og_ref[...]


def _post(x, a, y_t, u, ga, gs, mod3, d_skip, ffn_g, fin_g, perm_t, wglu, wba, wbs, wout, wfi, wfo):
    tm = 512
    n = CHUNK * CHUNK
    tok = lambda width: pl.BlockSpec((1, tm, width), lambda b, i: (b, i, 0))
    const = lambda r, c: pl.BlockSpec((r, c), lambda b, i: (0, 0), pipeline_mode=pl.Buffered(1))
    return pl.pallas_call(
        _post_kernel,
        grid=(BATCH, SEQ // tm),
        in_specs=[tok(D_MODEL), tok(ATTN_WIDTH),
                  pl.BlockSpec((SSM_GROUPS, 1, tm // CHUNK, CHUNK_COLS), lambda b, i: (0, b, i, 0)),
                  tok(SSM_WIDTH), tok(D_MODEL), tok(D_MODEL),
                  pl.BlockSpec((1, 1, N_MOD * D_MODEL), lambda b, i: (b, 0, 0)),
                  const(1, SSM_WIDTH), const(1, D_MODEL), const(1, D_MODEL), const(n, n),
                  const(SSM_WIDTH, 2 * SSM_WIDTH), const(ATTN_WIDTH, D_MODEL),
                  const(SSM_WIDTH, D_MODEL), const(D_MODEL, D_MODEL),
                  const(D_MODEL, 2 * FFN_HIDDEN), const(FFN_HIDDEN, D_MODEL)],
        out_specs=tok(D_MODEL),
        out_shape=jax.ShapeDtypeStruct((BATCH, SEQ, D_MODEL), F32),
        scratch_shapes=[pltpu.VMEM((tm, D_MODEL), F32), pltpu.VMEM((tm, D_MODEL), BF16),
                        pltpu.VMEM((tm, FFN_HIDDEN), BF16), pltpu.VMEM((tm // n, n, SSM_WIDTH), F32)],
        compiler_params=pltpu.CompilerParams(
            dimension_semantics=("arbitrary", "arbitrary"), vmem_limit_bytes=VMEM_LIMIT),
        name="post",
    )(x, a, y_t, u, ga, gs, mod3, d_skip, ffn_g, fin_g, perm_t, wglu, wba, wbs, wout, wfi, wfo)


def kernel(x, c, ctx, c_ctx, w_mod, b_mod, attn_norm_g, ffn_norm_g, w_in, rel_pos_bias,
           ssm_lambda_re, ssm_lambda_im, ssm_log_dt, ssm_b_re, ssm_b_im, ssm_c_re, ssm_c_im, ssm_d,
           w_glu, w_branch_attn, w_branch_ssm, w_out, w_ffn_in, w_ffn_out, final_norm_g):
    assert x.shape == (BATCH, SEQ, D_MODEL) and w_mod.shape[0] == 1
    c_rows = jnp.concatenate(
        [c, c_ctx[None, :], jnp.zeros((MOD_ROWS - BATCH - 1, D_MODEL), F32)], axis=0)
    mod3 = _modulation(c_rows, w_mod[0], b_mod[0]).reshape(MOD_ROWS, 1, N_MOD * D_MODEL)

    col_scale = jnp.concatenate([jnp.full((ATTN_WIDTH,), HEAD_DIM ** -0.5 * LOG2E, F32),
                                 jnp.ones((IN_COLS - ATTN_WIDTH,), F32)])
    w_in_bf16 = (w_in[0] * col_scale[None, :]).astype(BF16)
    norm_g = attn_norm_g[0].reshape(1, D_MODEL)

    perm = _chunk_perm()
    q, k, v, u, u_t, ga, gs = _input_projection(x, mod3, norm_g, w_in_bf16, _rope_tables(), perm)
    kc, vc, uc_t = _context_projection(ctx, mod3, norm_g, w_in_bf16, perm)

    attn = _attention(q, k, v, kc, vc, _bias_tables(rel_pos_bias[0]))

    m_intra, m_state, m_out, a16 = _s5_matrices(
        ssm_lambda_re[0], ssm_lambda_im[0], ssm_log_dt[0], ssm_b_re[0], ssm_b_im[0],
        ssm_c_re[0], ssm_c_im[0])
    y_t = _s5_scan(u_t, uc_t, m_intra, m_state, m_out, a16)

    return _post(x, attn, y_t, u, ga, gs, mod3,
                 ssm_d[0].reshape(1, SSM_WIDTH), ffn_norm_g[0].reshape(1, D_MODEL),
                 final_norm_g.reshape(1, D_MODEL), perm,
                 w_glu[0].astype(BF16), w_branch_attn[0].astype(BF16), w_branch_ssm[0].astype(BF16),
                 w_out[0].astype(BF16), w_ffn_in[0].astype(BF16), w_ffn_out[0].astype(BF16))
```

```python
import functools
import math

import numpy as np
import jax
import jax.numpy as jnp
from jax import lax
from jax.experimental import pallas as pl
from jax.experimental.pallas import tpu as pltpu

F32 = jnp.float32
BF16 = jnp.bfloat16

D_MODEL = 1024
BATCH = 8
SEQ = 4096
GRID_W = 64
GRID_ROWS = SEQ // GRID_W
CTX_LEN = 256
N_HEADS = 8
HEAD_DIM = 64
ATTN_WIDTH = N_HEADS * HEAD_DIM
WIN_H = 8
WIN_W = 16
ROPE_BASE = 10000.0
SSM_WIDTH = 512
SSM_GROUP = 16
SSM_GROUPS = SSM_WIDTH // SSM_GROUP
SSM_STATE = 64
FFN_HIDDEN = 2816
IN_COLS = 3 * ATTN_WIDTH + SSM_WIDTH + 2 * D_MODEL
N_MOD = 6
NORM_EPS = 1e-6
NEG_BIG = -1e30
LOG2E = math.log2(math.e)

LANES = 128
CHUNK = 16
N_CHUNKS = SEQ // CHUNK
N_CTX_CHUNKS = CTX_LEN // CHUNK
CHUNK_COLS = CHUNK * SSM_GROUP
MOD_ROWS = 16
CTX_MOD_ROW = BATCH
VMEM_LIMIT = 56 * 1024 * 1024

Q_ROWS_PER_STEP = 4
Q_BLOCK = Q_ROWS_PER_STEP * GRID_W
K_ROWS_PER_STEP = Q_ROWS_PER_STEP + WIN_H
K_BLOCK = K_ROWS_PER_STEP * GRID_W
N_QBLOCKS = GRID_ROWS // Q_ROWS_PER_STEP
KEY_GROUP = Q_BLOCK
SOFTMAX_ROWS = 32
N_CHAINS = 1


def _rms_modulate(x, g, shift, scale):
    xn = x * lax.rsqrt(jnp.mean(x * x, axis=-1, keepdims=True) + NORM_EPS)
    return (xn * g) * (1.0 + scale) + shift


def _block_transpose8(vs, lane):
    for shift in (64, 32, 16):
        keep = (lane & (2 * shift - 1)) < shift
        dist = shift // SSM_GROUP
        out = list(vs)
        for a in range(8):
            if a & dist:
                continue
            b = a + dist
            out[a] = jnp.where(keep, vs[a], pltpu.roll(vs[b], shift, 1))
            out[b] = jnp.where(keep, pltpu.roll(vs[a], LANES - shift, 1), vs[b])
        vs = out
    return vs


def _chunk_perm():
    n = CHUNK * CHUNK
    r = np.arange(n)
    m = np.zeros((n, n), np.float32)
    m[r, (r % CHUNK) * CHUNK + r // CHUNK] = 1.0
    return jnp.asarray(m, BF16)


def _store_chunk_layout(u16, perm_ref, out_ref, n_groups16, group0=0):
    lane = lax.broadcasted_iota(jnp.int32, (1, LANES), 1)
    n = CHUNK * CHUNK
    for hf in range(n_groups16):
        r = jnp.dot(perm_ref[...], u16[hf * n:(hf + 1) * n, :], preferred_element_type=F32)
        c0 = (group0 + hf) * CHUNK
        for v in range(SSM_WIDTH // LANES):
            for jh in range(2):
                vs = [r[CHUNK * (8 * jh + jp):CHUNK * (8 * jh + jp + 1), v * LANES:(v + 1) * LANES]
                      for jp in range(8)]
                outs = _block_transpose8(vs, lane)
                for gi in range(8):
                    out_ref[8 * v + gi, 0, c0:c0 + CHUNK, jh * LANES:(jh + 1) * LANES] = outs[gi].astype(BF16)


def _load_chunk_layout(yt_ref, perm_t_ref, r_ref, hf):
    lane = lax.broadcasted_iota(jnp.int32, (1, LANES), 1)
    for v in range(SSM_WIDTH // LANES):
        for jh in range(2):
            vs = [yt_ref[8 * v + gi, 0, hf * CHUNK:(hf + 1) * CHUNK,
                         jh * LANES:(jh + 1) * LANES].astype(F32) for gi in range(8)]
            outs = _block_transpose8(vs, lane)
            for jp in range(8):
                r_ref[hf, CHUNK * (8 * jh + jp):CHUNK * (8 * jh + jp + 1), v * LANES:(v + 1) * LANES] = outs[jp]
    return jnp.dot(perm_t_ref[...], r_ref[hf].astype(BF16), preferred_element_type=F32)


def _mod_kernel(c_ref, w_ref, b_ref, o_ref):
    c = c_ref[...]
    s = c * jax.nn.sigmoid(c)
    o_ref[...] = jnp.dot(s, w_ref[...], preferred_element_type=F32) + b_ref[...]


def _modulation(c_rows, w_mod, b_mod):
    n = N_MOD * D_MODEL
    tn = 1536
    return pl.pallas_call(
        _mod_kernel,
        grid=(n // tn,),
        in_specs=[pl.BlockSpec((MOD_ROWS, D_MODEL), lambda j: (0, 0)),
                  pl.BlockSpec((D_MODEL, tn), lambda j: (0, j)),
                  pl.BlockSpec((1, tn), lambda j: (0, j))],
        out_specs=pl.BlockSpec((MOD_ROWS, tn), lambda j: (0, j)),
        out_shape=jax.ShapeDtypeStruct((MOD_ROWS, n), F32),
        name="modulation",
    )(c_rows, w_mod, b_mod.reshape(1, n))


def _rope_tables():
    n_freq = HEAD_DIM // 4
    inv_freq = ROPE_BASE ** (-np.arange(n_freq, dtype=np.float64) / n_freq)
    t = np.arange(SEQ)
    lane = np.arange(LANES)
    d = lane % HEAD_DIM
    use_col = (d // (HEAD_DIM // 2)) == 1
    w = d % (HEAD_DIM // 2)
    first = w < n_freq
    pos = np.where(use_col[None, :], (t % GRID_W)[:, None], (t // GRID_W)[:, None]).astype(np.float64)
    ang = pos * inv_freq[w % n_freq][None, :]
    cos = np.cos(ang)
    sin = np.sin(ang)
    sin_a = np.where(first[None, :], -sin, 0.0)
    sin_b = np.where(first[None, :], 0.0, sin)
    return (jnp.asarray(cos, F32), jnp.asarray(sin_a, F32), jnp.asarray(sin_b, F32))


def _rope_store(r, cos, sin_a, sin_b, out_ref, row0, transposed=False):
    n_rows = r.shape[0]
    for j in range(ATTN_WIDTH // LANES):
        xs = r[:, j * LANES:(j + 1) * LANES]
        rot = (xs * cos + pltpu.roll(xs, LANES - HEAD_DIM // 4, 1) * sin_a
               + pltpu.roll(xs, HEAD_DIM // 4, 1) * sin_b)
        if transposed:
            rot_t = rot.T.astype(BF16)
            for gi in range(n_rows // KEY_GROUP):
                out_ref[0, row0 // KEY_GROUP + gi, j * LANES:(j + 1) * LANES, :] = (
                    rot_t[:, gi * KEY_GROUP:(gi + 1) * KEY_GROUP])
        else:
            out_ref[0, row0:row0 + n_rows, j * LANES:(j + 1) * LANES] = rot.astype(BF16)


INPROJ_SUBTILE = 512


def _inproj_kernel(x_ref, mod_ref, g_ref, w_ref, cos_ref, sa_ref, sb_ref, perm_ref,
                   q_ref, k_ref, v_ref, u_ref, ut_ref, ga_ref, gs_ref):
    shift = mod_ref[0, :, 0:D_MODEL]
    scale = mod_ref[0, :, D_MODEL:2 * D_MODEL]
    aw = ATTN_WIDTH
    dot = functools.partial(jnp.dot, preferred_element_type=F32)
    for row0 in range(0, x_ref.shape[1], INPROJ_SUBTILE):
        rows = slice(row0, row0 + INPROJ_SUBTILE)
        nb = _rms_modulate(x_ref[0, rows, :], g_ref[...], shift, scale).astype(BF16)
        cos = cos_ref[rows, :]
        sin_a = sa_ref[rows, :]
        sin_b = sb_ref[rows, :]
        _rope_store(dot(nb, w_ref[:, 0:aw]), cos, sin_a, sin_b, q_ref, row0)
        _rope_store(dot(nb, w_ref[:, aw:2 * aw]), cos, sin_a, sin_b, k_ref, row0, transposed=True)
        v_ref[0, rows, :] = dot(nb, w_ref[:, 2 * aw:3 * aw]).astype(BF16)
        c0 = 3 * aw
        u16 = dot(nb, w_ref[:, c0:c0 + SSM_WIDTH]).astype(BF16)
        u_ref[0, rows, :] = u16
        n16 = CHUNK * CHUNK
        _store_chunk_layout(u16, perm_ref, ut_ref, INPROJ_SUBTILE // n16, row0 // n16)
        c1 = c0 + SSM_WIDTH
        ga_ref[0, rows, :] = jax.nn.sigmoid(dot(nb, w_ref[:, c1:c1 + D_MODEL])).astype(BF16)
        c2 = c1 + D_MODEL
        gs_ref[0, rows, :] = jax.nn.sigmoid(dot(nb, w_ref[:, c2:c2 + D_MODEL])).astype(BF16)


def _input_projection(x, mod3, norm_g, w_in_bf16, rope, perm):
    tm = 2 * INPROJ_SUBTILE
    n = CHUNK * CHUNK
    cos, sin_a, sin_b = rope
    tok = lambda width: pl.BlockSpec((1, tm, width), lambda i, b: (b, i, 0))
    tab = pl.BlockSpec((tm, LANES), lambda i, b: (i, 0))
    out = lambda width: jax.ShapeDtypeStruct((BATCH, SEQ, width), BF16)
    return pl.pallas_call(
        _inproj_kernel,
        grid=(SEQ // tm, BATCH),
        in_specs=[tok(D_MODEL),
                  pl.BlockSpec((1, 1, N_MOD * D_MODEL), lambda i, b: (b, 0, 0)),
                  pl.BlockSpec((1, D_MODEL), lambda i, b: (0, 0)),
                  pl.BlockSpec((D_MODEL, IN_COLS), lambda i, b: (0, 0), pipeline_mode=pl.Buffered(1)),
                  tab, tab, tab,
                  pl.BlockSpec((n, n), lambda i, b: (0, 0), pipeline_mode=pl.Buffered(1))],
        out_specs=[tok(ATTN_WIDTH),
                   pl.BlockSpec((1, tm // KEY_GROUP, ATTN_WIDTH, KEY_GROUP), lambda i, b: (b, i, 0, 0)),
                   tok(ATTN_WIDTH), tok(SSM_WIDTH),
                   pl.BlockSpec((SSM_GROUPS, 1, tm // CHUNK, CHUNK_COLS), lambda i, b: (0, b, i, 0)),
                   tok(D_MODEL), tok(D_MODEL)],
        out_shape=[out(ATTN_WIDTH),
                   jax.ShapeDtypeStruct((BATCH, SEQ // KEY_GROUP, ATTN_WIDTH, KEY_GROUP), BF16),
                   out(ATTN_WIDTH), out(SSM_WIDTH),
                   jax.ShapeDtypeStruct((SSM_GROUPS, BATCH, N_CHUNKS, CHUNK_COLS), BF16),
                   out(D_MODEL), out(D_MODEL)],
        compiler_params=pltpu.CompilerParams(
            dimension_semantics=("arbitrary", "arbitrary"), vmem_limit_bytes=VMEM_LIMIT),
        name="input_projection",
    )(x, mod3, norm_g, w_in_bf16, cos, sin_a, sin_b, perm)


def _ctx_proj_kernel(x_ref, mod_ref, g_ref, wk_ref, wv_ref, wu_ref, perm_ref, k_ref, v_ref, ut_ref):
    x = x_ref[0]
    shift = mod_ref[0, :, 0:D_MODEL]
    scale = mod_ref[0, :, D_MODEL:2 * D_MODEL]
    nb = _rms_modulate(x, g_ref[...], shift, scale).astype(BF16)
    k_ref[0] = jnp.dot(nb, wk_ref[...], preferred_element_type=F32).T.astype(BF16)
    v_ref[0] = jnp.dot(nb, wv_ref[...], preferred_element_type=F32).astype(BF16)
    u16 = jnp.dot(nb, wu_ref[...], preferred_element_type=F32).astype(BF16)
    _store_chunk_layout(u16, perm_ref, ut_ref, 1)


def _context_projection(ctx, mod3, norm_g, w_in_bf16, perm):
    n = CHUNK * CHUNK
    aw = ATTN_WIDTH
    tok = lambda width: pl.BlockSpec((1, CTX_LEN, width), lambda b: (b, 0, 0))
    wcol = lambda j: pl.BlockSpec((D_MODEL, aw), lambda b: (0, j))
    out = jax.ShapeDtypeStruct((BATCH, CTX_LEN, aw), BF16)
    return pl.pallas_call(
        _ctx_proj_kernel,
        grid=(BATCH,),
        in_specs=[tok(D_MODEL),
                  pl.BlockSpec((1, 1, N_MOD * D_MODEL), lambda b: (CTX_MOD_ROW, 0, 0)),
                  pl.BlockSpec((1, D_MODEL), lambda b: (0, 0)),
                  wcol(1), wcol(2), wcol(3),
                  pl.BlockSpec((n, n), lambda b: (0, 0))],
        out_specs=[pl.BlockSpec((1, aw, CTX_LEN), lambda b: (b, 0, 0)), tok(aw),
                   pl.BlockSpec((SSM_GROUPS, 1, N_CTX_CHUNKS, CHUNK_COLS), lambda b: (0, b, 0, 0))],
        out_shape=[jax.ShapeDtypeStruct((BATCH, aw, CTX_LEN), BF16), out,
                   jax.ShapeDtypeStruct((SSM_GROUPS, BATCH, N_CTX_CHUNKS, CHUNK_COLS), BF16)],
        name="context_projection",
    )(ctx, mod3, norm_g, w_in_bf16, w_in_bf16, w_in_bf16, perm)


def _window_start_rows(r):
    return min(max(r - WIN_H // 2, 0), GRID_ROWS - WIN_H)


def _key_block_row(jb):
    return min(max(Q_ROWS_PER_STEP * jb - WIN_H // 2, 0), GRID_ROWS - K_ROWS_PER_STEP)


def _window_geometry(jb):
    key_row0 = _key_block_row(jb)
    offs, deltas = [], []
    for i in range(Q_ROWS_PER_STEP):
        r = Q_ROWS_PER_STEP * jb + i
        rs = _window_start_rows(r)
        offs.append(rs - key_row0)
        deltas.append(r - rs)
    return offs, deltas


def _attn_kernel(q_ref, k_ref, v_ref, kc_ref, vc_ref, bias_ref, o_ref,
                 s_ref, sc_ref, p_ref, pc_ref, l_ref, m_ref, acc_ref):
    lane = lax.broadcasted_iota(jnp.int32, (1, LANES), 1)
    left = lane < HEAD_DIM
    n_ktiles = K_BLOCK // LANES
    last = N_QBLOCKS - 1
    max_key_start = (GRID_ROWS - K_ROWS_PER_STEP) * GRID_W

    def scores(sl, e, q_start, key_start):
        q2 = q_ref[0, pl.ds(q_start, Q_BLOCK), :]
        qm = jnp.where(left if e == 0 else jnp.logical_not(left), q2, jnp.zeros_like(q2))
        group0 = key_start // KEY_GROUP
        for gi in range(K_BLOCK // KEY_GROUP):
            s_ref[sl, :, gi * KEY_GROUP:(gi + 1) * KEY_GROUP] = jnp.dot(
                qm, k_ref[0, group0 + gi], preferred_element_type=F32)
        sc_ref[sl] = jnp.dot(qm, kc_ref[0], preferred_element_type=F32)

    def softmax(sl, e, offs, deltas):
        def geometry(sub):
            i = sub * SOFTMAX_ROWS // GRID_W
            qcols = slice(sub * SOFTMAX_ROWS % GRID_W, sub * SOFTMAX_ROWS % GRID_W + SOFTMAX_ROWS)
            rows = slice(sub * SOFTMAX_ROWS, (sub + 1) * SOFTMAX_ROWS)
            par = offs[i] % 2
            return rows, qcols, par, offs[i] // 2, WIN_H // 2 + par, deltas[i]

        def tile(rows, qcols, par, t0, n_tiles, delta, xt):
            t = t0 + xt
            dr0 = 2 * xt - par - delta
            st = s_ref[sl, rows, t * LANES:(t + 1) * LANES] + bias_ref[0, e, dr0 + WIN_H, qcols, :]
            if par and xt == 0:
                st = jnp.where(left, NEG_BIG, st)
            if par and xt == n_tiles - 1:
                st = jnp.where(left, st, NEG_BIG)
            return st

        n_sub = Q_BLOCK // SOFTMAX_ROWS
        for sub in range(n_sub):
            geo = geometry(sub)
            rows, n_tiles = geo[0], geo[4]
            mt = jnp.maximum(sc_ref[sl, rows, 0:LANES], sc_ref[sl, rows, LANES:2 * LANES])
            for xt in range(n_tiles):
                mt = jnp.maximum(mt, tile(*geo, xt))
            m_ref[sl, rows, :] = jnp.broadcast_to(jnp.max(mt, axis=1, keepdims=True), (SOFTMAX_ROWS, LANES))
        for sub in range(n_sub):
            geo = geometry(sub)
            rows, t0, n_tiles = geo[0], geo[3], geo[4]
            m = m_ref[sl, rows, :]
            lt = None
            for xt in range(n_tiles):
                t = t0 + xt
                pt = jnp.exp2(tile(*geo, xt) - m)
                lt = pt if lt is None else lt + pt
                p_ref[sl, rows, t * LANES:(t + 1) * LANES] = pt.astype(BF16)
            for t in range(n_ktiles):
                if not (t0 <= t < t0 + n_tiles):
                    p_ref[sl, rows, t * LANES:(t + 1) * LANES] = jnp.zeros((SOFTMAX_ROWS, LANES), BF16)
            for ci in range(2):
                pt = jnp.exp2(sc_ref[sl, rows, ci * LANES:(ci + 1) * LANES] - m)
                lt = lt + pt
                pc_ref[sl, rows, ci * LANES:(ci + 1) * LANES] = pt.astype(BF16)
            l = jnp.sum(lt, axis=1, keepdims=True)
            l_ref[sl, rows, :] = jnp.broadcast_to(1.0 / l, (SOFTMAX_ROWS, LANES))

    def values(sl, key_start):
        vblk = v_ref[0, pl.ds(key_start, K_BLOCK), :]
        return (jnp.dot(p_ref[sl], vblk, preferred_element_type=F32)
                + jnp.dot(pc_ref[sl], vc_ref[0], preferred_element_type=F32)) * l_ref[sl]

    def block(jb_static, cur, nxt):
        offs, deltas = _window_geometry(jb_static)
        scores(1, 1, *cur)
        softmax(0, 0, offs, deltas)
        acc_ref[0] = values(0, cur[1])
        if nxt is not None:
            scores(0, 0, *nxt)
        softmax(1, 1, offs, deltas)
        o1 = values(1, cur[1])
        o_ref[0, pl.ds(cur[0], Q_BLOCK), :] = jnp.where(left, acc_ref[0], o1).astype(BF16)

    def starts(jb):
        q_start = pl.multiple_of(jb * Q_BLOCK, Q_BLOCK)
        key_start = jnp.clip((jb * Q_ROWS_PER_STEP - WIN_H // 2) * GRID_W, 0, max_key_start)
        return q_start, pl.multiple_of(key_start, Q_BLOCK)

    scores(0, 0, 0, 0)
    block(0, (0, 0), (Q_BLOCK, 0))

    blocks_per_trip = 2

    def interior(trip, carry):
        jb = 1 + blocks_per_trip * trip
        for d in range(blocks_per_trip):
            block(1, starts(jb + d), starts(jb + d + 1))
        return carry

    assert (last - 1) % blocks_per_trip == 0
    lax.fori_loop(0, (last - 1) // blocks_per_trip, interior, 0)
    block(last, (last * Q_BLOCK, max_key_start), None)


def _bias_tables(rpb):
    qcol = np.arange(GRID_W)
    kcol = np.arange(GRID_W)
    col_start = np.clip(qcol - WIN_W // 2, 0, GRID_W - WIN_W)
    in_win = (kcol[None, :] >= col_start[:, None]) & (kcol[None, :] < col_start[:, None] + WIN_W)
    dc_idx = np.clip(kcol[None, :] - qcol[:, None], -(WIN_W - 1), WIN_W - 1) + WIN_W - 1
    sel = (np.arange(2 * WIN_W - 1)[:, None, None] == dc_idx[None]).astype(np.float32)
    toe = jnp.einsum('hdt,tck->hdck', rpb * LOG2E, jnp.asarray(sel),
                     precision=lax.Precision.HIGHEST)
    toe = jnp.where(in_win[None, None], toe, NEG_BIG)
    neg = jnp.full((N_HEADS, 1, GRID_W, GRID_W), NEG_BIG, F32)
    ext = jnp.concatenate([neg, toe, neg], axis=1)
    pair = jnp.concatenate([ext[:, 0:16], ext[:, 1:17]], axis=-1)
    return pair.reshape(N_HEADS // 2, 2, 16, GRID_W, LANES)


def _attention(q, k, v, kc, vc, bias):
    n_slots = 2 * N_CHAINS
    kspec = pl.BlockSpec((1, SEQ, LANES), lambda hp, b: (b, 0, hp))
    ktspec =pl.BlockSpec((1, SEQ // KEY_GROUP, LANES, KEY_GROUP), lambda hp, b: (b, 0, hp, 0))
    cspec = pl.BlockSpec((1, CTX_LEN, LANES), lambda hp, b: (b, 0, hp))
    ctspec = pl.BlockSpec((1, LANES, CTX_LEN), lambda hp, b: (b, hp, 0))
    bspec = pl.BlockSpec((1, 2, 16, GRID_W, LANES), lambda hp, b: (hp, 0, 0, 0, 0))
    return pl.pallas_call(
        _attn_kernel,
        grid=(N_HEADS // 2, BATCH),
        in_specs=[kspec, ktspec, kspec, ctspec, cspec, bspec],
        out_specs=kspec,
        out_shape=jax.ShapeDtypeStruct((BATCH, SEQ, ATTN_WIDTH), BF16),
        scratch_shapes=[pltpu.VMEM((n_slots, Q_BLOCK, K_BLOCK), F32),
                        pltpu.VMEM((n_slots, Q_BLOCK, CTX_LEN), F32),
                        pltpu.VMEM((n_slots, Q_BLOCK, K_BLOCK), BF16),
                        pltpu.VMEM((n_slots, Q_BLOCK, CTX_LEN), BF16),
                        pltpu.VMEM((n_slots, Q_BLOCK, LANES), F32),
                        pltpu.VMEM((n_slots, Q_BLOCK, LANES), F32),
                        pltpu.VMEM((N_CHAINS, Q_BLOCK, LANES), F32)],
        compiler_params=pltpu.CompilerParams(
            dimension_semantics=("arbitrary", "arbitrary"),
            vmem_limit_bytes=VMEM_LIMIT),
        name="attention",
    )(q, k, v, kc, vc, bias)


def _rot256(a, b, s, lane):
    s %= 2 * LANES
    if s >= LANES:
        a, b, s = b, a, s - LANES
    if s == 0:
        return a, b
    ra = pltpu.roll(a, s, 1)
    rb = pltpu.roll(b, s, 1)
    keep = lane >= s
    return jnp.where(keep, ra, rb), jnp.where(keep, rb, ra)


S5_PREP_GROUPS = 2
POWER_ROWS = 24


def _s5_prep_kernel(*refs):
    for gi in range(S5_PREP_GROUPS):
        _s5_prep_group(gi, *refs)


def _s5_prep_group(gi, par_ref, b_ref, c_ref, t16_ref, e_ref, mi_ref, ms_ref, mo_ref, a_ref):
    dot = functools.partial(jnp.dot, preferred_element_type=F32)

    def split2(x):
        hi = x.astype(BF16)
        return hi, (x - hi.astype(F32)).astype(BF16)

    def split3(x):
        hi = x.astype(BF16)
        r1 = x - hi.astype(F32)
        mid = r1.astype(BF16)
        return hi, mid, (r1 - mid.astype(F32)).astype(BF16)

    def pick(x, onehot):
        hi, mid, lo = split3(x)
        return dot(hi, onehot) + dot(mid, onehot) + dot(lo, onehot)

    def dot_f32(a, b):
        ah, al = split2(a)
        bh, bl = split2(b)
        return dot(jnp.concatenate([ah, ah, al], axis=1), jnp.concatenate([bh, bl, bh], axis=0))

    ns = SSM_STATE
    lam_re, lam_im = par_ref[gi, 0:1, :], par_ref[gi, 1:2, :]
    dt = jnp.exp(par_ref[gi, 2:3, :])
    lane = lax.broadcasted_iota(jnp.int32, (1, LANES), 1)
    kf = lax.broadcasted_iota(jnp.int32, (POWER_ROWS, LANES), 0).astype(F32)
    mag = jnp.exp((lam_re * dt) * kf)
    ang = (lam_im * dt) * kf
    pwt_re, pwt_im = mag * jnp.cos(ang), mag * jnp.sin(ang)
    den = lam_re * lam_re + lam_im * lam_im
    nr, ni = pwt_re[1:2, :] - 1.0, pwt_im[1:2, :]
    f_re_row = (nr * lam_re + ni * lam_im) / den
    f_im_row = (ni * lam_re - nr * lam_im) / den
    r8 = lax.broadcasted_iota(jnp.int32, (8, LANES), 0)
    f_rows = jnp.where(r8 == 0, f_re_row, jnp.where(r8 == 1, f_im_row, 0.0))
    pad = jnp.zeros((LANES - POWER_ROWS - 8, LANES), F32)
    pw_re = jnp.concatenate([pwt_re, f_rows, pad], axis=0).T
    pw_im = jnp.concatenate([pwt_im, f_rows, pad], axis=0).T
    f_re, f_im = pw_re[:, POWER_ROWS:POWER_ROWS + 1], pw_re[:, POWER_ROWS + 1:POWER_ROWS + 2]
    b_re, b_im = b_ref[gi, :, 0:SSM_GROUP], b_ref[gi, :, SSM_GROUP:2 * SSM_GROUP]
    bb_re = f_re * b_re - f_im * b_im
    bb_im = f_re * b_im + f_im * b_re
    t16 = t16_ref[...]
    bbt_re, bbt_im = pick(bb_re, t16), pick(bb_im, t16)
    ct_re = pick(c_ref[gi, :, 0:SSM_GROUP], t16)
    ct_im = pick(c_ref[gi, :, SSM_GROUP:2 * SSM_GROUP], t16)
    pw_at = lambda x: (pick(pw_re, e_ref[x]), pick(pw_im, e_ref[x]))
    id_re, id_im = pw_at(0)
    rev_re, rev_im = pw_at(1)
    p1_re, p1_im = pw_at(2)
    r16_re, r16_im = pw_at(3)
    f, b = slice(0, ns), slice(ns, 2 * ns)
    cmul = lambda ar, ai, br, bi: (ar * br - ai * bi, ar * bi + ai * br)

    sf_re, sf_im = cmul(rev_re[f], rev_im[f], bbt_re[f], bbt_im[f])
    sb_re, sb_im = cmul(id_re[b], id_im[b], bbt_re[b], bbt_im[b])
    ms_ref[gi] = jnp.concatenate([sf_re, sb_re, sf_im, sb_im], axis=0).T.astype(BF16)

    of_re, of_im = cmul(p1_re[f], p1_im[f], ct_re[f], ct_im[f])
    ob_re, ob_im = cmul(r16_re[b], r16_im[b], ct_re[b], ct_im[b])
    mo_ref[gi] = jnp.concatenate([of_re, ob_re, -of_im, -ob_im], axis=0).astype(BF16)

    xf_re, xf_im = cmul(id_re[f], id_im[f], ct_re[f], ct_im[f])
    xb_re, xb_im = cmul(rev_re[b], rev_im[b], ct_re[b], ct_im[b])
    btf = jnp.concatenate([bbt_re[f], bbt_im[f]], axis=0).T
    btb = jnp.concatenate([bbt_re[b], bbt_im[b]], axis=0).T
    g_f = dot_f32(btf, jnp.concatenate([xf_re, -xf_im], axis=0))
    g_b = dot_f32(btb, jnp.concatenate([xb_re, -xb_im], axis=0))
    for j in range(CHUNK):
        rows = slice(SSM_GROUP * j, SSM_GROUP * (j + 1))
        lo_col, hi_col = SSM_GROUP * j, SSM_GROUP * (j + 1)
        f_lo, f_hi = _rot256(g_f[rows, :LANES], g_f[rows, LANES:], lo_col, lane)
        b_lo, b_hi = _rot256(g_b[rows, :LANES], g_b[rows, LANES:], -SSM_GROUP * (CHUNK - 1 - j), lane)
        lo = jnp.where(lane >= lo_col, f_lo, 0.0) + jnp.where(lane < hi_col, b_lo, 0.0)
        up = jnp.where(lane + LANES >= lo_col, f_hi, 0.0) + jnp.where(lane + LANES < hi_col, b_hi, 0.0)
        mi_ref[gi, rows, 0:LANES] = lo.astype(BF16)
        mi_ref[gi, rows, LANES:2 * LANES] = up.astype(BF16)

    a16 = jnp.concatenate([pwt_re[CHUNK:CHUNK + 1, :], pwt_im[CHUNK:CHUNK + 1, :]], axis=1)
    a_ref[gi] = jnp.broadcast_to(a16, (8, 2 * LANES))


def _s5_matrices(lam_re, lam_im, log_dt, b_re, b_im, c_re, c_im):
    g, p2 = SSM_GROUPS, 2 * SSM_STATE
    both = lambda a: jnp.transpose(a, (1, 0, 2)).reshape(g, p2)
    log_dt_rows = jnp.repeat(jnp.transpose(log_dt), SSM_STATE, axis=1)
    par = jnp.stack([both(lam_re), both(lam_im), log_dt_rows], axis=1)
    par = jnp.concatenate([par, jnp.zeros((g, LANES - 3, p2), F32)], axis=1)
    rows_b = lambda a: jnp.transpose(a, (1, 0, 2, 3)).reshape(g, p2, SSM_GROUP)
    rows_c = lambda a: jnp.transpose(a, (1, 0, 3, 2)).reshape(g, p2, SSM_GROUP)
    b_cat = jnp.concatenate([rows_b(b_re), rows_b(b_im)], axis=-1)
    c_cat = jnp.concatenate([rows_c(c_re), rows_c(c_im)], axis=-1)

    col = np.arange(CHUNK_COLS)
    tile16 = (col[None, :] % SSM_GROUP == np.arange(SSM_GROUP)[:, None]).astype(np.float32)
    pos = col // SSM_GROUP
    k_idx = np.arange(LANES)[:, None]
    expand = np.stack([k_idx == pos[None, :], k_idx == (CHUNK - 1 - pos)[None, :],
                       k_idx == (pos + 1)[None, :], k_idx == (CHUNK - pos)[None, :]]).astype(np.float32)

    gp = S5_PREP_GROUPS
    mat = lambda: pl.BlockSpec((gp, CHUNK_COLS, CHUNK_COLS), lambda i: (i, 0, 0))
    mat_shape = jax.ShapeDtypeStruct((g, CHUNK_COLS, CHUNK_COLS), BF16)
    return pl.pallas_call(
        _s5_prep_kernel,
        grid=(g // gp,),
        in_specs=[pl.BlockSpec((gp, LANES, p2), lambda i: (i, 0, 0)),
                  pl.BlockSpec((gp, p2, 2 * SSM_GROUP), lambda i: (i, 0, 0)),
                  pl.BlockSpec((gp, p2, 2 * SSM_GROUP), lambda i: (i, 0, 0)),
                  pl.BlockSpec((SSM_GROUP, CHUNK_COLS), lambda i: (0, 0)),
                  pl.BlockSpec((4, LANES, CHUNK_COLS), lambda i: (0, 0, 0))],
        out_specs=[mat(), mat(), mat(), pl.BlockSpec((gp, 8, 2 * LANES), lambda i: (i, 0, 0))],
        out_shape=[mat_shape, mat_shape, mat_shape, jax.ShapeDtypeStruct((g, 8, 2 * LANES), F32)],
        name="s5_prep",
    )(par, b_cat, c_cat, jnp.asarray(tile16, BF16), jnp.asarray(expand, BF16))


def _s5_kernel(ul_ref, uc_ref, ms_ref, mi_ref, mo_ref, a_ref, y_ref, s_ref, sc_ref, hp_ref, *, gb):
    for gi in range(gb):
        for b in range(BATCH):
            sb = jnp.dot(ul_ref[gi, b], ms_ref[gi], preferred_element_type=F32)
            s_ref[gi, 0, pl.ds(b, N_CHUNKS, stride=BATCH), :] = sb[:, :LANES]
            s_ref[gi, 1, pl.ds(b, N_CHUNKS, stride=BATCH), :] = sb[:, LANES:]
            cb = jnp.dot(uc_ref[gi, b], ms_ref[gi], preferred_element_type=F32)
            sc_ref[gi, 0, pl.ds(b, N_CTX_CHUNKS, stride=BATCH), :] = cb[:, :LANES]
            sc_ref[gi, 1, pl.ds(b, N_CTX_CHUNKS, stride=BATCH), :] = cb[:, LANES:]
    lane = lax.broadcasted_iota(jnp.int32, (BATCH, LANES), 1)
    fwd = lane < SSM_STATE
    half = SSM_STATE

    def advance(gi, h_re, h_im, row_f, row_b, src):
        s_re = jnp.where(fwd, src[gi, 0, pl.ds(row_f, BATCH), :], src[gi, 0, pl.ds(row_b, BATCH), :])
        s_im = jnp.where(fwd, src[gi, 1, pl.ds(row_f, BATCH), :], src[gi, 1, pl.ds(row_b, BATCH), :])
        a_re = a_ref[gi, :, 0:LANES]
        a_im = a_ref[gi, :, LANES:2 * LANES]
        n_re = a_re * h_re - a_im * h_im + s_re
        n_im = a_re * h_im + a_im * h_re + s_im
        return n_re, n_im

    def ctx_step(t, carry):
        row_f = pl.multiple_of(t * BATCH, BATCH)
        row_b = pl.multiple_of((N_CTX_CHUNKS - 1 - t) * BATCH, BATCH)
        return tuple(advance(gi, carry[gi][0], carry[gi][1], row_f, row_b, sc_ref) for gi in range(gb))

    def lat_step(t, carry):
        row_f = pl.multiple_of(t * BATCH, BATCH)
        row_b = pl.multiple_of((N_CHUNKS - 1 - t) * BATCH, BATCH)
        out = []
        for gi in range(gb):
            h_re, h_im = carry[gi]
            hp_ref[gi, 0, pl.ds(row_f, BATCH), 0:half] = h_re[:, 0:half]
            hp_ref[gi, 0, pl.ds(row_b, BATCH), half:2 * half] = h_re[:, half:]
            hp_ref[gi, 1, pl.ds(row_f, BATCH), 0:half] = h_im[:, 0:half]
            hp_ref[gi, 1, pl.ds(row_b, BATCH), half:2 * half] = h_im[:, half:]
            out.append(advance(gi, h_re, h_im, row_f, row_b, s_ref))
        return tuple(out)

    zero = jnp.zeros((BATCH, LANES), F32)
    carry = tuple((zero, zero) for _ in range(gb))
    carry = lax.fori_loop(0, N_CTX_CHUNKS, ctx_step, carry)
    lax.fori_loop(0, N_CHUNKS, lat_step, carry)
    for gi in range(gb):
        for b in range(BATCH):
            hb_re = hp_ref[gi, 0, pl.ds(b, N_CHUNKS, stride=BATCH), :].astype(BF16)
            hb_im = hp_ref[gi, 1, pl.ds(b, N_CHUNKS, stride=BATCH), :].astype(BF16)
            y = (jnp.dot(ul_ref[gi, b], mi_ref[gi], preferred_element_type=F32)
                 + jnp.dot(jnp.concatenate([hb_re, hb_im], axis=1), mo_ref[gi], preferred_element_type=F32))
            y_ref[gi, b] = y.astype(BF16)


def _s5_scan(u_lat_t, u_ctx_t, m_intra, m_state, m_out, a16):
    gb = 4
    rows = N_CHUNKS * BATCH
    crows = N_CTX_CHUNKS * BATCH
    grp = lambda r, c: pl.BlockSpec((gb, r, c), lambda g: (g, 0, 0))
    tok = lambda n: pl.BlockSpec((gb, BATCH, n, CHUNK_COLS), lambda g: (g, 0, 0, 0))
    return pl.pallas_call(
        functools.partial(_s5_kernel, gb=gb),
        grid=(SSM_GROUPS // gb,),
        in_specs=[tok(N_CHUNKS), tok(N_CTX_CHUNKS), grp(CHUNK_COLS, CHUNK_COLS),
                  grp(CHUNK_COLS, CHUNK_COLS), grp(CHUNK_COLS, CHUNK_COLS), grp(8, 2 * LANES)],
        out_specs=tok(N_CHUNKS),
        out_shape=jax.ShapeDtypeStruct((SSM_GROUPS, BATCH, N_CHUNKS, CHUNK_COLS), BF16),
        scratch_shapes=[pltpu.VMEM((gb, 2, rows, LANES), F32),
                        pltpu.VMEM((gb, 2, crows, LANES), F32),
                        pltpu.VMEM((gb, 2, rows, LANES), F32)],
        compiler_params=pltpu.CompilerParams(
            dimension_semantics=("arbitrary",), vmem_limit_bytes=VMEM_LIMIT),
        name="s5_scan",
    )(u_lat_t, u_ctx_t, m_state, m_intra, m_out, a16)


FFN_TILE = 256


def _post_kernel(x_ref, a_ref, yt_ref, u_ref, ga_ref, gs_ref, mod_ref, d_ref, fg_ref, og_ref, permt_ref,
                 wglu_ref, wba_ref, wbs_ref, wout_ref, wfi_ref, wfo_ref, o_ref, h1_ref, n2_ref, act_ref, r_ref):
    dm = D_MODEL
    g1 = mod_ref[0, :, 2 * dm:3 * dm]
    sh2 = mod_ref[0, :, 3 * dm:4 * dm]
    sc2 = mod_ref[0, :, 4 * dm:5 * dm]
    g2 = mod_ref[0, :, 5 * dm:6 * dm]
    half = CHUNK * CHUNK
    halves = [slice(h * half, (h + 1) * half) for h in range(x_ref.shape[1] // half)]
    dot = functools.partial(jnp.dot, preferred_element_type=F32)

    sp = []
    for h, rows in enumerate(halves):
        y = _load_chunk_layout(yt_ref, permt_ref, r_ref, h)
        sp.append(jax.nn.gelu(y + d_ref[...] * u_ref[0, rows, :].astype(F32)).astype(BF16))
    s = []
    for h, rows in enumerate(halves):
        vg = dot(sp[h], wglu_ref[...])
        s.append((vg[:, :SSM_WIDTH] * jax.nn.sigmoid(vg[:, SSM_WIDTH:])).astype(BF16))
    merged = []
    for h, rows in enumerate(halves):
        m = (ga_ref[0, rows, :].astype(F32) * dot(a_ref[0, rows, :], wba_ref[...])
             + gs_ref[0, rows, :].astype(F32) * dot(s[h], wbs_ref[...]))
        merged.append(m.astype(BF16))
    for h, rows in enumerate(halves):
        h1 = x_ref[0, rows, :] + g1 * dot(merged[h], wout_ref[...])
        h1_ref[rows, :] = h1
        n2_ref[rows, :] = _rms_modulate(h1, fg_ref[...], sh2, sc2).astype(BF16)
    for lo in range(0, FFN_HIDDEN, FFN_TILE):
        width = min(FFN_TILE, FFN_HIDDEN - lo)
        for rows in halves:
            n2 = n2_ref[rows, :]
            fa = dot(n2, wfi_ref[:, lo:lo + width])
            fb = dot(n2, wfi_ref[:, FFN_HIDDEN + lo:FFN_HIDDEN + lo + width])
            act_ref[rows, lo:lo + width] = (fa * jax.nn.sigmoid(fa) * fb).astype(BF16)
    for rows in halves:
        h2 = h1_ref[rows, :] + g2 * dot(act_ref[rows, :], wfo_ref[...])
        o_ref[0, rows, :] = (h2 * lax.rsqrt(jnp.mean(h2 * h2, axis=-1, keepdims=True) + NORM_EPS)) * og_ref[...]


def _post(x, a, y_t, u, ga, gs, mod3, d_skip, ffn_g, fin_g, perm_t, wglu, wba, wbs, wout, wfi, wfo):
    tm = 512
    n = CHUNK * CHUNK
    tok = lambda width: pl.BlockSpec((1, tm, width), lambda b, i: (b, i, 0))
    const = lambda r, c: pl.BlockSpec((r, c), lambda b, i: (0, 0), pipeline_mode=pl.Buffered(1))
    return pl.pallas_call(
        _post_kernel,
        grid=(BATCH, SEQ // tm),
        in_specs=[tok(D_MODEL), tok(ATTN_WIDTH),
                  pl.BlockSpec((SSM_GROUPS, 1, tm // CHUNK, CHUNK_COLS), lambda b, i: (0, b, i, 0)),
                  tok(SSM_WIDTH), tok(D_MODEL), tok(D_MODEL),
                  pl.BlockSpec((1, 1, N_MOD * D_MODEL), lambda b, i: (b, 0, 0)),
                  const(1, SSM_WIDTH), const(1, D_MODEL), const(1, D_MODEL), const(n, n),
                  const(SSM_WIDTH, 2 * SSM_WIDTH), const(ATTN_WIDTH, D_MODEL),
                  const(SSM_WIDTH, D_MODEL), const(D_MODEL, D_MODEL),
                  const(D_MODEL, 2 * FFN_HIDDEN), const(FFN_HIDDEN, D_MODEL)],
        out_specs=tok(D_MODEL),
        out_shape=jax.ShapeDtypeStruct((BATCH, SEQ, D_MODEL), F32),
        scratch_shapes=[pltpu.VMEM((tm, D_MODEL), F32), pltpu.VMEM((tm, D_MODEL), BF16),
                        pltpu.VMEM((tm, FFN_HIDDEN), BF16), pltpu.VMEM((tm // n, n, SSM_WIDTH), F32)],
        compiler_params=pltpu.CompilerParams(
            dimension_semantics=("arbitrary", "arbitrary"), vmem_limit_bytes=VMEM_LIMIT),
        name="post",
    )(x, a, y_t, u, ga, gs, mod3, d_skip, ffn_g, fin_g, perm_t, wglu, wba, wbs, wout, wfi, wfo)


def kernel(x, c, ctx, c_ctx, w_mod, b_mod, attn_norm_g, ffn_norm_g, w_in, rel_pos_bias,
           ssm_lambda_re, ssm_lambda_im, ssm_log_dt, ssm_b_re, ssm_b_im, ssm_c_re, ssm_c_im, ssm_d,
           w_glu, w_branch_attn, w_branch_ssm, w_out, w_ffn_in, w_ffn_out, final_norm_g):
    assert x.shape == (BATCH, SEQ, D_MODEL) and w_mod.shape[0] == 1
    c_rows = jnp.concatenate(
        [c, c_ctx[None, :], jnp.zeros((MOD_ROWS - BATCH - 1, D_MODEL), F32)], axis=0)
    mod3 = _modulation(c_rows, w_mod[0], b_mod[0]).reshape(MOD_ROWS, 1, N_MOD * D_MODEL)

    col_scale = jnp.concatenate([jnp.full((ATTN_WIDTH,), HEAD_DIM ** -0.5 * LOG2E, F32),
                                 jnp.ones((IN_COLS - ATTN_WIDTH,), F32)])
    w_in_bf16 = (w_in[0] * col_scale[None, :]).astype(BF16)
    norm_g = attn_norm_g[0].reshape(1, D_MODEL)

    perm = _chunk_perm()
    q, k, v, u, u_t, ga, gs = _input_projection(x, mod3, norm_g, w_in_bf16, _rope_tables(), perm)
    kc, vc, uc_t = _context_projection(ctx, mod3, norm_g, w_in_bf16, perm)

    attn = _attention(q, k, v, kc, vc, _bias_tables(rel_pos_bias[0]))

    m_intra, m_state, m_out, a16 = _s5_matrices(
        ssm_lambda_re[0], ssm_lambda_im[0], ssm_log_dt[0], ssm_b_re[0], ssm_b_im[0],
        ssm_c_re[0], ssm_c_im[0])
    y_t = _s5_scan(u_t, uc_t, m_intra, m_state, m_out, a16)

    return _post(x, attn, y_t, u, ga, gs, mod3,
                 ssm_d[0].reshape(1, SSM_WIDTH), ffn_norm_g[0].reshape(1, D_MODEL),
                 final_norm_g.reshape(1, D_MODEL), perm,
                 w_glu[0].astype(BF16), w_branch_attn[0].astype(BF16), w_branch_ssm[0].astype(BF16),
                 w_out[0].astype(BF16), w_ffn_in[0].astype(BF16), w_ffn_out[0].astype(BF16))
```

```python
import functools
import math

import numpy as np
import jax
import jax.numpy as jnp
from jax import lax
from jax.experimental import pallas as pl
from jax.experimental.pallas import tpu as pltpu

F32 = jnp.float32
BF16 = jnp.bfloat16

D_MODEL = 1024
BATCH = 8
SEQ = 4096
GRID_W = 64
GRID_ROWS = SEQ // GRID_W
CTX_LEN = 256
N_HEADS = 8
HEAD_DIM = 64
ATTN_WIDTH = N_HEADS * HEAD_DIM
WIN_H = 8
WIN_W = 16
ROPE_BASE = 10000.0
SSM_WIDTH = 512
SSM_GROUP = 16
SSM_GROUPS = SSM_WIDTH // SSM_GROUP
SSM_STATE = 64
FFN_HIDDEN = 2816
IN_COLS = 3 * ATTN_WIDTH + SSM_WIDTH + 2 * D_MODEL
N_MOD = 6
NORM_EPS = 1e-6
NEG_BIG = -1e30
LOG2E = math.log2(math.e)

LANES = 128
CHUNK = 16
N_CHUNKS = SEQ // CHUNK
N_CTX_CHUNKS = CTX_LEN // CHUNK
CHUNK_COLS = CHUNK * SSM_GROUP
MOD_ROWS = 16
CTX_MOD_ROW = BATCH
VMEM_LIMIT = 56 * 1024 * 1024

Q_ROWS_PER_STEP = 4
Q_BLOCK = Q_ROWS_PER_STEP * GRID_W
K_ROWS_PER_STEP = Q_ROWS_PER_STEP + WIN_H
K_BLOCK = K_ROWS_PER_STEP * GRID_W
N_QBLOCKS = GRID_ROWS // Q_ROWS_PER_STEP
KEY_GROUP = Q_BLOCK
SOFTMAX_ROWS = 32

def _rms_modulate(x, g, shift, scale):
    xn = x * lax.rsqrt(jnp.mean(x * x, axis=-1, keepdims=True) + NORM_EPS)
    return (xn * g) * (1.0 + scale) + shift


def _block_transpose8(vs, lane):
    for shift in (64, 32, 16):
        keep = (lane & (2 * shift - 1)) < shift
        dist = shift // SSM_GROUP
        out = list(vs)
        for a in range(8):
            if a & dist:
                continue
            b = a + dist
            out[a] = jnp.where(keep, vs[a], pltpu.roll(vs[b], shift, 1))
            out[b] = jnp.where(keep, pltpu.roll(vs[a], LANES - shift, 1), vs[b])
        vs = out
    return vs


def _chunk_perm():
    n = CHUNK * CHUNK
    r = np.arange(n)
    m = np.zeros((n, n), np.float32)
    m[r, (r % CHUNK) * CHUNK + r // CHUNK] = 1.0
    return jnp.asarray(m, BF16)


def _store_chunk_layout(u16, perm_ref, out_ref, n_groups16, group0=0):
    lane = lax.broadcasted_iota(jnp.int32, (1, LANES), 1)
    n = CHUNK * CHUNK
    for hf in range(n_groups16):
        r = jnp.dot(perm_ref[...], u16[hf * n:(hf + 1) * n, :], preferred_element_type=F32)
        c0 = (group0 + hf) * CHUNK
        for v in range(SSM_WIDTH // LANES):
            for jh in range(2):
                vs = [r[CHUNK * (8 * jh + jp):CHUNK * (8 * jh + jp + 1), v * LANES:(v + 1) * LANES]
                      for jp in range(8)]
                outs = _block_transpose8(vs, lane)
                for gi in range(8):
                    out_ref[8 * v + gi, 0, c0:c0 + CHUNK, jh * LANES:(jh + 1) * LANES] = outs[gi].astype(BF16)


def _load_chunk_layout(yt_ref, perm_t_ref, r_ref, hf):
    lane = lax.broadcasted_iota(jnp.int32, (1, LANES), 1)
    for v in range(SSM_WIDTH // LANES):
        for jh in range(2):
            vs = [yt_ref[8 * v + gi, 0, hf * CHUNK:(hf + 1) * CHUNK,
                         jh * LANES:(jh + 1) * LANES].astype(F32) for gi in range(8)]
            outs = _block_transpose8(vs, lane)
            for jp in range(8):
                r_ref[hf, CHUNK * (8 * jh + jp):CHUNK * (8 * jh + jp + 1), v * LANES:(v + 1) * LANES] = outs[jp]
    return jnp.dot(perm_t_ref[...], r_ref[hf].astype(BF16), preferred_element_type=F32)


def _mod_kernel(c_ref, w_ref, b_ref, o_ref):
    c = c_ref[...]
    s = c * jax.nn.sigmoid(c)
    o_ref[...] = jnp.dot(s, w_ref[...], preferred_element_type=F32) + b_ref[...]


def _modulation(c_rows, w_mod, b_mod):
    n = N_MOD * D_MODEL
    tn = 1536
    return pl.pallas_call(
        _mod_kernel,
        grid=(n // tn,),
        in_specs=[pl.BlockSpec((MOD_ROWS, D_MODEL), lambda j: (0, 0)),
                  pl.BlockSpec((D_MODEL, tn), lambda j: (0, j)),
                  pl.BlockSpec((1, tn), lambda j: (0, j))],
        out_specs=pl.BlockSpec((MOD_ROWS, tn), lambda j: (0, j)),
        out_shape=jax.ShapeDtypeStruct((MOD_ROWS, n), F32),
        name="modulation",
    )(c_rows, w_mod, b_mod.reshape(1, n))


def _rope_tables():
    n_freq = HEAD_DIM // 4
    inv_freq = ROPE_BASE ** (-np.arange(n_freq, dtype=np.float64) / n_freq)
    t = np.arange(SEQ)
    lane = np.arange(LANES)
    d = lane % HEAD_DIM
    use_col = (d // (HEAD_DIM // 2)) == 1
    w = d % (HEAD_DIM // 2)
    first = w < n_freq
    pos = np.where(use_col[None, :], (t % GRID_W)[:, None], (t // GRID_W)[:, None]).astype(np.float64)
    ang = pos * inv_freq[w % n_freq][None, :]
    cos = np.cos(ang)
    sin = np.sin(ang)
    sin_a = np.where(first[None, :], -sin, 0.0)
    sin_b = np.where(first[None, :], 0.0, sin)
    return (jnp.asarray(cos, F32), jnp.asarray(sin_a, F32), jnp.asarray(sin_b, F32))


def _rope_store(r, cos, sin_a, sin_b, out_ref, row0, transposed=False):
    n_rows = r.shape[0]
    for j in range(ATTN_WIDTH // LANES):
        xs = r[:, j * LANES:(j + 1) * LANES]
        rot = (xs * cos + pltpu.roll(xs, LANES - HEAD_DIM // 4, 1) * sin_a
               + pltpu.roll(xs, HEAD_DIM // 4, 1) * sin_b)
        if transposed:
            rot_t = rot.T.astype(BF16)
            for gi in range(n_rows // KEY_GROUP):
                out_ref[0, row0 // KEY_GROUP + gi, j * LANES:(j + 1) * LANES, :] = (
                    rot_t[:, gi * KEY_GROUP:(gi + 1) * KEY_GROUP])
        else:
            out_ref[0, j, row0:row0 + n_rows, :] = rot.astype(BF16)


INPROJ_SUBTILE = 512


def _inproj_kernel(x_ref, mod_ref, g_ref, w_ref, cos_ref, sa_ref, sb_ref, perm_ref,
                   q_ref, k_ref, v_ref, u_ref, ut_ref, ga_ref, gs_ref):
    shift = mod_ref[0, :, 0:D_MODEL]
    scale = mod_ref[0, :, D_MODEL:2 * D_MODEL]
    aw = ATTN_WIDTH
    dot = functools.partial(jnp.dot, preferred_element_type=F32)
    for row0 in range(0, x_ref.shape[1], INPROJ_SUBTILE):
        rows = slice(row0, row0 + INPROJ_SUBTILE)
        nb = _rms_modulate(x_ref[0, rows, :], g_ref[...], shift, scale).astype(BF16)
        cos = cos_ref[rows, :]
        sin_a = sa_ref[rows, :]
        sin_b = sb_ref[rows, :]
        _rope_store(dot(nb, w_ref[:, 0:aw]), cos, sin_a, sin_b, q_ref, row0)
        _rope_store(dot(nb, w_ref[:, aw:2 * aw]), cos, sin_a, sin_b, k_ref, row0, transposed=True)
        v = dot(nb, w_ref[:, 2 * aw:3 * aw]).astype(BF16)
        for j in range(aw // LANES):
            v_ref[0, j, rows, :] = v[:, j * LANES:(j + 1) * LANES]
        c0 = 3 * aw
        u16 = dot(nb, w_ref[:, c0:c0 + SSM_WIDTH]).astype(BF16)
        u_ref[0, rows, :] = u16
        n16 = CHUNK * CHUNK
        _store_chunk_layout(u16, perm_ref, ut_ref, INPROJ_SUBTILE // n16, row0 // n16)
        c1 = c0 + SSM_WIDTH
        ga_ref[0, rows, :] = jax.nn.sigmoid(dot(nb, w_ref[:, c1:c1 + D_MODEL])).astype(BF16)
        c2 = c1 + D_MODEL
        gs_ref[0, rows, :] = jax.nn.sigmoid(dot(nb, w_ref[:, c2:c2 + D_MODEL])).astype(BF16)


def _input_projection(x, mod3, norm_g, w_in_bf16, rope, perm):
    tm = 2 * INPROJ_SUBTILE
    n = CHUNK * CHUNK
    cos, sin_a, sin_b = rope
    tok = lambda width: pl.BlockSpec((1, tm, width), lambda i, b: (b, i, 0))
    tab = pl.BlockSpec((tm, LANES), lambda i, b: (i, 0))
    out = lambda width: jax.ShapeDtypeStruct((BATCH, SEQ, width), BF16)
    pairs = pl.BlockSpec((1, N_HEADS // 2, tm, LANES), lambda i, b: (b, 0, i, 0))
    pairs_shape = jax.ShapeDtypeStruct((BATCH, N_HEADS // 2, SEQ, LANES), BF16)
    return pl.pallas_call(
        _inproj_kernel,
        grid=(SEQ // tm, BATCH),
        in_specs=[tok(D_MODEL),
                  pl.BlockSpec((1, 1, N_MOD * D_MODEL), lambda i, b: (b, 0, 0)),
                  pl.BlockSpec((1, D_MODEL), lambda i, b: (0, 0)),
                  pl.BlockSpec((D_MODEL, IN_COLS), lambda i, b: (0, 0), pipeline_mode=pl.Buffered(1)),
                  tab, tab, tab,
                  pl.BlockSpec((n, n), lambda i, b: (0, 0), pipeline_mode=pl.Buffered(1))],
        out_specs=[pairs,
                   pl.BlockSpec((1, tm // KEY_GROUP, ATTN_WIDTH, KEY_GROUP), lambda i, b: (b, i, 0, 0)),
                   pairs, tok(SSM_WIDTH),
                   pl.BlockSpec((SSM_GROUPS, 1, tm // CHUNK, CHUNK_COLS), lambda i, b: (0, b, i, 0)),
                   tok(D_MODEL), tok(D_MODEL)],
        out_shape=[pairs_shape,
                   jax.ShapeDtypeStruct((BATCH, SEQ // KEY_GROUP, ATTN_WIDTH, KEY_GROUP), BF16),
                   pairs_shape, out(SSM_WIDTH),
                   jax.ShapeDtypeStruct((SSM_GROUPS, BATCH, N_CHUNKS, CHUNK_COLS), BF16),
                   out(D_MODEL), out(D_MODEL)],
        compiler_params=pltpu.CompilerParams(
            dimension_semantics=("arbitrary", "arbitrary"), vmem_limit_bytes=VMEM_LIMIT),
        name="input_projection",
    )(x, mod3, norm_g, w_in_bf16, cos, sin_a, sin_b, perm)


def _ctx_proj_kernel(x_ref, mod_ref, g_ref, wk_ref, wv_ref, wu_ref, perm_ref, k_ref, v_ref, ut_ref):
    x = x_ref[0]
    shift = mod_ref[0, :, 0:D_MODEL]
    scale = mod_ref[0, :, D_MODEL:2 * D_MODEL]
    nb = _rms_modulate(x, g_ref[...], shift, scale).astype(BF16)
    k_ref[0] = jnp.dot(nb, wk_ref[...], preferred_element_type=F32).T.astype(BF16)
    v_ref[0] = jnp.dot(nb, wv_ref[...], preferred_element_type=F32).astype(BF16)
    u16 = jnp.dot(nb, wu_ref[...], preferred_element_type=F32).astype(BF16)
    _store_chunk_layout(u16, perm_ref, ut_ref, 1)


def _context_projection(ctx, mod3, norm_g, w_in_bf16, perm):
    n = CHUNK * CHUNK
    aw = ATTN_WIDTH
    tok = lambda width: pl.BlockSpec((1, CTX_LEN, width), lambda b: (b, 0, 0))
    wcol = lambda j: pl.BlockSpec((D_MODEL, aw), lambda b: (0, j))
    out = jax.ShapeDtypeStruct((BATCH, CTX_LEN, aw), BF16)
    return pl.pallas_call(
        _ctx_proj_kernel,
        grid=(BATCH,),
        in_specs=[tok(D_MODEL),
                  pl.BlockSpec((1, 1, N_MOD * D_MODEL), lambda b: (CTX_MOD_ROW, 0, 0)),
                  pl.BlockSpec((1, D_MODEL), lambda b: (0, 0)),
                  wcol(1), wcol(2), wcol(3),
                  pl.BlockSpec((n, n), lambda b: (0, 0))],
        out_specs=[pl.BlockSpec((1, aw, CTX_LEN), lambda b: (b, 0, 0)), tok(aw),
                   pl.BlockSpec((SSM_GROUPS, 1, N_CTX_CHUNKS, CHUNK_COLS), lambda b: (0, b, 0, 0))],
        out_shape=[jax.ShapeDtypeStruct((BATCH, aw, CTX_LEN), BF16), out,
                   jax.ShapeDtypeStruct((SSM_GROUPS, BATCH, N_CTX_CHUNKS, CHUNK_COLS), BF16)],
        name="context_projection",
    )(ctx, mod3, norm_g, w_in_bf16, w_in_bf16, w_in_bf16, perm)


def _window_start_rows(r):
    return min(max(r - WIN_H // 2, 0), GRID_ROWS - WIN_H)


def _key_block_row(jb):
    return min(max(Q_ROWS_PER_STEP * jb - WIN_H // 2, 0), GRID_ROWS - K_ROWS_PER_STEP)


def _window_geometry(jb):
    key_row0 = _key_block_row(jb)
    offs, deltas = [], []
    for i in range(Q_ROWS_PER_STEP):
        r = Q_ROWS_PER_STEP * jb + i
        rs = _window_start_rows(r)
        offs.append(rs - key_row0)
        deltas.append(r - rs)
    return offs, deltas


def _attn_kernel(q_ref, k_ref, v_ref, kc_ref, vc_ref, bias_ref, o_ref,
                 s_ref, sc_ref, p_ref, pc_ref, l_ref, m_ref, acc_ref):
    lane = lax.broadcasted_iota(jnp.int32, (1, LANES), 1)
    left = lane < HEAD_DIM
    n_ktiles = K_BLOCK // LANES
    last = N_QBLOCKS - 1
    max_key_start = (GRID_ROWS - K_ROWS_PER_STEP) * GRID_W

    def scores(sl, e, q_start, key_start):
        q2 = q_ref[0, 0, pl.ds(q_start, Q_BLOCK), :]
        qm = jnp.where(left if e == 0 else jnp.logical_not(left), q2, jnp.zeros_like(q2))
        group0 = key_start // KEY_GROUP
        for gi in range(K_BLOCK // KEY_GROUP):
            s_ref[sl, :, gi * KEY_GROUP:(gi + 1) * KEY_GROUP] = jnp.dot(
                qm, k_ref[0, group0 + gi], preferred_element_type=F32)
        sc_ref[sl] = jnp.dot(qm, kc_ref[0], preferred_element_type=F32)

    def softmax(sl, e, offs, deltas):
        def geometry(sub):
            i = sub * SOFTMAX_ROWS // GRID_W
            qcols = slice(sub * SOFTMAX_ROWS % GRID_W, sub * SOFTMAX_ROWS % GRID_W + SOFTMAX_ROWS)
            rows = slice(sub * SOFTMAX_ROWS, (sub + 1) * SOFTMAX_ROWS)
            par = offs[i] % 2
            return rows, qcols, par, offs[i] // 2, WIN_H // 2 + par, deltas[i]

        def tile(rows, qcols, par, t0, n_tiles, delta, xt):
            t = t0 + xt
            dr0 = 2 * xt - par - delta
            st = s_ref[sl, rows, t * LANES:(t + 1) * LANES] + bias_ref[0, e, dr0 + WIN_H, qcols, :]
            if par and xt == 0:
                st = jnp.where(left, NEG_BIG, st)
            if par and xt == n_tiles - 1:
                st = jnp.where(left, st, NEG_BIG)
            return st

        n_sub = Q_BLOCK // SOFTMAX_ROWS
        for sub in range(n_sub):
            geo = geometry(sub)
            rows, n_tiles = geo[0], geo[4]
            mt = jnp.maximum(sc_ref[sl, rows, 0:LANES], sc_ref[sl, rows, LANES:2 * LANES])
            for xt in range(n_tiles):
                mt = jnp.maximum(mt, tile(*geo, xt))
            m_ref[sl, rows, :] = jnp.broadcast_to(jnp.max(mt, axis=1, keepdims=True), (SOFTMAX_ROWS, LANES))
        for sub in range(n_sub):
            geo = geometry(sub)
            rows, t0, n_tiles = geo[0], geo[3], geo[4]
            m = m_ref[sl, rows, :]
            lt = None
            for xt in range(n_tiles):
                t = t0 + xt
                pt = jnp.exp2(tile(*geo, xt) - m)
                lt = pt if lt is None else lt + pt
                p_ref[sl, rows, t * LANES:(t + 1) * LANES] = pt.astype(BF16)
            for t in range(n_ktiles):
                if not (t0 <= t < t0 + n_tiles):
                    p_ref[sl, rows, t * LANES:(t + 1) * LANES] = jnp.zeros((SOFTMAX_ROWS, LANES), BF16)
            for ci in range(2):
                pt = jnp.exp2(sc_ref[sl, rows, ci * LANES:(ci + 1) * LANES] - m)
                lt = lt + pt
                pc_ref[sl, rows, ci * LANES:(ci + 1) * LANES] = pt.astype(BF16)
            l = jnp.sum(lt, axis=1, keepdims=True)
            l_ref[sl, rows, :] = jnp.broadcast_to(1.0 / l, (SOFTMAX_ROWS, LANES))

    def values(sl, key_start):
        vblk = v_ref[0, 0, pl.ds(key_start, K_BLOCK), :]
        return (jnp.dot(p_ref[sl], vblk, preferred_element_type=F32)
                + jnp.dot(pc_ref[sl], vc_ref[0], preferred_element_type=F32)) * l_ref[sl]

    def block(jb_static, cur, nxt):
        offs, deltas = _window_geometry(jb_static)
        scores(1, 1, *cur)
        softmax(0, 0, offs, deltas)
        acc_ref[0] = values(0, cur[1])
        if nxt is not None:
            scores(0, 0, *nxt)
        softmax(1, 1, offs, deltas)
        o1 = values(1, cur[1])
        o_ref[0, 0, pl.ds(cur[0], Q_BLOCK), :] = jnp.where(left, acc_ref[0], o1).astype(BF16)

    def starts(jb):
        q_start = pl.multiple_of(jb * Q_BLOCK, Q_BLOCK)
        key_start = jnp.clip((jb * Q_ROWS_PER_STEP - WIN_H // 2) * GRID_W, 0, max_key_start)
        return q_start, pl.multiple_of(key_start, Q_BLOCK)

    scores(0, 0, 0, 0)
    block(0, (0, 0), (Q_BLOCK, 0))

    blocks_per_trip = 2

    def interior(trip, carry):
        jb = 1 + blocks_per_trip * trip
        for d in range(blocks_per_trip):
            block(1, starts(jb + d), starts(jb + d + 1))
        return carry

    assert (last - 1) % blocks_per_trip == 0
    lax.fori_loop(0, (last - 1) // blocks_per_trip, interior, 0)
    block(last, (last * Q_BLOCK, max_key_start), None)


def _bias_tables(rpb):
    qcol = np.arange(GRID_W)
    kcol = np.arange(GRID_W)
    col_start = np.clip(qcol - WIN_W // 2, 0, GRID_W - WIN_W)
    in_win = (kcol[None, :] >= col_start[:, None]) & (kcol[None, :] < col_start[:, None] + WIN_W)
    dc_idx = np.clip(kcol[None, :] - qcol[:, None], -(WIN_W - 1), WIN_W - 1) + WIN_W - 1
    sel = (np.arange(2 * WIN_W - 1)[:, None, None] == dc_idx[None]).astype(np.float32)
    toe = jnp.einsum('hdt,tck->hdck', rpb * LOG2E, jnp.asarray(sel),
                     precision=lax.Precision.HIGHEST)
    toe = jnp.where(in_win[None, None], toe, NEG_BIG)
    neg = jnp.full((N_HEADS, 1, GRID_W, GRID_W), NEG_BIG, F32)
    ext = jnp.concatenate([neg, toe, neg], axis=1)
    pair = jnp.concatenate([ext[:, 0:16], ext[:, 1:17]], axis=-1)
    return pair.reshape(N_HEADS // 2, 2, 16, GRID_W, LANES)


def _attention(q, k, v, kc, vc, bias):
    n_slots = 2
    kspec = pl.BlockSpec((1, 1, SEQ, LANES), lambda hp, b: (b, hp, 0, 0))
    ktspec =pl.BlockSpec((1, SEQ // KEY_GROUP, LANES, KEY_GROUP), lambda hp, b: (b, 0, hp, 0))
    cspec = pl.BlockSpec((1, CTX_LEN, LANES), lambda hp, b: (b, 0, hp))
    ctspec = pl.BlockSpec((1, LANES, CTX_LEN), lambda hp, b: (b, hp, 0))
    bspec = pl.BlockSpec((1, 2, 16, GRID_W, LANES), lambda hp, b: (hp, 0, 0, 0, 0))
    return pl.pallas_call(
        _attn_kernel,
        grid=(N_HEADS // 2, BATCH),
        in_specs=[kspec, ktspec, kspec, ctspec, cspec, bspec],
        out_specs=kspec,
        out_shape=jax.ShapeDtypeStruct((BATCH, N_HEADS // 2, SEQ, LANES), BF16),
        scratch_shapes=[pltpu.VMEM((n_slots, Q_BLOCK, K_BLOCK), F32),
                        pltpu.VMEM((n_slots, Q_BLOCK, CTX_LEN), F32),
                        pltpu.VMEM((n_slots, Q_BLOCK, K_BLOCK), BF16),
                        pltpu.VMEM((n_slots, Q_BLOCK, CTX_LEN), BF16),
                        pltpu.VMEM((n_slots, Q_BLOCK, LANES), F32),
                        pltpu.VMEM((n_slots, Q_BLOCK, LANES), F32),
                        pltpu.VMEM((1, Q_BLOCK, LANES), F32)],
        compiler_params=pltpu.CompilerParams(
            dimension_semantics=("arbitrary", "arbitrary"),
            vmem_limit_bytes=VMEM_LIMIT),
        name="attention",
    )(q, k, v, kc, vc, bias)


def _rot256(a, b, s, lane):
    s %= 2 * LANES
    if s >= LANES:
        a, b, s = b, a, s - LANES
    if s == 0:
        return a, b
    ra = pltpu.roll(a, s, 1)
    rb = pltpu.roll(b, s, 1)
    keep = lane >= s
    return jnp.where(keep, ra, rb), jnp.where(keep, rb, ra)


S5_PREP_GROUPS = 2
POWER_ROWS = 24


def _s5_prep_kernel(*refs):
    for gi in range(S5_PREP_GROUPS):
        _s5_prep_group(gi, *refs)


def _s5_prep_group(gi, par_ref, b_ref, c_ref, t16_ref, e_ref, mi_ref, ms_ref, mo_ref, a_ref):
    dot = functools.partial(jnp.dot, preferred_element_type=F32)

    def split2(x):
        hi = x.astype(BF16)
        return hi, (x - hi.astype(F32)).astype(BF16)

    def split3(x):
        hi = x.astype(BF16)
        r1 = x - hi.astype(F32)
        mid = r1.astype(BF16)
        return hi, mid, (r1 - mid.astype(F32)).astype(BF16)

    def pick(x, onehot):
        hi, mid, lo = split3(x)
        return dot(hi, onehot) + dot(mid, onehot) + dot(lo, onehot)

    def dot_f32(a, b):
        ah, al = split2(a)
        bh, bl = split2(b)
        return dot(jnp.concatenate([ah, ah, al], axis=1), jnp.concatenate([bh, bl, bh], axis=0))

    ns = SSM_STATE
    lam_re, lam_im = par_ref[gi, 0:1, :], par_ref[gi, 1:2, :]
    dt = jnp.exp(par_ref[gi, 2:3, :])
    lane = lax.broadcasted_iota(jnp.int32, (1, LANES), 1)
    kf = lax.broadcasted_iota(jnp.int32, (POWER_ROWS, LANES), 0).astype(F32)
    mag = jnp.exp((lam_re * dt) * kf)
    ang = (lam_im * dt) * kf
    pwt_re, pwt_im = mag * jnp.cos(ang), mag * jnp.sin(ang)
    den = lam_re * lam_re + lam_im * lam_im
    nr, ni = pwt_re[1:2, :] - 1.0, pwt_im[1:2, :]
    f_re_row = (nr * lam_re + ni * lam_im) / den
    f_im_row = (ni * lam_re - nr * lam_im) / den
    r8 = lax.broadcasted_iota(jnp.int32, (8, LANES), 0)
    f_rows = jnp.where(r8 == 0, f_re_row, jnp.where(r8 == 1, f_im_row, 0.0))
    pad = jnp.zeros((LANES - POWER_ROWS - 8, LANES), F32)
    pw_re = jnp.concatenate([pwt_re, f_rows, pad], axis=0).T
    pw_im = jnp.concatenate([pwt_im, f_rows, pad], axis=0).T
    f_re, f_im = pw_re[:, POWER_ROWS:POWER_ROWS + 1], pw_re[:, POWER_ROWS + 1:POWER_ROWS + 2]
    b_re, b_im = b_ref[gi, :, 0:SSM_GROUP], b_ref[gi, :, SSM_GROUP:2 * SSM_GROUP]
    bb_re = f_re * b_re - f_im * b_im
    bb_im = f_re * b_im + f_im * b_re
    t16 = t16_ref[...]
    bbt_re, bbt_im = pick(bb_re, t16), pick(bb_im, t16)
    ct_re = pick(c_ref[gi, :, 0:SSM_GROUP], t16)
    ct_im = pick(c_ref[gi, :, SSM_GROUP:2 * SSM_GROUP], t16)
    pw_at = lambda x: (pick(pw_re, e_ref[x]), pick(pw_im, e_ref[x]))
    id_re, id_im = pw_at(0)
    rev_re, rev_im = pw_at(1)
    p1_re, p1_im = pw_at(2)
    r16_re, r16_im = pw_at(3)
    f, b = slice(0, ns), slice(ns, 2 * ns)
    cmul = lambda ar, ai, br, bi: (ar * br - ai * bi, ar * bi + ai * br)

    sf_re, sf_im = cmul(rev_re[f], rev_im[f], bbt_re[f], bbt_im[f])
    sb_re, sb_im = cmul(id_re[b], id_im[b], bbt_re[b], bbt_im[b])
    ms_ref[gi] = jnp.concatenate([sf_re, sb_re, sf_im, sb_im], axis=0).T.astype(BF16)

    of_re, of_im = cmul(p1_re[f], p1_im[f], ct_re[f], ct_im[f])
    ob_re, ob_im = cmul(r16_re[b], r16_im[b], ct_re[b], ct_im[b])
    mo_ref[gi] = jnp.concatenate([of_re, ob_re, -of_im, -ob_im], axis=0).astype(BF16)

    xf_re, xf_im = cmul(id_re[f], id_im[f], ct_re[f], ct_im[f])
    xb_re, xb_im = cmul(rev_re[b], rev_im[b], ct_re[b], ct_im[b])
    btf = jnp.concatenate([bbt_re[f], bbt_im[f]], axis=0).T
    btb = jnp.concatenate([bbt_re[b], bbt_im[b]], axis=0).T
    g_f = dot_f32(btf, jnp.concatenate([xf_re, -xf_im], axis=0))
    g_b = dot_f32(btb, jnp.concatenate([xb_re, -xb_im], axis=0))
    for j in range(CHUNK):
        rows = slice(SSM_GROUP * j, SSM_GROUP * (j + 1))
        lo_col, hi_col = SSM_GROUP * j, SSM_GROUP * (j + 1)
        f_lo, f_hi = _rot256(g_f[rows, :LANES], g_f[rows, LANES:], lo_col, lane)
        b_lo, b_hi = _rot256(g_b[rows, :LANES], g_b[rows, LANES:], -SSM_GROUP * (CHUNK - 1 - j), lane)
        lo = jnp.where(lane >= lo_col, f_lo, 0.0) + jnp.where(lane < hi_col, b_lo, 0.0)
        up = jnp.where(lane + LANES >= lo_col, f_hi, 0.0) + jnp.where(lane + LANES < hi_col, b_hi, 0.0)
        mi_ref[gi, rows, 0:LANES] = lo.astype(BF16)
        mi_ref[gi, rows, LANES:2 * LANES] = up.astype(BF16)

    a16 = jnp.concatenate([pwt_re[CHUNK:CHUNK + 1, :], pwt_im[CHUNK:CHUNK + 1, :]], axis=1)
    a_ref[gi] = jnp.broadcast_to(a16, (8, 2 * LANES))


def _s5_matrices(lam_re, lam_im, log_dt, b_re, b_im, c_re, c_im):
    g, p2 = SSM_GROUPS, 2 * SSM_STATE
    both = lambda a: jnp.transpose(a, (1, 0, 2)).reshape(g, p2)
    log_dt_rows = jnp.repeat(jnp.transpose(log_dt), SSM_STATE, axis=1)
    par = jnp.stack([both(lam_re), both(lam_im), log_dt_rows], axis=1)
    par = jnp.concatenate([par, jnp.zeros((g, LANES - 3, p2), F32)], axis=1)
    rows_b = lambda a: jnp.transpose(a, (1, 0, 2, 3)).reshape(g, p2, SSM_GROUP)
    rows_c = lambda a: jnp.transpose(a, (1, 0, 3, 2)).reshape(g, p2, SSM_GROUP)
    b_cat = jnp.concatenate([rows_b(b_re), rows_b(b_im)], axis=-1)
    c_cat = jnp.concatenate([rows_c(c_re), rows_c(c_im)], axis=-1)

    col = np.arange(CHUNK_COLS)
    tile16 = (col[None, :] % SSM_GROUP == np.arange(SSM_GROUP)[:, None]).astype(np.float32)
    pos = col // SSM_GROUP
    k_idx = np.arange(LANES)[:, None]
    expand = np.stack([k_idx == pos[None, :], k_idx == (CHUNK - 1 - pos)[None, :],
                       k_idx == (pos + 1)[None, :], k_idx == (CHUNK - pos)[None, :]]).astype(np.float32)

    gp = S5_PREP_GROUPS
    mat = lambda: pl.BlockSpec((gp, CHUNK_COLS, CHUNK_COLS), lambda i: (i, 0, 0))
    mat_shape = jax.ShapeDtypeStruct((g, CHUNK_COLS, CHUNK_COLS), BF16)
    return pl.pallas_call(
        _s5_prep_kernel,
        grid=(g // gp,),
        in_specs=[pl.BlockSpec((gp, LANES, p2), lambda i: (i, 0, 0)),
                  pl.BlockSpec((gp, p2, 2 * SSM_GROUP), lambda i: (i, 0, 0)),
                  pl.BlockSpec((gp, p2, 2 * SSM_GROUP), lambda i: (i, 0, 0)),
                  pl.BlockSpec((SSM_GROUP, CHUNK_COLS), lambda i: (0, 0)),
                  pl.BlockSpec((4, LANES, CHUNK_COLS), lambda i: (0, 0, 0))],
        out_specs=[mat(), mat(), mat(), pl.BlockSpec((gp, 8, 2 * LANES), lambda i: (i, 0, 0))],
        out_shape=[mat_shape, mat_shape, mat_shape, jax.ShapeDtypeStruct((g, 8, 2 * LANES), F32)],
        name="s5_prep",
    )(par, b_cat, c_cat, jnp.asarray(tile16, BF16), jnp.asarray(expand, BF16))


def _s5_kernel(ul_ref, uc_ref, ms_ref, mi_ref, mo_ref, a_ref, y_ref, s_ref, sc_ref, hp_ref, *, gb):
    for gi in range(gb):
        for b in range(BATCH):
            sb = jnp.dot(ul_ref[gi, b], ms_ref[gi], preferred_element_type=F32)
            s_ref[gi, 0, pl.ds(b, N_CHUNKS, stride=BATCH), :] = sb[:, :LANES]
            s_ref[gi, 1, pl.ds(b, N_CHUNKS, stride=BATCH), :] = sb[:, LANES:]
            cb = jnp.dot(uc_ref[gi, b], ms_ref[gi], preferred_element_type=F32)
            sc_ref[gi, 0, pl.ds(b, N_CTX_CHUNKS, stride=BATCH), :] = cb[:, :LANES]
            sc_ref[gi, 1, pl.ds(b, N_CTX_CHUNKS, stride=BATCH), :] = cb[:, LANES:]
    lane = lax.broadcasted_iota(jnp.int32, (BATCH, LANES), 1)
    fwd = lane < SSM_STATE
    half = SSM_STATE

    def advance(gi, h_re, h_im, row_f, row_b, src):
        s_re = jnp.where(fwd, src[gi, 0, pl.ds(row_f, BATCH), :], src[gi, 0, pl.ds(row_b, BATCH), :])
        s_im = jnp.where(fwd, src[gi, 1, pl.ds(row_f, BATCH), :], src[gi, 1, pl.ds(row_b, BATCH), :])
        a_re = a_ref[gi, :, 0:LANES]
        a_im = a_ref[gi, :, LANES:2 * LANES]
        n_re = a_re * h_re - a_im * h_im + s_re
        n_im = a_re * h_im + a_im * h_re + s_im
        return n_re, n_im

    def ctx_step(t, carry):
        row_f = pl.multiple_of(t * BATCH, BATCH)
        row_b = pl.multiple_of((N_CTX_CHUNKS - 1 - t) * BATCH, BATCH)
        return tuple(advance(gi, carry[gi][0], carry[gi][1], row_f, row_b, sc_ref) for gi in range(gb))

    def lat_step(t, carry):
        row_f = pl.multiple_of(t * BATCH, BATCH)
        row_b = pl.multiple_of((N_CHUNKS - 1 - t) * BATCH, BATCH)
        out = []
        for gi in range(gb):
            h_re, h_im = carry[gi]
            hp_ref[gi, 0, pl.ds(row_f, BATCH), 0:half] = h_re[:, 0:half]
            hp_ref[gi, 0, pl.ds(row_b, BATCH), half:2 * half] = h_re[:, half:]
            hp_ref[gi, 1, pl.ds(row_f, BATCH), 0:half] = h_im[:, 0:half]
            hp_ref[gi, 1, pl.ds(row_b, BATCH), half:2 * half] = h_im[:, half:]
            out.append(advance(gi, h_re, h_im, row_f, row_b, s_ref))
        return tuple(out)

    zero = jnp.zeros((BATCH, LANES), F32)
    carry = tuple((zero, zero) for _ in range(gb))
    carry = lax.fori_loop(0, N_CTX_CHUNKS, ctx_step, carry)
    lax.fori_loop(0, N_CHUNKS, lat_step, carry)
    for gi in range(gb):
        for b in range(BATCH):
            hb_re = hp_ref[gi, 0, pl.ds(b, N_CHUNKS, stride=BATCH), :].astype(BF16)
            hb_im = hp_ref[gi, 1, pl.ds(b, N_CHUNKS, stride=BATCH), :].astype(BF16)
            y = (jnp.dot(ul_ref[gi, b], mi_ref[gi], preferred_element_type=F32)
                 + jnp.dot(jnp.concatenate([hb_re, hb_im], axis=1), mo_ref[gi], preferred_element_type=F32))
            y_ref[gi, b] = y.astype(BF16)


def _s5_scan(u_lat_t, u_ctx_t, m_intra, m_state, m_out, a16):
    gb = 4
    rows = N_CHUNKS * BATCH
    crows = N_CTX_CHUNKS * BATCH
    grp = lambda r, c: pl.BlockSpec((gb, r, c), lambda g: (g, 0, 0))
    tok = lambda n: pl.BlockSpec((gb, BATCH, n, CHUNK_COLS), lambda g: (g, 0, 0, 0))
    return pl.pallas_call(
        functools.partial(_s5_kernel, gb=gb),
        grid=(SSM_GROUPS // gb,),
        in_specs=[tok(N_CHUNKS), tok(N_CTX_CHUNKS), grp(CHUNK_COLS, CHUNK_COLS),
                  grp(CHUNK_COLS, CHUNK_COLS), grp(CHUNK_COLS, CHUNK_COLS), grp(8, 2 * LANES)],
        out_specs=tok(N_CHUNKS),
        out_shape=jax.ShapeDtypeStruct((SSM_GROUPS, BATCH, N_CHUNKS, CHUNK_COLS), BF16),
        scratch_shapes=[pltpu.VMEM((gb, 2, rows, LANES), F32),
                        pltpu.VMEM((gb, 2, crows, LANES), F32),
                        pltpu.VMEM((gb, 2, rows, LANES), F32)],
        compiler_params=pltpu.CompilerParams(
            dimension_semantics=("arbitrary",), vmem_limit_bytes=VMEM_LIMIT),
        name="s5_scan",
    )(u_lat_t, u_ctx_t, m_state, m_intra, m_out, a16)


FFN_TILE = 256


def _post_kernel(x_ref, a_ref, yt_ref, u_ref, ga_ref, gs_ref, mod_ref, d_ref, fg_ref, og_ref, permt_ref,
                 wglu_ref, wba_ref, wbs_ref, wout_ref, wfi_ref, wfo_ref, o_ref, h1_ref, n2_ref, act_ref, r_ref):
    dm = D_MODEL
    g1 = mod_ref[0, :, 2 * dm:3 * dm]
    sh2 = mod_ref[0, :, 3 * dm:4 * dm]
    sc2 = mod_ref[0, :, 4 * dm:5 * dm]
    g2 = mod_ref[0, :, 5 * dm:6 * dm]
    half = CHUNK * CHUNK
    halves = [slice(h * half, (h + 1) * half) for h in range(x_ref.shape[1] // half)]
    dot = functools.partial(jnp.dot, preferred_element_type=F32)

    sp = []
    for h, rows in enumerate(halves):
        y = _load_chunk_layout(yt_ref, permt_ref, r_ref, h)
        sp.append(jax.nn.gelu(y + d_ref[...] * u_ref[0, rows, :].astype(F32)).astype(BF16))
    s = []
    for h, rows in enumerate(halves):
        vg = dot(sp[h], wglu_ref[...])
        s.append((vg[:, :SSM_WIDTH] * jax.nn.sigmoid(vg[:, SSM_WIDTH:])).astype(BF16))
    merged = []
    for h, rows in enumerate(halves):
        a = jnp.concatenate([a_ref[0, j, rows, :] for j in range(ATTN_WIDTH // LANES)], axis=1)
        m = (ga_ref[0, rows, :].astype(F32) * dot(a, wba_ref[...])
             + gs_ref[0, rows, :].astype(F32) * dot(s[h], wbs_ref[...]))
        merged.append(m.astype(BF16))
    for h, rows in enumerate(halves):
        h1 = x_ref[0, rows, :] + g1 * dot(merged[h], wout_ref[...])
        h1_ref[rows, :] = h1
        n2_ref[rows, :] = _rms_modulate(h1, fg_ref[...], sh2, sc2).astype(BF16)
    for lo in range(0, FFN_HIDDEN, FFN_TILE):
        width = min(FFN_TILE, FFN_HIDDEN - lo)
        for rows in halves:
            n2 = n2_ref[rows, :]
            fa = dot(n2, wfi_ref[:, lo:lo + width])
            fb = dot(n2, wfi_ref[:, FFN_HIDDEN + lo:FFN_HIDDEN + lo + width])
            act_ref[rows, lo:lo + width] = (fa * jax.nn.sigmoid(fa) * fb).astype(BF16)
    for rows in halves:
        h2 = h1_ref[rows, :] + g2 * dot(act_ref[rows, :], wfo_ref[...])
        o_ref[0, rows, :] = (h2 * lax.rsqrt(jnp.mean(h2 * h2, axis=-1, keepdims=True) + NORM_EPS)) * og_ref[...]


def _post(x, a, y_t, u, ga, gs, mod3, d_skip, ffn_g, fin_g, perm_t, wglu, wba, wbs, wout, wfi, wfo):
    tm = 512
    n = CHUNK * CHUNK
    tok = lambda width: pl.BlockSpec((1, tm, width), lambda b, i: (b, i, 0))
    const = lambda r, c: pl.BlockSpec((r, c), lambda b, i: (0, 0), pipeline_mode=pl.Buffered(1))
    return pl.pallas_call(
        _post_kernel,
        grid=(BATCH, SEQ // tm),
        in_specs=[tok(D_MODEL),
                  pl.BlockSpec((1, N_HEADS // 2, tm, LANES), lambda b, i: (b, 0, i, 0)),
                  pl.BlockSpec((SSM_GROUPS, 1, tm // CHUNK, CHUNK_COLS), lambda b, i: (0, b, i, 0)),
                  tok(SSM_WIDTH), tok(D_MODEL), tok(D_MODEL),
                  pl.BlockSpec((1, 1, N_MOD * D_MODEL), lambda b, i: (b, 0, 0)),
                  const(1, SSM_WIDTH), const(1, D_MODEL), const(1, D_MODEL), const(n, n),
                  const(SSM_WIDTH, 2 * SSM_WIDTH), const(ATTN_WIDTH, D_MODEL),
                  const(SSM_WIDTH, D_MODEL), const(D_MODEL, D_MODEL),
                  const(D_MODEL, 2 * FFN_HIDDEN), const(FFN_HIDDEN, D_MODEL)],
        out_specs=tok(D_MODEL),
        out_shape=jax.ShapeDtypeStruct((BATCH, SEQ, D_MODEL), F32),
        scratch_shapes=[pltpu.VMEM((tm, D_MODEL), F32), pltpu.VMEM((tm, D_MODEL), BF16),
                        pltpu.VMEM((tm, FFN_HIDDEN), BF16), pltpu.VMEM((tm // n, n, SSM_WIDTH), F32)],
        compiler_params=pltpu.CompilerParams(
            dimension_semantics=("arbitrary", "arbitrary"), vmem_limit_bytes=VMEM_LIMIT),
        name="post",
    )(x, a, y_t, u, ga, gs, mod3, d_skip, ffn_g, fin_g, perm_t, wglu, wba, wbs, wout, wfi, wfo)


def kernel(x, c, ctx, c_ctx, w_mod, b_mod, attn_norm_g, ffn_norm_g, w_in, rel_pos_bias,
           ssm_lambda_re, ssm_lambda_im, ssm_log_dt, ssm_b_re, ssm_b_im, ssm_c_re, ssm_c_im, ssm_d,
           w_glu, w_branch_attn, w_branch_ssm, w_out, w_ffn_in, w_ffn_out, final_norm_g):
    assert x.shape == (BATCH, SEQ, D_MODEL) and w_mod.shape[0] == 1
    c_rows = jnp.concatenate(
        [c, c_ctx[None, :], jnp.zeros((MOD_ROWS - BATCH - 1, D_MODEL), F32)], axis=0)
    mod3 = _modulation(c_rows, w_mod[0], b_mod[0]).reshape(MOD_ROWS, 1, N_MOD * D_MODEL)

    col_scale = jnp.concatenate([jnp.full((ATTN_WIDTH,), HEAD_DIM ** -0.5 * LOG2E, F32),
                                 jnp.ones((IN_COLS - ATTN_WIDTH,), F32)])
    w_in_bf16 = (w_in[0] * col_scale[None, :]).astype(BF16)
    norm_g = attn_norm_g[0].reshape(1, D_MODEL)

    perm = _chunk_perm()
    q, k, v, u, u_t, ga, gs = _input_projection(x, mod3, norm_g, w_in_bf16, _rope_tables(), perm)
    kc, vc, uc_t = _context_projection(ctx, mod3, norm_g, w_in_bf16, perm)

    attn = _attention(q, k, v, kc, vc, _bias_tables(rel_pos_bias[0]))

    m_intra, m_state, m_out, a16 = _s5_matrices(
        ssm_lambda_re[0], ssm_lambda_im[0], ssm_log_dt[0], ssm_b_re[0], ssm_b_im[0],
        ssm_c_re[0], ssm_c_im[0])
    y_t = _s5_scan(u_t, uc_t, m_intra, m_state, m_out, a16)

    return _post(x, attn, y_t, u, ga, gs, mod3,
                 ssm_d[0].reshape(1, SSM_WIDTH), ffn_norm_g[0].reshape(1, D_MODEL),
                 final_norm_g.reshape(1, D_MODEL), perm,
                 w_glu[0].astype(BF16), w_branch_attn[0].astype(BF16), w_branch_ssm[0].astype(BF16),
                 w_out[0].astype(BF16), w_ffn_in[0].astype(BF16), w_ffn_out[0].astype(BF16))
```

```python
import functools
import math

import numpy as np
import jax
import jax.numpy as jnp
from jax import lax
from jax.experimental import pallas as pl
from jax.experimental.pallas import tpu as pltpu

F32 = jnp.float32
BF16 = jnp.bfloat16

D_MODEL = 1024
BATCH = 8
SEQ = 4096
GRID_W = 64
GRID_ROWS = SEQ // GRID_W
CTX_LEN = 256
N_HEADS = 8
HEAD_DIM = 64
ATTN_WIDTH = N_HEADS * HEAD_DIM
WIN_H = 8
WIN_W = 16
ROPE_BASE = 10000.0
SSM_WIDTH = 512
SSM_GROUP = 16
SSM_GROUPS = SSM_WIDTH // SSM_GROUP
SSM_STATE = 64
FFN_HIDDEN = 2816
IN_COLS = 3 * ATTN_WIDTH + SSM_WIDTH + 2 * D_MODEL
N_MOD = 6
NORM_EPS = 1e-6
NEG_BIG = -1e30
LOG2E = math.log2(math.e)

LANES = 128
CHUNK = 16
N_CHUNKS = SEQ // CHUNK
N_CTX_CHUNKS = CTX_LEN // CHUNK
CHUNK_COLS = CHUNK * SSM_GROUP
MOD_ROWS = 16
CTX_MOD_ROW = BATCH
VMEM_LIMIT = 56 * 1024 * 1024

Q_ROWS_PER_STEP = 4
Q_BLOCK = Q_ROWS_PER_STEP * GRID_W
K_ROWS_PER_STEP = Q_ROWS_PER_STEP + WIN_H
K_BLOCK = K_ROWS_PER_STEP * GRID_W
N_QBLOCKS = GRID_ROWS // Q_ROWS_PER_STEP
KEY_GROUP = Q_BLOCK
SOFTMAX_ROWS = 32

def _rms_modulate(x, g, shift, scale):
    xn = x * lax.rsqrt(jnp.mean(x * x, axis=-1, keepdims=True) + NORM_EPS)
    return (xn * g) * (1.0 + scale) + shift


def _block_transpose8(vs, lane):
    for shift in (64, 32, 16):
        keep = (lane & (2 * shift - 1)) < shift
        dist = shift // SSM_GROUP
        out = list(vs)
        for a in range(8):
            if a & dist:
                continue
            b = a + dist
            out[a] = jnp.where(keep, vs[a], pltpu.roll(vs[b], shift, 1))
            out[b] = jnp.where(keep, pltpu.roll(vs[a], LANES - shift, 1), vs[b])
        vs = out
    return vs


def _chunk_perm():
    n = CHUNK * CHUNK
    r = np.arange(n)
    m = np.zeros((n, n), np.float32)
    m[r, (r % CHUNK) * CHUNK + r // CHUNK] = 1.0
    return jnp.asarray(m, BF16)


def _store_chunk_layout(u16, perm_ref, out_ref, n_groups16, group0=0):
    lane = lax.broadcasted_iota(jnp.int32, (1, LANES), 1)
    n = CHUNK * CHUNK
    for hf in range(n_groups16):
        r = jnp.dot(perm_ref[...], u16[hf * n:(hf + 1) * n, :], preferred_element_type=F32)
        c0 = (group0 + hf) * CHUNK
        for v in range(SSM_WIDTH // LANES):
            for jh in range(2):
                vs = [r[CHUNK * (8 * jh + jp):CHUNK * (8 * jh + jp + 1), v * LANES:(v + 1) * LANES]
                      for jp in range(8)]
                outs = _block_transpose8(vs, lane)
                for gi in range(8):
                    out_ref[8 * v + gi, 0, c0:c0 + CHUNK, jh * LANES:(jh + 1) * LANES] = outs[gi].astype(BF16)


def _load_chunk_layout(yt_ref, perm_t_ref, r_ref, hf):
    lane = lax.broadcasted_iota(jnp.int32, (1, LANES), 1)
    for v in range(SSM_WIDTH // LANES):
        for jh in range(2):
            vs = [yt_ref[8 * v + gi, 0, hf * CHUNK:(hf + 1) * CHUNK,
                         jh * LANES:(jh + 1) * LANES].astype(F32) for gi in range(8)]
            outs = _block_transpose8(vs, lane)
            for jp in range(8):
                r_ref[hf, CHUNK * (8 * jh + jp):CHUNK * (8 * jh + jp + 1), v * LANES:(v + 1) * LANES] = outs[jp]
    return jnp.dot(perm_t_ref[...], r_ref[hf].astype(BF16), preferred_element_type=F32)


def _mod_kernel(c_ref, w_ref, b_ref, o_ref):
    c = c_ref[...]
    s = c * jax.nn.sigmoid(c)
    o_ref[...] = jnp.dot(s, w_ref[...], preferred_element_type=F32) + b_ref[...]


def _modulation(c_rows, w_mod, b_mod):
    n = N_MOD * D_MODEL
    tn = 1536
    return pl.pallas_call(
        _mod_kernel,
        grid=(n // tn,),
        in_specs=[pl.BlockSpec((MOD_ROWS, D_MODEL), lambda j: (0, 0)),
                  pl.BlockSpec((D_MODEL, tn), lambda j: (0, j)),
                  pl.BlockSpec((1, tn), lambda j: (0, j))],
        out_specs=pl.BlockSpec((MOD_ROWS, tn), lambda j: (0, j)),
        out_shape=jax.ShapeDtypeStruct((MOD_ROWS, n), F32),
        name="modulation",
    )(c_rows, w_mod, b_mod.reshape(1, n))


def _rope_tables():
    n_freq = HEAD_DIM // 4
    inv_freq = ROPE_BASE ** (-np.arange(n_freq, dtype=np.float64) / n_freq)
    t = np.arange(SEQ)
    lane = np.arange(LANES)
    d = lane % HEAD_DIM
    use_col = (d // (HEAD_DIM // 2)) == 1
    w = d % (HEAD_DIM // 2)
    first = w < n_freq
    pos = np.where(use_col[None, :], (t % GRID_W)[:, None], (t // GRID_W)[:, None]).astype(np.float64)
    ang = pos * inv_freq[w % n_freq][None, :]
    cos = np.cos(ang)
    sin = np.sin(ang)
    sin_a = np.where(first[None, :], -sin, 0.0)
    sin_b = np.where(first[None, :], 0.0, sin)
    return (jnp.asarray(cos, F32), jnp.asarray(sin_a, F32), jnp.asarray(sin_b, F32))


def _rope_store(r, cos, sin_a, sin_b, out_ref, row0, transposed=False):
    n_rows = r.shape[0]
    for j in range(ATTN_WIDTH // LANES):
        xs = r[:, j * LANES:(j + 1) * LANES]
        rot = (xs * cos + pltpu.roll(xs, LANES - HEAD_DIM // 4, 1) * sin_a
               + pltpu.roll(xs, HEAD_DIM // 4, 1) * sin_b)
        if transposed:
            rot_t = rot.T.astype(BF16)
            for gi in range(n_rows // KEY_GROUP):
                out_ref[0, row0 // KEY_GROUP + gi, j * LANES:(j + 1) * LANES, :] = (
                    rot_t[:, gi * KEY_GROUP:(gi + 1) * KEY_GROUP])
        else:
            out_ref[0, j, row0:row0 + n_rows, :] = rot.astype(BF16)


INPROJ_SUBTILE = 512


def _inproj_kernel(x_ref, mod_ref, g_ref, w_ref, cos_ref, sa_ref, sb_ref, perm_ref,
                   q_ref, k_ref, v_ref, u_ref, ut_ref, ga_ref, gs_ref):
    shift = mod_ref[0, :, 0:D_MODEL]
    scale = mod_ref[0, :, D_MODEL:2 * D_MODEL]
    aw = ATTN_WIDTH
    dot = functools.partial(jnp.dot, preferred_element_type=F32)
    for row0 in range(0, x_ref.shape[1], INPROJ_SUBTILE):
        rows = slice(row0, row0 + INPROJ_SUBTILE)
        nb = _rms_modulate(x_ref[0, rows, :], g_ref[...], shift, scale).astype(BF16)
        cos = cos_ref[rows, :]
        sin_a = sa_ref[rows, :]
        sin_b = sb_ref[rows, :]
        _rope_store(dot(nb, w_ref[:, 0:aw]), cos, sin_a, sin_b, q_ref, row0)
        _rope_store(dot(nb, w_ref[:, aw:2 * aw]), cos, sin_a, sin_b, k_ref, row0, transposed=True)
        v = dot(nb, w_ref[:, 2 * aw:3 * aw]).astype(BF16)
        for j in range(aw // LANES):
            v_ref[0, j, rows, :] = v[:, j * LANES:(j + 1) * LANES]
        c0 = 3 * aw
        u16 = dot(nb, w_ref[:, c0:c0 + SSM_WIDTH]).astype(BF16)
        u_ref[0, rows, :] = u16
        n16 = CHUNK * CHUNK
        _store_chunk_layout(u16, perm_ref, ut_ref, INPROJ_SUBTILE // n16, row0 // n16)
        c1 = c0 + SSM_WIDTH
        ga_ref[0, rows, :] = jax.nn.sigmoid(dot(nb, w_ref[:, c1:c1 + D_MODEL])).astype(BF16)
        c2 = c1 + D_MODEL
        gs_ref[0, rows, :] = jax.nn.sigmoid(dot(nb, w_ref[:, c2:c2 + D_MODEL])).astype(BF16)


def _input_projection(x, mod3, norm_g, w_in_bf16, rope, perm):
    tm = 2 * INPROJ_SUBTILE
    n = CHUNK * CHUNK
    cos, sin_a, sin_b = rope
    tok = lambda width: pl.BlockSpec((1, tm, width), lambda i, b: (b, i, 0))
    tab = pl.BlockSpec((tm, LANES), lambda i, b: (i, 0))
    out = lambda width: jax.ShapeDtypeStruct((BATCH, SEQ, width), BF16)
    pairs = pl.BlockSpec((1, N_HEADS // 2, tm, LANES), lambda i, b: (b, 0, i, 0))
    pairs_shape = jax.ShapeDtypeStruct((BATCH, N_HEADS // 2, SEQ, LANES), BF16)
    return pl.pallas_call(
        _inproj_kernel,
        grid=(SEQ // tm, BATCH),
        in_specs=[tok(D_MODEL),
                  pl.BlockSpec((1, 1, N_MOD * D_MODEL), lambda i, b: (b, 0, 0)),
                  pl.BlockSpec((1, D_MODEL), lambda i, b: (0, 0)),
                  pl.BlockSpec((D_MODEL, IN_COLS), lambda i, b: (0, 0), pipeline_mode=pl.Buffered(1)),
                  tab, tab, tab,
                  pl.BlockSpec((n, n), lambda i, b: (0, 0), pipeline_mode=pl.Buffered(1))],
        out_specs=[pairs,
                   pl.BlockSpec((1, tm // KEY_GROUP, ATTN_WIDTH, KEY_GROUP), lambda i, b: (b, i, 0, 0)),
                   pairs, tok(SSM_WIDTH),
                   pl.BlockSpec((SSM_GROUPS, 1, tm // CHUNK, CHUNK_COLS), lambda i, b: (0, b, i, 0)),
                   tok(D_MODEL), tok(D_MODEL)],
        out_shape=[pairs_shape,
                   jax.ShapeDtypeStruct((BATCH, SEQ // KEY_GROUP, ATTN_WIDTH, KEY_GROUP), BF16),
                   pairs_shape, out(SSM_WIDTH),
                   jax.ShapeDtypeStruct((SSM_GROUPS, BATCH, N_CHUNKS, CHUNK_COLS), BF16),
                   out(D_MODEL), out(D_MODEL)],
        compiler_params=pltpu.CompilerParams(
            dimension_semantics=("arbitrary", "arbitrary"), vmem_limit_bytes=VMEM_LIMIT),
        name="input_projection",
    )(x, mod3, norm_g, w_in_bf16, cos, sin_a, sin_b, perm)


def _ctx_proj_kernel(x_ref, mod_ref, g_ref, wk_ref, wv_ref, wu_ref, perm_ref, k_ref, v_ref, ut_ref):
    x = x_ref[0]
    shift = mod_ref[0, :, 0:D_MODEL]
    scale = mod_ref[0, :, D_MODEL:2 * D_MODEL]
    nb = _rms_modulate(x, g_ref[...], shift, scale).astype(BF16)
    k_ref[0] = jnp.dot(nb, wk_ref[...], preferred_element_type=F32).T.astype(BF16)
    v_ref[0] = jnp.dot(nb, wv_ref[...], preferred_element_type=F32).astype(BF16)
    u16 = jnp.dot(nb, wu_ref[...], preferred_element_type=F32).astype(BF16)
    _store_chunk_layout(u16, perm_ref, ut_ref, 1)


def _context_projection(ctx, mod3, norm_g, w_in_bf16, perm):
    n = CHUNK * CHUNK
    aw = ATTN_WIDTH
    tok = lambda width: pl.BlockSpec((1, CTX_LEN, width), lambda b: (b, 0, 0))
    wcol = lambda j: pl.BlockSpec((D_MODEL, aw), lambda b: (0, j))
    out = jax.ShapeDtypeStruct((BATCH, CTX_LEN, aw), BF16)
    return pl.pallas_call(
        _ctx_proj_kernel,
        grid=(BATCH,),
        in_specs=[tok(D_MODEL),
                  pl.BlockSpec((1, 1, N_MOD * D_MODEL), lambda b: (CTX_MOD_ROW, 0, 0)),
                  pl.BlockSpec((1, D_MODEL), lambda b: (0, 0)),
                  wcol(1), wcol(2), wcol(3),
                  pl.BlockSpec((n, n), lambda b: (0, 0))],
        out_specs=[pl.BlockSpec((1, aw, CTX_LEN), lambda b: (b, 0, 0)), tok(aw),
                   pl.BlockSpec((SSM_GROUPS, 1, N_CTX_CHUNKS, CHUNK_COLS), lambda b: (0, b, 0, 0))],
        out_shape=[jax.ShapeDtypeStruct((BATCH, aw, CTX_LEN), BF16), out,
                   jax.ShapeDtypeStruct((SSM_GROUPS, BATCH, N_CTX_CHUNKS, CHUNK_COLS), BF16)],
        name="context_projection",
    )(ctx, mod3, norm_g, w_in_bf16, w_in_bf16, w_in_bf16, perm)


def _window_start_rows(r):
    return min(max(r - WIN_H // 2, 0), GRID_ROWS - WIN_H)


def _key_block_row(jb):
    return min(max(Q_ROWS_PER_STEP * jb - WIN_H // 2, 0), GRID_ROWS - K_ROWS_PER_STEP)


def _window_geometry(jb):
    key_row0 = _key_block_row(jb)
    offs, deltas = [], []
    for i in range(Q_ROWS_PER_STEP):
        r = Q_ROWS_PER_STEP * jb + i
        rs = _window_start_rows(r)
        offs.append(rs - key_row0)
        deltas.append(r - rs)
    return offs, deltas


ATTN_SAMPLES_PER_STEP = 2


def _attn_kernel(*refs):
    def per_sample(bi, carry):
        _attn_sample(bi, *refs)
        return carry

    lax.fori_loop(0, ATTN_SAMPLES_PER_STEP, per_sample, 0)


def _attn_sample(bi, q_ref, k_ref, v_ref, kc_ref, vc_ref, bias_ref, o_ref,
                 s_ref, sc_ref, p_ref, pc_ref, l_ref, m_ref, acc_ref):
    lane = lax.broadcasted_iota(jnp.int32, (1, LANES), 1)
    left = lane < HEAD_DIM
    n_ktiles = K_BLOCK // LANES
    last = N_QBLOCKS - 1
    max_key_start = (GRID_ROWS - K_ROWS_PER_STEP) * GRID_W

    def scores(sl, e, q_start, key_start):
        q2 = q_ref[bi, 0, pl.ds(q_start, Q_BLOCK), :]
        qm = jnp.where(left if e == 0 else jnp.logical_not(left), q2, jnp.zeros_like(q2))
        group0 = key_start // KEY_GROUP
        for gi in range(K_BLOCK // KEY_GROUP):
            s_ref[sl, :, gi * KEY_GROUP:(gi + 1) * KEY_GROUP] = jnp.dot(
                qm, k_ref[bi, group0 + gi], preferred_element_type=F32)
        sc_ref[sl] = jnp.dot(qm, kc_ref[bi], preferred_element_type=F32)

    def softmax(sl, e, offs, deltas):
        def geometry(sub):
            i = sub * SOFTMAX_ROWS // GRID_W
            qcols = slice(sub * SOFTMAX_ROWS % GRID_W, sub * SOFTMAX_ROWS % GRID_W + SOFTMAX_ROWS)
            rows = slice(sub * SOFTMAX_ROWS, (sub + 1) * SOFTMAX_ROWS)
            par = offs[i] % 2
            return rows, qcols, par, offs[i] // 2, WIN_H // 2 + par, deltas[i]

        def tile(rows, qcols, par, t0, n_tiles, delta, xt):
            t = t0 + xt
            dr0 = 2 * xt - par - delta
            st = s_ref[sl, rows, t * LANES:(t + 1) * LANES] + bias_ref[0, e, dr0 + WIN_H, qcols, :]
            if par and xt == 0:
                st = jnp.where(left, NEG_BIG, st)
            if par and xt == n_tiles - 1:
                st = jnp.where(left, st, NEG_BIG)
            return st

        n_sub = Q_BLOCK // SOFTMAX_ROWS
        for sub in range(n_sub):
            geo = geometry(sub)
            rows, n_tiles = geo[0], geo[4]
            mt = jnp.maximum(sc_ref[sl, rows, 0:LANES], sc_ref[sl, rows, LANES:2 * LANES])
            for xt in range(n_tiles):
                mt = jnp.maximum(mt, tile(*geo, xt))
            m_ref[sl, rows, :] = jnp.broadcast_to(jnp.max(mt, axis=1, keepdims=True), (SOFTMAX_ROWS, LANES))
        for sub in range(n_sub):
            geo = geometry(sub)
            rows, t0, n_tiles = geo[0], geo[3], geo[4]
            m = m_ref[sl, rows, :]
            lt = None
            for xt in range(n_tiles):
                t = t0 + xt
                pt = jnp.exp2(tile(*geo, xt) - m)
                lt = pt if lt is None else lt + pt
                p_ref[sl, rows, t * LANES:(t + 1) * LANES] = pt.astype(BF16)
            for t in range(n_ktiles):
                if not (t0 <= t < t0 + n_tiles):
                    p_ref[sl, rows, t * LANES:(t + 1) * LANES] = jnp.zeros((SOFTMAX_ROWS, LANES), BF16)
            for ci in range(2):
                pt = jnp.exp2(sc_ref[sl, rows, ci * LANES:(ci + 1) * LANES] - m)
                lt = lt + pt
                pc_ref[sl, rows, ci * LANES:(ci + 1) * LANES] = pt.astype(BF16)
            l = jnp.sum(lt, axis=1, keepdims=True)
            l_ref[sl, rows, :] = jnp.broadcast_to(1.0 / l, (SOFTMAX_ROWS, LANES))

    def values(sl, key_start):
        vblk = v_ref[bi, 0, pl.ds(key_start, K_BLOCK), :]
        return (jnp.dot(p_ref[sl], vblk, preferred_element_type=F32)
                + jnp.dot(pc_ref[sl], vc_ref[bi], preferred_element_type=F32)) * l_ref[sl]

    def block(jb_static, cur, nxt):
        offs, deltas = _window_geometry(jb_static)
        scores(1, 1, *cur)
        softmax(0, 0, offs, deltas)
        acc_ref[0] = values(0, cur[1])
        if nxt is not None:
            scores(0, 0, *nxt)
        softmax(1, 1, offs, deltas)
        o1 = values(1, cur[1])
        o_ref[bi, 0, pl.ds(cur[0], Q_BLOCK), :] = jnp.where(left, acc_ref[0], o1).astype(BF16)

    def starts(jb):
        q_start = pl.multiple_of(jb * Q_BLOCK, Q_BLOCK)
        key_start = jnp.clip((jb * Q_ROWS_PER_STEP - WIN_H // 2) * GRID_W, 0, max_key_start)
        return q_start, pl.multiple_of(key_start, Q_BLOCK)

    scores(0, 0, 0, 0)
    block(0, (0, 0), (Q_BLOCK, 0))

    blocks_per_trip = 2

    def interior(trip, carry):
        jb = 1 + blocks_per_trip * trip
        for d in range(blocks_per_trip):
            block(1, starts(jb + d), starts(jb + d + 1))
        return carry

    assert (last - 1) % blocks_per_trip == 0
    lax.fori_loop(0, (last - 1) // blocks_per_trip, interior, 0)
    block(last, (last * Q_BLOCK, max_key_start), None)


def _bias_tables(rpb):
    qcol = np.arange(GRID_W)
    kcol = np.arange(GRID_W)
    col_start = np.clip(qcol - WIN_W // 2, 0, GRID_W - WIN_W)
    in_win = (kcol[None, :] >= col_start[:, None]) & (kcol[None, :] < col_start[:, None] + WIN_W)
    dc_idx = np.clip(kcol[None, :] - qcol[:, None], -(WIN_W - 1), WIN_W - 1) + WIN_W - 1
    sel = (np.arange(2 * WIN_W - 1)[:, None, None] == dc_idx[None]).astype(np.float32)
    toe = jnp.einsum('hdt,tck->hdck', rpb * LOG2E, jnp.asarray(sel),
                     precision=lax.Precision.HIGHEST)
    toe = jnp.where(in_win[None, None], toe, NEG_BIG)
    neg = jnp.full((N_HEADS, 1, GRID_W, GRID_W), NEG_BIG, F32)
    ext = jnp.concatenate([neg, toe, neg], axis=1)
    pair = jnp.concatenate([ext[:, 0:16], ext[:, 1:17]], axis=-1)
    return pair.reshape(N_HEADS // 2, 2, 16, GRID_W, LANES)


def _attention(q, k, v, kc, vc, bias):
    n_slots = 2
    nb = ATTN_SAMPLES_PER_STEP
    kspec = pl.BlockSpec((nb, 1, SEQ, LANES), lambda hp, b: (b, hp, 0, 0))
    ktspec = pl.BlockSpec((nb, SEQ // KEY_GROUP, LANES, KEY_GROUP), lambda hp, b: (b, 0, hp, 0))
    cspec = pl.BlockSpec((nb, CTX_LEN, LANES), lambda hp, b: (b, 0, hp))
    ctspec = pl.BlockSpec((nb, LANES, CTX_LEN), lambda hp, b: (b, hp, 0))
    bspec = pl.BlockSpec((1, 2, 16, GRID_W, LANES), lambda hp, b: (hp, 0, 0, 0, 0))
    return pl.pallas_call(
        _attn_kernel,
        grid=(N_HEADS // 2, BATCH // nb),
        in_specs=[kspec, ktspec, kspec, ctspec, cspec, bspec],
        out_specs=kspec,
        out_shape=jax.ShapeDtypeStruct((BATCH, N_HEADS // 2, SEQ, LANES), BF16),
        scratch_shapes=[pltpu.VMEM((n_slots, Q_BLOCK, K_BLOCK), F32),
                        pltpu.VMEM((n_slots, Q_BLOCK, CTX_LEN), F32),
                        pltpu.VMEM((n_slots, Q_BLOCK, K_BLOCK), BF16),
                        pltpu.VMEM((n_slots, Q_BLOCK, CTX_LEN), BF16),
                        pltpu.VMEM((n_slots, Q_BLOCK, LANES), F32),
                        pltpu.VMEM((n_slots, Q_BLOCK, LANES), F32),
                        pltpu.VMEM((1, Q_BLOCK, LANES), F32)],
        compiler_params=pltpu.CompilerParams(
            dimension_semantics=("arbitrary", "arbitrary"),
            vmem_limit_bytes=VMEM_LIMIT),
        name="attention",
    )(q, k, v, kc, vc, bias)


def _rot256(a, b, s, lane):
    s %= 2 * LANES
    if s >= LANES:
        a, b, s = b, a, s - LANES
    if s == 0:
        return a, b
    ra = pltpu.roll(a, s, 1)
    rb = pltpu.roll(b, s, 1)
    keep = lane >= s
    return jnp.where(keep, ra, rb), jnp.where(keep, rb, ra)


S5_PREP_GROUPS = 2
POWER_ROWS = 24


def _s5_prep_kernel(*refs):
    for gi in range(S5_PREP_GROUPS):
        _s5_prep_group(gi, *refs)


def _s5_prep_group(gi, par_ref, b_ref, c_ref, t16_ref, e_ref, mi_ref, ms_ref, mo_ref, a_ref):
    dot = functools.partial(jnp.dot, preferred_element_type=F32)

    def split2(x):
        hi = x.astype(BF16)
        return hi, (x - hi.astype(F32)).astype(BF16)

    def split3(x):
        hi = x.astype(BF16)
        r1 = x - hi.astype(F32)
        mid = r1.astype(BF16)
        return hi, mid, (r1 - mid.astype(F32)).astype(BF16)

    def pick(x, onehot):
        hi, mid, lo = split3(x)
        return dot(hi, onehot) + dot(mid, onehot) + dot(lo, onehot)

    def dot_f32(a, b):
        ah, al = split2(a)
        bh, bl = split2(b)
        return dot(jnp.concatenate([ah, ah, al], axis=1), jnp.concatenate([bh, bl, bh], axis=0))

    ns = SSM_STATE
    lam_re, lam_im = par_ref[gi, 0:1, :], par_ref[gi, 1:2, :]
    dt = jnp.exp(par_ref[gi, 2:3, :])
    lane = lax.broadcasted_iota(jnp.int32, (1, LANES), 1)
    kf = lax.broadcasted_iota(jnp.int32, (POWER_ROWS, LANES), 0).astype(F32)
    mag = jnp.exp((lam_re * dt) * kf)
    ang = (lam_im * dt) * kf
    pwt_re, pwt_im = mag * jnp.cos(ang), mag * jnp.sin(ang)
    den = lam_re * lam_re + lam_im * lam_im
    nr, ni = pwt_re[1:2, :] - 1.0, pwt_im[1:2, :]
    f_re_row = (nr * lam_re + ni * lam_im) / den
    f_im_row = (ni * lam_re - nr * lam_im) / den
    r8 = lax.broadcasted_iota(jnp.int32, (8, LANES), 0)
    f_rows = jnp.where(r8 == 0, f_re_row, jnp.where(r8 == 1, f_im_row, 0.0))
    pad = jnp.zeros((LANES - POWER_ROWS - 8, LANES), F32)
    pw_re = jnp.concatenate([pwt_re, f_rows, pad], axis=0).T
    pw_im = jnp.concatenate([pwt_im, f_rows, pad], axis=0).T
    f_re, f_im = pw_re[:, POWER_ROWS:POWER_ROWS + 1], pw_re[:, POWER_ROWS + 1:POWER_ROWS + 2]
    b_re, b_im = b_ref[gi, :, 0:SSM_GROUP], b_ref[gi, :, SSM_GROUP:2 * SSM_GROUP]
    bb_re = f_re * b_re - f_im * b_im
    bb_im = f_re * b_im + f_im * b_re
    t16 = t16_ref[...]
    bbt_re, bbt_im = pick(bb_re, t16), pick(bb_im, t16)
    ct_re = pick(c_ref[gi, :, 0:SSM_GROUP], t16)
    ct_im = pick(c_ref[gi, :, SSM_GROUP:2 * SSM_GROUP], t16)
    pw_at = lambda x: (pick(pw_re, e_ref[x]), pick(pw_im, e_ref[x]))
    id_re, id_im = pw_at(0)
    rev_re, rev_im = pw_at(1)
    p1_re, p1_im = pw_at(2)
    r16_re, r16_im = pw_at(3)
    f, b = slice(0, ns), slice(ns, 2 * ns)
    cmul = lambda ar, ai, br, bi: (ar * br - ai * bi, ar * bi + ai * br)

    sf_re, sf_im = cmul(rev_re[f], rev_im[f], bbt_re[f], bbt_im[f])
    sb_re, sb_im = cmul(id_re[b], id_im[b], bbt_re[b], bbt_im[b])
    ms_ref[gi] = jnp.concatenate([sf_re, sb_re, sf_im, sb_im], axis=0).T.astype(BF16)

    of_re, of_im = cmul(p1_re[f], p1_im[f], ct_re[f], ct_im[f])
    ob_re, ob_im = cmul(r16_re[b], r16_im[b], ct_re[b], ct_im[b])
    mo_ref[gi] = jnp.concatenate([of_re, ob_re, -of_im, -ob_im], axis=0).astype(BF16)

    xf_re, xf_im = cmul(id_re[f], id_im[f], ct_re[f], ct_im[f])
    xb_re, xb_im = cmul(rev_re[b], rev_im[b], ct_re[b], ct_im[b])
    btf = jnp.concatenate([bbt_re[f], bbt_im[f]], axis=0).T
    btb = jnp.concatenate([bbt_re[b], bbt_im[b]], axis=0).T
    g_f = dot_f32(btf, jnp.concatenate([xf_re, -xf_im], axis=0))
    g_b = dot_f32(btb, jnp.concatenate([xb_re, -xb_im], axis=0))
    for j in range(CHUNK):
        rows = slice(SSM_GROUP * j, SSM_GROUP * (j + 1))
        lo_col, hi_col = SSM_GROUP * j, SSM_GROUP * (j + 1)
        f_lo, f_hi = _rot256(g_f[rows, :LANES], g_f[rows, LANES:], lo_col, lane)
        b_lo, b_hi = _rot256(g_b[rows, :LANES], g_b[rows, LANES:], -SSM_GROUP * (CHUNK - 1 - j), lane)
        lo = jnp.where(lane >= lo_col, f_lo, 0.0) + jnp.where(lane < hi_col, b_lo, 0.0)
        up = jnp.where(lane + LANES >= lo_col, f_hi, 0.0) + jnp.where(lane + LANES < hi_col, b_hi, 0.0)
        mi_ref[gi, rows, 0:LANES] = lo.astype(BF16)
        mi_ref[gi, rows, LANES:2 * LANES] = up.astype(BF16)

    a16 = jnp.concatenate([pwt_re[CHUNK:CHUNK + 1, :], pwt_im[CHUNK:CHUNK + 1, :]], axis=1)
    a_ref[gi] = jnp.broadcast_to(a16, (8, 2 * LANES))


def _s5_matrices(lam_re, lam_im, log_dt, b_re, b_im, c_re, c_im):
    g, p2 = SSM_GROUPS, 2 * SSM_STATE
    both = lambda a: jnp.transpose(a, (1, 0, 2)).reshape(g, p2)
    log_dt_rows = jnp.repeat(jnp.transpose(log_dt), SSM_STATE, axis=1)
    par = jnp.stack([both(lam_re), both(lam_im), log_dt_rows], axis=1)
    par = jnp.concatenate([par, jnp.zeros((g, LANES - 3, p2), F32)], axis=1)
    rows_b = lambda a: jnp.transpose(a, (1, 0, 2, 3)).reshape(g, p2, SSM_GROUP)
    rows_c = lambda a: jnp.transpose(a, (1, 0, 3, 2)).reshape(g, p2, SSM_GROUP)
    b_cat = jnp.concatenate([rows_b(b_re), rows_b(b_im)], axis=-1)
    c_cat = jnp.concatenate([rows_c(c_re), rows_c(c_im)], axis=-1)

    col = np.arange(CHUNK_COLS)
    tile16 = (col[None, :] % SSM_GROUP == np.arange(SSM_GROUP)[:, None]).astype(np.float32)
    pos = col // SSM_GROUP
    k_idx = np.arange(LANES)[:, None]
    expand = np.stack([k_idx == pos[None, :], k_idx == (CHUNK - 1 - pos)[None, :],
                       k_idx == (pos + 1)[None, :], k_idx == (CHUNK - pos)[None, :]]).astype(np.float32)

    gp = S5_PREP_GROUPS
    mat = lambda: pl.BlockSpec((gp, CHUNK_COLS, CHUNK_COLS), lambda i: (i, 0, 0))
    mat_shape = jax.ShapeDtypeStruct((g, CHUNK_COLS, CHUNK_COLS), BF16)
    return pl.pallas_call(
        _s5_prep_kernel,
        grid=(g // gp,),
        in_specs=[pl.BlockSpec((gp, LANES, p2), lambda i: (i, 0, 0)),
                  pl.BlockSpec((gp, p2, 2 * SSM_GROUP), lambda i: (i, 0, 0)),
                  pl.BlockSpec((gp, p2, 2 * SSM_GROUP), lambda i: (i, 0, 0)),
                  pl.BlockSpec((SSM_GROUP, CHUNK_COLS), lambda i: (0, 0)),
                  pl.BlockSpec((4, LANES, CHUNK_COLS), lambda i: (0, 0, 0))],
        out_specs=[mat(), mat(), mat(), pl.BlockSpec((gp, 8, 2 * LANES), lambda i: (i, 0, 0))],
        out_shape=[mat_shape, mat_shape, mat_shape, jax.ShapeDtypeStruct((g, 8, 2 * LANES), F32)],
        name="s5_prep",
    )(par, b_cat, c_cat, jnp.asarray(tile16, BF16), jnp.asarray(expand, BF16))


def _s5_kernel(ul_ref, uc_ref, ms_ref, mi_ref, mo_ref, a_ref, y_ref, s_ref, sc_ref, hp_ref, *, gb):
    for gi in range(gb):
        for b in range(BATCH):
            sb = jnp.dot(ul_ref[gi, b], ms_ref[gi], preferred_element_type=F32)
            s_ref[gi, 0, pl.ds(b, N_CHUNKS, stride=BATCH), :] = sb[:, :LANES]
            s_ref[gi, 1, pl.ds(b, N_CHUNKS, stride=BATCH), :] = sb[:, LANES:]
            cb = jnp.dot(uc_ref[gi, b], ms_ref[gi], preferred_element_type=F32)
            sc_ref[gi, 0, pl.ds(b, N_CTX_CHUNKS, stride=BATCH), :] = cb[:, :LANES]
            sc_ref[gi, 1, pl.ds(b, N_CTX_CHUNKS, stride=BATCH), :] = cb[:, LANES:]
    lane = lax.broadcasted_iota(jnp.int32, (BATCH, LANES), 1)
    fwd = lane < SSM_STATE
    half = SSM_STATE

    def advance(gi, h_re, h_im, row_f, row_b, src):
        s_re = jnp.where(fwd, src[gi, 0, pl.ds(row_f, BATCH), :], src[gi, 0, pl.ds(row_b, BATCH), :])
        s_im = jnp.where(fwd, src[gi, 1, pl.ds(row_f, BATCH), :], src[gi, 1, pl.ds(row_b, BATCH), :])
        a_re = a_ref[gi, :, 0:LANES]
        a_im = a_ref[gi, :, LANES:2 * LANES]
        n_re = a_re * h_re - a_im * h_im + s_re
        n_im = a_re * h_im + a_im * h_re + s_im
        return n_re, n_im

    def ctx_step(t, carry):
        row_f = pl.multiple_of(t * BATCH, BATCH)
        row_b = pl.multiple_of((N_CTX_CHUNKS - 1 - t) * BATCH, BATCH)
        return tuple(advance(gi, carry[gi][0], carry[gi][1], row_f, row_b, sc_ref) for gi in range(gb))

    def lat_step(t, carry):
        row_f = pl.multiple_of(t * BATCH, BATCH)
        row_b = pl.multiple_of((N_CHUNKS - 1 - t) * BATCH, BATCH)
        out = []
        for gi in range(gb):
            h_re, h_im = carry[gi]
            hp_ref[gi, 0, pl.ds(row_f, BATCH), 0:half] = h_re[:, 0:half]
            hp_ref[gi, 0, pl.ds(row_b, BATCH), half:2 * half] = h_re[:, half:]
            hp_ref[gi, 1, pl.ds(row_f, BATCH), 0:half] = h_im[:, 0:half]
            hp_ref[gi, 1, pl.ds(row_b, BATCH), half:2 * half] = h_im[:, half:]
            out.append(advance(gi, h_re, h_im, row_f, row_b, s_ref))
        return tuple(out)

    zero = jnp.zeros((BATCH, LANES), F32)
    carry = tuple((zero, zero) for _ in range(gb))
    carry = lax.fori_loop(0, N_CTX_CHUNKS, ctx_step, carry)
    lax.fori_loop(0, N_CHUNKS, lat_step, carry)
    for gi in range(gb):
        for b in range(BATCH):
            hb_re = hp_ref[gi, 0, pl.ds(b, N_CHUNKS, stride=BATCH), :].astype(BF16)
            hb_im = hp_ref[gi, 1, pl.ds(b, N_CHUNKS, stride=BATCH), :].astype(BF16)
            y = (jnp.dot(ul_ref[gi, b], mi_ref[gi], preferred_element_type=F32)
                 + jnp.dot(jnp.concatenate([hb_re, hb_im], axis=1), mo_ref[gi], preferred_element_type=F32))
            y_ref[gi, b] = y.astype(BF16)


def _s5_scan(u_lat_t, u_ctx_t, m_intra, m_state, m_out, a16):
    gb = 4
    rows = N_CHUNKS * BATCH
    crows = N_CTX_CHUNKS * BATCH
    grp = lambda r, c: pl.BlockSpec((gb, r, c), lambda g: (g, 0, 0))
    tok = lambda n: pl.BlockSpec((gb, BATCH, n, CHUNK_COLS), lambda g: (g, 0, 0, 0))
    return pl.pallas_call(
        functools.partial(_s5_kernel, gb=gb),
        grid=(SSM_GROUPS // gb,),
        in_specs=[tok(N_CHUNKS), tok(N_CTX_CHUNKS), grp(CHUNK_COLS, CHUNK_COLS),
                  grp(CHUNK_COLS, CHUNK_COLS), grp(CHUNK_COLS, CHUNK_COLS), grp(8, 2 * LANES)],
        out_specs=tok(N_CHUNKS),
        out_shape=jax.ShapeDtypeStruct((SSM_GROUPS, BATCH, N_CHUNKS, CHUNK_COLS), BF16),
        scratch_shapes=[pltpu.VMEM((gb, 2, rows, LANES), F32),
                        pltpu.VMEM((gb, 2, crows, LANES), F32),
                        pltpu.VMEM((gb, 2, rows, LANES), F32)],
        compiler_params=pltpu.CompilerParams(
            dimension_semantics=("arbitrary",), vmem_limit_bytes=VMEM_LIMIT),
        name="s5_scan",
    )(u_lat_t, u_ctx_t, m_state, m_intra, m_out, a16)


FFN_TILE = 256


def _post_kernel(x_ref, a_ref, yt_ref, u_ref, ga_ref, gs_ref, mod_ref, d_ref, fg_ref, og_ref, permt_ref,
                 wglu_ref, wba_ref, wbs_ref, wout_ref, wfi_ref, wfo_ref, o_ref, h1_ref, n2_ref, act_ref, r_ref):
    dm = D_MODEL
    g1 = mod_ref[0, :, 2 * dm:3 * dm]
    sh2 = mod_ref[0, :, 3 * dm:4 * dm]
    sc2 = mod_ref[0, :, 4 * dm:5 * dm]
    g2 = mod_ref[0, :, 5 * dm:6 * dm]
    half = CHUNK * CHUNK
    halves = [slice(h * half, (h + 1) * half) for h in range(x_ref.shape[1] // half)]
    dot = functools.partial(jnp.dot, preferred_element_type=F32)

    sp = []
    for h, rows in enumerate(halves):
        y = _load_chunk_layout(yt_ref, permt_ref, r_ref, h)
        sp.append(jax.nn.gelu(y + d_ref[...] * u_ref[0, rows, :].astype(F32)).astype(BF16))
    s = []
    for h, rows in enumerate(halves):
        vg = dot(sp[h], wglu_ref[...])
        s.append((vg[:, :SSM_WIDTH] * jax.nn.sigmoid(vg[:, SSM_WIDTH:])).astype(BF16))
    merged = []
    for h, rows in enumerate(halves):
        a = jnp.concatenate([a_ref[0, j, rows, :] for j in range(ATTN_WIDTH // LANES)], axis=1)
        m = (ga_ref[0, rows, :].astype(F32) * dot(a, wba_ref[...])
             + gs_ref[0, rows, :].astype(F32) * dot(s[h], wbs_ref[...]))
        merged.append(m.astype(BF16))
    for h, rows in enumerate(halves):
        h1 = x_ref[0, rows, :] + g1 * dot(merged[h], wout_ref[...])
        h1_ref[rows, :] = h1
        n2_ref[rows, :] = _rms_modulate(h1, fg_ref[...], sh2, sc2).astype(BF16)
    for lo in range(0, FFN_HIDDEN, FFN_TILE):
        width = min(FFN_TILE, FFN_HIDDEN - lo)
        for rows in halves:
            n2 = n2_ref[rows, :]
            fa = dot(n2, wfi_ref[:, lo:lo + width])
            fb = dot(n2, wfi_ref[:, FFN_HIDDEN + lo:FFN_HIDDEN + lo + width])
            act_ref[rows, lo:lo + width] = (fa * jax.nn.sigmoid(fa) * fb).astype(BF16)
    for rows in halves:
        h2 = h1_ref[rows, :] + g2 * dot(act_ref[rows, :], wfo_ref[...])
        o_ref[0, rows, :] = (h2 * lax.rsqrt(jnp.mean(h2 * h2, axis=-1, keepdims=True) + NORM_EPS)) * og_ref[...]


def _post(x, a, y_t, u, ga, gs, mod3, d_skip, ffn_g, fin_g, perm_t, wglu, wba, wbs, wout, wfi, wfo):
    tm = 512
    n = CHUNK * CHUNK
    tok = lambda width: pl.BlockSpec((1, tm, width), lambda b, i: (b, i, 0))
    const = lambda r, c: pl.BlockSpec((r, c), lambda b, i: (0, 0), pipeline_mode=pl.Buffered(1))
    return pl.pallas_call(
        _post_kernel,
        grid=(BATCH, SEQ // tm),
        in_specs=[tok(D_MODEL),
                  pl.BlockSpec((1, N_HEADS // 2, tm, LANES), lambda b, i: (b, 0, i, 0)),
                  pl.BlockSpec((SSM_GROUPS, 1, tm // CHUNK, CHUNK_COLS), lambda b, i: (0, b, i, 0)),
                  tok(SSM_WIDTH), tok(D_MODEL), tok(D_MODEL),
                  pl.BlockSpec((1, 1, N_MOD * D_MODEL), lambda b, i: (b, 0, 0)),
                  const(1, SSM_WIDTH), const(1, D_MODEL), const(1, D_MODEL), const(n, n),
                  const(SSM_WIDTH, 2 * SSM_WIDTH), const(ATTN_WIDTH, D_MODEL),
                  const(SSM_WIDTH, D_MODEL), const(D_MODEL, D_MODEL),
                  const(D_MODEL, 2 * FFN_HIDDEN), const(FFN_HIDDEN, D_MODEL)],
        out_specs=tok(D_MODEL),
        out_shape=jax.ShapeDtypeStruct((BATCH, SEQ, D_MODEL), F32),
        scratch_shapes=[pltpu.VMEM((tm, D_MODEL), F32), pltpu.VMEM((tm, D_MODEL), BF16),
                        pltpu.VMEM((tm, FFN_HIDDEN), BF16), pltpu.VMEM((tm // n, n, SSM_WIDTH), F32)],
        compiler_params=pltpu.CompilerParams(
            dimension_semantics=("arbitrary", "arbitrary"), vmem_limit_bytes=VMEM_LIMIT),
        name="post",
    )(x, a, y_t, u, ga, gs, mod3, d_skip, ffn_g, fin_g, perm_t, wglu, wba, wbs, wout, wfi, wfo)


def kernel(x, c, ctx, c_ctx, w_mod, b_mod, attn_norm_g, ffn_norm_g, w_in, rel_pos_bias,
           ssm_lambda_re, ssm_lambda_im, ssm_log_dt, ssm_b_re, ssm_b_im, ssm_c_re, ssm_c_im, ssm_d,
           w_glu, w_branch_attn, w_branch_ssm, w_out, w_ffn_in, w_ffn_out, final_norm_g):
    assert x.shape == (BATCH, SEQ, D_MODEL) and w_mod.shape[0] == 1
    c_rows = jnp.concatenate(
        [c, c_ctx[None, :], jnp.zeros((MOD_ROWS - BATCH - 1, D_MODEL), F32)], axis=0)
    mod3 = _modulation(c_rows, w_mod[0], b_mod[0]).reshape(MOD_ROWS, 1, N_MOD * D_MODEL)

    col_scale = jnp.concatenate([jnp.full((ATTN_WIDTH,), HEAD_DIM ** -0.5 * LOG2E, F32),
                                 jnp.ones((IN_COLS - ATTN_WIDTH,), F32)])
    w_in_bf16 = (w_in[0] * col_scale[None, :]).astype(BF16)
    norm_g = attn_norm_g[0].reshape(1, D_MODEL)

    perm = _chunk_perm()
    q, k, v, u, u_t, ga, gs = _input_projection(x, mod3, norm_g, w_in_bf16, _rope_tables(), perm)
    kc, vc, uc_t = _context_projection(ctx, mod3, norm_g, w_in_bf16, perm)

    attn = _attention(q, k, v, kc, vc, _bias_tables(rel_pos_bias[0]))

    m_intra, m_state, m_out, a16 = _s5_matrices(
        ssm_lambda_re[0], ssm_lambda_im[0], ssm_log_dt[0], ssm_b_re[0], ssm_b_im[0],
        ssm_c_re[0], ssm_c_im[0])
    y_t = _s5_scan(u_t, uc_t, m_intra, m_state, m_out, a16)

    return _post(x, attn, y_t, u, ga, gs, mod3,
                 ssm_d[0].reshape(1, SSM_WIDTH), ffn_norm_g[0].reshape(1, D_MODEL),
                 final_norm_g.reshape(1, D_MODEL), perm,
                 w_glu[0].astype(BF16), w_branch_attn[0].astype(BF16), w_branch_ssm[0].astype(BF16),
                 w_out[0].astype(BF16), w_ffn_in[0].astype(BF16), w_ffn_out[0].astype(BF16))
```

```python
import functools
import math

import numpy as np
import jax
import jax.numpy as jnp
from jax import lax
from jax.experimental import pallas as pl
from jax.experimental.pallas import tpu as pltpu

F32 = jnp.float32
BF16 = jnp.bfloat16

D_MODEL = 1024
BATCH = 8
SEQ = 4096
GRID_W = 64
GRID_ROWS = SEQ // GRID_W
CTX_LEN = 256
N_HEADS = 8
HEAD_DIM = 64
ATTN_WIDTH = N_HEADS * HEAD_DIM
WIN_H = 8
WIN_W = 16
ROPE_BASE = 10000.0
SSM_WIDTH = 512
SSM_GROUP = 16
SSM_GROUPS = SSM_WIDTH // SSM_GROUP
SSM_STATE = 64
FFN_HIDDEN = 2816
IN_COLS = 3 * ATTN_WIDTH + SSM_WIDTH + 2 * D_MODEL
N_MOD = 6
NORM_EPS = 1e-6
NEG_BIG = -1e30
LOG2E = math.log2(math.e)

LANES = 128
CHUNK = 16
N_CHUNKS = SEQ // CHUNK
N_CTX_CHUNKS = CTX_LEN // CHUNK
CHUNK_COLS = CHUNK * SSM_GROUP
MOD_ROWS = 16
CTX_MOD_ROW = BATCH
VMEM_LIMIT = 56 * 1024 * 1024

Q_ROWS_PER_STEP = 4
Q_BLOCK = Q_ROWS_PER_STEP * GRID_W
K_ROWS_PER_STEP = Q_ROWS_PER_STEP + WIN_H
K_BLOCK = K_ROWS_PER_STEP * GRID_W
N_QBLOCKS = GRID_ROWS // Q_ROWS_PER_STEP
KEY_GROUP = Q_BLOCK
SOFTMAX_ROWS = 32

def _rms_modulate(x, g, shift, scale):
    xn = x * lax.rsqrt(jnp.mean(x * x, axis=-1, keepdims=True) + NORM_EPS)
    return (xn * g) * (1.0 + scale) + shift


def _block_transpose8(vs, lane):
    for shift in (64, 32, 16):
        keep = (lane & (2 * shift - 1)) < shift
        dist = shift // SSM_GROUP
        out = list(vs)
        for a in range(8):
            if a & dist:
                continue
            b = a + dist
            out[a] = jnp.where(keep, vs[a], pltpu.roll(vs[b], shift, 1))
            out[b] = jnp.where(keep, pltpu.roll(vs[a], LANES - shift, 1), vs[b])
        vs = out
    return vs


def _chunk_perm():
    n = CHUNK * CHUNK
    r = np.arange(n)
    m = np.zeros((n, n), np.float32)
    m[r, (r % CHUNK) * CHUNK + r // CHUNK] = 1.0
    return jnp.asarray(m, BF16)


def _store_chunk_layout(u16, perm_ref, out_ref, n_groups16, group0=0):
    lane = lax.broadcasted_iota(jnp.int32, (1, LANES), 1)
    n = CHUNK * CHUNK
    for hf in range(n_groups16):
        r = jnp.dot(perm_ref[...], u16[hf * n:(hf + 1) * n, :], preferred_element_type=F32)
        c0 = (group0 + hf) * CHUNK
        for v in range(SSM_WIDTH // LANES):
            for jh in range(2):
                vs = [r[CHUNK * (8 * jh + jp):CHUNK * (8 * jh + jp + 1), v * LANES:(v + 1) * LANES]
                      for jp in range(8)]
                outs = _block_transpose8(vs, lane)
                for gi in range(8):
                    out_ref[8 * v + gi, 0, c0:c0 + CHUNK, jh * LANES:(jh + 1) * LANES] = outs[gi].astype(BF16)


def _load_chunk_layout(yt_ref, perm_t_ref, r_ref, hf):
    lane = lax.broadcasted_iota(jnp.int32, (1, LANES), 1)
    for v in range(SSM_WIDTH // LANES):
        for jh in range(2):
            vs = [yt_ref[8 * v + gi, 0, hf * CHUNK:(hf + 1) * CHUNK,
                         jh * LANES:(jh + 1) * LANES].astype(F32) for gi in range(8)]
            outs = _block_transpose8(vs, lane)
            for jp in range(8):
                r_ref[hf, CHUNK * (8 * jh + jp):CHUNK * (8 * jh + jp + 1), v * LANES:(v + 1) * LANES] = outs[jp]
    return jnp.dot(perm_t_ref[...], r_ref[hf].astype(BF16), preferred_element_type=F32)


def _mod_kernel(c_ref, w_ref, b_ref, o_ref):
    c = c_ref[...]
    s = c * jax.nn.sigmoid(c)
    o_ref[...] = jnp.dot(s, w_ref[...], preferred_element_type=F32) + b_ref[...]


def _modulation(c_rows, w_mod, b_mod):
    n = N_MOD * D_MODEL
    tn = 1536
    return pl.pallas_call(
        _mod_kernel,
        grid=(n // tn,),
        in_specs=[pl.BlockSpec((MOD_ROWS, D_MODEL), lambda j: (0, 0)),
                  pl.BlockSpec((D_MODEL, tn), lambda j: (0, j)),
                  pl.BlockSpec((1, tn), lambda j: (0, j))],
        out_specs=pl.BlockSpec((MOD_ROWS, tn), lambda j: (0, j)),
        out_shape=jax.ShapeDtypeStruct((MOD_ROWS, n), F32),
        name="modulation",
    )(c_rows, w_mod, b_mod.reshape(1, n))


def _rope_tables():
    n_freq = HEAD_DIM // 4
    inv_freq = ROPE_BASE ** (-np.arange(n_freq, dtype=np.float64) / n_freq)
    t = np.arange(SEQ)
    lane = np.arange(LANES)
    d = lane % HEAD_DIM
    use_col = (d // (HEAD_DIM // 2)) == 1
    w = d % (HEAD_DIM // 2)
    first = w < n_freq
    pos = np.where(use_col[None, :], (t % GRID_W)[:, None], (t // GRID_W)[:, None]).astype(np.float64)
    ang = pos * inv_freq[w % n_freq][None, :]
    cos = np.cos(ang)
    sin = np.sin(ang)
    sin_a = np.where(first[None, :], -sin, 0.0)
    sin_b = np.where(first[None, :], 0.0, sin)
    return (jnp.asarray(cos, F32), jnp.asarray(sin_a, F32), jnp.asarray(sin_b, F32))


def _rope_store(r, cos, sin_a, sin_b, out_ref, row0, transposed=False):
    n_rows = r.shape[0]
    for j in range(ATTN_WIDTH // LANES):
        xs = r[:, j * LANES:(j + 1) * LANES]
        rot = (xs * cos + pltpu.roll(xs, LANES - HEAD_DIM // 4, 1) * sin_a
               + pltpu.roll(xs, HEAD_DIM // 4, 1) * sin_b)
        if transposed:
            rot_t = rot.T.astype(BF16)
            for gi in range(n_rows // KEY_GROUP):
                out_ref[0, row0 // KEY_GROUP + gi, j * LANES:(j + 1) * LANES, :] = (
                    rot_t[:, gi * KEY_GROUP:(gi + 1) * KEY_GROUP])
        else:
            out_ref[0, j, row0:row0 + n_rows, :] = rot.astype(BF16)


INPROJ_SUBTILE = 512


def _inproj_kernel(x_ref, mod_ref, g_ref, w_ref, cos_ref, sa_ref, sb_ref, perm_ref,
                   q_ref, k_ref, v_ref, u_ref, ut_ref, ga_ref, gs_ref):
    shift = mod_ref[0, :, 0:D_MODEL]
    scale = mod_ref[0, :, D_MODEL:2 * D_MODEL]
    aw = ATTN_WIDTH
    dot = functools.partial(jnp.dot, preferred_element_type=F32)
    for row0 in range(0, x_ref.shape[1], INPROJ_SUBTILE):
        rows = slice(row0, row0 + INPROJ_SUBTILE)
        nb = _rms_modulate(x_ref[0, rows, :], g_ref[...], shift, scale).astype(BF16)
        cos = cos_ref[rows, :]
        sin_a = sa_ref[rows, :]
        sin_b = sb_ref[rows, :]
        _rope_store(dot(nb, w_ref[:, 0:aw]), cos, sin_a, sin_b, q_ref, row0)
        _rope_store(dot(nb, w_ref[:, aw:2 * aw]), cos, sin_a, sin_b, k_ref, row0, transposed=True)
        v = dot(nb, w_ref[:, 2 * aw:3 * aw]).astype(BF16)
        for j in range(aw // LANES):
            v_ref[0, j, rows, :] = v[:, j * LANES:(j + 1) * LANES]
        c0 = 3 * aw
        u16 = dot(nb, w_ref[:, c0:c0 + SSM_WIDTH]).astype(BF16)
        u_ref[0, rows, :] = u16
        n16 = CHUNK * CHUNK
        _store_chunk_layout(u16, perm_ref, ut_ref, INPROJ_SUBTILE // n16, row0 // n16)
        c1 = c0 + SSM_WIDTH
        ga_ref[0, rows, :] = jax.nn.sigmoid(dot(nb, w_ref[:, c1:c1 + D_MODEL])).astype(BF16)
        c2 = c1 + D_MODEL
        gs_ref[0, rows, :] = jax.nn.sigmoid(dot(nb, w_ref[:, c2:c2 + D_MODEL])).astype(BF16)


def _input_projection(x, mod3, norm_g, w_in_bf16, rope, perm):
    tm = 2 * INPROJ_SUBTILE
    n = CHUNK * CHUNK
    cos, sin_a, sin_b = rope
    tok = lambda width: pl.BlockSpec((1, tm, width), lambda i, b: (b, i, 0))
    tab = pl.BlockSpec((tm, LANES), lambda i, b: (i, 0))
    out = lambda width: jax.ShapeDtypeStruct((BATCH, SEQ, width), BF16)
    pairs = pl.BlockSpec((1, N_HEADS // 2, tm, LANES), lambda i, b: (b, 0, i, 0))
    pairs_shape = jax.ShapeDtypeStruct((BATCH, N_HEADS // 2, SEQ, LANES), BF16)
    return pl.pallas_call(
        _inproj_kernel,
        grid=(SEQ // tm, BATCH),
        in_specs=[tok(D_MODEL),
                  pl.BlockSpec((1, 1, N_MOD * D_MODEL), lambda i, b: (b, 0, 0)),
                  pl.BlockSpec((1, D_MODEL), lambda i, b: (0, 0)),
                  pl.BlockSpec((D_MODEL, IN_COLS), lambda i, b: (0, 0), pipeline_mode=pl.Buffered(1)),
                  tab, tab, tab,
                  pl.BlockSpec((n, n), lambda i, b: (0, 0), pipeline_mode=pl.Buffered(1))],
        out_specs=[pairs,
                   pl.BlockSpec((1, tm // KEY_GROUP, ATTN_WIDTH, KEY_GROUP), lambda i, b: (b, i, 0, 0)),
                   pairs, tok(SSM_WIDTH),
                   pl.BlockSpec((SSM_GROUPS, 1, tm // CHUNK, CHUNK_COLS), lambda i, b: (0, b, i, 0)),
                   tok(D_MODEL), tok(D_MODEL)],
        out_shape=[pairs_shape,
                   jax.ShapeDtypeStruct((BATCH, SEQ // KEY_GROUP, ATTN_WIDTH, KEY_GROUP), BF16),
                   pairs_shape, out(SSM_WIDTH),
                   jax.ShapeDtypeStruct((SSM_GROUPS, BATCH, N_CHUNKS, CHUNK_COLS), BF16),
                   out(D_MODEL), out(D_MODEL)],
        compiler_params=pltpu.CompilerParams(
            dimension_semantics=("arbitrary", "arbitrary"), vmem_limit_bytes=VMEM_LIMIT),
        name="input_projection",
    )(x, mod3, norm_g, w_in_bf16, cos, sin_a, sin_b, perm)


def _ctx_proj_kernel(x_ref, mod_ref, g_ref, wk_ref, wv_ref, wu_ref, perm_ref, k_ref, v_ref, ut_ref):
    x = x_ref[0]
    shift = mod_ref[0, :, 0:D_MODEL]
    scale = mod_ref[0, :, D_MODEL:2 * D_MODEL]
    nb = _rms_modulate(x, g_ref[...], shift, scale).astype(BF16)
    k_ref[0] = jnp.dot(nb, wk_ref[...], preferred_element_type=F32).T.astype(BF16)
    v_ref[0] = jnp.dot(nb, wv_ref[...], preferred_element_type=F32).astype(BF16)
    u16 = jnp.dot(nb, wu_ref[...], preferred_element_type=F32).astype(BF16)
    _store_chunk_layout(u16, perm_ref, ut_ref, 1)


def _context_projection(ctx, mod3, norm_g, w_in_bf16, perm):
    n = CHUNK * CHUNK
    aw = ATTN_WIDTH
    tok = lambda width: pl.BlockSpec((1, CTX_LEN, width), lambda b: (b, 0, 0))
    wcol = lambda j: pl.BlockSpec((D_MODEL, aw), lambda b: (0, j))
    out = jax.ShapeDtypeStruct((BATCH, CTX_LEN, aw), BF16)
    return pl.pallas_call(
        _ctx_proj_kernel,
        grid=(BATCH,),
        in_specs=[tok(D_MODEL),
                  pl.BlockSpec((1, 1, N_MOD * D_MODEL), lambda b: (CTX_MOD_ROW, 0, 0)),
                  pl.BlockSpec((1, D_MODEL), lambda b: (0, 0)),
                  wcol(1), wcol(2), wcol(3),
                  pl.BlockSpec((n, n), lambda b: (0, 0))],
        out_specs=[pl.BlockSpec((1, aw, CTX_LEN), lambda b: (b, 0, 0)), tok(aw),
                   pl.BlockSpec((SSM_GROUPS, 1, N_CTX_CHUNKS, CHUNK_COLS), lambda b: (0, b, 0, 0))],
        out_shape=[jax.ShapeDtypeStruct((BATCH, aw, CTX_LEN), BF16), out,
                   jax.ShapeDtypeStruct((SSM_GROUPS, BATCH, N_CTX_CHUNKS, CHUNK_COLS), BF16)],
        name="context_projection",
    )(ctx, mod3, norm_g, w_in_bf16, w_in_bf16, w_in_bf16, perm)


def _window_start_rows(r):
    return min(max(r - WIN_H // 2, 0), GRID_ROWS - WIN_H)


def _key_block_row(jb):
    return min(max(Q_ROWS_PER_STEP * jb - WIN_H // 2, 0), GRID_ROWS - K_ROWS_PER_STEP)


def _window_geometry(jb):
    key_row0 = _key_block_row(jb)
    offs, deltas = [], []
    for i in range(Q_ROWS_PER_STEP):
        r = Q_ROWS_PER_STEP * jb + i
        rs = _window_start_rows(r)
        offs.append(rs - key_row0)
        deltas.append(r - rs)
    return offs, deltas


ATTN_SAMPLES_PER_STEP = 2


def _attn_kernel(*refs):
    def per_sample(bi, carry):
        _attn_sample(bi, *refs)
        return carry

    lax.fori_loop(0, ATTN_SAMPLES_PER_STEP, per_sample, 0)


def _attn_sample(bi, q_ref, k_ref, v_ref, kc_ref, vc_ref, bias_ref, o_ref,
                 s_ref, sc_ref, p_ref, pc_ref, l_ref, m_ref, acc_ref):
    lane = lax.broadcasted_iota(jnp.int32, (1, LANES), 1)
    left = lane < HEAD_DIM
    n_ktiles = K_BLOCK // LANES
    last = N_QBLOCKS - 1
    max_key_start = (GRID_ROWS - K_ROWS_PER_STEP) * GRID_W

    def scores(sl, e, q_start, key_start):
        q2 = q_ref[bi, 0, pl.ds(q_start, Q_BLOCK), :]
        qm = jnp.where(left if e == 0 else jnp.logical_not(left), q2, jnp.zeros_like(q2))
        group0 = key_start // KEY_GROUP
        for gi in range(K_BLOCK // KEY_GROUP):
            s_ref[sl, :, gi * KEY_GROUP:(gi + 1) * KEY_GROUP] = jnp.dot(
                qm, k_ref[bi, group0 + gi], preferred_element_type=F32)
        sc_ref[sl] = jnp.dot(qm, kc_ref[bi], preferred_element_type=F32)

    def softmax(sl, e, offs, deltas):
        def geometry(sub):
            i = sub * SOFTMAX_ROWS // GRID_W
            qcols = slice(sub * SOFTMAX_ROWS % GRID_W, sub * SOFTMAX_ROWS % GRID_W + SOFTMAX_ROWS)
            rows = slice(sub * SOFTMAX_ROWS, (sub + 1) * SOFTMAX_ROWS)
            par = offs[i] % 2
            return rows, qcols, par, offs[i] // 2, WIN_H // 2 + par, deltas[i]

        def tile(rows, qcols, par, t0, n_tiles, delta, xt):
            t = t0 + xt
            dr0 = 2 * xt - par - delta
            st = s_ref[sl, rows, t * LANES:(t + 1) * LANES] + bias_ref[0, e, dr0 + WIN_H, qcols, :]
            if par and xt == 0:
                st = jnp.where(left, NEG_BIG, st)
            if par and xt == n_tiles - 1:
                st = jnp.where(left, st, NEG_BIG)
            return st

        n_sub = Q_BLOCK // SOFTMAX_ROWS
        for sub in range(n_sub):
            geo = geometry(sub)
            rows, n_tiles = geo[0], geo[4]
            mt = jnp.maximum(sc_ref[sl, rows, 0:LANES], sc_ref[sl, rows, LANES:2 * LANES])
            for xt in range(n_tiles):
                mt = jnp.maximum(mt, tile(*geo, xt))
            m_ref[sl, rows, :] = jnp.broadcast_to(jnp.max(mt, axis=1, keepdims=True), (SOFTMAX_ROWS, LANES))
        for sub in range(n_sub):
            geo = geometry(sub)
            rows, t0, n_tiles = geo[0], geo[3], geo[4]
            m = m_ref[sl, rows, :]
            lt = None
            for xt in range(n_tiles):
                t = t0 + xt
                pt = jnp.exp2(tile(*geo, xt) - m)
                lt = pt if lt is None else lt + pt
                p_ref[sl, rows, t * LANES:(t + 1) * LANES] = pt.astype(BF16)
            for t in range(n_ktiles):
                if not (t0 <= t < t0 + n_tiles):
                    p_ref[sl, rows, t * LANES:(t + 1) * LANES] = jnp.zeros((SOFTMAX_ROWS, LANES), BF16)
            for ci in range(2):
                pt = jnp.exp2(sc_ref[sl, rows, ci * LANES:(ci + 1) * LANES] - m)
                lt = lt + pt
                pc_ref[sl, rows, ci * LANES:(ci + 1) * LANES] = pt.astype(BF16)
            l = jnp.sum(lt, axis=1, keepdims=True)
            l_ref[sl, rows, :] = jnp.broadcast_to(1.0 / l, (SOFTMAX_ROWS, LANES))

    def values(sl, key_start):
        vblk = v_ref[bi, 0, pl.ds(key_start, K_BLOCK), :]
        return (jnp.dot(p_ref[sl], vblk, preferred_element_type=F32)
                + jnp.dot(pc_ref[sl], vc_ref[bi], preferred_element_type=F32)) * l_ref[sl]

    def block(jb_static, cur, nxt):
        offs, deltas = _window_geometry(jb_static)
        scores(1, 1, *cur)
        softmax(0, 0, offs, deltas)
        acc_ref[0] = values(0, cur[1])
        if nxt is not None:
            scores(0, 0, *nxt)
        softmax(1, 1, offs, deltas)
        o1 = values(1, cur[1])
        o_ref[bi, 0, pl.ds(cur[0], Q_BLOCK), :] = jnp.where(left, acc_ref[0], o1).astype(BF16)

    def starts(jb):
        q_start = pl.multiple_of(jb * Q_BLOCK, Q_BLOCK)
        key_start = jnp.clip((jb * Q_ROWS_PER_STEP - WIN_H // 2) * GRID_W, 0, max_key_start)
        return q_start, pl.multiple_of(key_start, Q_BLOCK)

    scores(0, 0, 0, 0)
    block(0, (0, 0), (Q_BLOCK, 0))

    blocks_per_trip = 7

    def interior(trip, carry):
        jb = 1 + blocks_per_trip * trip
        for d in range(blocks_per_trip):
            block(1, starts(jb + d), starts(jb + d + 1))
        return carry

    assert (last - 1) % blocks_per_trip == 0
    lax.fori_loop(0, (last - 1) // blocks_per_trip, interior, 0)
    block(last, (last * Q_BLOCK, max_key_start), None)


def _bias_tables(rpb):
    qcol = np.arange(GRID_W)
    kcol = np.arange(GRID_W)
    col_start = np.clip(qcol - WIN_W // 2, 0, GRID_W - WIN_W)
    in_win = (kcol[None, :] >= col_start[:, None]) & (kcol[None, :] < col_start[:, None] + WIN_W)
    dc_idx = np.clip(kcol[None, :] - qcol[:, None], -(WIN_W - 1), WIN_W - 1) + WIN_W - 1
    sel = (np.arange(2 * WIN_W - 1)[:, None, None] == dc_idx[None]).astype(np.float32)
    toe = jnp.einsum('hdt,tck->hdck', rpb * LOG2E, jnp.asarray(sel),
                     precision=lax.Precision.HIGHEST)
    toe = jnp.where(in_win[None, None], toe, NEG_BIG)
    neg = jnp.full((N_HEADS, 1, GRID_W, GRID_W), NEG_BIG, F32)
    ext = jnp.concatenate([neg, toe, neg], axis=1)
    pair = jnp.concatenate([ext[:, 0:16], ext[:, 1:17]], axis=-1)
    return pair.reshape(N_HEADS // 2, 2, 16, GRID_W, LANES)


def _attention(q, k, v, kc, vc, bias):
    n_slots = 2
    nb = ATTN_SAMPLES_PER_STEP
    kspec = pl.BlockSpec((nb, 1, SEQ, LANES), lambda hp, b: (b, hp, 0, 0))
    ktspec = pl.BlockSpec((nb, SEQ // KEY_GROUP, LANES, KEY_GROUP), lambda hp, b: (b, 0, hp, 0))
    cspec = pl.BlockSpec((nb, CTX_LEN, LANES), lambda hp, b: (b, 0, hp))
    ctspec = pl.BlockSpec((nb, LANES, CTX_LEN), lambda hp, b: (b, hp, 0))
    bspec = pl.BlockSpec((1, 2, 16, GRID_W, LANES), lambda hp, b: (hp, 0, 0, 0, 0))
    return pl.pallas_call(
        _attn_kernel,
        grid=(N_HEADS // 2, BATCH // nb),
        in_specs=[kspec, ktspec, kspec, ctspec, cspec, bspec],
        out_specs=kspec,
        out_shape=jax.ShapeDtypeStruct((BATCH, N_HEADS // 2, SEQ, LANES), BF16),
        scratch_shapes=[pltpu.VMEM((n_slots, Q_BLOCK, K_BLOCK), F32),
                        pltpu.VMEM((n_slots, Q_BLOCK, CTX_LEN), F32),
                        pltpu.VMEM((n_slots, Q_BLOCK, K_BLOCK), BF16),
                        pltpu.VMEM((n_slots, Q_BLOCK, CTX_LEN), BF16),
                        pltpu.VMEM((n_slots, Q_BLOCK, LANES), F32),
                        pltpu.VMEM((n_slots, Q_BLOCK, LANES), F32),
                        pltpu.VMEM((1, Q_BLOCK, LANES), F32)],
        compiler_params=pltpu.CompilerParams(
            dimension_semantics=("arbitrary", "arbitrary"),
            vmem_limit_bytes=VMEM_LIMIT),
        name="attention",
    )(q, k, v, kc, vc, bias)


def _rot256(a, b, s, lane):
    s %= 2 * LANES
    if s >= LANES:
        a, b, s = b, a, s - LANES
    if s == 0:
        return a, b
    ra = pltpu.roll(a, s, 1)
    rb = pltpu.roll(b, s, 1)
    keep = lane >= s
    return jnp.where(keep, ra, rb), jnp.where(keep, rb, ra)


S5_PREP_GROUPS = 2
POWER_ROWS = 24


def _s5_prep_kernel(*refs):
    for gi in range(S5_PREP_GROUPS):
        _s5_prep_group(gi, *refs)


def _s5_prep_group(gi, par_ref, b_ref, c_ref, t16_ref, e_ref, mi_ref, ms_ref, mo_ref, a_ref):
    dot = functools.partial(jnp.dot, preferred_element_type=F32)

    def split2(x):
        hi = x.astype(BF16)
        return hi, (x - hi.astype(F32)).astype(BF16)

    def split3(x):
        hi = x.astype(BF16)
        r1 = x - hi.astype(F32)
        mid = r1.astype(BF16)
        return hi, mid, (r1 - mid.astype(F32)).astype(BF16)

    def pick(x, onehot):
        hi, mid, lo = split3(x)
        return dot(hi, onehot) + dot(mid, onehot) + dot(lo, onehot)

    def dot_f32(a, b):
        ah, al = split2(a)
        bh, bl = split2(b)
        return dot(jnp.concatenate([ah, ah, al], axis=1), jnp.concatenate([bh, bl, bh], axis=0))

    ns = SSM_STATE
    lam_re, lam_im = par_ref[gi, 0:1, :], par_ref[gi, 1:2, :]
    dt = jnp.exp(par_ref[gi, 2:3, :])
    lane = lax.broadcasted_iota(jnp.int32, (1, LANES), 1)
    kf = lax.broadcasted_iota(jnp.int32, (POWER_ROWS, LANES), 0).astype(F32)
    mag = jnp.exp((lam_re * dt) * kf)
    ang = (lam_im * dt) * kf
    pwt_re, pwt_im = mag * jnp.cos(ang), mag * jnp.sin(ang)
    den = lam_re * lam_re + lam_im * lam_im
    nr, ni = pwt_re[1:2, :] - 1.0, pwt_im[1:2, :]
    f_re_row = (nr * lam_re + ni * lam_im) / den
    f_im_row = (ni * lam_re - nr * lam_im) / den
    r8 = lax.broadcasted_iota(jnp.int32, (8, LANES), 0)
    f_rows = jnp.where(r8 == 0, f_re_row, jnp.where(r8 == 1, f_im_row, 0.0))
    pad = jnp.zeros((LANES - POWER_ROWS - 8, LANES), F32)
    pw_re = jnp.concatenate([pwt_re, f_rows, pad], axis=0).T
    pw_im = jnp.concatenate([pwt_im, f_rows, pad], axis=0).T
    f_re, f_im = pw_re[:, POWER_ROWS:POWER_ROWS + 1], pw_re[:, POWER_ROWS + 1:POWER_ROWS + 2]
    b_re, b_im = b_ref[gi, :, 0:SSM_GROUP], b_ref[gi, :, SSM_GROUP:2 * SSM_GROUP]
    bb_re = f_re * b_re - f_im * b_im
    bb_im = f_re * b_im + f_im * b_re
    t16 = t16_ref[...]
    bbt_re, bbt_im = pick(bb_re, t16), pick(bb_im, t16)
    ct_re = pick(c_ref[gi, :, 0:SSM_GROUP], t16)
    ct_im = pick(c_ref[gi, :, SSM_GROUP:2 * SSM_GROUP], t16)
    pw_at = lambda x: (pick(pw_re, e_ref[x]), pick(pw_im, e_ref[x]))
    id_re, id_im = pw_at(0)
    rev_re, rev_im = pw_at(1)
    p1_re, p1_im = pw_at(2)
    r16_re, r16_im = pw_at(3)
    f, b = slice(0, ns), slice(ns, 2 * ns)
    cmul = lambda ar, ai, br, bi: (ar * br - ai * bi, ar * bi + ai * br)

    sf_re, sf_im = cmul(rev_re[f], rev_im[f], bbt_re[f], bbt_im[f])
    sb_re, sb_im = cmul(id_re[b], id_im[b], bbt_re[b], bbt_im[b])
    ms_ref[gi] = jnp.concatenate([sf_re, sb_re, sf_im, sb_im], axis=0).T.astype(BF16)

    of_re, of_im = cmul(p1_re[f], p1_im[f], ct_re[f], ct_im[f])
    ob_re, ob_im = cmul(r16_re[b], r16_im[b], ct_re[b], ct_im[b])
    mo_ref[gi] = jnp.concatenate([of_re, ob_re, -of_im, -ob_im], axis=0).astype(BF16)

    xf_re, xf_im = cmul(id_re[f], id_im[f], ct_re[f], ct_im[f])
    xb_re, xb_im = cmul(rev_re[b], rev_im[b], ct_re[b], ct_im[b])
    btf = jnp.concatenate([bbt_re[f], bbt_im[f]], axis=0).T
    btb = jnp.concatenate([bbt_re[b], bbt_im[b]], axis=0).T
    g_f = dot_f32(btf, jnp.concatenate([xf_re, -xf_im], axis=0))
    g_b = dot_f32(btb, jnp.concatenate([xb_re, -xb_im], axis=0))
    for j in range(CHUNK):
        rows = slice(SSM_GROUP * j, SSM_GROUP * (j + 1))
        lo_col, hi_col = SSM_GROUP * j, SSM_GROUP * (j + 1)
        f_lo, f_hi = _rot256(g_f[rows, :LANES], g_f[rows, LANES:], lo_col, lane)
        b_lo, b_hi = _rot256(g_b[rows, :LANES], g_b[rows, LANES:], -SSM_GROUP * (CHUNK - 1 - j), lane)
        lo = jnp.where(lane >= lo_col, f_lo, 0.0) + jnp.where(lane < hi_col, b_lo, 0.0)
        up = jnp.where(lane + LANES >= lo_col, f_hi, 0.0) + jnp.where(lane + LANES < hi_col, b_hi, 0.0)
        mi_ref[gi, rows, 0:LANES] = lo.astype(BF16)
        mi_ref[gi, rows, LANES:2 * LANES] = up.astype(BF16)

    a16 = jnp.concatenate([pwt_re[CHUNK:CHUNK + 1, :], pwt_im[CHUNK:CHUNK + 1, :]], axis=1)
    a_ref[gi] = jnp.broadcast_to(a16, (8, 2 * LANES))


def _s5_matrices(lam_re, lam_im, log_dt, b_re, b_im, c_re, c_im):
    g, p2 = SSM_GROUPS, 2 * SSM_STATE
    both = lambda a: jnp.transpose(a, (1, 0, 2)).reshape(g, p2)
    log_dt_rows = jnp.repeat(jnp.transpose(log_dt), SSM_STATE, axis=1)
    par = jnp.stack([both(lam_re), both(lam_im), log_dt_rows], axis=1)
    par = jnp.concatenate([par, jnp.zeros((g, LANES - 3, p2), F32)], axis=1)
    rows_b = lambda a: jnp.transpose(a, (1, 0, 2, 3)).reshape(g, p2, SSM_GROUP)
    rows_c = lambda a: jnp.transpose(a, (1, 0, 3, 2)).reshape(g, p2, SSM_GROUP)
    b_cat = jnp.concatenate([rows_b(b_re), rows_b(b_im)], axis=-1)
    c_cat = jnp.concatenate([rows_c(c_re), rows_c(c_im)], axis=-1)

    col = np.arange(CHUNK_COLS)
    tile16 = (col[None, :] % SSM_GROUP == np.arange(SSM_GROUP)[:, None]).astype(np.float32)
    pos = col // SSM_GROUP
    k_idx = np.arange(LANES)[:, None]
    expand = np.stack([k_idx == pos[None, :], k_idx == (CHUNK - 1 - pos)[None, :],
                       k_idx == (pos + 1)[None, :], k_idx == (CHUNK - pos)[None, :]]).astype(np.float32)

    gp = S5_PREP_GROUPS
    mat = lambda: pl.BlockSpec((gp, CHUNK_COLS, CHUNK_COLS), lambda i: (i, 0, 0))
    mat_shape = jax.ShapeDtypeStruct((g, CHUNK_COLS, CHUNK_COLS), BF16)
    return pl.pallas_call(
        _s5_prep_kernel,
        grid=(g // gp,),
        in_specs=[pl.BlockSpec((gp, LANES, p2), lambda i: (i, 0, 0)),
                  pl.BlockSpec((gp, p2, 2 * SSM_GROUP), lambda i: (i, 0, 0)),
                  pl.BlockSpec((gp, p2, 2 * SSM_GROUP), lambda i: (i, 0, 0)),
                  pl.BlockSpec((SSM_GROUP, CHUNK_COLS), lambda i: (0, 0)),
                  pl.BlockSpec((4, LANES, CHUNK_COLS), lambda i: (0, 0, 0))],
        out_specs=[mat(), mat(), mat(), pl.BlockSpec((gp, 8, 2 * LANES), lambda i: (i, 0, 0))],
        out_shape=[mat_shape, mat_shape, mat_shape, jax.ShapeDtypeStruct((g, 8, 2 * LANES), F32)],
        name="s5_prep",
    )(par, b_cat, c_cat, jnp.asarray(tile16, BF16), jnp.asarray(expand, BF16))


def _s5_kernel(ul_ref, uc_ref, ms_ref, mi_ref, mo_ref, a_ref, y_ref, s_ref, sc_ref, hp_ref, *, gb):
    for gi in range(gb):
        for b in range(BATCH):
            sb = jnp.dot(ul_ref[gi, b], ms_ref[gi], preferred_element_type=F32)
            s_ref[gi, 0, pl.ds(b, N_CHUNKS, stride=BATCH), :] = sb[:, :LANES]
            s_ref[gi, 1, pl.ds(b, N_CHUNKS, stride=BATCH), :] = sb[:, LANES:]
            cb = jnp.dot(uc_ref[gi, b], ms_ref[gi], preferred_element_type=F32)
            sc_ref[gi, 0, pl.ds(b, N_CTX_CHUNKS, stride=BATCH), :] = cb[:, :LANES]
            sc_ref[gi, 1, pl.ds(b, N_CTX_CHUNKS, stride=BATCH), :] = cb[:, LANES:]
    lane = lax.broadcasted_iota(jnp.int32, (BATCH, LANES), 1)
    fwd = lane < SSM_STATE
    half = SSM_STATE

    def advance(gi, h_re, h_im, row_f, row_b, src):
        s_re = jnp.where(fwd, src[gi, 0, pl.ds(row_f, BATCH), :], src[gi, 0, pl.ds(row_b, BATCH), :])
        s_im = jnp.where(fwd, src[gi, 1, pl.ds(row_f, BATCH), :], src[gi, 1, pl.ds(row_b, BATCH), :])
        a_re = a_ref[gi, :, 0:LANES]
        a_im = a_ref[gi, :, LANES:2 * LANES]
        n_re = a_re * h_re - a_im * h_im + s_re
        n_im = a_re * h_im + a_im * h_re + s_im
        return n_re, n_im

    def ctx_step(t, carry):
        row_f = pl.multiple_of(t * BATCH, BATCH)
        row_b = pl.multiple_of((N_CTX_CHUNKS - 1 - t) * BATCH, BATCH)
        return tuple(advance(gi, carry[gi][0], carry[gi][1], row_f, row_b, sc_ref) for gi in range(gb))

    def lat_step(t, carry):
        row_f = pl.multiple_of(t * BATCH, BATCH)
        row_b = pl.multiple_of((N_CHUNKS - 1 - t) * BATCH, BATCH)
        out = []
        for gi in range(gb):
            h_re, h_im = carry[gi]
            hp_ref[gi, 0, pl.ds(row_f, BATCH), 0:half] = h_re[:, 0:half]
            hp_ref[gi, 0, pl.ds(row_b, BATCH), half:2 * half] = h_re[:, half:]
            hp_ref[gi, 1, pl.ds(row_f, BATCH), 0:half] = h_im[:, 0:half]
            hp_ref[gi, 1, pl.ds(row_b, BATCH), half:2 * half] = h_im[:, half:]
            out.append(advance(gi, h_re, h_im, row_f, row_b, s_ref))
        return tuple(out)

    zero = jnp.zeros((BATCH, LANES), F32)
    carry = tuple((zero, zero) for _ in range(gb))
    carry = lax.fori_loop(0, N_CTX_CHUNKS, ctx_step, carry)
    lax.fori_loop(0, N_CHUNKS, lat_step, carry)
    for gi in range(gb):
        for b in range(BATCH):
            hb_re = hp_ref[gi, 0, pl.ds(b, N_CHUNKS, stride=BATCH), :].astype(BF16)
            hb_im = hp_ref[gi, 1, pl.ds(b, N_CHUNKS, stride=BATCH), :].astype(BF16)
            y = (jnp.dot(ul_ref[gi, b], mi_ref[gi], preferred_element_type=F32)
                 + jnp.dot(jnp.concatenate([hb_re, hb_im], axis=1), mo_ref[gi], preferred_element_type=F32))
            y_ref[gi, b] = y.astype(BF16)


def _s5_scan(u_lat_t, u_ctx_t, m_intra, m_state, m_out, a16):
    gb = 4
    rows = N_CHUNKS * BATCH
    crows = N_CTX_CHUNKS * BATCH
    grp = lambda r, c: pl.BlockSpec((gb, r, c), lambda g: (g, 0, 0))
    tok = lambda n: pl.BlockSpec((gb, BATCH, n, CHUNK_COLS), lambda g: (g, 0, 0, 0))
    return pl.pallas_call(
        functools.partial(_s5_kernel, gb=gb),
        grid=(SSM_GROUPS // gb,),
        in_specs=[tok(N_CHUNKS), tok(N_CTX_CHUNKS), grp(CHUNK_COLS, CHUNK_COLS),
                  grp(CHUNK_COLS, CHUNK_COLS), grp(CHUNK_COLS, CHUNK_COLS), grp(8, 2 * LANES)],
        out_specs=tok(N_CHUNKS),
        out_shape=jax.ShapeDtypeStruct((SSM_GROUPS, BATCH, N_CHUNKS, CHUNK_COLS), BF16),
        scratch_shapes=[pltpu.VMEM((gb, 2, rows, LANES), F32),
                        pltpu.VMEM((gb, 2, crows, LANES), F32),
                        pltpu.VMEM((gb, 2, rows, LANES), F32)],
        compiler_params=pltpu.CompilerParams(
            dimension_semantics=("arbitrary",), vmem_limit_bytes=VMEM_LIMIT),
        name="s5_scan",
    )(u_lat_t, u_ctx_t, m_state, m_intra, m_out, a16)


FFN_TILE = 256


def _post_kernel(x_ref, a_ref, yt_ref, u_ref, ga_ref, gs_ref, mod_ref, d_ref, fg_ref, og_ref, permt_ref,
                 wglu_ref, wba_ref, wbs_ref, wout_ref, wfi_ref, wfo_ref, o_ref, h1_ref, n2_ref, act_ref, r_ref):
    dm = D_MODEL
    g1 = mod_ref[0, :, 2 * dm:3 * dm]
    sh2 = mod_ref[0, :, 3 * dm:4 * dm]
    sc2 = mod_ref[0, :, 4 * dm:5 * dm]
    g2 = mod_ref[0, :, 5 * dm:6 * dm]
    half = CHUNK * CHUNK
    halves = [slice(h * half, (h + 1) * half) for h in range(x_ref.shape[1] // half)]
    dot = functools.partial(jnp.dot, preferred_element_type=F32)

    sp = []
    for h, rows in enumerate(halves):
        y = _load_chunk_layout(yt_ref, permt_ref, r_ref, h)
        sp.append(jax.nn.gelu(y + d_ref[...] * u_ref[0, rows, :].astype(F32)).astype(BF16))
    s = []
    for h, rows in enumerate(halves):
        vg = dot(sp[h], wglu_ref[...])
        s.append((vg[:, :SSM_WIDTH] * jax.nn.sigmoid(vg[:, SSM_WIDTH:])).astype(BF16))
    merged = []
    for h, rows in enumerate(halves):
        a = jnp.concatenate([a_ref[0, j, rows, :] for j in range(ATTN_WIDTH // LANES)], axis=1)
        m = (ga_ref[0, rows, :].astype(F32) * dot(a, wba_ref[...])
             + gs_ref[0, rows, :].astype(F32) * dot(s[h], wbs_ref[...]))
        merged.append(m.astype(BF16))
    for h, rows in enumerate(halves):
        h1 = x_ref[0, rows, :] + g1 * dot(merged[h], wout_ref[...])
        h1_ref[rows, :] = h1
        n2_ref[rows, :] = _rms_modulate(h1, fg_ref[...], sh2, sc2).astype(BF16)
    for lo in range(0, FFN_HIDDEN, FFN_TILE):
        width = min(FFN_TILE, FFN_HIDDEN - lo)
        for rows in halves:
            n2 = n2_ref[rows, :]
            fa = dot(n2, wfi_ref[:, lo:lo + width])
            fb = dot(n2, wfi_ref[:, FFN_HIDDEN + lo:FFN_HIDDEN + lo + width])
            act_ref[rows, lo:lo + width] = (fa * jax.nn.sigmoid(fa) * fb).astype(BF16)
    for rows in halves:
        h2 = h1_ref[rows, :] + g2 * dot(act_ref[rows, :], wfo_ref[...])
        o_ref[0, rows, :] = (h2 * lax.rsqrt(jnp.mean(h2 * h2, axis=-1, keepdims=True) + NORM_EPS)) * og_ref[...]


def _post(x, a, y_t, u, ga, gs, mod3, d_skip, ffn_g, fin_g, perm_t, wglu, wba, wbs, wout, wfi, wfo):
    tm = 512
    n = CHUNK * CHUNK
    tok = lambda width: pl.BlockSpec((1, tm, width), lambda b, i: (b, i, 0))
    const = lambda r, c: pl.BlockSpec((r, c), lambda b, i: (0, 0), pipeline_mode=pl.Buffered(1))
    return pl.pallas_call(
        _post_kernel,
        grid=(BATCH, SEQ // tm),
        in_specs=[tok(D_MODEL),
                  pl.BlockSpec((1, N_HEADS // 2, tm, LANES), lambda b, i: (b, 0, i, 0)),
                  pl.BlockSpec((SSM_GROUPS, 1, tm // CHUNK, CHUNK_COLS), lambda b, i: (0, b, i, 0)),
                  tok(SSM_WIDTH), tok(D_MODEL), tok(D_MODEL),
                  pl.BlockSpec((1, 1, N_MOD * D_MODEL), lambda b, i: (b, 0, 0)),
                  const(1, SSM_WIDTH), const(1, D_MODEL), const(1, D_MODEL), const(n, n),
                  const(SSM_WIDTH, 2 * SSM_WIDTH), const(ATTN_WIDTH, D_MODEL),
                  const(SSM_WIDTH, D_MODEL), const(D_MODEL, D_MODEL),
                  const(D_MODEL, 2 * FFN_HIDDEN), const(FFN_HIDDEN, D_MODEL)],
        out_specs=tok(D_MODEL),
        out_shape=jax.ShapeDtypeStruct((BATCH, SEQ, D_MODEL), F32),
        scratch_shapes=[pltpu.VMEM((tm, D_MODEL), F32), pltpu.VMEM((tm, D_MODEL), BF16),
                        pltpu.VMEM((tm, FFN_HIDDEN), BF16), pltpu.VMEM((tm // n, n, SSM_WIDTH), F32)],
        compiler_params=pltpu.CompilerParams(
            dimension_semantics=("arbitrary", "arbitrary"), vmem_limit_bytes=VMEM_LIMIT),
        name="post",
    )(x, a, y_t, u, ga, gs, mod3, d_skip, ffn_g, fin_g, perm_t, wglu, wba, wbs, wout, wfi, wfo)


def kernel(x, c, ctx, c_ctx, w_mod, b_mod, attn_norm_g, ffn_norm_g, w_in, rel_pos_bias,
           ssm_lambda_re, ssm_lambda_im, ssm_log_dt, ssm_b_re, ssm_b_im, ssm_c_re, ssm_c_im, ssm_d,
           w_glu, w_branch_attn, w_branch_ssm, w_out, w_ffn_in, w_ffn_out, final_norm_g):
    assert x.shape == (BATCH, SEQ, D_MODEL) and w_mod.shape[0] == 1
    c_rows = jnp.concatenate(
        [c, c_ctx[None, :], jnp.zeros((MOD_ROWS - BATCH - 1, D_MODEL), F32)], axis=0)
    mod3 = _modulation(c_rows, w_mod[0], b_mod[0]).reshape(MOD_ROWS, 1, N_MOD * D_MODEL)

    col_scale = jnp.concatenate([jnp.full((ATTN_WIDTH,), HEAD_DIM ** -0.5 * LOG2E, F32),
                                 jnp.ones((IN_COLS - ATTN_WIDTH,), F32)])
    w_in_bf16 = (w_in[0] * col_scale[None, :]).astype(BF16)
    norm_g = attn_norm_g[0].reshape(1, D_MODEL)

    perm = _chunk_perm()
    q, k, v, u, u_t, ga, gs = _input_projection(x, mod3, norm_g, w_in_bf16, _rope_tables(), perm)
    kc, vc, uc_t = _context_projection(ctx, mod3, norm_g, w_in_bf16, perm)

    attn = _attention(q, k, v, kc, vc, _bias_tables(rel_pos_bias[0]))

    m_intra, m_state, m_out, a16 = _s5_matrices(
        ssm_lambda_re[0], ssm_lambda_im[0], ssm_log_dt[0], ssm_b_re[0], ssm_b_im[0],
        ssm_c_re[0], ssm_c_im[0])
    y_t = _s5_scan(u_t, uc_t, m_intra, m_state, m_out, a16)

    return _post(x, attn, y_t, u, ga, gs, mod3,
                 ssm_d[0].reshape(1, SSM_WIDTH), ffn_norm_g[0].reshape(1, D_MODEL),
                 final_norm_g.reshape(1, D_MODEL), perm,
                 w_glu[0].astype(BF16), w_branch_attn[0].astype(BF16), w_branch_ssm[0].astype(BF16),
                 w_out[0].astype(BF16), w_ffn_in[0].astype(BF16), w_ffn_out[0].astype(BF16))
```

```python
import functools
import math

import numpy as np
import jax
import jax.numpy as jnp
from jax import lax
from jax.experimental import pallas as pl
from jax.experimental.pallas import tpu as pltpu

F32 = jnp.float32
BF16 = jnp.bfloat16

D_MODEL = 1024
BATCH = 8
SEQ = 4096
GRID_W = 64
GRID_ROWS = SEQ // GRID_W
CTX_LEN = 256
N_HEADS = 8
HEAD_DIM = 64
ATTN_WIDTH = N_HEADS * HEAD_DIM
WIN_H = 8
WIN_W = 16
ROPE_BASE = 10000.0
SSM_WIDTH = 512
SSM_GROUP = 16
SSM_GROUPS = SSM_WIDTH // SSM_GROUP
SSM_STATE = 64
FFN_HIDDEN = 2816
IN_COLS = 3 * ATTN_WIDTH + SSM_WIDTH + 2 * D_MODEL
N_MOD = 6
NORM_EPS = 1e-6
NEG_BIG = -1e30
LOG2E = math.log2(math.e)

LANES = 128
CHUNK = 16
N_CHUNKS = SEQ // CHUNK
N_CTX_CHUNKS = CTX_LEN // CHUNK
CHUNK_COLS = CHUNK * SSM_GROUP
MOD_ROWS = 16
CTX_MOD_ROW = BATCH
VMEM_LIMIT = 56 * 1024 * 1024

Q_ROWS_PER_STEP = 8
Q_BLOCK = Q_ROWS_PER_STEP * GRID_W
K_ROWS_PER_STEP = Q_ROWS_PER_STEP + WIN_H
K_BLOCK = K_ROWS_PER_STEP * GRID_W
N_QBLOCKS = GRID_ROWS // Q_ROWS_PER_STEP
KEY_GROUP = (WIN_H // 2) * GRID_W
SOFTMAX_ROWS = 32

def _rms_modulate(x, g, shift, scale):
    xn = x * lax.rsqrt(jnp.mean(x * x, axis=-1, keepdims=True) + NORM_EPS)
    return (xn * g) * (1.0 + scale) + shift


def _block_transpose8(vs, lane):
    for shift in (64, 32, 16):
        keep = (lane & (2 * shift - 1)) < shift
        dist = shift // SSM_GROUP
        out = list(vs)
        for a in range(8):
            if a & dist:
                continue
            b = a + dist
            out[a] = jnp.where(keep, vs[a], pltpu.roll(vs[b], shift, 1))
            out[b] = jnp.where(keep, pltpu.roll(vs[a], LANES - shift, 1), vs[b])
        vs = out
    return vs


def _chunk_perm():
    n = CHUNK * CHUNK
    r = np.arange(n)
    m = np.zeros((n, n), np.float32)
    m[r, (r % CHUNK) * CHUNK + r // CHUNK] = 1.0
    return jnp.asarray(m, BF16)


def _store_chunk_layout(u16, perm_ref, out_ref, n_groups16, group0=0):
    lane = lax.broadcasted_iota(jnp.int32, (1, LANES), 1)
    n = CHUNK * CHUNK
    for hf in range(n_groups16):
        r = jnp.dot(perm_ref[...], u16[hf * n:(hf + 1) * n, :], preferred_element_type=F32)
        c0 = (group0 + hf) * CHUNK
        for v in range(SSM_WIDTH // LANES):
            for jh in range(2):
                vs = [r[CHUNK * (8 * jh + jp):CHUNK * (8 * jh + jp + 1), v * LANES:(v + 1) * LANES]
                      for jp in range(8)]
                outs = _block_transpose8(vs, lane)
                for gi in range(8):
                    out_ref[8 * v + gi, 0, c0:c0 + CHUNK, jh * LANES:(jh + 1) * LANES] = outs[gi].astype(BF16)


def _load_chunk_layout(yt_ref, perm_t_ref, r_ref, hf):
    lane = lax.broadcasted_iota(jnp.int32, (1, LANES), 1)
    for v in range(SSM_WIDTH // LANES):
        for jh in range(2):
            vs = [yt_ref[8 * v + gi, 0, hf * CHUNK:(hf + 1) * CHUNK,
                         jh * LANES:(jh + 1) * LANES].astype(F32) for gi in range(8)]
            outs = _block_transpose8(vs, lane)
            for jp in range(8):
                r_ref[hf, CHUNK * (8 * jh + jp):CHUNK * (8 * jh + jp + 1), v * LANES:(v + 1) * LANES] = outs[jp]
    return jnp.dot(perm_t_ref[...], r_ref[hf].astype(BF16), preferred_element_type=F32)


def _mod_kernel(c_ref, w_ref, b_ref, o_ref):
    c = c_ref[...]
    s = c * jax.nn.sigmoid(c)
    o_ref[...] = jnp.dot(s, w_ref[...], preferred_element_type=F32) + b_ref[...]


def _modulation(c_rows, w_mod, b_mod):
    n = N_MOD * D_MODEL
    tn = 1536
    return pl.pallas_call(
        _mod_kernel,
        grid=(n // tn,),
        in_specs=[pl.BlockSpec((MOD_ROWS, D_MODEL), lambda j: (0, 0)),
                  pl.BlockSpec((D_MODEL, tn), lambda j: (0, j)),
                  pl.BlockSpec((1, tn), lambda j: (0, j))],
        out_specs=pl.BlockSpec((MOD_ROWS, tn), lambda j: (0, j)),
        out_shape=jax.ShapeDtypeStruct((MOD_ROWS, n), F32),
        name="modulation",
    )(c_rows, w_mod, b_mod.reshape(1, n))


def _rope_tables():
    n_freq = HEAD_DIM // 4
    inv_freq = ROPE_BASE ** (-np.arange(n_freq, dtype=np.float64) / n_freq)
    t = np.arange(SEQ)
    lane = np.arange(LANES)
    d = lane % HEAD_DIM
    use_col = (d // (HEAD_DIM // 2)) == 1
    w = d % (HEAD_DIM // 2)
    first = w < n_freq
    pos = np.where(use_col[None, :], (t % GRID_W)[:, None], (t // GRID_W)[:, None]).astype(np.float64)
    ang = pos * inv_freq[w % n_freq][None, :]
    cos = np.cos(ang)
    sin = np.sin(ang)
    sin_a = np.where(first[None, :], -sin, 0.0)
    sin_b = np.where(first[None, :], 0.0, sin)
    return (jnp.asarray(cos, F32), jnp.asarray(sin_a, F32), jnp.asarray(sin_b, F32))


def _rope_store(r, cos, sin_a, sin_b, out_ref, row0, transposed=False):
    n_rows = r.shape[0]
    for j in range(ATTN_WIDTH // LANES):
        xs = r[:, j * LANES:(j + 1) * LANES]
        rot = (xs * cos + pltpu.roll(xs, LANES - HEAD_DIM // 4, 1) * sin_a
               + pltpu.roll(xs, HEAD_DIM // 4, 1) * sin_b)
        if transposed:
            rot_t = rot.T.astype(BF16)
            for gi in range(n_rows // KEY_GROUP):
                out_ref[0, row0 // KEY_GROUP + gi, j * LANES:(j + 1) * LANES, :] = (
                    rot_t[:, gi * KEY_GROUP:(gi + 1) * KEY_GROUP])
        else:
            out_ref[0, j, row0:row0 + n_rows, :] = rot.astype(BF16)


INPROJ_SUBTILE = 512


def _inproj_kernel(x_ref, mod_ref, g_ref, w_ref, cos_ref, sa_ref, sb_ref, perm_ref,
                   q_ref, k_ref, v_ref, u_ref, ut_ref, ga_ref, gs_ref):
    shift = mod_ref[0, :, 0:D_MODEL]
    scale = mod_ref[0, :, D_MODEL:2 * D_MODEL]
    aw = ATTN_WIDTH
    dot = functools.partial(jnp.dot, preferred_element_type=F32)
    for row0 in range(0, x_ref.shape[1], INPROJ_SUBTILE):
        rows = slice(row0, row0 + INPROJ_SUBTILE)
        nb = _rms_modulate(x_ref[0, rows, :], g_ref[...], shift, scale).astype(BF16)
        cos = cos_ref[rows, :]
        sin_a = sa_ref[rows, :]
        sin_b = sb_ref[rows, :]
        _rope_store(dot(nb, w_ref[:, 0:aw]), cos, sin_a, sin_b, q_ref, row0)
        _rope_store(dot(nb, w_ref[:, aw:2 * aw]), cos, sin_a, sin_b, k_ref, row0, transposed=True)
        v = dot(nb, w_ref[:, 2 * aw:3 * aw]).astype(BF16)
        for j in range(aw // LANES):
            v_ref[0, j, rows, :] = v[:, j * LANES:(j + 1) * LANES]
        c0 = 3 * aw
        u16 = dot(nb, w_ref[:, c0:c0 + SSM_WIDTH]).astype(BF16)
        u_ref[0, rows, :] = u16
        n16 = CHUNK * CHUNK
        _store_chunk_layout(u16, perm_ref, ut_ref, INPROJ_SUBTILE // n16, row0 // n16)
        c1 = c0 + SSM_WIDTH
        ga_ref[0, rows, :] = jax.nn.sigmoid(dot(nb, w_ref[:, c1:c1 + D_MODEL])).astype(BF16)
        c2 = c1 + D_MODEL
        gs_ref[0, rows, :] = jax.nn.sigmoid(dot(nb, w_ref[:, c2:c2 + D_MODEL])).astype(BF16)


def _input_projection(x, mod3, norm_g, w_in_bf16, rope, perm):
    tm = 2 * INPROJ_SUBTILE
    n = CHUNK * CHUNK
    cos, sin_a, sin_b = rope
    tok = lambda width: pl.BlockSpec((1, tm, width), lambda i, b: (b, i, 0))
    tab = pl.BlockSpec((tm, LANES), lambda i, b: (i, 0))
    out = lambda width: jax.ShapeDtypeStruct((BATCH, SEQ, width), BF16)
    pairs = pl.BlockSpec((1, N_HEADS // 2, tm, LANES), lambda i, b: (b, 0, i, 0))
    pairs_shape = jax.ShapeDtypeStruct((BATCH, N_HEADS // 2, SEQ, LANES), BF16)
    return pl.pallas_call(
        _inproj_kernel,
        grid=(SEQ // tm, BATCH),
        in_specs=[tok(D_MODEL),
                  pl.BlockSpec((1, 1, N_MOD * D_MODEL), lambda i, b: (b, 0, 0)),
                  pl.BlockSpec((1, D_MODEL), lambda i, b: (0, 0)),
                  pl.BlockSpec((D_MODEL, IN_COLS), lambda i, b: (0, 0), pipeline_mode=pl.Buffered(1)),
                  tab, tab, tab,
                  pl.BlockSpec((n, n), lambda i, b: (0, 0), pipeline_mode=pl.Buffered(1))],
        out_specs=[pairs,
                   pl.BlockSpec((1, tm // KEY_GROUP, ATTN_WIDTH, KEY_GROUP), lambda i, b: (b, i, 0, 0)),
                   pairs, tok(SSM_WIDTH),
                   pl.BlockSpec((SSM_GROUPS, 1, tm // CHUNK, CHUNK_COLS), lambda i, b: (0, b, i, 0)),
                   tok(D_MODEL), tok(D_MODEL)],
        out_shape=[pairs_shape,
                   jax.ShapeDtypeStruct((BATCH, SEQ // KEY_GROUP, ATTN_WIDTH, KEY_GROUP), BF16),
                   pairs_shape, out(SSM_WIDTH),
                   jax.ShapeDtypeStruct((SSM_GROUPS, BATCH, N_CHUNKS, CHUNK_COLS), BF16),
                   out(D_MODEL), out(D_MODEL)],
        compiler_params=pltpu.CompilerParams(
            dimension_semantics=("arbitrary", "arbitrary"), vmem_limit_bytes=VMEM_LIMIT),
        name="input_projection",
    )(x, mod3, norm_g, w_in_bf16, cos, sin_a, sin_b, perm)


def _ctx_proj_kernel(x_ref, mod_ref, g_ref, wk_ref, wv_ref, wu_ref, perm_ref, k_ref, v_ref, ut_ref):
    x = x_ref[0]
    shift = mod_ref[0, :, 0:D_MODEL]
    scale = mod_ref[0, :, D_MODEL:2 * D_MODEL]
    nb = _rms_modulate(x, g_ref[...], shift, scale).astype(BF16)
    k_ref[0] = jnp.dot(nb, wk_ref[...], preferred_element_type=F32).T.astype(BF16)
    v_ref[0] = jnp.dot(nb, wv_ref[...], preferred_element_type=F32).astype(BF16)
    u16 = jnp.dot(nb, wu_ref[...], preferred_element_type=F32).astype(BF16)
    _store_chunk_layout(u16, perm_ref, ut_ref, 1)


def _context_projection(ctx, mod3, norm_g, w_in_bf16, perm):
    n = CHUNK * CHUNK
    aw = ATTN_WIDTH
    tok = lambda width: pl.BlockSpec((1, CTX_LEN, width), lambda b: (b, 0, 0))
    wcol = lambda j: pl.BlockSpec((D_MODEL, aw), lambda b: (0, j))
    out = jax.ShapeDtypeStruct((BATCH, CTX_LEN, aw), BF16)
    return pl.pallas_call(
        _ctx_proj_kernel,
        grid=(BATCH,),
        in_specs=[tok(D_MODEL),
                  pl.BlockSpec((1, 1, N_MOD * D_MODEL), lambda b: (CTX_MOD_ROW, 0, 0)),
                  pl.BlockSpec((1, D_MODEL), lambda b: (0, 0)),
                  wcol(1), wcol(2), wcol(3),
                  pl.BlockSpec((n, n), lambda b: (0, 0))],
        out_specs=[pl.BlockSpec((1, aw, CTX_LEN), lambda b: (b, 0, 0)), tok(aw),
                   pl.BlockSpec((SSM_GROUPS, 1, N_CTX_CHUNKS, CHUNK_COLS), lambda b: (0, b, 0, 0))],
        out_shape=[jax.ShapeDtypeStruct((BATCH, aw, CTX_LEN), BF16), out,
                   jax.ShapeDtypeStruct((SSM_GROUPS, BATCH, N_CTX_CHUNKS, CHUNK_COLS), BF16)],
        name="context_projection",
    )(ctx, mod3, norm_g, w_in_bf16, w_in_bf16, w_in_bf16, perm)


def _window_start_rows(r):
    return min(max(r - WIN_H // 2, 0), GRID_ROWS - WIN_H)


def _key_block_row(jb):
    return min(max(Q_ROWS_PER_STEP * jb - WIN_H // 2, 0), GRID_ROWS - K_ROWS_PER_STEP)


def _window_geometry(jb):
    key_row0 = _key_block_row(jb)
    offs, deltas = [], []
    for i in range(Q_ROWS_PER_STEP):
        r = Q_ROWS_PER_STEP * jb + i
        rs = _window_start_rows(r)
        offs.append(rs - key_row0)
        deltas.append(r - rs)
    return offs, deltas


ATTN_SAMPLES_PER_STEP = 2


def _attn_kernel(*refs):
    def per_sample(bi, carry):
        _attn_sample(bi, *refs)
        return carry

    lax.fori_loop(0, ATTN_SAMPLES_PER_STEP, per_sample, 0)


def _attn_sample(bi, q_ref, k_ref, v_ref, kc_ref, vc_ref, bias_ref, o_ref,
                 s_ref, sc_ref, p_ref, pc_ref, l_ref, m_ref, acc_ref):
    lane = lax.broadcasted_iota(jnp.int32, (1, LANES), 1)
    left = lane < HEAD_DIM
    n_ktiles = K_BLOCK // LANES
    last = N_QBLOCKS - 1
    max_key_start = (GRID_ROWS - K_ROWS_PER_STEP) * GRID_W

    def scores(sl, e, q_start, key_start):
        q2 = q_ref[bi, 0, pl.ds(q_start, Q_BLOCK), :]
        qm = jnp.where(left if e == 0 else jnp.logical_not(left), q2, jnp.zeros_like(q2))
        group0 = key_start // KEY_GROUP
        for gi in range(K_BLOCK // KEY_GROUP):
            s_ref[sl, :, gi * KEY_GROUP:(gi + 1) * KEY_GROUP] = jnp.dot(
                qm, k_ref[bi, group0 + gi], preferred_element_type=F32)
        sc_ref[sl] = jnp.dot(qm, kc_ref[bi], preferred_element_type=F32)

    def softmax(sl, e, offs, deltas):
        def geometry(sub):
            i = sub * SOFTMAX_ROWS // GRID_W
            qcols = slice(sub * SOFTMAX_ROWS % GRID_W, sub * SOFTMAX_ROWS % GRID_W + SOFTMAX_ROWS)
            rows = slice(sub * SOFTMAX_ROWS, (sub + 1) * SOFTMAX_ROWS)
            par = offs[i] % 2
            return rows, qcols, par, offs[i] // 2, WIN_H // 2 + par, deltas[i]

        def tile(rows, qcols, par, t0, n_tiles, delta, xt):
            t = t0 + xt
            dr0 = 2 * xt - par - delta
            st = s_ref[sl, rows, t * LANES:(t + 1) * LANES] + bias_ref[0, e, dr0 + WIN_H, qcols, :]
            if par and xt == 0:
                st = jnp.where(left, NEG_BIG, st)
            if par and xt == n_tiles - 1:
                st = jnp.where(left, st, NEG_BIG)
            return st

        n_sub = Q_BLOCK // SOFTMAX_ROWS
        for sub in range(n_sub):
            geo = geometry(sub)
            rows, n_tiles = geo[0], geo[4]
            mt = jnp.maximum(sc_ref[sl, rows, 0:LANES], sc_ref[sl, rows, LANES:2 * LANES])
            for xt in range(n_tiles):
                mt = jnp.maximum(mt, tile(*geo, xt))
            m_ref[sl, rows, :] = jnp.broadcast_to(jnp.max(mt, axis=1, keepdims=True), (SOFTMAX_ROWS, LANES))
        for sub in range(n_sub):
            geo = geometry(sub)
            rows, t0, n_tiles = geo[0], geo[3], geo[4]
            m = m_ref[sl, rows, :]
            lt = None
            for xt in range(n_tiles):
                t = t0 + xt
                pt = jnp.exp2(tile(*geo, xt) - m)
                lt = pt if lt is None else lt + pt
                p_ref[sl, rows, t * LANES:(t + 1) * LANES] = pt.astype(BF16)
            for t in range(n_ktiles):
                if not (t0 <= t < t0 + n_tiles):
                    p_ref[sl, rows, t * LANES:(t + 1) * LANES] = jnp.zeros((SOFTMAX_ROWS, LANES), BF16)
            for ci in range(2):
                pt = jnp.exp2(sc_ref[sl, rows, ci * LANES:(ci + 1) * LANES] - m)
                lt = lt + pt
                pc_ref[sl, rows, ci * LANES:(ci + 1) * LANES] = pt.astype(BF16)
            l = jnp.sum(lt, axis=1, keepdims=True)
            l_ref[sl, rows, :] = jnp.broadcast_to(1.0 / l, (SOFTMAX_ROWS, LANES))

    def values(sl, key_start):
        vblk = v_ref[bi, 0, pl.ds(key_start, K_BLOCK), :]
        return (jnp.dot(p_ref[sl], vblk, preferred_element_type=F32)
                + jnp.dot(pc_ref[sl], vc_ref[bi], preferred_element_type=F32)) * l_ref[sl]

    def block(jb_static, cur, nxt):
        offs, deltas = _window_geometry(jb_static)
        scores(1, 1, *cur)
        softmax(0, 0, offs, deltas)
        acc_ref[0] = values(0, cur[1])
        if nxt is not None:
            scores(0, 0, *nxt)
        softmax(1, 1, offs, deltas)
        o1 = values(1, cur[1])
        o_ref[bi, 0, pl.ds(cur[0], Q_BLOCK), :] = jnp.where(left, acc_ref[0], o1).astype(BF16)

    def starts(jb):
        q_start = pl.multiple_of(jb * Q_BLOCK, Q_BLOCK)
        key_start = jnp.clip((jb * Q_ROWS_PER_STEP - WIN_H // 2) * GRID_W, 0, max_key_start)
        return q_start, pl.multiple_of(key_start, KEY_GROUP)

    scores(0, 0, 0, 0)
    block(0, (0, 0), (Q_BLOCK, max(Q_ROWS_PER_STEP - WIN_H // 2, 0) * GRID_W))

    blocks_per_trip = 3

    def interior(trip, carry):
        jb = 1 + blocks_per_trip * trip
        for d in range(blocks_per_trip):
            block(1, starts(jb + d), starts(jb + d + 1))
        return carry

    assert (last - 1) % blocks_per_trip == 0
    lax.fori_loop(0, (last - 1) // blocks_per_trip, interior, 0)
    block(last, (last * Q_BLOCK, max_key_start), None)


def _bias_tables(rpb):
    qcol = np.arange(GRID_W)
    kcol = np.arange(GRID_W)
    col_start = np.clip(qcol - WIN_W // 2, 0, GRID_W - WIN_W)
    in_win = (kcol[None, :] >= col_start[:, None]) & (kcol[None, :] < col_start[:, None] + WIN_W)
    dc_idx = np.clip(kcol[None, :] - qcol[:, None], -(WIN_W - 1), WIN_W - 1) + WIN_W - 1
    sel = (np.arange(2 * WIN_W - 1)[:, None, None] == dc_idx[None]).astype(np.float32)
    toe = jnp.einsum('hdt,tck->hdck', rpb * LOG2E, jnp.asarray(sel),
                     precision=lax.Precision.HIGHEST)
    toe = jnp.where(in_win[None, None], toe, NEG_BIG)
    neg = jnp.full((N_HEADS, 1, GRID_W, GRID_W), NEG_BIG, F32)
    ext = jnp.concatenate([neg, toe, neg], axis=1)
    pair = jnp.concatenate([ext[:, 0:16], ext[:, 1:17]], axis=-1)
    return pair.reshape(N_HEADS // 2, 2, 16, GRID_W, LANES)


def _attention(q, k, v, kc, vc, bias):
    n_slots = 2
    nb = ATTN_SAMPLES_PER_STEP
    kspec = pl.BlockSpec((nb, 1, SEQ, LANES), lambda hp, b: (b, hp, 0, 0))
    ktspec = pl.BlockSpec((nb, SEQ // KEY_GROUP, LANES, KEY_GROUP), lambda hp, b: (b, 0, hp, 0))
    cspec = pl.BlockSpec((nb, CTX_LEN, LANES), lambda hp, b: (b, 0, hp))
    ctspec = pl.BlockSpec((nb, LANES, CTX_LEN), lambda hp, b: (b, hp, 0))
    bspec = pl.BlockSpec((1, 2, 16, GRID_W, LANES), lambda hp, b: (hp, 0, 0, 0, 0))
    return pl.pallas_call(
        _attn_kernel,
        grid=(N_HEADS // 2, BATCH // nb),
        in_specs=[kspec, ktspec, kspec, ctspec, cspec, bspec],
        out_specs=kspec,
        out_shape=jax.ShapeDtypeStruct((BATCH, N_HEADS // 2, SEQ, LANES), BF16),
        scratch_shapes=[pltpu.VMEM((n_slots, Q_BLOCK, K_BLOCK), F32),
                        pltpu.VMEM((n_slots, Q_BLOCK, CTX_LEN), F32),
                        pltpu.VMEM((n_slots, Q_BLOCK, K_BLOCK), BF16),
                        pltpu.VMEM((n_slots, Q_BLOCK, CTX_LEN), BF16),
                        pltpu.VMEM((n_slots, Q_BLOCK, LANES), F32),
                        pltpu.VMEM((n_slots, Q_BLOCK, LANES), F32),
                        pltpu.VMEM((1, Q_BLOCK, LANES), F32)],
        compiler_params=pltpu.CompilerParams(
            dimension_semantics=("arbitrary", "arbitrary"),
            vmem_limit_bytes=VMEM_LIMIT),
        name="attention",
    )(q, k, v, kc, vc, bias)


def _rot256(a, b, s, lane):
    s %= 2 * LANES
    if s >= LANES:
        a, b, s = b, a, s - LANES
    if s == 0:
        return a, b
    ra = pltpu.roll(a, s, 1)
    rb = pltpu.roll(b, s, 1)
    keep = lane >= s
    return jnp.where(keep, ra, rb), jnp.where(keep, rb, ra)


S5_PREP_GROUPS = 2
POWER_ROWS = 24


def _s5_prep_kernel(*refs):
    for gi in range(S5_PREP_GROUPS):
        _s5_prep_group(gi, *refs)


def _s5_prep_group(gi, par_ref, b_ref, c_ref, t16_ref, e_ref, mi_ref, ms_ref, mo_ref, a_ref):
    dot = functools.partial(jnp.dot, preferred_element_type=F32)

    def split2(x):
        hi = x.astype(BF16)
        return hi, (x - hi.astype(F32)).astype(BF16)

    def split3(x):
        hi = x.astype(BF16)
        r1 = x - hi.astype(F32)
        mid = r1.astype(BF16)
        return hi, mid, (r1 - mid.astype(F32)).astype(BF16)

    def pick(x, onehot):
        hi, mid, lo = split3(x)
        return dot(hi, onehot) + dot(mid, onehot) + dot(lo, onehot)

    def dot_f32(a, b):
        ah, al = split2(a)
        bh, bl = split2(b)
        return dot(jnp.concatenate([ah, ah, al], axis=1), jnp.concatenate([bh, bl, bh], axis=0))

    ns = SSM_STATE
    lam_re, lam_im = par_ref[gi, 0:1, :], par_ref[gi, 1:2, :]
    dt = jnp.exp(par_ref[gi, 2:3, :])
    lane = lax.broadcasted_iota(jnp.int32, (1, LANES), 1)
    kf = lax.broadcasted_iota(jnp.int32, (POWER_ROWS, LANES), 0).astype(F32)
    mag = jnp.exp((lam_re * dt) * kf)
    ang = (lam_im * dt) * kf
    pwt_re, pwt_im = mag * jnp.cos(ang), mag * jnp.sin(ang)
    den = lam_re * lam_re + lam_im * lam_im
    nr, ni = pwt_re[1:2, :] - 1.0, pwt_im[1:2, :]
    f_re_row = (nr * lam_re + ni * lam_im) / den
    f_im_row = (ni * lam_re - nr * lam_im) / den
    r8 = lax.broadcasted_iota(jnp.int32, (8, LANES), 0)
    f_rows = jnp.where(r8 == 0, f_re_row, jnp.where(r8 == 1, f_im_row, 0.0))
    pad = jnp.zeros((LANES - POWER_ROWS - 8, LANES), F32)
    pw_re = jnp.concatenate([pwt_re, f_rows, pad], axis=0).T
    pw_im = jnp.concatenate([pwt_im, f_rows, pad], axis=0).T
    f_re, f_im = pw_re[:, POWER_ROWS:POWER_ROWS + 1], pw_re[:, POWER_ROWS + 1:POWER_ROWS + 2]
    b_re, b_im = b_ref[gi, :, 0:SSM_GROUP], b_ref[gi, :, SSM_GROUP:2 * SSM_GROUP]
    bb_re = f_re * b_re - f_im * b_im
    bb_im = f_re * b_im + f_im * b_re
    t16 = t16_ref[...]
    bbt_re, bbt_im = pick(bb_re, t16), pick(bb_im, t16)
    ct_re = pick(c_ref[gi, :, 0:SSM_GROUP], t16)
    ct_im = pick(c_ref[gi, :, SSM_GROUP:2 * SSM_GROUP], t16)
    pw_at = lambda x: (pick(pw_re, e_ref[x]), pick(pw_im, e_ref[x]))
    id_re, id_im = pw_at(0)
    rev_re, rev_im = pw_at(1)
    p1_re, p1_im = pw_at(2)
    r16_re, r16_im = pw_at(3)
    f, b = slice(0, ns), slice(ns, 2 * ns)
    cmul = lambda ar, ai, br, bi: (ar * br - ai * bi, ar * bi + ai * br)

    sf_re, sf_im = cmul(rev_re[f], rev_im[f], bbt_re[f], bbt_im[f])
    sb_re, sb_im = cmul(id_re[b], id_im[b], bbt_re[b], bbt_im[b])
    ms_ref[gi] = jnp.concatenate([sf_re, sb_re, sf_im, sb_im], axis=0).T.astype(BF16)

    of_re, of_im = cmul(p1_re[f], p1_im[f], ct_re[f], ct_im[f])
    ob_re, ob_im = cmul(r16_re[b], r16_im[b], ct_re[b], ct_im[b])
    mo_ref[gi] = jnp.concatenate([of_re, ob_re, -of_im, -ob_im], axis=0).astype(BF16)

    xf_re, xf_im = cmul(id_re[f], id_im[f], ct_re[f], ct_im[f])
    xb_re, xb_im = cmul(rev_re[b], rev_im[b], ct_re[b], ct_im[b])
    btf = jnp.concatenate([bbt_re[f], bbt_im[f]], axis=0).T
    btb = jnp.concatenate([bbt_re[b], bbt_im[b]], axis=0).T
    g_f = dot_f32(btf, jnp.concatenate([xf_re, -xf_im], axis=0))
    g_b = dot_f32(btb, jnp.concatenate([xb_re, -xb_im], axis=0))
    for j in range(CHUNK):
        rows = slice(SSM_GROUP * j, SSM_GROUP * (j + 1))
        lo_col, hi_col = SSM_GROUP * j, SSM_GROUP * (j + 1)
        f_lo, f_hi = _rot256(g_f[rows, :LANES], g_f[rows, LANES:], lo_col, lane)
        b_lo, b_hi = _rot256(g_b[rows, :LANES], g_b[rows, LANES:], -SSM_GROUP * (CHUNK - 1 - j), lane)
        lo = jnp.where(lane >= lo_col, f_lo, 0.0) + jnp.where(lane < hi_col, b_lo, 0.0)
        up = jnp.where(lane + LANES >= lo_col, f_hi, 0.0) + jnp.where(lane + LANES < hi_col, b_hi, 0.0)
        mi_ref[gi, rows, 0:LANES] = lo.astype(BF16)
        mi_ref[gi, rows, LANES:2 * LANES] = up.astype(BF16)

    a16 = jnp.concatenate([pwt_re[CHUNK:CHUNK + 1, :], pwt_im[CHUNK:CHUNK + 1, :]], axis=1)
    a_ref[gi] = jnp.broadcast_to(a16, (8, 2 * LANES))


def _s5_matrices(lam_re, lam_im, log_dt, b_re, b_im, c_re, c_im):
    g, p2 = SSM_GROUPS, 2 * SSM_STATE
    both = lambda a: jnp.transpose(a, (1, 0, 2)).reshape(g, p2)
    log_dt_rows = jnp.repeat(jnp.transpose(log_dt), SSM_STATE, axis=1)
    par = jnp.stack([both(lam_re), both(lam_im), log_dt_rows], axis=1)
    par = jnp.concatenate([par, jnp.zeros((g, LANES - 3, p2), F32)], axis=1)
    rows_b = lambda a: jnp.transpose(a, (1, 0, 2, 3)).reshape(g, p2, SSM_GROUP)
    rows_c = lambda a: jnp.transpose(a, (1, 0, 3, 2)).reshape(g, p2, SSM_GROUP)
    b_cat = jnp.concatenate([rows_b(b_re), rows_b(b_im)], axis=-1)
    c_cat = jnp.concatenate([rows_c(c_re), rows_c(c_im)], axis=-1)

    col = np.arange(CHUNK_COLS)
    tile16 = (col[None, :] % SSM_GROUP == np.arange(SSM_GROUP)[:, None]).astype(np.float32)
    pos = col // SSM_GROUP
    k_idx = np.arange(LANES)[:, None]
    expand = np.stack([k_idx == pos[None, :], k_idx == (CHUNK - 1 - pos)[None, :],
                       k_idx == (pos + 1)[None, :], k_idx == (CHUNK - pos)[None, :]]).astype(np.float32)

    gp = S5_PREP_GROUPS
    mat = lambda: pl.BlockSpec((gp, CHUNK_COLS, CHUNK_COLS), lambda i: (i, 0, 0))
    mat_shape = jax.ShapeDtypeStruct((g, CHUNK_COLS, CHUNK_COLS), BF16)
    return pl.pallas_call(
        _s5_prep_kernel,
        grid=(g // gp,),
        in_specs=[pl.BlockSpec((gp, LANES, p2), lambda i: (i, 0, 0)),
                  pl.BlockSpec((gp, p2, 2 * SSM_GROUP), lambda i: (i, 0, 0)),
                  pl.BlockSpec((gp, p2, 2 * SSM_GROUP), lambda i: (i, 0, 0)),
                  pl.BlockSpec((SSM_GROUP, CHUNK_COLS), lambda i: (0, 0)),
                  pl.BlockSpec((4, LANES, CHUNK_COLS), lambda i: (0, 0, 0))],
        out_specs=[mat(), mat(), mat(), pl.BlockSpec((gp, 8, 2 * LANES), lambda i: (i, 0, 0))],
        out_shape=[mat_shape, mat_shape, mat_shape, jax.ShapeDtypeStruct((g, 8, 2 * LANES), F32)],
        name="s5_prep",
    )(par, b_cat, c_cat, jnp.asarray(tile16, BF16), jnp.asarray(expand, BF16))


def _s5_kernel(ul_ref, uc_ref, ms_ref, mi_ref, mo_ref, a_ref, y_ref, s_ref, sc_ref, hp_ref, *, gb):
    for gi in range(gb):
        for b in range(BATCH):
            sb = jnp.dot(ul_ref[gi, b], ms_ref[gi], preferred_element_type=F32)
            s_ref[gi, 0, pl.ds(b, N_CHUNKS, stride=BATCH), :] = sb[:, :LANES]
            s_ref[gi, 1, pl.ds(b, N_CHUNKS, stride=BATCH), :] = sb[:, LANES:]
            cb = jnp.dot(uc_ref[gi, b], ms_ref[gi], preferred_element_type=F32)
            sc_ref[gi, 0, pl.ds(b, N_CTX_CHUNKS, stride=BATCH), :] = cb[:, :LANES]
            sc_ref[gi, 1, pl.ds(b, N_CTX_CHUNKS, stride=BATCH), :] = cb[:, LANES:]
    lane = lax.broadcasted_iota(jnp.int32, (BATCH, LANES), 1)
    fwd = lane < SSM_STATE
    half = SSM_STATE

    def advance(gi, h_re, h_im, row_f, row_b, src):
        s_re = jnp.where(fwd, src[gi, 0, pl.ds(row_f, BATCH), :], src[gi, 0, pl.ds(row_b, BATCH), :])
        s_im = jnp.where(fwd, src[gi, 1, pl.ds(row_f, BATCH), :], src[gi, 1, pl.ds(row_b, BATCH), :])
        a_re = a_ref[gi, :, 0:LANES]
        a_im = a_ref[gi, :, LANES:2 * LANES]
        n_re = a_re * h_re - a_im * h_im + s_re
        n_im = a_re * h_im + a_im * h_re + s_im
        return n_re, n_im

    def ctx_step(t, carry):
        row_f = pl.multiple_of(t * BATCH, BATCH)
        row_b = pl.multiple_of((N_CTX_CHUNKS - 1 - t) * BATCH, BATCH)
        return tuple(advance(gi, carry[gi][0], carry[gi][1], row_f, row_b, sc_ref) for gi in range(gb))

    def lat_step(t, carry):
        row_f = pl.multiple_of(t * BATCH, BATCH)
        row_b = pl.multiple_of((N_CHUNKS - 1 - t) * BATCH, BATCH)
        out = []
        for gi in range(gb):
            h_re, h_im = carry[gi]
            hp_ref[gi, 0, pl.ds(row_f, BATCH), 0:half] = h_re[:, 0:half]
            hp_ref[gi, 0, pl.ds(row_b, BATCH), half:2 * half] = h_re[:, half:]
            hp_ref[gi, 1, pl.ds(row_f, BATCH), 0:half] = h_im[:, 0:half]
            hp_ref[gi, 1, pl.ds(row_b, BATCH), half:2 * half] = h_im[:, half:]
            out.append(advance(gi, h_re, h_im, row_f, row_b, s_ref))
        return tuple(out)

    zero = jnp.zeros((BATCH, LANES), F32)
    carry = tuple((zero, zero) for _ in range(gb))
    carry = lax.fori_loop(0, N_CTX_CHUNKS, ctx_step, carry)
    lax.fori_loop(0, N_CHUNKS, lat_step, carry)
    for gi in range(gb):
        for b in range(BATCH):
            hb_re = hp_ref[gi, 0, pl.ds(b, N_CHUNKS, stride=BATCH), :].astype(BF16)
            hb_im = hp_ref[gi, 1, pl.ds(b, N_CHUNKS, stride=BATCH), :].astype(BF16)
            y = (jnp.dot(ul_ref[gi, b], mi_ref[gi], preferred_element_type=F32)
                 + jnp.dot(jnp.concatenate([hb_re, hb_im], axis=1), mo_ref[gi], preferred_element_type=F32))
            y_ref[gi, b] = y.astype(BF16)


def _s5_scan(u_lat_t, u_ctx_t, m_intra, m_state, m_out, a16):
    gb = 4
    rows = N_CHUNKS * BATCH
    crows = N_CTX_CHUNKS * BATCH
    grp = lambda r, c: pl.BlockSpec((gb, r, c), lambda g: (g, 0, 0))
    tok = lambda n: pl.BlockSpec((gb, BATCH, n, CHUNK_COLS), lambda g: (g, 0, 0, 0))
    return pl.pallas_call(
        functools.partial(_s5_kernel, gb=gb),
        grid=(SSM_GROUPS // gb,),
        in_specs=[tok(N_CHUNKS), tok(N_CTX_CHUNKS), grp(CHUNK_COLS, CHUNK_COLS),
                  grp(CHUNK_COLS, CHUNK_COLS), grp(CHUNK_COLS, CHUNK_COLS), grp(8, 2 * LANES)],
        out_specs=tok(N_CHUNKS),
        out_shape=jax.ShapeDtypeStruct((SSM_GROUPS, BATCH, N_CHUNKS, CHUNK_COLS), BF16),
        scratch_shapes=[pltpu.VMEM((gb, 2, rows, LANES), F32),
                        pltpu.VMEM((gb, 2, crows, LANES), F32),
                        pltpu.VMEM((gb, 2, rows, LANES), F32)],
        compiler_params=pltpu.CompilerParams(
            dimension_semantics=("arbitrary",), vmem_limit_bytes=VMEM_LIMIT),
        name="s5_scan",
    )(u_lat_t, u_ctx_t, m_state, m_intra, m_out, a16)


FFN_TILE = 256


def _post_kernel(x_ref, a_ref, yt_ref, u_ref, ga_ref, gs_ref, mod_ref, d_ref, fg_ref, og_ref, permt_ref,
                 wglu_ref, wba_ref, wbs_ref, wout_ref, wfi_ref, wfo_ref, o_ref, h1_ref, n2_ref, act_ref, r_ref):
    dm = D_MODEL
    g1 = mod_ref[0, :, 2 * dm:3 * dm]
    sh2 = mod_ref[0, :, 3 * dm:4 * dm]
    sc2 = mod_ref[0, :, 4 * dm:5 * dm]
    g2 = mod_ref[0, :, 5 * dm:6 * dm]
    half = CHUNK * CHUNK
    halves = [slice(h * half, (h + 1) * half) for h in range(x_ref.shape[1] // half)]
    dot = functools.partial(jnp.dot, preferred_element_type=F32)

    sp = []
    for h, rows in enumerate(halves):
        y = _load_chunk_layout(yt_ref, permt_ref, r_ref, h)
        sp.append(jax.nn.gelu(y + d_ref[...] * u_ref[0, rows, :].astype(F32)).astype(BF16))
    s = []
    for h, rows in enumerate(halves):
        vg = dot(sp[h], wglu_ref[...])
        s.append((vg[:, :SSM_WIDTH] * jax.nn.sigmoid(vg[:, SSM_WIDTH:])).astype(BF16))
    merged = []
    for h, rows in enumerate(halves):
        a = jnp.concatenate([a_ref[0, j, rows, :] for j in range(ATTN_WIDTH // LANES)], axis=1)
        m = (ga_ref[0, rows, :].astype(F32) * dot(a, wba_ref[...])
             + gs_ref[0, rows, :].astype(F32) * dot(s[h], wbs_ref[...]))
        merged.append(m.astype(BF16))
    for h, rows in enumerate(halves):
        h1 = x_ref[0, rows, :] + g1 * dot(merged[h], wout_ref[...])
        h1_ref[rows, :] = h1
        n2_ref[rows, :] = _rms_modulate(h1, fg_ref[...], sh2, sc2).astype(BF16)
    for lo in range(0, FFN_HIDDEN, FFN_TILE):
        width = min(FFN_TILE, FFN_HIDDEN - lo)
        for rows in halves:
            n2 = n2_ref[rows, :]
            fa = dot(n2, wfi_ref[:, lo:lo + width])
            fb = dot(n2, wfi_ref[:, FFN_HIDDEN + lo:FFN_HIDDEN + lo + width])
            act_ref[rows, lo:lo + width] = (fa * jax.nn.sigmoid(fa) * fb).astype(BF16)
    for rows in halves:
        h2 = h1_ref[rows, :] + g2 * dot(act_ref[rows, :], wfo_ref[...])
        o_ref[0, rows, :] = (h2 * lax.rsqrt(jnp.mean(h2 * h2, axis=-1, keepdims=True) + NORM_EPS)) * og_ref[...]


def _post(x, a, y_t, u, ga, gs, mod3, d_skip, ffn_g, fin_g, perm_t, wglu, wba, wbs, wout, wfi, wfo):
    tm = 512
    n = CHUNK * CHUNK
    tok = lambda width: pl.BlockSpec((1, tm, width), lambda b, i: (b, i, 0))
    const = lambda r, c: pl.BlockSpec((r, c), lambda b, i: (0, 0), pipeline_mode=pl.Buffered(1))
    return pl.pallas_call(
        _post_kernel,
        grid=(BATCH, SEQ // tm),
        in_specs=[tok(D_MODEL),
                  pl.BlockSpec((1, N_HEADS // 2, tm, LANES), lambda b, i: (b, 0, i, 0)),
                  pl.BlockSpec((SSM_GROUPS, 1, tm // CHUNK, CHUNK_COLS), lambda b, i: (0, b, i, 0)),
                  tok(SSM_WIDTH), tok(D_MODEL), tok(D_MODEL),
                  pl.BlockSpec((1, 1, N_MOD * D_MODEL), lambda b, i: (b, 0, 0)),
                  const(1, SSM_WIDTH), const(1, D_MODEL), const(1, D_MODEL), const(n, n),
                  const(SSM_WIDTH, 2 * SSM_WIDTH), const(ATTN_WIDTH, D_MODEL),
                  const(SSM_WIDTH, D_MODEL), const(D_MODEL, D_MODEL),
                  const(D_MODEL, 2 * FFN_HIDDEN), const(FFN_HIDDEN, D_MODEL)],
        out_specs=tok(D_MODEL),
        out_shape=jax.ShapeDtypeStruct((BATCH, SEQ, D_MODEL), F32),
        scratch_shapes=[pltpu.VMEM((tm, D_MODEL), F32), pltpu.VMEM((tm, D_MODEL), BF16),
                        pltpu.VMEM((tm, FFN_HIDDEN), BF16), pltpu.VMEM((tm // n, n, SSM_WIDTH), F32)],
        compiler_params=pltpu.CompilerParams(
            dimension_semantics=("arbitrary", "arbitrary"), vmem_limit_bytes=VMEM_LIMIT),
        name="post",
    )(x, a, y_t, u, ga, gs, mod3, d_skip, ffn_g, fin_g, perm_t, wglu, wba, wbs, wout, wfi, wfo)


def kernel(x, c, ctx, c_ctx, w_mod, b_mod, attn_norm_g, ffn_norm_g, w_in, rel_pos_bias,
           ssm_lambda_re, ssm_lambda_im, ssm_log_dt, ssm_b_re, ssm_b_im, ssm_c_re, ssm_c_im, ssm_d,
           w_glu, w_branch_attn, w_branch_ssm, w_out, w_ffn_in, w_ffn_out, final_norm_g):
    assert x.shape == (BATCH, SEQ, D_MODEL) and w_mod.shape[0] == 1
    c_rows = jnp.concatenate(
        [c, c_ctx[None, :], jnp.zeros((MOD_ROWS - BATCH - 1, D_MODEL), F32)], axis=0)
    mod3 = _modulation(c_rows, w_mod[0], b_mod[0]).reshape(MOD_ROWS, 1, N_MOD * D_MODEL)

    col_scale = jnp.concatenate([jnp.full((ATTN_WIDTH,), HEAD_DIM ** -0.5 * LOG2E, F32),
                                 jnp.ones((IN_COLS - ATTN_WIDTH,), F32)])
    w_in_bf16 = (w_in[0] * col_scale[None, :]).astype(BF16)
    norm_g = attn_norm_g[0].reshape(1, D_MODEL)

    perm = _chunk_perm()
    q, k, v, u, u_t, ga, gs = _input_projection(x, mod3, norm_g, w_in_bf16, _rope_tables(), perm)
    kc, vc, uc_t = _context_projection(ctx, mod3, norm_g, w_in_bf16, perm)

    attn = _attention(q, k, v, kc, vc, _bias_tables(rel_pos_bias[0]))

    m_intra, m_state, m_out, a16 = _s5_matrices(
        ssm_lambda_re[0], ssm_lambda_im[0], ssm_log_dt[0], ssm_b_re[0], ssm_b_im[0],
        ssm_c_re[0], ssm_c_im[0])
    y_t = _s5_scan(u_t, uc_t, m_intra, m_state, m_out, a16)

    return _post(x, attn, y_t, u, ga, gs, mod3,
                 ssm_d[0].reshape(1, SSM_WIDTH), ffn_norm_g[0].reshape(1, D_MODEL),
                 final_norm_g.reshape(1, D_MODEL), perm,
                 w_glu[0].astype(BF16), w_branch_attn[0].astype(BF16), w_branch_ssm[0].astype(BF16),
                 w_out[0].astype(BF16), w_ffn_in[0].astype(BF16), w_ffn_out[0].astype(BF16))
```

```python
import functools
import math

import numpy as np
import jax
import jax.numpy as jnp
from jax import lax
from jax.experimental import pallas as pl
from jax.experimental.pallas import tpu as pltpu

F32 = jnp.float32
BF16 = jnp.bfloat16

D_MODEL = 1024
BATCH = 8
SEQ = 4096
GRID_W = 64
GRID_ROWS = SEQ // GRID_W
CTX_LEN = 256
N_HEADS = 8
HEAD_DIM = 64
ATTN_WIDTH = N_HEADS * HEAD_DIM
WIN_H = 8
WIN_W = 16
ROPE_BASE = 10000.0
SSM_WIDTH = 512
SSM_GROUP = 16
SSM_GROUPS = SSM_WIDTH // SSM_GROUP
SSM_STATE = 64
FFN_HIDDEN = 2816
IN_COLS = 3 * ATTN_WIDTH + SSM_WIDTH + 2 * D_MODEL
N_MOD = 6
NORM_EPS = 1e-6
NEG_BIG = -1e30
LOG2E = math.log2(math.e)

LANES = 128
CHUNK = 16
N_CHUNKS = SEQ // CHUNK
N_CTX_CHUNKS = CTX_LEN // CHUNK
CHUNK_COLS = CHUNK * SSM_GROUP
MOD_ROWS = 16
CTX_MOD_ROW = BATCH
VMEM_LIMIT = 56 * 1024 * 1024

Q_ROWS_PER_STEP = 4
Q_BLOCK = Q_ROWS_PER_STEP * GRID_W
K_ROWS_PER_STEP = Q_ROWS_PER_STEP + WIN_H
K_BLOCK = K_ROWS_PER_STEP * GRID_W
N_QBLOCKS = GRID_ROWS // Q_ROWS_PER_STEP
KEY_GROUP = (WIN_H // 2) * GRID_W
SOFTMAX_ROWS = 32

def _rms_modulate(x, g, shift, scale):
    xn = x * lax.rsqrt(jnp.mean(x * x, axis=-1, keepdims=True) + NORM_EPS)
    return (xn * g) * (1.0 + scale) + shift


def _block_transpose8(vs, lane):
    for shift in (64, 32, 16):
        keep = (lane & (2 * shift - 1)) < shift
        dist = shift // SSM_GROUP
        out = list(vs)
        for a in range(8):
            if a & dist:
                continue
            b = a + dist
            out[a] = jnp.where(keep, vs[a], pltpu.roll(vs[b], shift, 1))
            out[b] = jnp.where(keep, pltpu.roll(vs[a], LANES - shift, 1), vs[b])
        vs = out
    return vs


def _chunk_perm():
    n = CHUNK * CHUNK
    r = np.arange(n)
    m = np.zeros((n, n), np.float32)
    m[r, (r % CHUNK) * CHUNK + r // CHUNK] = 1.0
    return jnp.asarray(m, BF16)


def _store_chunk_layout(u16, perm_ref, out_ref, n_groups16, group0=0):
    lane = lax.broadcasted_iota(jnp.int32, (1, LANES), 1)
    n = CHUNK * CHUNK
    for hf in range(n_groups16):
        r = jnp.dot(perm_ref[...], u16[hf * n:(hf + 1) * n, :], preferred_element_type=F32)
        c0 = (group0 + hf) * CHUNK
        for v in range(SSM_WIDTH // LANES):
            for jh in range(2):
                vs = [r[CHUNK * (8 * jh + jp):CHUNK * (8 * jh + jp + 1), v * LANES:(v + 1) * LANES]
                      for jp in range(8)]
                outs = _block_transpose8(vs, lane)
                for gi in range(8):
                    out_ref[8 * v + gi, 0, c0:c0 + CHUNK, jh * LANES:(jh + 1) * LANES] = outs[gi].astype(BF16)


def _load_chunk_layout(yt_ref, perm_t_ref, r_ref, hf):
    lane = lax.broadcasted_iota(jnp.int32, (1, LANES), 1)
    for v in range(SSM_WIDTH // LANES):
        for jh in range(2):
            vs = [yt_ref[8 * v + gi, 0, hf * CHUNK:(hf + 1) * CHUNK,
                         jh * LANES:(jh + 1) * LANES].astype(F32) for gi in range(8)]
            outs = _block_transpose8(vs, lane)
            for jp in range(8):
                r_ref[hf, CHUNK * (8 * jh + jp):CHUNK * (8 * jh + jp + 1), v * LANES:(v + 1) * LANES] = outs[jp]
    return jnp.dot(perm_t_ref[...], r_ref[hf].astype(BF16), preferred_element_type=F32)


def _mod_kernel(c_ref, w_ref, b_ref, o_ref):
    c = c_ref[...]
    s = c * jax.nn.sigmoid(c)
    o_ref[...] = jnp.dot(s, w_ref[...], preferred_element_type=F32) + b_ref[...]


def _modulation(c_rows, w_mod, b_mod):
    n = N_MOD * D_MODEL
    tn = 1536
    return pl.pallas_call(
        _mod_kernel,
        grid=(n // tn,),
        in_specs=[pl.BlockSpec((MOD_ROWS, D_MODEL), lambda j: (0, 0)),
                  pl.BlockSpec((D_MODEL, tn), lambda j: (0, j)),
                  pl.BlockSpec((1, tn), lambda j: (0, j))],
        out_specs=pl.BlockSpec((MOD_ROWS, tn), lambda j: (0, j)),
        out_shape=jax.ShapeDtypeStruct((MOD_ROWS, n), F32),
        name="modulation",
    )(c_rows, w_mod, b_mod.reshape(1, n))


def _rope_tables():
    n_freq = HEAD_DIM // 4
    inv_freq = ROPE_BASE ** (-np.arange(n_freq, dtype=np.float64) / n_freq)
    t = np.arange(SEQ)
    lane = np.arange(LANES)
    d = lane % HEAD_DIM
    use_col = (d // (HEAD_DIM // 2)) == 1
    w = d % (HEAD_DIM // 2)
    first = w < n_freq
    pos = np.where(use_col[None, :], (t % GRID_W)[:, None], (t // GRID_W)[:, None]).astype(np.float64)
    ang = pos * inv_freq[w % n_freq][None, :]
    cos = np.cos(ang)
    sin = np.sin(ang)
    sin_a = np.where(first[None, :], -sin, 0.0)
    sin_b = np.where(first[None, :], 0.0, sin)
    return (jnp.asarray(cos, F32), jnp.asarray(sin_a, F32), jnp.asarray(sin_b, F32))


def _rope_store(r, cos, sin_a, sin_b, out_ref, row0, transposed=False):
    n_rows = r.shape[0]
    for j in range(ATTN_WIDTH // LANES):
        xs = r[:, j * LANES:(j + 1) * LANES]
        rot = (xs * cos + pltpu.roll(xs, LANES - HEAD_DIM // 4, 1) * sin_a
               + pltpu.roll(xs, HEAD_DIM // 4, 1) * sin_b)
        if transposed:
            rot_t = rot.T.astype(BF16)
            for gi in range(n_rows // KEY_GROUP):
                out_ref[0, row0 // KEY_GROUP + gi, j * LANES:(j + 1) * LANES, :] = (
                    rot_t[:, gi * KEY_GROUP:(gi + 1) * KEY_GROUP])
        else:
            out_ref[0, j, row0:row0 + n_rows, :] = rot.astype(BF16)


INPROJ_SUBTILE = 512


def _inproj_kernel(x_ref, mod_ref, g_ref, w_ref, cos_ref, sa_ref, sb_ref, perm_ref,
                   q_ref, k_ref, v_ref, u_ref, ut_ref, ga_ref, gs_ref):
    shift = mod_ref[0, :, 0:D_MODEL]
    scale = mod_ref[0, :, D_MODEL:2 * D_MODEL]
    aw = ATTN_WIDTH
    dot = functools.partial(jnp.dot, preferred_element_type=F32)
    for row0 in range(0, x_ref.shape[1], INPROJ_SUBTILE):
        rows = slice(row0, row0 + INPROJ_SUBTILE)
        nb = _rms_modulate(x_ref[0, rows, :], g_ref[...], shift, scale).astype(BF16)
        cos = cos_ref[rows, :]
        sin_a = sa_ref[rows, :]
        sin_b = sb_ref[rows, :]
        _rope_store(dot(nb, w_ref[:, 0:aw]), cos, sin_a, sin_b, q_ref, row0)
        _rope_store(dot(nb, w_ref[:, aw:2 * aw]), cos, sin_a, sin_b, k_ref, row0, transposed=True)
        v = dot(nb, w_ref[:, 2 * aw:3 * aw]).astype(BF16)
        for j in range(aw // LANES):
            v_ref[0, j, rows, :] = v[:, j * LANES:(j + 1) * LANES]
        c0 = 3 * aw
        u16 = dot(nb, w_ref[:, c0:c0 + SSM_WIDTH]).astype(BF16)
        u_ref[0, rows, :] = u16
        n16 = CHUNK * CHUNK
        _store_chunk_layout(u16, perm_ref, ut_ref, INPROJ_SUBTILE // n16, row0 // n16)
        c1 = c0 + SSM_WIDTH
        ga_ref[0, rows, :] = jax.nn.sigmoid(dot(nb, w_ref[:, c1:c1 + D_MODEL])).astype(BF16)
        c2 = c1 + D_MODEL
        gs_ref[0, rows, :] = jax.nn.sigmoid(dot(nb, w_ref[:, c2:c2 + D_MODEL])).astype(BF16)


def _input_projection(x, mod3, norm_g, w_in_bf16, rope, perm):
    tm = 2 * INPROJ_SUBTILE
    n = CHUNK * CHUNK
    cos, sin_a, sin_b = rope
    tok = lambda width: pl.BlockSpec((1, tm, width), lambda i, b: (b, i, 0))
    tab = pl.BlockSpec((tm, LANES), lambda i, b: (i, 0))
    out = lambda width: jax.ShapeDtypeStruct((BATCH, SEQ, width), BF16)
    pairs = pl.BlockSpec((1, N_HEADS // 2, tm, LANES), lambda i, b: (b, 0, i, 0))
    pairs_shape = jax.ShapeDtypeStruct((BATCH, N_HEADS // 2, SEQ, LANES), BF16)
    return pl.pallas_call(
        _inproj_kernel,
        grid=(SEQ // tm, BATCH),
        in_specs=[tok(D_MODEL),
                  pl.BlockSpec((1, 1, N_MOD * D_MODEL), lambda i, b: (b, 0, 0)),
                  pl.BlockSpec((1, D_MODEL), lambda i, b: (0, 0)),
                  pl.BlockSpec((D_MODEL, IN_COLS), lambda i, b: (0, 0), pipeline_mode=pl.Buffered(1)),
                  tab, tab, tab,
                  pl.BlockSpec((n, n), lambda i, b: (0, 0), pipeline_mode=pl.Buffered(1))],
        out_specs=[pairs,
                   pl.BlockSpec((1, tm // KEY_GROUP, ATTN_WIDTH, KEY_GROUP), lambda i, b: (b, i, 0, 0)),
                   pairs, tok(SSM_WIDTH),
                   pl.BlockSpec((SSM_GROUPS, 1, tm // CHUNK, CHUNK_COLS), lambda i, b: (0, b, i, 0)),
                   tok(D_MODEL), tok(D_MODEL)],
        out_shape=[pairs_shape,
                   jax.ShapeDtypeStruct((BATCH, SEQ // KEY_GROUP, ATTN_WIDTH, KEY_GROUP), BF16),
                   pairs_shape, out(SSM_WIDTH),
                   jax.ShapeDtypeStruct((SSM_GROUPS, BATCH, N_CHUNKS, CHUNK_COLS), BF16),
                   out(D_MODEL), out(D_MODEL)],
        compiler_params=pltpu.CompilerParams(
            dimension_semantics=("arbitrary", "arbitrary"), vmem_limit_bytes=VMEM_LIMIT),
        name="input_projection",
    )(x, mod3, norm_g, w_in_bf16, cos, sin_a, sin_b, perm)


def _ctx_proj_kernel(x_ref, mod_ref, g_ref, wk_ref, wv_ref, wu_ref, perm_ref, k_ref, v_ref, ut_ref):
    x = x_ref[0]
    shift = mod_ref[0, :, 0:D_MODEL]
    scale = mod_ref[0, :, D_MODEL:2 * D_MODEL]
    nb = _rms_modulate(x, g_ref[...], shift, scale).astype(BF16)
    k_ref[0] = jnp.dot(nb, wk_ref[...], preferred_element_type=F32).T.astype(BF16)
    v_ref[0] = jnp.dot(nb, wv_ref[...], preferred_element_type=F32).astype(BF16)
    u16 = jnp.dot(nb, wu_ref[...], preferred_element_type=F32).astype(BF16)
    _store_chunk_layout(u16, perm_ref, ut_ref, 1)


def _context_projection(ctx, mod3, norm_g, w_in_bf16, perm):
    n = CHUNK * CHUNK
    aw = ATTN_WIDTH
    tok = lambda width: pl.BlockSpec((1, CTX_LEN, width), lambda b: (b, 0, 0))
    wcol = lambda j: pl.BlockSpec((D_MODEL, aw), lambda b: (0, j))
    out = jax.ShapeDtypeStruct((BATCH, CTX_LEN, aw), BF16)
    return pl.pallas_call(
        _ctx_proj_kernel,
        grid=(BATCH,),
        in_specs=[tok(D_MODEL),
                  pl.BlockSpec((1, 1, N_MOD * D_MODEL), lambda b: (CTX_MOD_ROW, 0, 0)),
                  pl.BlockSpec((1, D_MODEL), lambda b: (0, 0)),
                  wcol(1), wcol(2), wcol(3),
                  pl.BlockSpec((n, n), lambda b: (0, 0))],
        out_specs=[pl.BlockSpec((1, aw, CTX_LEN), lambda b: (b, 0, 0)), tok(aw),
                   pl.BlockSpec((SSM_GROUPS, 1, N_CTX_CHUNKS, CHUNK_COLS), lambda b: (0, b, 0, 0))],
        out_shape=[jax.ShapeDtypeStruct((BATCH, aw, CTX_LEN), BF16), out,
                   jax.ShapeDtypeStruct((SSM_GROUPS, BATCH, N_CTX_CHUNKS, CHUNK_COLS), BF16)],
        name="context_projection",
    )(ctx, mod3, norm_g, w_in_bf16, w_in_bf16, w_in_bf16, perm)


def _window_start_rows(r):
    return min(max(r - WIN_H // 2, 0), GRID_ROWS - WIN_H)


def _key_block_row(jb):
    return min(max(Q_ROWS_PER_STEP * jb - WIN_H // 2, 0), GRID_ROWS - K_ROWS_PER_STEP)


def _window_geometry(jb):
    key_row0 = _key_block_row(jb)
    offs, deltas = [], []
    for i in range(Q_ROWS_PER_STEP):
        r = Q_ROWS_PER_STEP * jb + i
        rs = _window_start_rows(r)
        offs.append(rs - key_row0)
        deltas.append(r - rs)
    return offs, deltas


ATTN_SAMPLES_PER_STEP = 2


def _attn_kernel(*refs):
    def per_sample(bi, carry):
        _attn_sample(bi, *refs)
        return carry

    lax.fori_loop(0, ATTN_SAMPLES_PER_STEP, per_sample, 0)


def _attn_sample(bi, q_ref, k_ref, v_ref, kc_ref, vc_ref, bias_ref, o_ref,
                 s_ref, sc_ref, p_ref, l_ref, m_ref, acc_ref):
    lane = lax.broadcasted_iota(jnp.int32, (1, LANES), 1)
    left = lane < HEAD_DIM
    n_ktiles = K_BLOCK // LANES
    last = N_QBLOCKS - 1
    max_key_start = (GRID_ROWS - K_ROWS_PER_STEP) * GRID_W

    def scores(sl, e, q_start, key_start):
        q2 = q_ref[bi, 0, pl.ds(q_start, Q_BLOCK), :]
        qm = jnp.where(left if e == 0 else jnp.logical_not(left), q2, jnp.zeros_like(q2))
        group0 = key_start // KEY_GROUP
        for gi in range(K_BLOCK // KEY_GROUP):
            s_ref[sl, :, gi * KEY_GROUP:(gi + 1) * KEY_GROUP] = jnp.dot(
                qm, k_ref[bi, group0 + gi], preferred_element_type=F32)
        sc_ref[sl] = jnp.dot(qm, kc_ref[bi], preferred_element_type=F32)

    def softmax(sl, e, offs, deltas):
        def geometry(sub):
            i = sub * SOFTMAX_ROWS // GRID_W
            qcols = slice(sub * SOFTMAX_ROWS % GRID_W, sub * SOFTMAX_ROWS % GRID_W + SOFTMAX_ROWS)
            rows = slice(sub * SOFTMAX_ROWS, (sub + 1) * SOFTMAX_ROWS)
            par = offs[i] % 2
            return rows, qcols, par, offs[i] // 2, WIN_H // 2 + par, deltas[i]

        def tile(rows, qcols, par, t0, n_tiles, delta, xt):
            t = t0 + xt
            dr0 = 2 * xt - par - delta
            st = s_ref[sl, rows, t * LANES:(t + 1) * LANES] + bias_ref[0, e, dr0 + WIN_H, qcols, :]
            if par and xt == 0:
                st = jnp.where(left, NEG_BIG, st)
            if par and xt == n_tiles - 1:
                st = jnp.where(left, st, NEG_BIG)
            return st

        n_sub = Q_BLOCK // SOFTMAX_ROWS
        for sub in range(n_sub):
            geo = geometry(sub)
            rows, n_tiles = geo[0], geo[4]
            mt = jnp.maximum(sc_ref[sl, rows, 0:LANES], sc_ref[sl, rows, LANES:2 * LANES])
            for xt in range(n_tiles):
                mt = jnp.maximum(mt, tile(*geo, xt))
            m_ref[sl, rows, :] = jnp.broadcast_to(jnp.max(mt, axis=1, keepdims=True), (SOFTMAX_ROWS, LANES))
        for sub in range(n_sub):
            geo = geometry(sub)
            rows, t0, n_tiles = geo[0], geo[3], geo[4]
            m = m_ref[sl, rows, :]
            lt = None
            for xt in range(n_tiles):
                t = t0 + xt
                pt = jnp.exp2(tile(*geo, xt) - m)
                lt = pt if lt is None else lt + pt
                p_ref[sl, rows, t * LANES:(t + 1) * LANES] = pt.astype(BF16)
            for t in range(n_ktiles):
                if not (t0 <= t < t0 + n_tiles):
                    p_ref[sl, rows, t * LANES:(t + 1) * LANES] = jnp.zeros((SOFTMAX_ROWS, LANES), BF16)
            for ci in range(2):
                pt = jnp.exp2(sc_ref[sl, rows, ci * LANES:(ci + 1) * LANES] - m)
                lt = lt + pt
                p_ref[sl, rows, K_BLOCK + ci * LANES:K_BLOCK + (ci + 1) * LANES] = pt.astype(BF16)
            l = jnp.sum(lt, axis=1, keepdims=True)
            l_ref[sl, rows, :] = jnp.broadcast_to(1.0 / l, (SOFTMAX_ROWS, LANES))

    def values(sl, key_start):
        v_all = jnp.concatenate([v_ref[bi, 0, pl.ds(key_start, K_BLOCK), :], vc_ref[bi]], axis=0)
        return jnp.dot(p_ref[sl], v_all, preferred_element_type=F32) * l_ref[sl]

    def block(jb_static, cur, nxt):
        offs, deltas = _window_geometry(jb_static)
        scores(1, 1, *cur)
        softmax(0, 0, offs, deltas)
        acc_ref[0] = values(0, cur[1])
        if nxt is not None:
            scores(0, 0, *nxt)
        softmax(1, 1, offs, deltas)
        o1 = values(1, cur[1])
        o_ref[bi, 0, pl.ds(cur[0], Q_BLOCK), :] = jnp.where(left, acc_ref[0], o1).astype(BF16)

    def starts(jb):
        q_start = pl.multiple_of(jb * Q_BLOCK, Q_BLOCK)
        key_start = jnp.clip((jb * Q_ROWS_PER_STEP - WIN_H // 2) * GRID_W, 0, max_key_start)
        return q_start, pl.multiple_of(key_start, KEY_GROUP)

    scores(0, 0, 0, 0)
    block(0, (0, 0), (Q_BLOCK, max(Q_ROWS_PER_STEP - WIN_H // 2, 0) * GRID_W))

    blocks_per_trip = 7

    def interior(trip, carry):
        jb = 1 + blocks_per_trip * trip
        for d in range(blocks_per_trip):
            block(1, starts(jb + d), starts(jb + d + 1))
        return carry

    assert (last - 1) % blocks_per_trip == 0
    lax.fori_loop(0, (last - 1) // blocks_per_trip, interior, 0)
    block(last, (last * Q_BLOCK, max_key_start), None)


def _bias_tables(rpb):
    qcol = np.arange(GRID_W)
    kcol = np.arange(GRID_W)
    col_start = np.clip(qcol - WIN_W // 2, 0, GRID_W - WIN_W)
    in_win = (kcol[None, :] >= col_start[:, None]) & (kcol[None, :] < col_start[:, None] + WIN_W)
    dc_idx = np.clip(kcol[None, :] - qcol[:, None], -(WIN_W - 1), WIN_W - 1) + WIN_W - 1
    sel = (np.arange(2 * WIN_W - 1)[:, None, None] == dc_idx[None]).astype(np.float32)
    toe = jnp.einsum('hdt,tck->hdck', rpb * LOG2E, jnp.asarray(sel),
                     precision=lax.Precision.HIGHEST)
    toe = jnp.where(in_win[None, None], toe, NEG_BIG)
    neg = jnp.full((N_HEADS, 1, GRID_W, GRID_W), NEG_BIG, F32)
    ext = jnp.concatenate([neg, toe, neg], axis=1)
    pair = jnp.concatenate([ext[:, 0:16], ext[:, 1:17]], axis=-1)
    return pair.reshape(N_HEADS // 2, 2, 16, GRID_W, LANES)


def _attention(q, k, v, kc, vc, bias):
    n_slots = 2
    nb = ATTN_SAMPLES_PER_STEP
    kspec = pl.BlockSpec((nb, 1, SEQ, LANES), lambda hp, b: (b, hp, 0, 0))
    ktspec = pl.BlockSpec((nb, SEQ // KEY_GROUP, LANES, KEY_GROUP), lambda hp, b: (b, 0, hp, 0))
    cspec = pl.BlockSpec((nb, CTX_LEN, LANES), lambda hp, b: (b, 0, hp))
    ctspec = pl.BlockSpec((nb, LANES, CTX_LEN), lambda hp, b: (b, hp, 0))
    bspec = pl.BlockSpec((1, 2, 16, GRID_W, LANES), lambda hp, b: (hp, 0, 0, 0, 0))
    return pl.pallas_call(
        _attn_kernel,
        grid=(N_HEADS // 2, BATCH // nb),
        in_specs=[kspec, ktspec, kspec, ctspec, cspec, bspec],
        out_specs=kspec,
        out_shape=jax.ShapeDtypeStruct((BATCH, N_HEADS // 2, SEQ, LANES), BF16),
        scratch_shapes=[pltpu.VMEM((n_slots, Q_BLOCK, K_BLOCK), F32),
                        pltpu.VMEM((n_slots, Q_BLOCK, CTX_LEN), F32),
                        pltpu.VMEM((n_slots, Q_BLOCK, K_BLOCK + CTX_LEN), BF16),
                        pltpu.VMEM((n_slots, Q_BLOCK, LANES), F32),
                        pltpu.VMEM((n_slots, Q_BLOCK, LANES), F32),
                        pltpu.VMEM((1, Q_BLOCK, LANES), F32)],
        compiler_params=pltpu.CompilerParams(
            dimension_semantics=("arbitrary", "arbitrary"),
            vmem_limit_bytes=VMEM_LIMIT),
        name="attention",
    )(q, k, v, kc, vc, bias)


def _rot256(a, b, s, lane):
    s %= 2 * LANES
    if s >= LANES:
        a, b, s = b, a, s - LANES
    if s == 0:
        return a, b
    ra = pltpu.roll(a, s, 1)
    rb = pltpu.roll(b, s, 1)
    keep = lane >= s
    return jnp.where(keep, ra, rb), jnp.where(keep, rb, ra)


S5_PREP_GROUPS = 2
POWER_ROWS = 24


def _s5_prep_kernel(*refs):
    for gi in range(S5_PREP_GROUPS):
        _s5_prep_group(gi, *refs)


def _s5_prep_group(gi, par_ref, b_ref, c_ref, t16_ref, e_ref, mi_ref, ms_ref, mo_ref, a_ref):
    dot = functools.partial(jnp.dot, preferred_element_type=F32)

    def split2(x):
        hi = x.astype(BF16)
        return hi, (x - hi.astype(F32)).astype(BF16)

    def split3(x):
        hi = x.astype(BF16)
        r1 = x - hi.astype(F32)
        mid = r1.astype(BF16)
        return hi, mid, (r1 - mid.astype(F32)).astype(BF16)

    def pick(x, onehot):
        hi, mid, lo = split3(x)
        return dot(hi, onehot) + dot(mid, onehot) + dot(lo, onehot)

    def dot_f32(a, b):
        ah, al = split2(a)
        bh, bl = split2(b)
        return dot(jnp.concatenate([ah, ah, al], axis=1), jnp.concatenate([bh, bl, bh], axis=0))

    ns = SSM_STATE
    lam_re, lam_im = par_ref[gi, 0:1, :], par_ref[gi, 1:2, :]
    dt = jnp.exp(par_ref[gi, 2:3, :])
    lane = lax.broadcasted_iota(jnp.int32, (1, LANES), 1)
    kf = lax.broadcasted_iota(jnp.int32, (POWER_ROWS, LANES), 0).astype(F32)
    mag = jnp.exp((lam_re * dt) * kf)
    ang = (lam_im * dt) * kf
    pwt_re, pwt_im = mag * jnp.cos(ang), mag * jnp.sin(ang)
    den = lam_re * lam_re + lam_im * lam_im
    nr, ni = pwt_re[1:2, :] - 1.0, pwt_im[1:2, :]
    f_re_row = (nr * lam_re + ni * lam_im) / den
    f_im_row = (ni * lam_re - nr * lam_im) / den
    r8 = lax.broadcasted_iota(jnp.int32, (8, LANES), 0)
    f_rows = jnp.where(r8 == 0, f_re_row, jnp.where(r8 == 1, f_im_row, 0.0))
    pad = jnp.zeros((LANES - POWER_ROWS - 8, LANES), F32)
    pw_re = jnp.concatenate([pwt_re, f_rows, pad], axis=0).T
    pw_im = jnp.concatenate([pwt_im, f_rows, pad], axis=0).T
    f_re, f_im = pw_re[:, POWER_ROWS:POWER_ROWS + 1], pw_re[:, POWER_ROWS + 1:POWER_ROWS + 2]
    b_re, b_im = b_ref[gi, :, 0:SSM_GROUP], b_ref[gi, :, SSM_GROUP:2 * SSM_GROUP]
    bb_re = f_re * b_re - f_im * b_im
    bb_im = f_re * b_im + f_im * b_re
    t16 = t16_ref[...]
    bbt_re, bbt_im = pick(bb_re, t16), pick(bb_im, t16)
    ct_re = pick(c_ref[gi, :, 0:SSM_GROUP], t16)
    ct_im = pick(c_ref[gi, :, SSM_GROUP:2 * SSM_GROUP], t16)
    pw_at = lambda x: (pick(pw_re, e_ref[x]), pick(pw_im, e_ref[x]))
    id_re, id_im = pw_at(0)
    rev_re, rev_im = pw_at(1)
    p1_re, p1_im = pw_at(2)
    r16_re, r16_im = pw_at(3)
    f, b = slice(0, ns), slice(ns, 2 * ns)
    cmul = lambda ar, ai, br, bi: (ar * br - ai * bi, ar * bi + ai * br)

    sf_re, sf_im = cmul(rev_re[f], rev_im[f], bbt_re[f], bbt_im[f])
    sb_re, sb_im = cmul(id_re[b], id_im[b], bbt_re[b], bbt_im[b])
    ms_ref[gi] = jnp.concatenate([sf_re, sb_re, sf_im, sb_im], axis=0).T.astype(BF16)

    of_re, of_im = cmul(p1_re[f], p1_im[f], ct_re[f], ct_im[f])
    ob_re, ob_im = cmul(r16_re[b], r16_im[b], ct_re[b], ct_im[b])
    mo_ref[gi] = jnp.concatenate([of_re, ob_re, -of_im, -ob_im], axis=0).astype(BF16)

    xf_re, xf_im = cmul(id_re[f], id_im[f], ct_re[f], ct_im[f])
    xb_re, xb_im = cmul(rev_re[b], rev_im[b], ct_re[b], ct_im[b])
    btf = jnp.concatenate([bbt_re[f], bbt_im[f]], axis=0).T
    btb = jnp.concatenate([bbt_re[b], bbt_im[b]], axis=0).T
    g_f = dot_f32(btf, jnp.concatenate([xf_re, -xf_im], axis=0))
    g_b = dot_f32(btb, jnp.concatenate([xb_re, -xb_im], axis=0))
    for j in range(CHUNK):
        rows = slice(SSM_GROUP * j, SSM_GROUP * (j + 1))
        lo_col, hi_col = SSM_GROUP * j, SSM_GROUP * (j + 1)
        f_lo, f_hi = _rot256(g_f[rows, :LANES], g_f[rows, LANES:], lo_col, lane)
        b_lo, b_hi = _rot256(g_b[rows, :LANES], g_b[rows, LANES:], -SSM_GROUP * (CHUNK - 1 - j), lane)
        lo = jnp.where(lane >= lo_col, f_lo, 0.0) + jnp.where(lane < hi_col, b_lo, 0.0)
        up = jnp.where(lane + LANES >= lo_col, f_hi, 0.0) + jnp.where(lane + LANES < hi_col, b_hi, 0.0)
        mi_ref[gi, rows, 0:LANES] = lo.astype(BF16)
        mi_ref[gi, rows, LANES:2 * LANES] = up.astype(BF16)

    a16 = jnp.concatenate([pwt_re[CHUNK:CHUNK + 1, :], pwt_im[CHUNK:CHUNK + 1, :]], axis=1)
    a_ref[gi] = jnp.broadcast_to(a16, (8, 2 * LANES))


def _s5_matrices(lam_re, lam_im, log_dt, b_re, b_im, c_re, c_im):
    g, p2 = SSM_GROUPS, 2 * SSM_STATE
    both = lambda a: jnp.transpose(a, (1, 0, 2)).reshape(g, p2)
    log_dt_rows = jnp.repeat(jnp.transpose(log_dt), SSM_STATE, axis=1)
    par = jnp.stack([both(lam_re), both(lam_im), log_dt_rows], axis=1)
    par = jnp.concatenate([par, jnp.zeros((g, LANES - 3, p2), F32)], axis=1)
    rows_b = lambda a: jnp.transpose(a, (1, 0, 2, 3)).reshape(g, p2, SSM_GROUP)
    rows_c = lambda a: jnp.transpose(a, (1, 0, 3, 2)).reshape(g, p2, SSM_GROUP)
    b_cat = jnp.concatenate([rows_b(b_re), rows_b(b_im)], axis=-1)
    c_cat = jnp.concatenate([rows_c(c_re), rows_c(c_im)], axis=-1)

    col = np.arange(CHUNK_COLS)
    tile16 = (col[None, :] % SSM_GROUP == np.arange(SSM_GROUP)[:, None]).astype(np.float32)
    pos = col // SSM_GROUP
    k_idx = np.arange(LANES)[:, None]
    expand = np.stack([k_idx == pos[None, :], k_idx == (CHUNK - 1 - pos)[None, :],
                       k_idx == (pos + 1)[None, :], k_idx == (CHUNK - pos)[None, :]]).astype(np.float32)

    gp = S5_PREP_GROUPS
    mat = lambda: pl.BlockSpec((gp, CHUNK_COLS, CHUNK_COLS), lambda i: (i, 0, 0))
    mat_shape = jax.ShapeDtypeStruct((g, CHUNK_COLS, CHUNK_COLS), BF16)
    return pl.pallas_call(
        _s5_prep_kernel,
        grid=(g // gp,),
        in_specs=[pl.BlockSpec((gp, LANES, p2), lambda i: (i, 0, 0)),
                  pl.BlockSpec((gp, p2, 2 * SSM_GROUP), lambda i: (i, 0, 0)),
                  pl.BlockSpec((gp, p2, 2 * SSM_GROUP), lambda i: (i, 0, 0)),
                  pl.BlockSpec((SSM_GROUP, CHUNK_COLS), lambda i: (0, 0)),
                  pl.BlockSpec((4, LANES, CHUNK_COLS), lambda i: (0, 0, 0))],
        out_specs=[mat(), mat(), mat(), pl.BlockSpec((gp, 8, 2 * LANES), lambda i: (i, 0, 0))],
        out_shape=[mat_shape, mat_shape, mat_shape, jax.ShapeDtypeStruct((g, 8, 2 * LANES), F32)],
        name="s5_prep",
    )(par, b_cat, c_cat, jnp.asarray(tile16, BF16), jnp.asarray(expand, BF16))


def _s5_kernel(ul_ref, uc_ref, ms_ref, mi_ref, mo_ref, a_ref, y_ref, s_ref, sc_ref, hp_ref, *, gb):
    for gi in range(gb):
        for b in range(BATCH):
            sb = jnp.dot(ul_ref[gi, b], ms_ref[gi], preferred_element_type=F32)
            s_ref[gi, 0, pl.ds(b, N_CHUNKS, stride=BATCH), :] = sb[:, :LANES]
            s_ref[gi, 1, pl.ds(b, N_CHUNKS, stride=BATCH), :] = sb[:, LANES:]
            cb = jnp.dot(uc_ref[gi, b], ms_ref[gi], preferred_element_type=F32)
            sc_ref[gi, 0, pl.ds(b, N_CTX_CHUNKS, stride=BATCH), :] = cb[:, :LANES]
            sc_ref[gi, 1, pl.ds(b, N_CTX_CHUNKS, stride=BATCH), :] = cb[:, LANES:]
    lane = lax.broadcasted_iota(jnp.int32, (BATCH, LANES), 1)
    fwd = lane < SSM_STATE
    half = SSM_STATE

    def advance(gi, h_re, h_im, row_f, row_b, src):
        s_re = jnp.where(fwd, src[gi, 0, pl.ds(row_f, BATCH), :], src[gi, 0, pl.ds(row_b, BATCH), :])
        s_im = jnp.where(fwd, src[gi, 1, pl.ds(row_f, BATCH), :], src[gi, 1, pl.ds(row_b, BATCH), :])
        a_re = a_ref[gi, :, 0:LANES]
        a_im = a_ref[gi, :, LANES:2 * LANES]
        n_re = a_re * h_re - a_im * h_im + s_re
        n_im = a_re * h_im + a_im * h_re + s_im
        return n_re, n_im

    def ctx_step(t, carry):
        row_f = pl.multiple_of(t * BATCH, BATCH)
        row_b = pl.multiple_of((N_CTX_CHUNKS - 1 - t) * BATCH, BATCH)
        return tuple(advance(gi, carry[gi][0], carry[gi][1], row_f, row_b, sc_ref) for gi in range(gb))

    def lat_step(t, carry):
        row_f = pl.multiple_of(t * BATCH, BATCH)
        row_b = pl.multiple_of((N_CHUNKS - 1 - t) * BATCH, BATCH)
        out = []
        for gi in range(gb):
            h_re, h_im = carry[gi]
            hp_ref[gi, 0, pl.ds(row_f, BATCH), 0:half] = h_re[:, 0:half]
            hp_ref[gi, 0, pl.ds(row_b, BATCH), half:2 * half] = h_re[:, half:]
            hp_ref[gi, 1, pl.ds(row_f, BATCH), 0:half] = h_im[:, 0:half]
            hp_ref[gi, 1, pl.ds(row_b, BATCH), half:2 * half] = h_im[:, half:]
            out.append(advance(gi, h_re, h_im, row_f, row_b, s_ref))
        return tuple(out)

    zero = jnp.zeros((BATCH, LANES), F32)
    carry = tuple((zero, zero) for _ in range(gb))
    carry = lax.fori_loop(0, N_CTX_CHUNKS, ctx_step, carry)
    lax.fori_loop(0, N_CHUNKS, lat_step, carry)
    for gi in range(gb):
        for b in range(BATCH):
            hb_re = hp_ref[gi, 0, pl.ds(b, N_CHUNKS, stride=BATCH), :].astype(BF16)
            hb_im = hp_ref[gi, 1, pl.ds(b, N_CHUNKS, stride=BATCH), :].astype(BF16)
            y = (jnp.dot(ul_ref[gi, b], mi_ref[gi], preferred_element_type=F32)
                 + jnp.dot(jnp.concatenate([hb_re, hb_im], axis=1), mo_ref[gi], preferred_element_type=F32))
            y_ref[gi, b] = y.astype(BF16)


def _s5_scan(u_lat_t, u_ctx_t, m_intra, m_state, m_out, a16):
    gb = 4
    rows = N_CHUNKS * BATCH
    crows = N_CTX_CHUNKS * BATCH
    grp = lambda r, c: pl.BlockSpec((gb, r, c), lambda g: (g, 0, 0))
    tok = lambda n: pl.BlockSpec((gb, BATCH, n, CHUNK_COLS), lambda g: (g, 0, 0, 0))
    return pl.pallas_call(
        functools.partial(_s5_kernel, gb=gb),
        grid=(SSM_GROUPS // gb,),
        in_specs=[tok(N_CHUNKS), tok(N_CTX_CHUNKS), grp(CHUNK_COLS, CHUNK_COLS),
                  grp(CHUNK_COLS, CHUNK_COLS), grp(CHUNK_COLS, CHUNK_COLS), grp(8, 2 * LANES)],
        out_specs=tok(N_CHUNKS),
        out_shape=jax.ShapeDtypeStruct((SSM_GROUPS, BATCH, N_CHUNKS, CHUNK_COLS), BF16),
        scratch_shapes=[pltpu.VMEM((gb, 2, rows, LANES), F32),
                        pltpu.VMEM((gb, 2, crows, LANES), F32),
                        pltpu.VMEM((gb, 2, rows, LANES), F32)],
        compiler_params=pltpu.CompilerParams(
            dimension_semantics=("arbitrary",), vmem_limit_bytes=VMEM_LIMIT),
        name="s5_scan",
    )(u_lat_t, u_ctx_t, m_state, m_intra, m_out, a16)


FFN_TILE = 256


def _post_kernel(x_ref, a_ref, yt_ref, u_ref, ga_ref, gs_ref, mod_ref, d_ref, fg_ref, og_ref, permt_ref,
                 wglu_ref, wba_ref, wbs_ref, wout_ref, wfi_ref, wfo_ref, o_ref, h1_ref, n2_ref, act_ref, r_ref):
    dm = D_MODEL
    g1 = mod_ref[0, :, 2 * dm:3 * dm]
    sh2 = mod_ref[0, :, 3 * dm:4 * dm]
    sc2 = mod_ref[0, :, 4 * dm:5 * dm]
    g2 = mod_ref[0, :, 5 * dm:6 * dm]
    half = CHUNK * CHUNK
    halves = [slice(h * half, (h + 1) * half) for h in range(x_ref.shape[1] // half)]
    dot = functools.partial(jnp.dot, preferred_element_type=F32)

    sp = []
    for h, rows in enumerate(halves):
        y = _load_chunk_layout(yt_ref, permt_ref, r_ref, h)
        sp.append(jax.nn.gelu(y + d_ref[...] * u_ref[0, rows, :].astype(F32)).astype(BF16))
    s = []
    for h, rows in enumerate(halves):
        vg = dot(sp[h], wglu_ref[...])
        s.append((vg[:, :SSM_WIDTH] * jax.nn.sigmoid(vg[:, SSM_WIDTH:])).astype(BF16))
    merged = []
    for h, rows in enumerate(halves):
        a = jnp.concatenate([a_ref[0, j, rows, :] for j in range(ATTN_WIDTH // LANES)], axis=1)
        m = (ga_ref[0, rows, :].astype(F32) * dot(a, wba_ref[...])
             + gs_ref[0, rows, :].astype(F32) * dot(s[h], wbs_ref[...]))
        merged.append(m.astype(BF16))
    for h, rows in enumerate(halves):
        h1 = x_ref[0, rows, :] + g1 * dot(merged[h], wout_ref[...])
        h1_ref[rows, :] = h1
        n2_ref[rows, :] = _rms_modulate(h1, fg_ref[...], sh2, sc2).astype(BF16)
    for lo in range(0, FFN_HIDDEN, FFN_TILE):
        width = min(FFN_TILE, FFN_HIDDEN - lo)
        for rows in halves:
            n2 = n2_ref[rows, :]
            fa = dot(n2, wfi_ref[:, lo:lo + width])
            fb = dot(n2, wfi_ref[:, FFN_HIDDEN + lo:FFN_HIDDEN + lo + width])
            act_ref[rows, lo:lo + width] = (fa * jax.nn.sigmoid(fa) * fb).astype(BF16)
    for rows in halves:
        h2 = h1_ref[rows, :] + g2 * dot(act_ref[rows, :], wfo_ref[...])
        o_ref[0, rows, :] = (h2 * lax.rsqrt(jnp.mean(h2 * h2, axis=-1, keepdims=True) + NORM_EPS)) * og_ref[...]


def _post(x, a, y_t, u, ga, gs, mod3, d_skip, ffn_g, fin_g, perm_t, wglu, wba, wbs, wout, wfi, wfo):
    tm = 512
    n = CHUNK * CHUNK
    tok = lambda width: pl.BlockSpec((1, tm, width), lambda b, i: (b, i, 0))
    const = lambda r, c: pl.BlockSpec((r, c), lambda b, i: (0, 0), pipeline_mode=pl.Buffered(1))
    return pl.pallas_call(
        _post_kernel,
        grid=(BATCH, SEQ // tm),
        in_specs=[tok(D_MODEL),
                  pl.BlockSpec((1, N_HEADS // 2, tm, LANES), lambda b, i: (b, 0, i, 0)),
                  pl.BlockSpec((SSM_GROUPS, 1, tm // CHUNK, CHUNK_COLS), lambda b, i: (0, b, i, 0)),
                  tok(SSM_WIDTH), tok(D_MODEL), tok(D_MODEL),
                  pl.BlockSpec((1, 1, N_MOD * D_MODEL), lambda b, i: (b, 0, 0)),
                  const(1, SSM_WIDTH), const(1, D_MODEL), const(1, D_MODEL), const(n, n),
                  const(SSM_WIDTH, 2 * SSM_WIDTH), const(ATTN_WIDTH, D_MODEL),
                  const(SSM_WIDTH, D_MODEL), const(D_MODEL, D_MODEL),
                  const(D_MODEL, 2 * FFN_HIDDEN), const(FFN_HIDDEN, D_MODEL)],
        out_specs=tok(D_MODEL),
        out_shape=jax.ShapeDtypeStruct((BATCH, SEQ, D_MODEL), F32),
        scratch_shapes=[pltpu.VMEM((tm, D_MODEL), F32), pltpu.VMEM((tm, D_MODEL), BF16),
                        pltpu.VMEM((tm, FFN_HIDDEN), BF16), pltpu.VMEM((tm // n, n, SSM_WIDTH), F32)],
        compiler_params=pltpu.CompilerParams(
            dimension_semantics=("arbitrary", "arbitrary"), vmem_limit_bytes=VMEM_LIMIT),
        name="post",
    )(x, a, y_t, u, ga, gs, mod3, d_skip, ffn_g, fin_g, perm_t, wglu, wba, wbs, wout, wfi, wfo)


def kernel(x, c, ctx, c_ctx, w_mod, b_mod, attn_norm_g, ffn_norm_g, w_in, rel_pos_bias,
           ssm_lambda_re, ssm_lambda_im, ssm_log_dt, ssm_b_re, ssm_b_im, ssm_c_re, ssm_c_im, ssm_d,
           w_glu, w_branch_attn, w_branch_ssm, w_out, w_ffn_in, w_ffn_out, final_norm_g):
    assert x.shape == (BATCH, SEQ, D_MODEL) and w_mod.shape[0] == 1
    c_rows = jnp.concatenate(
        [c, c_ctx[None, :], jnp.zeros((MOD_ROWS - BATCH - 1, D_MODEL), F32)], axis=0)
    mod3 = _modulation(c_rows, w_mod[0], b_mod[0]).reshape(MOD_ROWS, 1, N_MOD * D_MODEL)

    col_scale = jnp.concatenate([jnp.full((ATTN_WIDTH,), HEAD_DIM ** -0.5 * LOG2E, F32),
                                 jnp.ones((IN_COLS - ATTN_WIDTH,), F32)])
    w_in_bf16 = (w_in[0] * col_scale[None, :]).astype(BF16)
    norm_g = attn_norm_g[0].reshape(1, D_MODEL)

    perm = _chunk_perm()
    q, k, v, u, u_t, ga, gs = _input_projection(x, mod3, norm_g, w_in_bf16, _rope_tables(), perm)
    kc, vc, uc_t = _context_projection(ctx, mod3, norm_g, w_in_bf16, perm)

    attn = _attention(q, k, v, kc, vc, _bias_tables(rel_pos_bias[0]))

    m_intra, m_state, m_out, a16 = _s5_matrices(
        ssm_lambda_re[0], ssm_lambda_im[0], ssm_log_dt[0], ssm_b_re[0], ssm_b_im[0],
        ssm_c_re[0], ssm_c_im[0])
    y_t = _s5_scan(u_t, uc_t, m_intra, m_state, m_out, a16)

    return _post(x, attn, y_t, u, ga, gs, mod3,
                 ssm_d[0].reshape(1, SSM_WIDTH), ffn_norm_g[0].reshape(1, D_MODEL),
                 final_norm_g.reshape(1, D_MODEL), perm,
                 w_glu[0].astype(BF16), w_branch_attn[0].astype(BF16), w_branch_ssm[0].astype(BF16),
                 w_out[0].astype(BF16), w_ffn_in[0].astype(BF16), w_ffn_out[0].astype(BF16))
```

```python
import functools
import math

import numpy as np
import jax
import jax.numpy as jnp
from jax import lax
from jax.experimental import pallas as pl
from jax.experimental.pallas import tpu as pltpu

F32 = jnp.float32
BF16 = jnp.bfloat16

D_MODEL = 1024
BATCH = 8
SEQ = 4096
GRID_W = 64
GRID_ROWS = SEQ // GRID_W
CTX_LEN = 256
N_HEADS = 8
HEAD_DIM = 64
ATTN_WIDTH = N_HEADS * HEAD_DIM
WIN_H = 8
WIN_W = 16
ROPE_BASE = 10000.0
SSM_WIDTH = 512
SSM_GROUP = 16
SSM_GROUPS = SSM_WIDTH // SSM_GROUP
SSM_STATE = 64
FFN_HIDDEN = 2816
IN_COLS = 3 * ATTN_WIDTH + SSM_WIDTH + 2 * D_MODEL
N_MOD = 6
NORM_EPS = 1e-6
NEG_BIG = -1e30
LOG2E = math.log2(math.e)

LANES = 128
CHUNK = 16
N_CHUNKS = SEQ // CHUNK
N_CTX_CHUNKS = CTX_LEN // CHUNK
CHUNK_COLS = CHUNK * SSM_GROUP
MOD_ROWS = 16
CTX_MOD_ROW = BATCH
VMEM_LIMIT = 56 * 1024 * 1024

Q_ROWS_PER_STEP = 4
Q_BLOCK = Q_ROWS_PER_STEP * GRID_W
K_ROWS_PER_STEP = Q_ROWS_PER_STEP + WIN_H
K_BLOCK = K_ROWS_PER_STEP * GRID_W
N_QBLOCKS = GRID_ROWS // Q_ROWS_PER_STEP
KEY_GROUP = (WIN_H // 2) * GRID_W
SOFTMAX_ROWS = 32


def _rms_modulate(x, g, shift, scale):
    xn = x * lax.rsqrt(jnp.mean(x * x, axis=-1, keepdims=True) + NORM_EPS)
    return (xn * g) * (1.0 + scale) + shift


def _block_transpose8(vs, lane):
    for shift in (64, 32, 16):
        keep = (lane & (2 * shift - 1)) < shift
        dist = shift // SSM_GROUP
        out = list(vs)
        for a in range(8):
            if a & dist:
                continue
            b = a + dist
            out[a] = jnp.where(keep, vs[a], pltpu.roll(vs[b], shift, 1))
            out[b] = jnp.where(keep, pltpu.roll(vs[a], LANES - shift, 1), vs[b])
        vs = out
    return vs


def _chunk_perm():
    n = CHUNK * CHUNK
    r = np.arange(n)
    m = np.zeros((n, n), np.float32)
    m[r, (r % CHUNK) * CHUNK + r // CHUNK] = 1.0
    return jnp.asarray(m, BF16)


def _store_chunk_layout(u16, perm_ref, out_ref, n_groups16, group0=0):
    lane = lax.broadcasted_iota(jnp.int32, (1, LANES), 1)
    n = CHUNK * CHUNK
    for hf in range(n_groups16):
        r = jnp.dot(perm_ref[...], u16[hf * n:(hf + 1) * n, :], preferred_element_type=F32)
        c0 = (group0 + hf) * CHUNK
        for v in range(SSM_WIDTH // LANES):
            for jh in range(2):
                vs = [r[CHUNK * (8 * jh + jp):CHUNK * (8 * jh + jp + 1), v * LANES:(v + 1) * LANES]
                      for jp in range(8)]
                outs = _block_transpose8(vs, lane)
                for gi in range(8):
                    out_ref[8 * v + gi, 0, c0:c0 + CHUNK, jh * LANES:(jh + 1) * LANES] = outs[gi].astype(BF16)


def _load_chunk_layout(yt_ref, perm_t_ref, r_ref, hf):
    lane = lax.broadcasted_iota(jnp.int32, (1, LANES), 1)
    for v in range(SSM_WIDTH // LANES):
        for jh in range(2):
            vs = [yt_ref[8 * v + gi, 0, hf * CHUNK:(hf + 1) * CHUNK,
                         jh * LANES:(jh + 1) * LANES].astype(F32) for gi in range(8)]
            outs = _block_transpose8(vs, lane)
            for jp in range(8):
                r_ref[hf, CHUNK * (8 * jh + jp):CHUNK * (8 * jh + jp + 1), v * LANES:(v + 1) * LANES] = outs[jp]
    return jnp.dot(perm_t_ref[...], r_ref[hf].astype(BF16), preferred_element_type=F32)


def _mod_kernel(c_ref, w_ref, b_ref, o_ref):
    c = c_ref[...]
    s = c * jax.nn.sigmoid(c)
    o_ref[...] = jnp.dot(s, w_ref[...], preferred_element_type=F32) + b_ref[...]


def _modulation(c_rows, w_mod, b_mod):
    n = N_MOD * D_MODEL
    tn = 1536
    return pl.pallas_call(
        _mod_kernel,
        grid=(n // tn,),
        in_specs=[pl.BlockSpec((MOD_ROWS, D_MODEL), lambda j: (0, 0)),
                  pl.BlockSpec((D_MODEL, tn), lambda j: (0, j)),
                  pl.BlockSpec((1, tn), lambda j: (0, j))],
        out_specs=pl.BlockSpec((MOD_ROWS, tn), lambda j: (0, j)),
        out_shape=jax.ShapeDtypeStruct((MOD_ROWS, n), F32),
        name="modulation",
    )(c_rows, w_mod, b_mod.reshape(1, n))


def _rope_tables():
    n_freq = HEAD_DIM // 4
    inv_freq = ROPE_BASE ** (-np.arange(n_freq, dtype=np.float64) / n_freq)
    t = np.arange(SEQ)
    lane = np.arange(LANES)
    d = lane % HEAD_DIM
    use_col = (d // (HEAD_DIM // 2)) == 1
    w = d % (HEAD_DIM // 2)
    first = w < n_freq
    pos = np.where(use_col[None, :], (t % GRID_W)[:, None], (t // GRID_W)[:, None]).astype(np.float64)
    ang = pos * inv_freq[w % n_freq][None, :]
    cos = np.cos(ang)
    sin = np.sin(ang)
    sin_a = np.where(first[None, :], -sin, 0.0)
    sin_b = np.where(first[None, :], 0.0, sin)
    return (jnp.asarray(cos, F32), jnp.asarray(sin_a, F32), jnp.asarray(sin_b, F32))


def _rope_store(r, cos, sin_a, sin_b, out_ref, row0, transposed=False):
    n_rows = r.shape[0]
    for j in range(ATTN_WIDTH // LANES):
        xs = r[:, j * LANES:(j + 1) * LANES]
        rot = (xs * cos + pltpu.roll(xs, LANES - HEAD_DIM // 4, 1) * sin_a
               + pltpu.roll(xs, HEAD_DIM // 4, 1) * sin_b)
        if transposed:
            rot_t = rot.T.astype(BF16)
            for gi in range(n_rows // KEY_GROUP):
                out_ref[0, row0 // KEY_GROUP + gi, j * LANES:(j + 1) * LANES, :] = (
                    rot_t[:, gi * KEY_GROUP:(gi + 1) * KEY_GROUP])
        else:
            out_ref[0, j, row0:row0 + n_rows, :] = rot.astype(BF16)


INPROJ_SUBTILE = 512


def _inproj_kernel(x_ref, mod_ref, g_ref, w_ref, cos_ref, sa_ref, sb_ref, perm_ref,
                   q_ref, k_ref, v_ref, u_ref, ut_ref, ga_ref, gs_ref):
    shift = mod_ref[0, :, 0:D_MODEL]
    scale = mod_ref[0, :, D_MODEL:2 * D_MODEL]
    aw = ATTN_WIDTH
    dot = functools.partial(jnp.dot, preferred_element_type=F32)
    for row0 in range(0, x_ref.shape[1], INPROJ_SUBTILE):
        rows = slice(row0, row0 + INPROJ_SUBTILE)
        nb = _rms_modulate(x_ref[0, rows, :], g_ref[...], shift, scale).astype(BF16)
        cos = cos_ref[rows, :]
        sin_a = sa_ref[rows, :]
        sin_b = sb_ref[rows, :]
        _rope_store(dot(nb, w_ref[:, 0:aw]), cos, sin_a, sin_b, q_ref, row0)
        _rope_store(dot(nb, w_ref[:, aw:2 * aw]), cos, sin_a, sin_b, k_ref, row0, transposed=True)
        v = dot(nb, w_ref[:, 2 * aw:3 * aw]).astype(BF16)
        for j in range(aw // LANES):
            v_ref[0, j, rows, :] = v[:, j * LANES:(j + 1) * LANES]
        c0 = 3 * aw
        u16 = dot(nb, w_ref[:, c0:c0 + SSM_WIDTH]).astype(BF16)
        u_ref[0, rows, :] = u16
        n16 = CHUNK * CHUNK
        _store_chunk_layout(u16, perm_ref, ut_ref, INPROJ_SUBTILE // n16, row0 // n16)
        c1 = c0 + SSM_WIDTH
        ga_ref[0, rows, :] = jax.nn.sigmoid(dot(nb, w_ref[:, c1:c1 + D_MODEL])).astype(BF16)
        c2 = c1 + D_MODEL
        gs_ref[0, rows, :] = jax.nn.sigmoid(dot(nb, w_ref[:, c2:c2 + D_MODEL])).astype(BF16)


def _input_projection(x, mod3, norm_g, w_in_bf16, rope, perm):
    tm = 2 * INPROJ_SUBTILE
    n = CHUNK * CHUNK
    cos, sin_a, sin_b = rope
    tok = lambda width: pl.BlockSpec((1, tm, width), lambda i, b: (b, i, 0))
    tab = pl.BlockSpec((tm, LANES), lambda i, b: (i, 0))
    out = lambda width: jax.ShapeDtypeStruct((BATCH, SEQ, width), BF16)
    pairs = pl.BlockSpec((1, N_HEADS // 2, tm, LANES), lambda i, b: (b, 0, i, 0))
    pairs_shape = jax.ShapeDtypeStruct((BATCH, N_HEADS // 2, SEQ, LANES), BF16)
    return pl.pallas_call(
        _inproj_kernel,
        grid=(SEQ // tm, BATCH),
        in_specs=[tok(D_MODEL),
                  pl.BlockSpec((1, 1, N_MOD * D_MODEL), lambda i, b: (b, 0, 0)),
                  pl.BlockSpec((1, D_MODEL), lambda i, b: (0, 0)),
                  pl.BlockSpec((D_MODEL, IN_COLS), lambda i, b: (0, 0), pipeline_mode=pl.Buffered(1)),
                  tab, tab, tab,
                  pl.BlockSpec((n, n), lambda i, b: (0, 0), pipeline_mode=pl.Buffered(1))],
        out_specs=[pairs,
                   pl.BlockSpec((1, tm // KEY_GROUP, ATTN_WIDTH, KEY_GROUP), lambda i, b: (b, i, 0, 0)),
                   pairs, tok(SSM_WIDTH),
                   pl.BlockSpec((SSM_GROUPS, 1, tm // CHUNK, CHUNK_COLS), lambda i, b: (0, b, i, 0)),
                   tok(D_MODEL), tok(D_MODEL)],
        out_shape=[pairs_shape,
                   jax.ShapeDtypeStruct((BATCH, SEQ // KEY_GROUP, ATTN_WIDTH, KEY_GROUP), BF16),
                   pairs_shape, out(SSM_WIDTH),
                   jax.ShapeDtypeStruct((SSM_GROUPS, BATCH, N_CHUNKS, CHUNK_COLS), BF16),
                   out(D_MODEL), out(D_MODEL)],
        compiler_params=pltpu.CompilerParams(
            dimension_semantics=("arbitrary", "arbitrary"), vmem_limit_bytes=VMEM_LIMIT),
        name="input_projection",
    )(x, mod3, norm_g, w_in_bf16, cos, sin_a, sin_b, perm)


def _ctx_proj_kernel(x_ref, mod_ref, g_ref, wk_ref, wv_ref, wu_ref, perm_ref, k_ref, v_ref, ut_ref):
    x = x_ref[0]
    shift = mod_ref[0, :, 0:D_MODEL]
    scale = mod_ref[0, :, D_MODEL:2 * D_MODEL]
    nb = _rms_modulate(x, g_ref[...], shift, scale).astype(BF16)
    k_ref[0] = jnp.dot(nb, wk_ref[...], preferred_element_type=F32).T.astype(BF16)
    v_ref[0] = jnp.dot(nb, wv_ref[...], preferred_element_type=F32).astype(BF16)
    u16 = jnp.dot(nb, wu_ref[...], preferred_element_type=F32).astype(BF16)
    _store_chunk_layout(u16, perm_ref, ut_ref, 1)


def _context_projection(ctx, mod3, norm_g, w_in_bf16, perm):
    n = CHUNK * CHUNK
    aw = ATTN_WIDTH
    tok = lambda width: pl.BlockSpec((1, CTX_LEN, width), lambda b: (b, 0, 0))
    wcol = lambda j: pl.BlockSpec((D_MODEL, aw), lambda b: (0, j))
    out = jax.ShapeDtypeStruct((BATCH, CTX_LEN, aw), BF16)
    return pl.pallas_call(
        _ctx_proj_kernel,
        grid=(BATCH,),
        in_specs=[tok(D_MODEL),
                  pl.BlockSpec((1, 1, N_MOD * D_MODEL), lambda b: (CTX_MOD_ROW, 0, 0)),
                  pl.BlockSpec((1, D_MODEL), lambda b: (0, 0)),
                  wcol(1), wcol(2), wcol(3),
                  pl.BlockSpec((n, n), lambda b: (0, 0))],
        out_specs=[pl.BlockSpec((1, aw, CTX_LEN), lambda b: (b, 0, 0)), tok(aw),
                   pl.BlockSpec((SSM_GROUPS, 1, N_CTX_CHUNKS, CHUNK_COLS), lambda b: (0, b, 0, 0))],
        out_shape=[jax.ShapeDtypeStruct((BATCH, aw, CTX_LEN), BF16), out,
                   jax.ShapeDtypeStruct((SSM_GROUPS, BATCH, N_CTX_CHUNKS, CHUNK_COLS), BF16)],
        name="context_projection",
    )(ctx, mod3, norm_g, w_in_bf16, w_in_bf16, w_in_bf16, perm)


def _window_start_rows(r):
    return min(max(r - WIN_H // 2, 0), GRID_ROWS - WIN_H)


def _key_block_row(jb):
    return min(max(Q_ROWS_PER_STEP * jb - WIN_H // 2, 0), GRID_ROWS - K_ROWS_PER_STEP)


def _window_geometry(jb):
    key_row0 = _key_block_row(jb)
    offs, deltas = [], []
    for i in range(Q_ROWS_PER_STEP):
        r = Q_ROWS_PER_STEP * jb + i
        rs = _window_start_rows(r)
        offs.append(rs - key_row0)
        deltas.append(r - rs)
    return offs, deltas


ATTN_SAMPLES_PER_STEP = 2


def _attn_kernel(*refs):
    def per_sample(bi, carry):
        _attn_sample(bi, *refs)
        return carry

    lax.fori_loop(0, ATTN_SAMPLES_PER_STEP, per_sample, 0)


def _attn_sample(bi, q_ref, k_ref, v_ref, kc_ref, vc_ref, bias_ref, o_ref,
                 s_ref, sc_ref, p_ref, pc_ref, l_ref, m_ref, acc_ref):
    lane = lax.broadcasted_iota(jnp.int32, (1, LANES), 1)
    left = lane < HEAD_DIM
    n_ktiles = K_BLOCK // LANES
    last = N_QBLOCKS - 1
    max_key_start = (GRID_ROWS - K_ROWS_PER_STEP) * GRID_W

    def scores(e, q_start, key_start):
        q2 = q_ref[bi, 0, pl.ds(q_start, Q_BLOCK), :]
        qm = jnp.where(left if e == 0 else jnp.logical_not(left), q2, jnp.zeros_like(q2))
        group0 = key_start // KEY_GROUP
        for gi in range(K_BLOCK // KEY_GROUP):
            s_ref[e, :, gi * KEY_GROUP:(gi + 1) * KEY_GROUP] = jnp.dot(
                qm, k_ref[bi, group0 + gi], preferred_element_type=F32)
        sc_ref[e] = jnp.dot(qm, kc_ref[bi], preferred_element_type=F32)

    def softmax(e, offs, deltas):
        def geometry(sub):
            i = sub * SOFTMAX_ROWS // GRID_W
            qcols = slice(sub * SOFTMAX_ROWS % GRID_W, sub * SOFTMAX_ROWS % GRID_W + SOFTMAX_ROWS)
            rows = slice(sub * SOFTMAX_ROWS, (sub + 1) * SOFTMAX_ROWS)
            par = offs[i] % 2
            return rows, qcols, par, offs[i] // 2, WIN_H // 2 + par, deltas[i]

        def tile(rows, qcols, par, t0, n_tiles, delta, xt):
            t = t0 + xt
            dr0 = 2 * xt - par - delta
            st = s_ref[e, rows, t * LANES:(t + 1) * LANES] + bias_ref[0, e, dr0 + WIN_H, qcols, :]
            if par and xt == 0:
                st = jnp.where(left, NEG_BIG, st)
            if par and xt == n_tiles - 1:
                st = jnp.where(left, st, NEG_BIG)
            return st

        n_sub = Q_BLOCK // SOFTMAX_ROWS
        for sub in range(n_sub):
            geo = geometry(sub)
            rows, n_tiles = geo[0], geo[4]
            mt = jnp.maximum(sc_ref[e, rows, 0:LANES], sc_ref[e, rows, LANES:2 * LANES])
            for xt in range(n_tiles):
                mt = jnp.maximum(mt, tile(*geo, xt))
            m_ref[e, rows, :] = jnp.broadcast_to(jnp.max(mt, axis=1, keepdims=True), (SOFTMAX_ROWS, LANES))
        for sub in range(n_sub):
            geo = geometry(sub)
            rows, t0, n_tiles = geo[0], geo[3], geo[4]
            m = m_ref[e, rows, :]
            lt = None
            for xt in range(n_tiles):
                t = t0 + xt
                pt = jnp.exp2(tile(*geo, xt) - m)
                lt = pt if lt is None else lt + pt
                p_ref[e, rows, t * LANES:(t + 1) * LANES] = pt.astype(BF16)
            for t in range(n_ktiles):
                if not (t0 <= t < t0 + n_tiles):
                    p_ref[e, rows, t * LANES:(t + 1) * LANES] = jnp.zeros((SOFTMAX_ROWS, LANES), BF16)
            for ci in range(2):
                pt = jnp.exp2(sc_ref[e, rows, ci * LANES:(ci + 1) * LANES] - m)
                lt = lt + pt
                pc_ref[e, rows, ci * LANES:(ci + 1) * LANES] = pt.astype(BF16)
            l = jnp.sum(lt, axis=1, keepdims=True)
            l_ref[e, rows, :] = jnp.broadcast_to(1.0 / l, (SOFTMAX_ROWS, LANES))

    def values(e, key_start):
        vblk = v_ref[bi, 0, pl.ds(key_start, K_BLOCK), :]
        return (jnp.dot(p_ref[e], vblk, preferred_element_type=F32)
                + jnp.dot(pc_ref[e], vc_ref[bi], preferred_element_type=F32)) * l_ref[e]

    def block(jb_static, cur, nxt):
        offs, deltas = _window_geometry(jb_static)
        scores(1, *cur)
        softmax(0, offs, deltas)
        acc_ref[...] = values(0, cur[1])
        if nxt is not None:
            scores(0, *nxt)
        softmax(1, offs, deltas)
        o1 = values(1, cur[1])
        o_ref[bi, 0, pl.ds(cur[0], Q_BLOCK), :] = jnp.where(left, acc_ref[...], o1).astype(BF16)

    def starts(jb):
        q_start = pl.multiple_of(jb * Q_BLOCK, Q_BLOCK)
        key_start = jnp.clip((jb * Q_ROWS_PER_STEP - WIN_H // 2) * GRID_W, 0, max_key_start)
        return q_start, pl.multiple_of(key_start, KEY_GROUP)

    scores(0, 0, 0)
    block(0, (0, 0), (Q_BLOCK, max(Q_ROWS_PER_STEP - WIN_H // 2, 0) * GRID_W))

    blocks_per_trip = 7

    def interior(trip, carry):
        jb = 1 + blocks_per_trip * trip
        for d in range(blocks_per_trip):
            block(1, starts(jb + d), starts(jb + d + 1))
        return carry

    assert (last - 1) % blocks_per_trip == 0
    lax.fori_loop(0, (last - 1) // blocks_per_trip, interior, 0)
    block(last, (last * Q_BLOCK, max_key_start), None)


def _bias_tables(rpb):
    qcol = np.arange(GRID_W)
    kcol = np.arange(GRID_W)
    col_start = np.clip(qcol - WIN_W // 2, 0, GRID_W - WIN_W)
    in_win = (kcol[None, :] >= col_start[:, None]) & (kcol[None, :] < col_start[:, None] + WIN_W)
    dc_idx = np.clip(kcol[None, :] - qcol[:, None], -(WIN_W - 1), WIN_W - 1) + WIN_W - 1
    sel = (np.arange(2 * WIN_W - 1)[:, None, None] == dc_idx[None]).astype(np.float32)
    toe = jnp.einsum('hdt,tck->hdck', rpb * LOG2E, jnp.asarray(sel),
                     precision=lax.Precision.HIGHEST)
    toe = jnp.where(in_win[None, None], toe, NEG_BIG)
    neg = jnp.full((N_HEADS, 1, GRID_W, GRID_W), NEG_BIG, F32)
    ext = jnp.concatenate([neg, toe, neg], axis=1)
    pair = jnp.concatenate([ext[:, 0:16], ext[:, 1:17]], axis=-1)
    return pair.reshape(N_HEADS // 2, 2, 16, GRID_W, LANES)


def _attention(q, k, v, kc, vc, bias):
    n_slots = 2
    nb = ATTN_SAMPLES_PER_STEP
    kspec = pl.BlockSpec((nb, 1, SEQ, LANES), lambda hp, b: (b, hp, 0, 0))
    ktspec = pl.BlockSpec((nb, SEQ // KEY_GROUP, LANES, KEY_GROUP), lambda hp, b: (b, 0, hp, 0))
    cspec = pl.BlockSpec((nb, CTX_LEN, LANES), lambda hp, b: (b, 0, hp))
    ctspec = pl.BlockSpec((nb, LANES, CTX_LEN), lambda hp, b: (b, hp, 0))
    bspec = pl.BlockSpec((1, 2, 16, GRID_W, LANES), lambda hp, b: (hp, 0, 0, 0, 0))
    return pl.pallas_call(
        _attn_kernel,
        grid=(N_HEADS // 2, BATCH // nb),
        in_specs=[kspec, ktspec, kspec, ctspec, cspec, bspec],
        out_specs=kspec,
        out_shape=jax.ShapeDtypeStruct((BATCH, N_HEADS // 2, SEQ, LANES), BF16),
        scratch_shapes=[pltpu.VMEM((n_slots, Q_BLOCK, K_BLOCK), F32),
                        pltpu.VMEM((n_slots, Q_BLOCK, CTX_LEN), F32),
                        pltpu.VMEM((n_slots, Q_BLOCK, K_BLOCK), BF16),
                        pltpu.VMEM((n_slots, Q_BLOCK, CTX_LEN), BF16),
                        pltpu.VMEM((n_slots, Q_BLOCK, LANES), F32),
                        pltpu.VMEM((n_slots, Q_BLOCK, LANES), F32),
                        pltpu.VMEM((Q_BLOCK, LANES), F32)],
        compiler_params=pltpu.CompilerParams(
            dimension_semantics=("arbitrary", "arbitrary"),
            vmem_limit_bytes=VMEM_LIMIT),
        name="attention",
    )(q, k, v, kc, vc, bias)


def _rot256(a, b, s, lane):
    s %= 2 * LANES
    if s >= LANES:
        a, b, s = b, a, s - LANES
    if s == 0:
        return a, b
    ra = pltpu.roll(a, s, 1)
    rb = pltpu.roll(b, s, 1)
    keep = lane >= s
    return jnp.where(keep, ra, rb), jnp.where(keep, rb, ra)


S5_PREP_GROUPS = 2
POWER_ROWS = 24


def _s5_prep_kernel(*refs):
    for gi in range(S5_PREP_GROUPS):
        _s5_prep_group(gi, *refs)


def _s5_prep_group(gi, par_ref, b_ref, c_ref, t16_ref, e_ref, mi_ref, ms_ref, mo_ref, a_ref):
    dot = functools.partial(jnp.dot, preferred_element_type=F32)

    def split2(x):
        hi = x.astype(BF16)
        return hi, (x - hi.astype(F32)).astype(BF16)

    def split3(x):
        hi = x.astype(BF16)
        r1 = x - hi.astype(F32)
        mid = r1.astype(BF16)
        return hi, mid, (r1 - mid.astype(F32)).astype(BF16)

    def pick(x, onehot):
        hi, mid, lo = split3(x)
        return dot(hi, onehot) + dot(mid, onehot) + dot(lo, onehot)

    def dot_f32(a, b):
        ah, al = split2(a)
        bh, bl = split2(b)
        return dot(jnp.concatenate([ah, ah, al], axis=1), jnp.concatenate([bh, bl, bh], axis=0))

    ns = SSM_STATE
    lam_re, lam_im = par_ref[gi, 0:1, :], par_ref[gi, 1:2, :]
    dt = jnp.exp(par_ref[gi, 2:3, :])
    lane = lax.broadcasted_iota(jnp.int32, (1, LANES), 1)
    kf = lax.broadcasted_iota(jnp.int32, (POWER_ROWS, LANES), 0).astype(F32)
    mag = jnp.exp((lam_re * dt) * kf)
    ang = (lam_im * dt) * kf
    pwt_re, pwt_im = mag * jnp.cos(ang), mag * jnp.sin(ang)
    den = lam_re * lam_re + lam_im * lam_im
    nr, ni = pwt_re[1:2, :] - 1.0, pwt_im[1:2, :]
    f_re_row = (nr * lam_re + ni * lam_im) / den
    f_im_row = (ni * lam_re - nr * lam_im) / den
    r8 = lax.broadcasted_iota(jnp.int32, (8, LANES), 0)
    f_rows = jnp.where(r8 == 0, f_re_row, jnp.where(r8 == 1, f_im_row, 0.0))
    pad = jnp.zeros((LANES - POWER_ROWS - 8, LANES), F32)
    pw_re = jnp.concatenate([pwt_re, f_rows, pad], axis=0).T
    pw_im = jnp.concatenate([pwt_im, f_rows, pad], axis=0).T
    f_re, f_im = pw_re[:, POWER_ROWS:POWER_ROWS + 1], pw_re[:, POWER_ROWS + 1:POWER_ROWS + 2]
    b_re, b_im = b_ref[gi, :, 0:SSM_GROUP], b_ref[gi, :, SSM_GROUP:2 * SSM_GROUP]
    bb_re = f_re * b_re - f_im * b_im
    bb_im = f_re * b_im + f_im * b_re
    t16 = t16_ref[...]
    bbt_re, bbt_im = pick(bb_re, t16), pick(bb_im, t16)
    ct_re = pick(c_ref[gi, :, 0:SSM_GROUP], t16)
    ct_im = pick(c_ref[gi, :, SSM_GROUP:2 * SSM_GROUP], t16)
    pw_at = lambda x: (pick(pw_re, e_ref[x]), pick(pw_im, e_ref[x]))
    id_re, id_im = pw_at(0)
    rev_re, rev_im = pw_at(1)
    p1_re, p1_im = pw_at(2)
    r16_re, r16_im = pw_at(3)
    f, b = slice(0, ns), slice(ns, 2 * ns)
    cmul = lambda ar, ai, br, bi: (ar * br - ai * bi, ar * bi + ai * br)

    sf_re, sf_im = cmul(rev_re[f], rev_im[f], bbt_re[f], bbt_im[f])
    sb_re, sb_im = cmul(id_re[b], id_im[b], bbt_re[b], bbt_im[b])
    ms_ref[gi] = jnp.concatenate([sf_re, sb_re, sf_im, sb_im], axis=0).T.astype(BF16)

    of_re, of_im = cmul(p1_re[f], p1_im[f], ct_re[f], ct_im[f])
    ob_re, ob_im = cmul(r16_re[b], r16_im[b], ct_re[b], ct_im[b])
    mo_ref[gi] = jnp.concatenate([of_re, ob_re, -of_im, -ob_im], axis=0).astype(BF16)

    xf_re, xf_im = cmul(id_re[f], id_im[f], ct_re[f], ct_im[f])
    xb_re, xb_im = cmul(rev_re[b], rev_im[b], ct_re[b], ct_im[b])
    btf = jnp.concatenate([bbt_re[f], bbt_im[f]], axis=0).T
    btb = jnp.concatenate([bbt_re[b], bbt_im[b]], axis=0).T
    g_f = dot_f32(btf, jnp.concatenate([xf_re, -xf_im], axis=0))
    g_b = dot_f32(btb, jnp.concatenate([xb_re, -xb_im], axis=0))
    for j in range(CHUNK):
        rows = slice(SSM_GROUP * j, SSM_GROUP * (j + 1))
        lo_col, hi_col = SSM_GROUP * j, SSM_GROUP * (j + 1)
        f_lo, f_hi = _rot256(g_f[rows, :LANES], g_f[rows, LANES:], lo_col, lane)
        b_lo, b_hi = _rot256(g_b[rows, :LANES], g_b[rows, LANES:], -SSM_GROUP * (CHUNK - 1 - j), lane)
        lo = jnp.where(lane >= lo_col, f_lo, 0.0) + jnp.where(lane < hi_col, b_lo, 0.0)
        up = jnp.where(lane + LANES >= lo_col, f_hi, 0.0) + jnp.where(lane + LANES < hi_col, b_hi, 0.0)
        mi_ref[gi, rows, 0:LANES] = lo.astype(BF16)
        mi_ref[gi, rows, LANES:2 * LANES] = up.astype(BF16)

    a16 = jnp.concatenate([pwt_re[CHUNK:CHUNK + 1, :], pwt_im[CHUNK:CHUNK + 1, :]], axis=1)
    a_ref[gi] = jnp.broadcast_to(a16, (8, 2 * LANES))


def _s5_matrices(lam_re, lam_im, log_dt, b_re, b_im, c_re, c_im):
    g, p2 = SSM_GROUPS, 2 * SSM_STATE
    both = lambda a: jnp.transpose(a, (1, 0, 2)).reshape(g, p2)
    log_dt_rows = jnp.repeat(jnp.transpose(log_dt), SSM_STATE, axis=1)
    par = jnp.stack([both(lam_re), both(lam_im), log_dt_rows], axis=1)
    par = jnp.concatenate([par, jnp.zeros((g, LANES - 3, p2), F32)], axis=1)
    rows_b = lambda a: jnp.transpose(a, (1, 0, 2, 3)).reshape(g, p2, SSM_GROUP)
    rows_c = lambda a: jnp.transpose(a, (1, 0, 3, 2)).reshape(g, p2, SSM_GROUP)
    b_cat = jnp.concatenate([rows_b(b_re), rows_b(b_im)], axis=-1)
    c_cat = jnp.concatenate([rows_c(c_re), rows_c(c_im)], axis=-1)

    col = np.arange(CHUNK_COLS)
    tile16 = (col[None, :] % SSM_GROUP == np.arange(SSM_GROUP)[:, None]).astype(np.float32)
    pos = col // SSM_GROUP
    k_idx = np.arange(LANES)[:, None]
    expand = np.stack([k_idx == pos[None, :], k_idx == (CHUNK - 1 - pos)[None, :],
                       k_idx == (pos + 1)[None, :], k_idx == (CHUNK - pos)[None, :]]).astype(np.float32)

    gp = S5_PREP_GROUPS
    mat = lambda: pl.BlockSpec((gp, CHUNK_COLS, CHUNK_COLS), lambda i: (i, 0, 0))
    mat_shape = jax.ShapeDtypeStruct((g, CHUNK_COLS, CHUNK_COLS), BF16)
    return pl.pallas_call(
        _s5_prep_kernel,
        grid=(g // gp,),
        in_specs=[pl.BlockSpec((gp, LANES, p2), lambda i: (i, 0, 0)),
                  pl.BlockSpec((gp, p2, 2 * SSM_GROUP), lambda i: (i, 0, 0)),
                  pl.BlockSpec((gp, p2, 2 * SSM_GROUP), lambda i: (i, 0, 0)),
                  pl.BlockSpec((SSM_GROUP, CHUNK_COLS), lambda i: (0, 0)),
                  pl.BlockSpec((4, LANES, CHUNK_COLS), lambda i: (0, 0, 0))],
        out_specs=[mat(), mat(), mat(), pl.BlockSpec((gp, 8, 2 * LANES), lambda i: (i, 0, 0))],
        out_shape=[mat_shape, mat_shape, mat_shape, jax.ShapeDtypeStruct((g, 8, 2 * LANES), F32)],
        name="s5_prep",
    )(par, b_cat, c_cat, jnp.asarray(tile16, BF16), jnp.asarray(expand, BF16))


def _s5_kernel(ul_ref, uc_ref, ms_ref, mi_ref, mo_ref, a_ref, y_ref, s_ref, sc_ref, hp_ref, *, gb):
    for gi in range(gb):
        for b in range(BATCH):
            sb = jnp.dot(ul_ref[gi, b], ms_ref[gi], preferred_element_type=F32)
            s_ref[gi, 0, pl.ds(b, N_CHUNKS, stride=BATCH), :] = sb[:, :LANES]
            s_ref[gi, 1, pl.ds(b, N_CHUNKS, stride=BATCH), :] = sb[:, LANES:]
            cb = jnp.dot(uc_ref[gi, b], ms_ref[gi], preferred_element_type=F32)
            sc_ref[gi, 0, pl.ds(b, N_CTX_CHUNKS, stride=BATCH), :] = cb[:, :LANES]
            sc_ref[gi, 1, pl.ds(b, N_CTX_CHUNKS, stride=BATCH), :] = cb[:, LANES:]
    lane = lax.broadcasted_iota(jnp.int32, (BATCH, LANES), 1)
    fwd = lane < SSM_STATE
    half = SSM_STATE

    def advance(gi, h_re, h_im, row_f, row_b, src):
        s_re = jnp.where(fwd, src[gi, 0, pl.ds(row_f, BATCH), :], src[gi, 0, pl.ds(row_b, BATCH), :])
        s_im = jnp.where(fwd, src[gi, 1, pl.ds(row_f, BATCH), :], src[gi, 1, pl.ds(row_b, BATCH), :])
        a_re = a_ref[gi, :, 0:LANES]
        a_im = a_ref[gi, :, LANES:2 * LANES]
        n_re = a_re * h_re - a_im * h_im + s_re
        n_im = a_re * h_im + a_im * h_re + s_im
        return n_re, n_im

    def ctx_step(t, carry):
        row_f = pl.multiple_of(t * BATCH, BATCH)
        row_b = pl.multiple_of((N_CTX_CHUNKS - 1 - t) * BATCH, BATCH)
        return tuple(advance(gi, carry[gi][0], carry[gi][1], row_f, row_b, sc_ref) for gi in range(gb))

    def lat_step(t, carry):
        row_f = pl.multiple_of(t * BATCH, BATCH)
        row_b = pl.multiple_of((N_CHUNKS - 1 - t) * BATCH, BATCH)
        out = []
        for gi in range(gb):
            h_re, h_im = carry[gi]
            hp_ref[gi, 0, pl.ds(row_f, BATCH), 0:half] = h_re[:, 0:half]
            hp_ref[gi, 0, pl.ds(row_b, BATCH), half:2 * half] = h_re[:, half:]
            hp_ref[gi, 1, pl.ds(row_f, BATCH), 0:half] = h_im[:, 0:half]
            hp_ref[gi, 1, pl.ds(row_b, BATCH), half:2 * half] = h_im[:, half:]
            out.append(advance(gi, h_re, h_im, row_f, row_b, s_ref))
        return tuple(out)

    zero = jnp.zeros((BATCH, LANES), F32)
    carry = tuple((zero, zero) for _ in range(gb))
    carry = lax.fori_loop(0, N_CTX_CHUNKS, ctx_step, carry)
    lax.fori_loop(0, N_CHUNKS, lat_step, carry)
    for gi in range(gb):
        for b in range(BATCH):
            hb_re = hp_ref[gi, 0, pl.ds(b, N_CHUNKS, stride=BATCH), :].astype(BF16)
            hb_im = hp_ref[gi, 1, pl.ds(b, N_CHUNKS, stride=BATCH), :].astype(BF16)
            y = (jnp.dot(ul_ref[gi, b], mi_ref[gi], preferred_element_type=F32)
                 + jnp.dot(jnp.concatenate([hb_re, hb_im], axis=1), mo_ref[gi], preferred_element_type=F32))
            y_ref[gi, b] = y.astype(BF16)


def _s5_scan(u_lat_t, u_ctx_t, m_intra, m_state, m_out, a16):
    gb = 4
    rows = N_CHUNKS * BATCH
    crows = N_CTX_CHUNKS * BATCH
    grp = lambda r, c: pl.BlockSpec((gb, r, c), lambda g: (g, 0, 0))
    tok = lambda n: pl.BlockSpec((gb, BATCH, n, CHUNK_COLS), lambda g: (g, 0, 0, 0))
    return pl.pallas_call(
        functools.partial(_s5_kernel, gb=gb),
        grid=(SSM_GROUPS // gb,),
        in_specs=[tok(N_CHUNKS), tok(N_CTX_CHUNKS), grp(CHUNK_COLS, CHUNK_COLS),
                  grp(CHUNK_COLS, CHUNK_COLS), grp(CHUNK_COLS, CHUNK_COLS), grp(8, 2 * LANES)],
        out_specs=tok(N_CHUNKS),
        out_shape=jax.ShapeDtypeStruct((SSM_GROUPS, BATCH, N_CHUNKS, CHUNK_COLS), BF16),
        scratch_shapes=[pltpu.VMEM((gb, 2, rows, LANES), F32),
                        pltpu.VMEM((gb, 2, crows, LANES), F32),
                        pltpu.VMEM((gb, 2, rows, LANES), F32)],
        compiler_params=pltpu.CompilerParams(
            dimension_semantics=("arbitrary",), vmem_limit_bytes=VMEM_LIMIT),
        name="s5_scan",
    )(u_lat_t, u_ctx_t, m_state, m_intra, m_out, a16)


FFN_TILE = 256


def _post_kernel(x_ref, a_ref, yt_ref, u_ref, ga_ref, gs_ref, mod_ref, d_ref, fg_ref, og_ref, permt_ref,
                 wglu_ref, wba_ref, wbs_ref, wout_ref, wfi_ref, wfo_ref, o_ref, h1_ref, n2_ref, act_ref, r_ref):
    dm = D_MODEL
    g1 = mod_ref[0, :, 2 * dm:3 * dm]
    sh2 = mod_ref[0, :, 3 * dm:4 * dm]
    sc2 = mod_ref[0, :, 4 * dm:5 * dm]
    g2 = mod_ref[0, :, 5 * dm:6 * dm]
    half = CHUNK * CHUNK
    halves = [slice(h * half, (h + 1) * half) for h in range(x_ref.shape[1] // half)]
    dot = functools.partial(jnp.dot, preferred_element_type=F32)

    sp = []
    for h, rows in enumerate(halves):
        y = _load_chunk_layout(yt_ref, permt_ref, r_ref, h)
        sp.append(jax.nn.gelu(y + d_ref[...] * u_ref[0, rows, :].astype(F32)).astype(BF16))
    s = []
    for h, rows in enumerate(halves):
        vg = dot(sp[h], wglu_ref[...])
        s.append((vg[:, :SSM_WIDTH] * jax.nn.sigmoid(vg[:, SSM_WIDTH:])).astype(BF16))
    merged = []
    for h, rows in enumerate(halves):
        a = jnp.concatenate([a_ref[0, j, rows, :] for j in range(ATTN_WIDTH // LANES)], axis=1)
        m = (ga_ref[0, rows, :].astype(F32) * dot(a, wba_ref[...])
             + gs_ref[0, rows, :].astype(F32) * dot(s[h], wbs_ref[...]))
        merged.append(m.astype(BF16))
    for h, rows in enumerate(halves):
        h1 = x_ref[0, rows, :] + g1 * dot(merged[h], wout_ref[...])
        h1_ref[rows, :] = h1
        n2_ref[rows, :] = _rms_modulate(h1, fg_ref[...], sh2, sc2).astype(BF16)
    for lo in range(0, FFN_HIDDEN, FFN_TILE):
        width = min(FFN_TILE, FFN_HIDDEN - lo)
        for rows in halves:
            n2 = n2_ref[rows, :]
            fa = dot(n2, wfi_ref[:, lo:lo + width])
            fb = dot(n2, wfi_ref[:, FFN_HIDDEN + lo:FFN_HIDDEN + lo + width])
            act_ref[rows, lo:lo + width] = (fa * jax.nn.sigmoid(fa) * fb).astype(BF16)
    for rows in halves:
        h2 = h1_ref[rows, :] + g2 * dot(act_ref[rows, :], wfo_ref[...])
        o_ref[0, rows, :] = (h2 * lax.rsqrt(jnp.mean(h2 * h2, axis=-1, keepdims=True) + NORM_EPS)) * og_ref[...]


def _post(x, a, y_t, u, ga, gs, mod3, d_skip, ffn_g, fin_g, perm_t, wglu, wba, wbs, wout, wfi, wfo):
    tm = 512
    n = CHUNK * CHUNK
    tok = lambda width: pl.BlockSpec((1, tm, width), lambda b, i: (b, i, 0))
    const = lambda r, c: pl.BlockSpec((r, c), lambda b, i: (0, 0), pipeline_mode=pl.Buffered(1))
    return pl.pallas_call(
        _post_kernel,
        grid=(BATCH, SEQ // tm),
        in_specs=[tok(D_MODEL),
                  pl.BlockSpec((1, N_HEADS // 2, tm, LANES), lambda b, i: (b, 0, i, 0)),
                  pl.BlockSpec((SSM_GROUPS, 1, tm // CHUNK, CHUNK_COLS), lambda b, i: (0, b, i, 0)),
                  tok(SSM_WIDTH), tok(D_MODEL), tok(D_MODEL),
                  pl.BlockSpec((1, 1, N_MOD * D_MODEL), lambda b, i: (b, 0, 0)),
                  const(1, SSM_WIDTH), const(1, D_MODEL), const(1, D_MODEL), const(n, n),
                  const(SSM_WIDTH, 2 * SSM_WIDTH), const(ATTN_WIDTH, D_MODEL),
                  const(SSM_WIDTH, D_MODEL), const(D_MODEL, D_MODEL),
                  const(D_MODEL, 2 * FFN_HIDDEN), const(FFN_HIDDEN, D_MODEL)],
        out_specs=tok(D_MODEL),
        out_shape=jax.ShapeDtypeStruct((BATCH, SEQ, D_MODEL), F32),
        scratch_shapes=[pltpu.VMEM((tm, D_MODEL), F32), pltpu.VMEM((tm, D_MODEL), BF16),
                        pltpu.VMEM((tm, FFN_HIDDEN), BF16), pltpu.VMEM((tm // n, n, SSM_WIDTH), F32)],
        compiler_params=pltpu.CompilerParams(
            dimension_semantics=("arbitrary", "arbitrary"), vmem_limit_bytes=VMEM_LIMIT),
        name="post",
    )(x, a, y_t, u, ga, gs, mod3, d_skip, ffn_g, fin_g, perm_t, wglu, wba, wbs, wout, wfi, wfo)


def kernel(x, c, ctx, c_ctx, w_mod, b_mod, attn_norm_g, ffn_norm_g, w_in, rel_pos_bias,
           ssm_lambda_re, ssm_lambda_im, ssm_log_dt, ssm_b_re, ssm_b_im, ssm_c_re, ssm_c_im, ssm_d,
           w_glu, w_branch_attn, w_branch_ssm, w_out, w_ffn_in, w_ffn_out, final_norm_g):
    assert x.shape == (BATCH, SEQ, D_MODEL) and w_mod.shape[0] == 1
    c_rows = jnp.concatenate(
        [c, c_ctx[None, :], jnp.zeros((MOD_ROWS - BATCH - 1, D_MODEL), F32)], axis=0)
    mod3 = _modulation(c_rows, w_mod[0], b_mod[0]).reshape(MOD_ROWS, 1, N_MOD * D_MODEL)

    col_scale = jnp.concatenate([jnp.full((ATTN_WIDTH,), HEAD_DIM ** -0.5 * LOG2E, F32),
                                 jnp.ones((IN_COLS - ATTN_WIDTH,), F32)])
    w_in_bf16 = (w_in[0] * col_scale[None, :]).astype(BF16)
    norm_g = attn_norm_g[0].reshape(1, D_MODEL)

    perm = _chunk_perm()
    q, k, v, u, u_t, ga, gs = _input_projection(x, mod3, norm_g, w_in_bf16, _rope_tables(), perm)
    kc, vc, uc_t = _context_projection(ctx, mod3, norm_g, w_in_bf16, perm)

    attn = _attention(q, k, v, kc, vc, _bias_tables(rel_pos_bias[0]))

    m_intra, m_state, m_out, a16 = _s5_matrices(
        ssm_lambda_re[0], ssm_lambda_im[0], ssm_log_dt[0], ssm_b_re[0], ssm_b_im[0],
        ssm_c_re[0], ssm_c_im[0])
    y_t = _s5_scan(u_t, uc_t, m_intra, m_state, m_out, a16)

    return _post(x, attn, y_t, u, ga, gs, mod3,
                 ssm_d[0].reshape(1, SSM_WIDTH), ffn_norm_g[0].reshape(1, D_MODEL),
                 final_norm_g.reshape(1, D_MODEL), perm,
                 w_glu[0].astype(BF16), w_branch_attn[0].astype(BF16), w_branch_ssm[0].astype(BF16),
                 w_out[0].astype(BF16), w_ffn_in[0].astype(BF16), w_ffn_out[0].astype(BF16))
```

```python
import functools
import math

import numpy as np
import jax
import jax.numpy as jnp
from jax import lax
from jax.experimental import pallas as pl
from jax.experimental.pallas import tpu as pltpu

F32 = jnp.float32
BF16 = jnp.bfloat16

D_MODEL = 1024
BATCH = 8
SEQ = 4096
GRID_W = 64
GRID_ROWS = SEQ // GRID_W
CTX_LEN = 256
N_HEADS = 8
HEAD_DIM = 64
ATTN_WIDTH = N_HEADS * HEAD_DIM
WIN_H = 8
WIN_W = 16
ROPE_BASE = 10000.0
SSM_WIDTH = 512
SSM_GROUP = 16
SSM_GROUPS = SSM_WIDTH // SSM_GROUP
SSM_STATE = 64
FFN_HIDDEN = 2816
IN_COLS = 3 * ATTN_WIDTH + SSM_WIDTH + 2 * D_MODEL
N_MOD = 6
NORM_EPS = 1e-6
NEG_BIG = -1e30
LOG2E = math.log2(math.e)

LANES = 128
CHUNK = 16
N_CHUNKS = SEQ // CHUNK
N_CTX_CHUNKS = CTX_LEN // CHUNK
CHUNK_COLS = CHUNK * SSM_GROUP
MOD_ROWS = 16
CTX_MOD_ROW = BATCH
VMEM_LIMIT = 56 * 1024 * 1024

Q_ROWS_PER_STEP = 4
Q_BLOCK = Q_ROWS_PER_STEP * GRID_W
K_ROWS_PER_STEP = Q_ROWS_PER_STEP + WIN_H
K_BLOCK = K_ROWS_PER_STEP * GRID_W
N_QBLOCKS = GRID_ROWS // Q_ROWS_PER_STEP
KEY_GROUP = (WIN_H // 2) * GRID_W
SOFTMAX_ROWS = 32


def _rms_modulate(x, g, shift, scale):
    xn = x * lax.rsqrt(jnp.mean(x * x, axis=-1, keepdims=True) + NORM_EPS)
    return (xn * g) * (1.0 + scale) + shift


def _block_transpose8(vs, lane):
    for shift in (64, 32, 16):
        keep = (lane & (2 * shift - 1)) < shift
        dist = shift // SSM_GROUP
        out = list(vs)
        for a in range(8):
            if a & dist:
                continue
            b = a + dist
            out[a] = jnp.where(keep, vs[a], pltpu.roll(vs[b], shift, 1))
            out[b] = jnp.where(keep, pltpu.roll(vs[a], LANES - shift, 1), vs[b])
        vs = out
    return vs


def _chunk_perm():
    n = CHUNK * CHUNK
    r = np.arange(n)
    m = np.zeros((n, n), np.float32)
    m[r, (r % CHUNK) * CHUNK + r // CHUNK] = 1.0
    return jnp.asarray(m, BF16)


def _store_chunk_layout(u16, perm_ref, out_ref, n_groups16, group0=0):
    lane = lax.broadcasted_iota(jnp.int32, (1, LANES), 1)
    n = CHUNK * CHUNK
    for hf in range(n_groups16):
        r = jnp.dot(perm_ref[...], u16[hf * n:(hf + 1) * n, :], preferred_element_type=F32)
        c0 = (group0 + hf) * CHUNK
        for v in range(SSM_WIDTH // LANES):
            for jh in range(2):
                vs = [r[CHUNK * (8 * jh + jp):CHUNK * (8 * jh + jp + 1), v * LANES:(v + 1) * LANES]
                      for jp in range(8)]
                outs = _block_transpose8(vs, lane)
                for gi in range(8):
                    out_ref[8 * v + gi, 0, c0:c0 + CHUNK, jh * LANES:(jh + 1) * LANES] = outs[gi].astype(BF16)


def _load_chunk_layout(yt_ref, perm_t_ref, r_ref, hf):
    lane = lax.broadcasted_iota(jnp.int32, (1, LANES), 1)
    for v in range(SSM_WIDTH // LANES):
        for jh in range(2):
            vs = [yt_ref[8 * v + gi, 0, hf * CHUNK:(hf + 1) * CHUNK,
                         jh * LANES:(jh + 1) * LANES].astype(F32) for gi in range(8)]
            outs = _block_transpose8(vs, lane)
            for jp in range(8):
                r_ref[hf, CHUNK * (8 * jh + jp):CHUNK * (8 * jh + jp + 1), v * LANES:(v + 1) * LANES] = outs[jp]
    return jnp.dot(perm_t_ref[...], r_ref[hf].astype(BF16), preferred_element_type=F32)


def _mod_kernel(c_ref, w_ref, b_ref, o_ref):
    c = c_ref[...]
    s = c * jax.nn.sigmoid(c)
    o_ref[...] = jnp.dot(s, w_ref[...], preferred_element_type=F32) + b_ref[...]


def _modulation(c_rows, w_mod, b_mod):
    n = N_MOD * D_MODEL
    tn = 1536
    return pl.pallas_call(
        _mod_kernel,
        grid=(n // tn,),
        in_specs=[pl.BlockSpec((MOD_ROWS, D_MODEL), lambda j: (0, 0)),
                  pl.BlockSpec((D_MODEL, tn), lambda j: (0, j)),
                  pl.BlockSpec((1, tn), lambda j: (0, j))],
        out_specs=pl.BlockSpec((MOD_ROWS, tn), lambda j: (0, j)),
        out_shape=jax.ShapeDtypeStruct((MOD_ROWS, n), F32),
        name="modulation",
    )(c_rows, w_mod, b_mod.reshape(1, n))


def _rope_tables():
    n_freq = HEAD_DIM // 4
    inv_freq = ROPE_BASE ** (-np.arange(n_freq, dtype=np.float64) / n_freq)
    t = np.arange(SEQ)
    lane = np.arange(LANES)
    d = lane % HEAD_DIM
    use_col = (d // (HEAD_DIM // 2)) == 1
    w = d % (HEAD_DIM // 2)
    first = w < n_freq
    pos = np.where(use_col[None, :], (t % GRID_W)[:, None], (t // GRID_W)[:, None]).astype(np.float64)
    ang = pos * inv_freq[w % n_freq][None, :]
    cos = np.cos(ang)
    sin = np.sin(ang)
    sin_a = np.where(first[None, :], -sin, 0.0)
    sin_b = np.where(first[None, :], 0.0, sin)
    return (jnp.asarray(cos, F32), jnp.asarray(sin_a, F32), jnp.asarray(sin_b, F32))


def _rope_store(r, cos, sin_a, sin_b, out_ref, row0, transposed=False):
    n_rows = r.shape[0]
    for j in range(ATTN_WIDTH // LANES):
        xs = r[:, j * LANES:(j + 1) * LANES]
        rot = (xs * cos + pltpu.roll(xs, LANES - HEAD_DIM // 4, 1) * sin_a
               + pltpu.roll(xs, HEAD_DIM // 4, 1) * sin_b)
        if transposed:
            rot_t = rot.T.astype(BF16)
            for gi in range(n_rows // KEY_GROUP):
                out_ref[0, row0 // KEY_GROUP + gi, j * LANES:(j + 1) * LANES, :] = (
                    rot_t[:, gi * KEY_GROUP:(gi + 1) * KEY_GROUP])
        else:
            out_ref[0, j, row0:row0 + n_rows, :] = rot.astype(BF16)


INPROJ_SUBTILE = 512


N_SIDE_CASTS = 6


def _inproj_kernel(x_ref, mod_ref, g_ref, w_ref, cos_ref, sa_ref, sb_ref, perm_ref, *rest):
    cast_src, rest = rest[:N_SIDE_CASTS], rest[N_SIDE_CASTS:]
    q_ref, k_ref, v_ref, u_ref, ut_ref, ga_ref, gs_ref = rest[:7]
    for src, dst in zip(cast_src, rest[7:]):
        dst[...] = src[...].astype(BF16)
    shift = mod_ref[0, :, 0:D_MODEL]
    scale = mod_ref[0, :, D_MODEL:2 * D_MODEL]
    aw = ATTN_WIDTH
    dot = functools.partial(jnp.dot, preferred_element_type=F32)
    for row0 in range(0, x_ref.shape[1], INPROJ_SUBTILE):
        rows = slice(row0, row0 + INPROJ_SUBTILE)
        nb = _rms_modulate(x_ref[0, rows, :], g_ref[...], shift, scale).astype(BF16)
        cos = cos_ref[rows, :]
        sin_a = sa_ref[rows, :]
        sin_b = sb_ref[rows, :]
        _rope_store(dot(nb, w_ref[:, 0:aw]), cos, sin_a, sin_b, q_ref, row0)
        _rope_store(dot(nb, w_ref[:, aw:2 * aw]), cos, sin_a, sin_b, k_ref, row0, transposed=True)
        v = dot(nb, w_ref[:, 2 * aw:3 * aw]).astype(BF16)
        for j in range(aw // LANES):
            v_ref[0, j, rows, :] = v[:, j * LANES:(j + 1) * LANES]
        c0 = 3 * aw
        u16 = dot(nb, w_ref[:, c0:c0 + SSM_WIDTH]).astype(BF16)
        u_ref[0, rows, :] = u16
        n16 = CHUNK * CHUNK
        _store_chunk_layout(u16, perm_ref, ut_ref, INPROJ_SUBTILE // n16, row0 // n16)
        c1 = c0 + SSM_WIDTH
        ga_ref[0, rows, :] = jax.nn.sigmoid(dot(nb, w_ref[:, c1:c1 + D_MODEL])).astype(BF16)
        c2 = c1 + D_MODEL
        gs_ref[0, rows, :] = jax.nn.sigmoid(dot(nb, w_ref[:, c2:c2 + D_MODEL])).astype(BF16)


def _input_projection(x, mod3, norm_g, w_in_bf16, rope, perm, side_weights):
    tm = 2 * INPROJ_SUBTILE
    n = CHUNK * CHUNK
    cos, sin_a, sin_b = rope
    n_steps = (SEQ // tm) * BATCH
    assert len(side_weights) == N_SIDE_CASTS
    side_specs = []
    for w in side_weights:
        rows = w.shape[0] // n_steps
        stride = 1
        while (rows * stride) % 16:
            stride *= 2
        assert w.shape[0] % (n_steps // stride) == 0
        side_specs.append(pl.BlockSpec(
            (rows * stride, w.shape[1]),
            lambda i, b, stride=stride: ((i * BATCH + b) // stride, 0)))
    tok = lambda width: pl.BlockSpec((1, tm, width), lambda i, b: (b, i, 0))
    tab = pl.BlockSpec((tm, LANES), lambda i, b: (i, 0))
    out = lambda width: jax.ShapeDtypeStruct((BATCH, SEQ, width), BF16)
    pairs = pl.BlockSpec((1, N_HEADS // 2, tm, LANES), lambda i, b: (b, 0, i, 0))
    pairs_shape = jax.ShapeDtypeStruct((BATCH, N_HEADS // 2, SEQ, LANES), BF16)
    return pl.pallas_call(
        _inproj_kernel,
        grid=(SEQ // tm, BATCH),
        in_specs=[tok(D_MODEL),
                  pl.BlockSpec((1, 1, N_MOD * D_MODEL), lambda i, b: (b, 0, 0)),
                  pl.BlockSpec((1, D_MODEL), lambda i, b: (0, 0)),
                  pl.BlockSpec((D_MODEL, IN_COLS), lambda i, b: (0, 0), pipeline_mode=pl.Buffered(1)),
                  tab, tab, tab,
                  pl.BlockSpec((n, n), lambda i, b: (0, 0), pipeline_mode=pl.Buffered(1))] + side_specs,
        out_specs=[pairs,
                   pl.BlockSpec((1, tm // KEY_GROUP, ATTN_WIDTH, KEY_GROUP), lambda i, b: (b, i, 0, 0)),
                   pairs, tok(SSM_WIDTH),
                   pl.BlockSpec((SSM_GROUPS, 1, tm // CHUNK, CHUNK_COLS), lambda i, b: (0, b, i, 0)),
                   tok(D_MODEL), tok(D_MODEL)] + side_specs,
        out_shape=[pairs_shape,
                   jax.ShapeDtypeStruct((BATCH, SEQ // KEY_GROUP, ATTN_WIDTH, KEY_GROUP), BF16),
                   pairs_shape, out(SSM_WIDTH),
                   jax.ShapeDtypeStruct((SSM_GROUPS, BATCH, N_CHUNKS, CHUNK_COLS), BF16),
                   out(D_MODEL), out(D_MODEL)] + [jax.ShapeDtypeStruct(w.shape, BF16) for w in side_weights],
        compiler_params=pltpu.CompilerParams(
            dimension_semantics=("arbitrary", "arbitrary"), vmem_limit_bytes=VMEM_LIMIT),
        name="input_projection",
    )(x, mod3, norm_g, w_in_bf16, cos, sin_a, sin_b, perm, *side_weights)


def _ctx_proj_kernel(x_ref, mod_ref, g_ref, wk_ref, wv_ref, wu_ref, perm_ref, k_ref, v_ref, ut_ref):
    x = x_ref[0]
    shift = mod_ref[0, :, 0:D_MODEL]
    scale = mod_ref[0, :, D_MODEL:2 * D_MODEL]
    nb = _rms_modulate(x, g_ref[...], shift, scale).astype(BF16)
    k_ref[0] = jnp.dot(nb, wk_ref[...], preferred_element_type=F32).T.astype(BF16)
    v_ref[0] = jnp.dot(nb, wv_ref[...], preferred_element_type=F32).astype(BF16)
    u16 = jnp.dot(nb, wu_ref[...], preferred_element_type=F32).astype(BF16)
    _store_chunk_layout(u16, perm_ref, ut_ref, 1)


def _context_projection(ctx, mod3, norm_g, w_in_bf16, perm):
    n = CHUNK * CHUNK
    aw = ATTN_WIDTH
    tok = lambda width: pl.BlockSpec((1, CTX_LEN, width), lambda b: (b, 0, 0))
    wcol = lambda j: pl.BlockSpec((D_MODEL, aw), lambda b: (0, j))
    out = jax.ShapeDtypeStruct((BATCH, CTX_LEN, aw), BF16)
    return pl.pallas_call(
        _ctx_proj_kernel,
        grid=(BATCH,),
        in_specs=[tok(D_MODEL),
                  pl.BlockSpec((1, 1, N_MOD * D_MODEL), lambda b: (CTX_MOD_ROW, 0, 0)),
                  pl.BlockSpec((1, D_MODEL), lambda b: (0, 0)),
                  wcol(1), wcol(2), wcol(3),
                  pl.BlockSpec((n, n), lambda b: (0, 0))],
        out_specs=[pl.BlockSpec((1, aw, CTX_LEN), lambda b: (b, 0, 0)), tok(aw),
                   pl.BlockSpec((SSM_GROUPS, 1, N_CTX_CHUNKS, CHUNK_COLS), lambda b: (0, b, 0, 0))],
        out_shape=[jax.ShapeDtypeStruct((BATCH, aw, CTX_LEN), BF16), out,
                   jax.ShapeDtypeStruct((SSM_GROUPS, BATCH, N_CTX_CHUNKS, CHUNK_COLS), BF16)],
        name="context_projection",
    )(ctx, mod3, norm_g, w_in_bf16, w_in_bf16, w_in_bf16, perm)


def _window_start_rows(r):
    return min(max(r - WIN_H // 2, 0), GRID_ROWS - WIN_H)


def _key_block_row(jb):
    return min(max(Q_ROWS_PER_STEP * jb - WIN_H // 2, 0), GRID_ROWS - K_ROWS_PER_STEP)


def _window_geometry(jb):
    key_row0 = _key_block_row(jb)
    offs, deltas = [], []
    for i in range(Q_ROWS_PER_STEP):
        r = Q_ROWS_PER_STEP * jb + i
        rs = _window_start_rows(r)
        offs.append(rs - key_row0)
        deltas.append(r - rs)
    return offs, deltas


ATTN_SAMPLES_PER_STEP = 2


def _attn_kernel(*refs):
    def per_sample(bi, carry):
        _attn_sample(bi, *refs)
        return carry

    lax.fori_loop(0, ATTN_SAMPLES_PER_STEP, per_sample, 0)


def _attn_sample(bi, q_ref, k_ref, v_ref, kc_ref, vc_ref, bias_ref, o_ref,
                 s_ref, sc_ref, p_ref, pc_ref, l_ref, m_ref, acc_ref):
    lane = lax.broadcasted_iota(jnp.int32, (1, LANES), 1)
    left = lane < HEAD_DIM
    n_ktiles = K_BLOCK // LANES
    last = N_QBLOCKS - 1
    max_key_start = (GRID_ROWS - K_ROWS_PER_STEP) * GRID_W

    def scores(e, q_start, key_start):
        q2 = q_ref[bi, 0, pl.ds(q_start, Q_BLOCK), :]
        qm = jnp.where(left if e == 0 else jnp.logical_not(left), q2, jnp.zeros_like(q2))
        group0 = key_start // KEY_GROUP
        for gi in range(K_BLOCK // KEY_GROUP):
            s_ref[e, :, gi * KEY_GROUP:(gi + 1) * KEY_GROUP] = jnp.dot(
                qm, k_ref[bi, group0 + gi], preferred_element_type=F32)
        sc_ref[e] = jnp.dot(qm, kc_ref[bi], preferred_element_type=F32)

    def softmax(e, offs, deltas):
        def geometry(sub):
            i = sub * SOFTMAX_ROWS // GRID_W
            qcols = slice(sub * SOFTMAX_ROWS % GRID_W, sub * SOFTMAX_ROWS % GRID_W + SOFTMAX_ROWS)
            rows = slice(sub * SOFTMAX_ROWS, (sub + 1) * SOFTMAX_ROWS)
            par = offs[i] % 2
            return rows, qcols, par, offs[i] // 2, WIN_H // 2 + par, deltas[i]

        def tile(rows, qcols, par, t0, n_tiles, delta, xt):
            t = t0 + xt
            dr0 = 2 * xt - par - delta
            st = s_ref[e, rows, t * LANES:(t + 1) * LANES] + bias_ref[0, e, dr0 + WIN_H, qcols, :]
            if par and xt == 0:
                st = jnp.where(left, NEG_BIG, st)
            if par and xt == n_tiles - 1:
                st = jnp.where(left, st, NEG_BIG)
            return st

        n_sub = Q_BLOCK // SOFTMAX_ROWS
        for sub in range(n_sub):
            geo = geometry(sub)
            rows, n_tiles = geo[0], geo[4]
            mt = jnp.maximum(sc_ref[e, rows, 0:LANES], sc_ref[e, rows, LANES:2 * LANES])
            for xt in range(n_tiles):
                mt = jnp.maximum(mt, tile(*geo, xt))
            m_ref[e, rows, :] = jnp.broadcast_to(jnp.max(mt, axis=1, keepdims=True), (SOFTMAX_ROWS, LANES))
        for sub in range(n_sub):
            geo = geometry(sub)
            rows, t0, n_tiles = geo[0], geo[3], geo[4]
            m = m_ref[e, rows, :]
            lt = None
            for xt in range(n_tiles):
                t = t0 + xt
                pt = jnp.exp2(tile(*geo, xt) - m)
                lt = pt if lt is None else lt + pt
                p_ref[e, rows, t * LANES:(t + 1) * LANES] = pt.astype(BF16)
            for t in range(n_ktiles):
                if not (t0 <= t < t0 + n_tiles):
                    p_ref[e, rows, t * LANES:(t + 1) * LANES] = jnp.zeros((SOFTMAX_ROWS, LANES), BF16)
            for ci in range(2):
                pt = jnp.exp2(sc_ref[e, rows, ci * LANES:(ci + 1) * LANES] - m)
                lt = lt + pt
                pc_ref[e, rows, ci * LANES:(ci + 1) * LANES] = pt.astype(BF16)
            l = jnp.sum(lt, axis=1, keepdims=True)
            l_ref[e, rows, :] = jnp.broadcast_to(1.0 / l, (SOFTMAX_ROWS, LANES))

    def values(e, key_start):
        vblk = v_ref[bi, 0, pl.ds(key_start, K_BLOCK), :]
        return (jnp.dot(p_ref[e], vblk, preferred_element_type=F32)
                + jnp.dot(pc_ref[e], vc_ref[bi], preferred_element_type=F32)) * l_ref[e]

    def block(jb_static, cur, nxt):
        offs, deltas = _window_geometry(jb_static)
        scores(1, *cur)
        softmax(0, offs, deltas)
        acc_ref[...] = values(0, cur[1])
        if nxt is not None:
            scores(0, *nxt)
        softmax(1, offs, deltas)
        o1 = values(1, cur[1])
        o_ref[bi, 0, pl.ds(cur[0], Q_BLOCK), :] = jnp.where(left, acc_ref[...], o1).astype(BF16)

    def starts(jb):
        q_start = pl.multiple_of(jb * Q_BLOCK, Q_BLOCK)
        key_start = jnp.clip((jb * Q_ROWS_PER_STEP - WIN_H // 2) * GRID_W, 0, max_key_start)
        return q_start, pl.multiple_of(key_start, KEY_GROUP)

    scores(0, 0, 0)
    block(0, (0, 0), (Q_BLOCK, max(Q_ROWS_PER_STEP - WIN_H // 2, 0) * GRID_W))

    blocks_per_trip = 7

    def interior(trip, carry):
        jb = 1 + blocks_per_trip * trip
        for d in range(blocks_per_trip):
            block(1, starts(jb + d), starts(jb + d + 1))
        return carry

    assert (last - 1) % blocks_per_trip == 0
    lax.fori_loop(0, (last - 1) // blocks_per_trip, interior, 0)
    block(last, (last * Q_BLOCK, max_key_start), None)


def _bias_tables(rpb):
    qcol = np.arange(GRID_W)
    kcol = np.arange(GRID_W)
    col_start = np.clip(qcol - WIN_W // 2, 0, GRID_W - WIN_W)
    in_win = (kcol[None, :] >= col_start[:, None]) & (kcol[None, :] < col_start[:, None] + WIN_W)
    dc_idx = np.clip(kcol[None, :] - qcol[:, None], -(WIN_W - 1), WIN_W - 1) + WIN_W - 1
    sel = (np.arange(2 * WIN_W - 1)[:, None, None] == dc_idx[None]).astype(np.float32)
    toe = jnp.einsum('hdt,tck->hdck', rpb * LOG2E, jnp.asarray(sel),
                     precision=lax.Precision.HIGHEST)
    toe = jnp.where(in_win[None, None], toe, NEG_BIG)
    neg = jnp.full((N_HEADS, 1, GRID_W, GRID_W), NEG_BIG, F32)
    ext = jnp.concatenate([neg, toe, neg], axis=1)
    pair = jnp.concatenate([ext[:, 0:16], ext[:, 1:17]], axis=-1)
    return pair.reshape(N_HEADS // 2, 2, 16, GRID_W, LANES)


def _attention(q, k, v, kc, vc, bias):
    n_slots = 2
    nb = ATTN_SAMPLES_PER_STEP
    kspec = pl.BlockSpec((nb, 1, SEQ, LANES), lambda hp, b: (b, hp, 0, 0))
    ktspec = pl.BlockSpec((nb, SEQ // KEY_GROUP, LANES, KEY_GROUP), lambda hp, b: (b, 0, hp, 0))
    cspec = pl.BlockSpec((nb, CTX_LEN, LANES), lambda hp, b: (b, 0, hp))
    ctspec = pl.BlockSpec((nb, LANES, CTX_LEN), lambda hp, b: (b, hp, 0))
    bspec = pl.BlockSpec((1, 2, 16, GRID_W, LANES), lambda hp, b: (hp, 0, 0, 0, 0))
    return pl.pallas_call(
        _attn_kernel,
        grid=(N_HEADS // 2, BATCH // nb),
        in_specs=[kspec, ktspec, kspec, ctspec, cspec, bspec],
        out_specs=kspec,
        out_shape=jax.ShapeDtypeStruct((BATCH, N_HEADS // 2, SEQ, LANES), BF16),
        scratch_shapes=[pltpu.VMEM((n_slots, Q_BLOCK, K_BLOCK), F32),
                        pltpu.VMEM((n_slots, Q_BLOCK, CTX_LEN), F32),
                        pltpu.VMEM((n_slots, Q_BLOCK, K_BLOCK), BF16),
                        pltpu.VMEM((n_slots, Q_BLOCK, CTX_LEN), BF16),
                        pltpu.VMEM((n_slots, Q_BLOCK, LANES), F32),
                        pltpu.VMEM((n_slots, Q_BLOCK, LANES), F32),
                        pltpu.VMEM((Q_BLOCK, LANES), F32)],
        compiler_params=pltpu.CompilerParams(
            dimension_semantics=("arbitrary", "arbitrary"),
            vmem_limit_bytes=VMEM_LIMIT),
        name="attention",
    )(q, k, v, kc, vc, bias)


def _rot256(a, b, s, lane):
    s %= 2 * LANES
    if s >= LANES:
        a, b, s = b, a, s - LANES
    if s == 0:
        return a, b
    ra = pltpu.roll(a, s, 1)
    rb = pltpu.roll(b, s, 1)
    keep = lane >= s
    return jnp.where(keep, ra, rb), jnp.where(keep, rb, ra)


S5_PREP_GROUPS = 2
POWER_ROWS = 24


def _s5_prep_kernel(*refs):
    for gi in range(S5_PREP_GROUPS):
        _s5_prep_group(gi, *refs)


def _s5_prep_group(gi, par_ref, b_ref, c_ref, t16_ref, e_ref, mi_ref, ms_ref, mo_ref, a_ref):
    dot = functools.partial(jnp.dot, preferred_element_type=F32)

    def split2(x):
        hi = x.astype(BF16)
        return hi, (x - hi.astype(F32)).astype(BF16)

    def split3(x):
        hi = x.astype(BF16)
        r1 = x - hi.astype(F32)
        mid = r1.astype(BF16)
        return hi, mid, (r1 - mid.astype(F32)).astype(BF16)

    def pick(x, onehot):
        hi, mid, lo = split3(x)
        return dot(hi, onehot) + dot(mid, onehot) + dot(lo, onehot)

    def dot_f32(a, b):
        ah, al = split2(a)
        bh, bl = split2(b)
        return dot(jnp.concatenate([ah, ah, al], axis=1), jnp.concatenate([bh, bl, bh], axis=0))

    ns = SSM_STATE
    lam_re, lam_im = par_ref[gi, 0:1, :], par_ref[gi, 1:2, :]
    dt = jnp.exp(par_ref[gi, 2:3, :])
    lane = lax.broadcasted_iota(jnp.int32, (1, LANES), 1)
    kf = lax.broadcasted_iota(jnp.int32, (POWER_ROWS, LANES), 0).astype(F32)
    mag = jnp.exp((lam_re * dt) * kf)
    ang = (lam_im * dt) * kf
    pwt_re, pwt_im = mag * jnp.cos(ang), mag * jnp.sin(ang)
    den = lam_re * lam_re + lam_im * lam_im
    nr, ni = pwt_re[1:2, :] - 1.0, pwt_im[1:2, :]
    f_re_row = (nr * lam_re + ni * lam_im) / den
    f_im_row = (ni * lam_re - nr * lam_im) / den
    r8 = lax.broadcasted_iota(jnp.int32, (8, LANES), 0)
    f_rows = jnp.where(r8 == 0, f_re_row, jnp.where(r8 == 1, f_im_row, 0.0))
    pad = jnp.zeros((LANES - POWER_ROWS - 8, LANES), F32)
    pw_re = jnp.concatenate([pwt_re, f_rows, pad], axis=0).T
    pw_im = jnp.concatenate([pwt_im, f_rows, pad], axis=0).T
    f_re, f_im = pw_re[:, POWER_ROWS:POWER_ROWS + 1], pw_re[:, POWER_ROWS + 1:POWER_ROWS + 2]
    b_re, b_im = b_ref[gi, :, 0:SSM_GROUP], b_ref[gi, :, SSM_GROUP:2 * SSM_GROUP]
    bb_re = f_re * b_re - f_im * b_im
    bb_im = f_re * b_im + f_im * b_re
    t16 = t16_ref[...]
    bbt_re, bbt_im = pick(bb_re, t16), pick(bb_im, t16)
    ct_re = pick(c_ref[gi, :, 0:SSM_GROUP], t16)
    ct_im = pick(c_ref[gi, :, SSM_GROUP:2 * SSM_GROUP], t16)
    pw_at = lambda x: (pick(pw_re, e_ref[x]), pick(pw_im, e_ref[x]))
    id_re, id_im = pw_at(0)
    rev_re, rev_im = pw_at(1)
    p1_re, p1_im = pw_at(2)
    r16_re, r16_im = pw_at(3)
    f, b = slice(0, ns), slice(ns, 2 * ns)
    cmul = lambda ar, ai, br, bi: (ar * br - ai * bi, ar * bi + ai * br)

    sf_re, sf_im = cmul(rev_re[f], rev_im[f], bbt_re[f], bbt_im[f])
    sb_re, sb_im = cmul(id_re[b], id_im[b], bbt_re[b], bbt_im[b])
    ms_ref[gi] = jnp.concatenate([sf_re, sb_re, sf_im, sb_im], axis=0).T.astype(BF16)

    of_re, of_im = cmul(p1_re[f], p1_im[f], ct_re[f], ct_im[f])
    ob_re, ob_im = cmul(r16_re[b], r16_im[b], ct_re[b], ct_im[b])
    mo_ref[gi] = jnp.concatenate([of_re, ob_re, -of_im, -ob_im], axis=0).astype(BF16)

    xf_re, xf_im = cmul(id_re[f], id_im[f], ct_re[f], ct_im[f])
    xb_re, xb_im = cmul(rev_re[b], rev_im[b], ct_re[b], ct_im[b])
    btf = jnp.concatenate([bbt_re[f], bbt_im[f]], axis=0).T
    btb = jnp.concatenate([bbt_re[b], bbt_im[b]], axis=0).T
    g_f = dot_f32(btf, jnp.concatenate([xf_re, -xf_im], axis=0))
    g_b = dot_f32(btb, jnp.concatenate([xb_re, -xb_im], axis=0))
    for j in range(CHUNK):
        rows = slice(SSM_GROUP * j, SSM_GROUP * (j + 1))
        lo_col, hi_col = SSM_GROUP * j, SSM_GROUP * (j + 1)
        f_lo, f_hi = _rot256(g_f[rows, :LANES], g_f[rows, LANES:], lo_col, lane)
        b_lo, b_hi = _rot256(g_b[rows, :LANES], g_b[rows, LANES:], -SSM_GROUP * (CHUNK - 1 - j), lane)
        lo = jnp.where(lane >= lo_col, f_lo, 0.0) + jnp.where(lane < hi_col, b_lo, 0.0)
        up = jnp.where(lane + LANES >= lo_col, f_hi, 0.0) + jnp.where(lane + LANES < hi_col, b_hi, 0.0)
        mi_ref[gi, rows, 0:LANES] = lo.astype(BF16)
        mi_ref[gi, rows, LANES:2 * LANES] = up.astype(BF16)

    a16 = jnp.concatenate([pwt_re[CHUNK:CHUNK + 1, :], pwt_im[CHUNK:CHUNK + 1, :]], axis=1)
    a_ref[gi] = jnp.broadcast_to(a16, (8, 2 * LANES))


def _s5_matrices(lam_re, lam_im, log_dt, b_re, b_im, c_re, c_im):
    g, p2 = SSM_GROUPS, 2 * SSM_STATE
    both = lambda a: jnp.transpose(a, (1, 0, 2)).reshape(g, p2)
    log_dt_rows = jnp.repeat(jnp.transpose(log_dt), SSM_STATE, axis=1)
    par = jnp.stack([both(lam_re), both(lam_im), log_dt_rows], axis=1)
    par = jnp.concatenate([par, jnp.zeros((g, LANES - 3, p2), F32)], axis=1)
    rows_b = lambda a: jnp.transpose(a, (1, 0, 2, 3)).reshape(g, p2, SSM_GROUP)
    rows_c = lambda a: jnp.transpose(a, (1, 0, 3, 2)).reshape(g, p2, SSM_GROUP)
    b_cat = jnp.concatenate([rows_b(b_re), rows_b(b_im)], axis=-1)
    c_cat = jnp.concatenate([rows_c(c_re), rows_c(c_im)], axis=-1)

    col = np.arange(CHUNK_COLS)
    tile16 = (col[None, :] % SSM_GROUP == np.arange(SSM_GROUP)[:, None]).astype(np.float32)
    pos = col // SSM_GROUP
    k_idx = np.arange(LANES)[:, None]
    expand = np.stack([k_idx == pos[None, :], k_idx == (CHUNK - 1 - pos)[None, :],
                       k_idx == (pos + 1)[None, :], k_idx == (CHUNK - pos)[None, :]]).astype(np.float32)

    gp = S5_PREP_GROUPS
    mat = lambda: pl.BlockSpec((gp, CHUNK_COLS, CHUNK_COLS), lambda i: (i, 0, 0))
    mat_shape = jax.ShapeDtypeStruct((g, CHUNK_COLS, CHUNK_COLS), BF16)
    return pl.pallas_call(
        _s5_prep_kernel,
        grid=(g // gp,),
        in_specs=[pl.BlockSpec((gp, LANES, p2), lambda i: (i, 0, 0)),
                  pl.BlockSpec((gp, p2, 2 * SSM_GROUP), lambda i: (i, 0, 0)),
                  pl.BlockSpec((gp, p2, 2 * SSM_GROUP), lambda i: (i, 0, 0)),
                  pl.BlockSpec((SSM_GROUP, CHUNK_COLS), lambda i: (0, 0)),
                  pl.BlockSpec((4, LANES, CHUNK_COLS), lambda i: (0, 0, 0))],
        out_specs=[mat(), mat(), mat(), pl.BlockSpec((gp, 8, 2 * LANES), lambda i: (i, 0, 0))],
        out_shape=[mat_shape, mat_shape, mat_shape, jax.ShapeDtypeStruct((g, 8, 2 * LANES), F32)],
        name="s5_prep",
    )(par, b_cat, c_cat, jnp.asarray(tile16, BF16), jnp.asarray(expand, BF16))


def _s5_kernel(ul_ref, uc_ref, ms_ref, mi_ref, mo_ref, a_ref, y_ref, s_ref, sc_ref, hp_ref, *, gb):
    for gi in range(gb):
        for b in range(BATCH):
            sb = jnp.dot(ul_ref[gi, b], ms_ref[gi], preferred_element_type=F32)
            s_ref[gi, 0, pl.ds(b, N_CHUNKS, stride=BATCH), :] = sb[:, :LANES]
            s_ref[gi, 1, pl.ds(b, N_CHUNKS, stride=BATCH), :] = sb[:, LANES:]
            cb = jnp.dot(uc_ref[gi, b], ms_ref[gi], preferred_element_type=F32)
            sc_ref[gi, 0, pl.ds(b, N_CTX_CHUNKS, stride=BATCH), :] = cb[:, :LANES]
            sc_ref[gi, 1, pl.ds(b, N_CTX_CHUNKS, stride=BATCH), :] = cb[:, LANES:]
    lane = lax.broadcasted_iota(jnp.int32, (BATCH, LANES), 1)
    fwd = lane < SSM_STATE
    half = SSM_STATE

    def advance(gi, h_re, h_im, row_f, row_b, src):
        s_re = jnp.where(fwd, src[gi, 0, pl.ds(row_f, BATCH), :], src[gi, 0, pl.ds(row_b, BATCH), :])
        s_im = jnp.where(fwd, src[gi, 1, pl.ds(row_f, BATCH), :], src[gi, 1, pl.ds(row_b, BATCH), :])
        a_re = a_ref[gi, :, 0:LANES]
        a_im = a_ref[gi, :, LANES:2 * LANES]
        n_re = a_re * h_re - a_im * h_im + s_re
        n_im = a_re * h_im + a_im * h_re + s_im
        return n_re, n_im

    def ctx_step(t, carry):
        row_f = pl.multiple_of(t * BATCH, BATCH)
        row_b = pl.multiple_of((N_CTX_CHUNKS - 1 - t) * BATCH, BATCH)
        return tuple(advance(gi, carry[gi][0], carry[gi][1], row_f, row_b, sc_ref) for gi in range(gb))

    def lat_step(t, carry):
        row_f = pl.multiple_of(t * BATCH, BATCH)
        row_b = pl.multiple_of((N_CHUNKS - 1 - t) * BATCH, BATCH)
        out = []
        for gi in range(gb):
            h_re, h_im = carry[gi]
            hp_ref[gi, 0, pl.ds(row_f, BATCH), 0:half] = h_re[:, 0:half]
            hp_ref[gi, 0, pl.ds(row_b, BATCH), half:2 * half] = h_re[:, half:]
            hp_ref[gi, 1, pl.ds(row_f, BATCH), 0:half] = h_im[:, 0:half]
            hp_ref[gi, 1, pl.ds(row_b, BATCH), half:2 * half] = h_im[:, half:]
            out.append(advance(gi, h_re, h_im, row_f, row_b, s_ref))
        return tuple(out)

    zero = jnp.zeros((BATCH, LANES), F32)
    carry = tuple((zero, zero) for _ in range(gb))
    carry = lax.fori_loop(0, N_CTX_CHUNKS, ctx_step, carry)
    lax.fori_loop(0, N_CHUNKS, lat_step, carry)
    for gi in range(gb):
        for b in range(BATCH):
            hb_re = hp_ref[gi, 0, pl.ds(b, N_CHUNKS, stride=BATCH), :].astype(BF16)
            hb_im = hp_ref[gi, 1, pl.ds(b, N_CHUNKS, stride=BATCH), :].astype(BF16)
            y = (jnp.dot(ul_ref[gi, b], mi_ref[gi], preferred_element_type=F32)
                 + jnp.dot(jnp.concatenate([hb_re, hb_im], axis=1), mo_ref[gi], preferred_element_type=F32))
            y_ref[gi, b] = y.astype(BF16)


def _s5_scan(u_lat_t, u_ctx_t, m_intra, m_state, m_out, a16):
    gb = 4
    rows = N_CHUNKS * BATCH
    crows = N_CTX_CHUNKS * BATCH
    grp = lambda r, c: pl.BlockSpec((gb, r, c), lambda g: (g, 0, 0))
    tok = lambda n: pl.BlockSpec((gb, BATCH, n, CHUNK_COLS), lambda g: (g, 0, 0, 0))
    return pl.pallas_call(
        functools.partial(_s5_kernel, gb=gb),
        grid=(SSM_GROUPS // gb,),
        in_specs=[tok(N_CHUNKS), tok(N_CTX_CHUNKS), grp(CHUNK_COLS, CHUNK_COLS),
                  grp(CHUNK_COLS, CHUNK_COLS), grp(CHUNK_COLS, CHUNK_COLS), grp(8, 2 * LANES)],
        out_specs=tok(N_CHUNKS),
        out_shape=jax.ShapeDtypeStruct((SSM_GROUPS, BATCH, N_CHUNKS, CHUNK_COLS), BF16),
        scratch_shapes=[pltpu.VMEM((gb, 2, rows, LANES), F32),
                        pltpu.VMEM((gb, 2, crows, LANES), F32),
                        pltpu.VMEM((gb, 2, rows, LANES), F32)],
        compiler_params=pltpu.CompilerParams(
            dimension_semantics=("arbitrary",), vmem_limit_bytes=VMEM_LIMIT),
        name="s5_scan",
    )(u_lat_t, u_ctx_t, m_state, m_intra, m_out, a16)


FFN_TILE = 256


def _post_kernel(x_ref, a_ref, yt_ref, u_ref, ga_ref, gs_ref, mod_ref, d_ref, fg_ref, og_ref, permt_ref,
                 wglu_ref, wba_ref, wbs_ref, wout_ref, wfi_ref, wfo_ref, o_ref, h1_ref, n2_ref, act_ref, r_ref):
    dm = D_MODEL
    g1 = mod_ref[0, :, 2 * dm:3 * dm]
    sh2 = mod_ref[0, :, 3 * dm:4 * dm]
    sc2 = mod_ref[0, :, 4 * dm:5 * dm]
    g2 = mod_ref[0, :, 5 * dm:6 * dm]
    half = CHUNK * CHUNK
    halves = [slice(h * half, (h + 1) * half) for h in range(x_ref.shape[1] // half)]
    dot = functools.partial(jnp.dot, preferred_element_type=F32)

    sp = []
    for h, rows in enumerate(halves):
        y = _load_chunk_layout(yt_ref, permt_ref, r_ref, h)
        sp.append(jax.nn.gelu(y + d_ref[...] * u_ref[0, rows, :].astype(F32)).astype(BF16))
    s = []
    for h, rows in enumerate(halves):
        vg = dot(sp[h], wglu_ref[...])
        s.append((vg[:, :SSM_WIDTH] * jax.nn.sigmoid(vg[:, SSM_WIDTH:])).astype(BF16))
    merged = []
    for h, rows in enumerate(halves):
        a = jnp.concatenate([a_ref[0, j, rows, :] for j in range(ATTN_WIDTH // LANES)], axis=1)
        m = (ga_ref[0, rows, :].astype(F32) * dot(a, wba_ref[...])
             + gs_ref[0, rows, :].astype(F32) * dot(s[h], wbs_ref[...]))
        merged.append(m.astype(BF16))
    for h, rows in enumerate(halves):
        h1 = x_ref[0, rows, :] + g1 * dot(merged[h], wout_ref[...])
        h1_ref[rows, :] = h1
        n2_ref[rows, :] = _rms_modulate(h1, fg_ref[...], sh2, sc2).astype(BF16)
    for lo in range(0, FFN_HIDDEN, FFN_TILE):
        width = min(FFN_TILE, FFN_HIDDEN - lo)
        for rows in halves:
            n2 = n2_ref[rows, :]
            fa = dot(n2, wfi_ref[:, lo:lo + width])
            fb = dot(n2, wfi_ref[:, FFN_HIDDEN + lo:FFN_HIDDEN + lo + width])
            act_ref[rows, lo:lo + width] = (fa * jax.nn.sigmoid(fa) * fb).astype(BF16)
    for rows in halves:
        h2 = h1_ref[rows, :] + g2 * dot(act_ref[rows, :], wfo_ref[...])
        o_ref[0, rows, :] = (h2 * lax.rsqrt(jnp.mean(h2 * h2, axis=-1, keepdims=True) + NORM_EPS)) * og_ref[...]


def _post(x, a, y_t, u, ga, gs, mod3, d_skip, ffn_g, fin_g, perm_t, wglu, wba, wbs, wout, wfi, wfo):
    tm = 512
    n = CHUNK * CHUNK
    tok = lambda width: pl.BlockSpec((1, tm, width), lambda b, i: (b, i, 0))
    const = lambda r, c: pl.BlockSpec((r, c), lambda b, i: (0, 0), pipeline_mode=pl.Buffered(1))
    return pl.pallas_call(
        _post_kernel,
        grid=(BATCH, SEQ // tm),
        in_specs=[tok(D_MODEL),
                  pl.BlockSpec((1, N_HEADS // 2, tm, LANES), lambda b, i: (b, 0, i, 0)),
                  pl.BlockSpec((SSM_GROUPS, 1, tm // CHUNK, CHUNK_COLS), lambda b, i: (0, b, i, 0)),
                  tok(SSM_WIDTH), tok(D_MODEL), tok(D_MODEL),
                  pl.BlockSpec((1, 1, N_MOD * D_MODEL), lambda b, i: (b, 0, 0)),
                  const(1, SSM_WIDTH), const(1, D_MODEL), const(1, D_MODEL), const(n, n),
                  const(SSM_WIDTH, 2 * SSM_WIDTH), const(ATTN_WIDTH, D_MODEL),
                  const(SSM_WIDTH, D_MODEL), const(D_MODEL, D_MODEL),
                  const(D_MODEL, 2 * FFN_HIDDEN), const(FFN_HIDDEN, D_MODEL)],
        out_specs=tok(D_MODEL),
        out_shape=jax.ShapeDtypeStruct((BATCH, SEQ, D_MODEL), F32),
        scratch_shapes=[pltpu.VMEM((tm, D_MODEL), F32), pltpu.VMEM((tm, D_MODEL), BF16),
                        pltpu.VMEM((tm, FFN_HIDDEN), BF16), pltpu.VMEM((tm // n, n, SSM_WIDTH), F32)],
        compiler_params=pltpu.CompilerParams(
            dimension_semantics=("arbitrary", "arbitrary"), vmem_limit_bytes=VMEM_LIMIT),
        name="post",
    )(x, a, y_t, u, ga, gs, mod3, d_skip, ffn_g, fin_g, perm_t, wglu, wba, wbs, wout, wfi, wfo)


def kernel(x, c, ctx, c_ctx, w_mod, b_mod, attn_norm_g, ffn_norm_g, w_in, rel_pos_bias,
           ssm_lambda_re, ssm_lambda_im, ssm_log_dt, ssm_b_re, ssm_b_im, ssm_c_re, ssm_c_im, ssm_d,
           w_glu, w_branch_attn, w_branch_ssm, w_out, w_ffn_in, w_ffn_out, final_norm_g):
    assert x.shape == (BATCH, SEQ, D_MODEL) and w_mod.shape[0] == 1
    c_rows = jnp.concatenate(
        [c, c_ctx[None, :], jnp.zeros((MOD_ROWS - BATCH - 1, D_MODEL), F32)], axis=0)
    mod3 = _modulation(c_rows, w_mod[0], b_mod[0]).reshape(MOD_ROWS, 1, N_MOD * D_MODEL)

    col_scale = jnp.concatenate([jnp.full((ATTN_WIDTH,), HEAD_DIM ** -0.5 * LOG2E, F32),
                                 jnp.ones((IN_COLS - ATTN_WIDTH,), F32)])
    w_in_bf16 = (w_in[0] * col_scale[None, :]).astype(BF16)
    norm_g = attn_norm_g[0].reshape(1, D_MODEL)

    perm = _chunk_perm()
    post_weights = [w_glu[0], w_branch_attn[0], w_branch_ssm[0], w_out[0], w_ffn_in[0], w_ffn_out[0]]
    q, k, v, u, u_t, ga, gs, *post_weights_bf16 = _input_projection(
        x, mod3, norm_g, w_in_bf16, _rope_tables(), perm, post_weights)
    kc, vc, uc_t = _context_projection(ctx, mod3, norm_g, w_in_bf16, perm)

    attn = _attention(q, k, v, kc, vc, _bias_tables(rel_pos_bias[0]))

    m_intra, m_state, m_out, a16 = _s5_matrices(
        ssm_lambda_re[0], ssm_lambda_im[0], ssm_log_dt[0], ssm_b_re[0], ssm_b_im[0],
        ssm_c_re[0], ssm_c_im[0])
    y_t = _s5_scan(u_t, uc_t, m_intra, m_state, m_out, a16)

    return _post(x, attn, y_t, u, ga, gs, mod3,
                 ssm_d[0].reshape(1, SSM_WIDTH), ffn_norm_g[0].reshape(1, D_MODEL),
                 final_norm_g.reshape(1, D_MODEL), perm,
                 *post_weights_bf16)
```

```python
import functools
import math

import numpy as np
import jax
import jax.numpy as jnp
from jax import lax
from jax.experimental import pallas as pl
from jax.experimental.pallas import tpu as pltpu

F32 = jnp.float32
BF16 = jnp.bfloat16

D_MODEL = 1024
BATCH = 8
SEQ = 4096
GRID_W = 64
GRID_ROWS = SEQ // GRID_W
CTX_LEN = 256
N_HEADS = 8
HEAD_DIM = 64
ATTN_WIDTH = N_HEADS * HEAD_DIM
WIN_H = 8
WIN_W = 16
ROPE_BASE = 10000.0
SSM_WIDTH = 512
SSM_GROUP = 16
SSM_GROUPS = SSM_WIDTH // SSM_GROUP
SSM_STATE = 64
FFN_HIDDEN = 2816
IN_COLS = 3 * ATTN_WIDTH + SSM_WIDTH + 2 * D_MODEL
N_MOD = 6
NORM_EPS = 1e-6
NEG_BIG = -1e30
LOG2E = math.log2(math.e)

LANES = 128
CHUNK = 16
N_CHUNKS = SEQ // CHUNK
N_CTX_CHUNKS = CTX_LEN // CHUNK
CHUNK_COLS = CHUNK * SSM_GROUP
MOD_ROWS = 16
CTX_MOD_ROW = BATCH
VMEM_LIMIT = 56 * 1024 * 1024

Q_ROWS_PER_STEP = 4
Q_BLOCK = Q_ROWS_PER_STEP * GRID_W
K_ROWS_PER_STEP = Q_ROWS_PER_STEP + WIN_H
K_BLOCK = K_ROWS_PER_STEP * GRID_W
N_QBLOCKS = GRID_ROWS // Q_ROWS_PER_STEP
KEY_GROUP = (WIN_H // 2) * GRID_W
SOFTMAX_ROWS = 32


def _rms_modulate(x, g, shift, scale):
    xn = x * lax.rsqrt(jnp.mean(x * x, axis=-1, keepdims=True) + NORM_EPS)
    return (xn * g) * (1.0 + scale) + shift


def _block_transpose8(vs, lane):
    for shift in (64, 32, 16):
        keep = (lane & (2 * shift - 1)) < shift
        dist = shift // SSM_GROUP
        out = list(vs)
        for a in range(8):
            if a & dist:
                continue
            b = a + dist
            out[a] = jnp.where(keep, vs[a], pltpu.roll(vs[b], shift, 1))
            out[b] = jnp.where(keep, pltpu.roll(vs[a], LANES - shift, 1), vs[b])
        vs = out
    return vs


def _chunk_perm():
    n = CHUNK * CHUNK
    r = np.arange(n)
    m = np.zeros((n, n), np.float32)
    m[r, (r % CHUNK) * CHUNK + r // CHUNK] = 1.0
    return jnp.asarray(m, BF16)


def _store_chunk_layout(u16, perm_ref, out_ref, n_groups16, group0=0):
    lane = lax.broadcasted_iota(jnp.int32, (1, LANES), 1)
    n = CHUNK * CHUNK
    for hf in range(n_groups16):
        r = jnp.dot(perm_ref[...], u16[hf * n:(hf + 1) * n, :], preferred_element_type=F32)
        c0 = (group0 + hf) * CHUNK
        for v in range(SSM_WIDTH // LANES):
            for jh in range(2):
                vs = [r[CHUNK * (8 * jh + jp):CHUNK * (8 * jh + jp + 1), v * LANES:(v + 1) * LANES]
                      for jp in range(8)]
                outs = _block_transpose8(vs, lane)
                for gi in range(8):
                    out_ref[8 * v + gi, 0, c0:c0 + CHUNK, jh * LANES:(jh + 1) * LANES] = outs[gi].astype(BF16)


def _load_chunk_layout(yt_ref, perm_t_ref, r_ref, hf):
    lane = lax.broadcasted_iota(jnp.int32, (1, LANES), 1)
    for v in range(SSM_WIDTH // LANES):
        for jh in range(2):
            vs = [yt_ref[8 * v + gi, 0, hf * CHUNK:(hf + 1) * CHUNK,
                         jh * LANES:(jh + 1) * LANES].astype(F32) for gi in range(8)]
            outs = _block_transpose8(vs, lane)
            for jp in range(8):
                r_ref[hf, CHUNK * (8 * jh + jp):CHUNK * (8 * jh + jp + 1), v * LANES:(v + 1) * LANES] = outs[jp]
    return jnp.dot(perm_t_ref[...], r_ref[hf].astype(BF16), preferred_element_type=F32)


def _mod_kernel(c_ref, w_ref, b_ref, o_ref):
    c = c_ref[...]
    s = c * jax.nn.sigmoid(c)
    o_ref[...] = jnp.dot(s, w_ref[...], preferred_element_type=F32) + b_ref[...]


def _modulation(c_rows, w_mod, b_mod):
    n = N_MOD * D_MODEL
    tn = 1536
    return pl.pallas_call(
        _mod_kernel,
        grid=(n // tn,),
        in_specs=[pl.BlockSpec((MOD_ROWS, D_MODEL), lambda j: (0, 0)),
                  pl.BlockSpec((D_MODEL, tn), lambda j: (0, j)),
                  pl.BlockSpec((1, tn), lambda j: (0, j))],
        out_specs=pl.BlockSpec((MOD_ROWS, tn), lambda j: (0, j)),
        out_shape=jax.ShapeDtypeStruct((MOD_ROWS, n), F32),
        name="modulation",
    )(c_rows, w_mod, b_mod.reshape(1, n))


def _rope_tables():
    n_freq = HEAD_DIM // 4
    inv_freq = ROPE_BASE ** (-np.arange(n_freq, dtype=np.float64) / n_freq)
    t = np.arange(SEQ)
    lane = np.arange(LANES)
    d = lane % HEAD_DIM
    use_col = (d // (HEAD_DIM // 2)) == 1
    w = d % (HEAD_DIM // 2)
    first = w < n_freq
    pos = np.where(use_col[None, :], (t % GRID_W)[:, None], (t // GRID_W)[:, None]).astype(np.float64)
    ang = pos * inv_freq[w % n_freq][None, :]
    cos = np.cos(ang)
    sin = np.sin(ang)
    sin_a = np.where(first[None, :], -sin, 0.0)
    sin_b = np.where(first[None, :], 0.0, sin)
    return (jnp.asarray(cos, F32), jnp.asarray(sin_a, F32), jnp.asarray(sin_b, F32))


def _rope_store(r, cos, sin_a, sin_b, out_ref, row0, transposed=False):
    n_rows = r.shape[0]
    for j in range(ATTN_WIDTH // LANES):
        xs = r[:, j * LANES:(j + 1) * LANES]
        rot = (xs * cos + pltpu.roll(xs, LANES - HEAD_DIM // 4, 1) * sin_a
               + pltpu.roll(xs, HEAD_DIM // 4, 1) * sin_b)
        if transposed:
            rot_t = rot.T.astype(BF16)
            for gi in range(n_rows // KEY_GROUP):
                out_ref[0, row0 // KEY_GROUP + gi, j * LANES:(j + 1) * LANES, :] = (
                    rot_t[:, gi * KEY_GROUP:(gi + 1) * KEY_GROUP])
        else:
            out_ref[0, j, row0:row0 + n_rows, :] = rot.astype(BF16)


INPROJ_SUBTILE = 512


N_SIDE_CASTS = 6
N_S5_PREP_INPUTS = 5


def _inproj_kernel(x_ref, mod_ref, g_ref, w_ref, cos_ref, sa_ref, sb_ref, perm_ref, *rest):
    cast_src, rest = rest[:N_SIDE_CASTS], rest[N_SIDE_CASTS:]
    s5_in, rest = rest[:N_S5_PREP_INPUTS], rest[N_S5_PREP_INPUTS:]
    q_ref, k_ref, v_ref, u_ref, ut_ref, ga_ref, gs_ref = rest[:7]
    cast_dst, s5_out = rest[7:7 + N_SIDE_CASTS], rest[7 + N_SIDE_CASTS:]
    _s5_prep_group(0, *s5_in, *s5_out)
    for src, dst in zip(cast_src, cast_dst):
        dst[...] = src[...].astype(BF16)
    shift = mod_ref[0, :, 0:D_MODEL]
    scale = mod_ref[0, :, D_MODEL:2 * D_MODEL]
    aw = ATTN_WIDTH
    dot = functools.partial(jnp.dot, preferred_element_type=F32)
    for row0 in range(0, x_ref.shape[1], INPROJ_SUBTILE):
        rows = slice(row0, row0 + INPROJ_SUBTILE)
        nb = _rms_modulate(x_ref[0, rows, :], g_ref[...], shift, scale).astype(BF16)
        cos = cos_ref[rows, :]
        sin_a = sa_ref[rows, :]
        sin_b = sb_ref[rows, :]
        _rope_store(dot(nb, w_ref[:, 0:aw]), cos, sin_a, sin_b, q_ref, row0)
        _rope_store(dot(nb, w_ref[:, aw:2 * aw]), cos, sin_a, sin_b, k_ref, row0, transposed=True)
        v = dot(nb, w_ref[:, 2 * aw:3 * aw]).astype(BF16)
        for j in range(aw // LANES):
            v_ref[0, j, rows, :] = v[:, j * LANES:(j + 1) * LANES]
        c0 = 3 * aw
        u16 = dot(nb, w_ref[:, c0:c0 + SSM_WIDTH]).astype(BF16)
        u_ref[0, rows, :] = u16
        n16 = CHUNK * CHUNK
        _store_chunk_layout(u16, perm_ref, ut_ref, INPROJ_SUBTILE // n16, row0 // n16)
        c1 = c0 + SSM_WIDTH
        ga_ref[0, rows, :] = jax.nn.sigmoid(dot(nb, w_ref[:, c1:c1 + D_MODEL])).astype(BF16)
        c2 = c1 + D_MODEL
        gs_ref[0, rows, :] = jax.nn.sigmoid(dot(nb, w_ref[:, c2:c2 + D_MODEL])).astype(BF16)


def _input_projection(x, mod3, norm_g, w_in_bf16, rope, perm, side_weights, s5_params):
    tm = 2 * INPROJ_SUBTILE
    n = CHUNK * CHUNK
    cos, sin_a, sin_b = rope
    n_steps = (SEQ // tm) * BATCH
    assert len(side_weights) == N_SIDE_CASTS and len(s5_params) == N_S5_PREP_INPUTS and n_steps == SSM_GROUPS
    step = lambda i, b: i * BATCH + b
    per_group = lambda r, c: pl.BlockSpec((1, r, c), lambda i, b: (step(i, b), 0, 0))
    p2 = 2 * SSM_STATE
    s5_in_specs = [per_group(LANES, p2), per_group(p2, 2 * SSM_GROUP), per_group(p2, 2 * SSM_GROUP),
                   pl.BlockSpec((SSM_GROUP, CHUNK_COLS), lambda i, b: (0, 0)),
                   pl.BlockSpec((4, LANES, CHUNK_COLS), lambda i, b: (0, 0, 0))]
    s5_out_specs = [per_group(CHUNK_COLS, CHUNK_COLS)] * 3 + [per_group(8, 2 * LANES)]
    mat_shape = jax.ShapeDtypeStruct((SSM_GROUPS, CHUNK_COLS, CHUNK_COLS), BF16)
    s5_out_shapes = [mat_shape] * 3 + [jax.ShapeDtypeStruct((SSM_GROUPS, 8, 2 * LANES), F32)]
    side_specs = []
    for w in side_weights:
        rows = w.shape[0] // n_steps
        stride = 1
        while (rows * stride) % 16:
            stride *= 2
        assert w.shape[0] % (n_steps // stride) == 0
        side_specs.append(pl.BlockSpec(
            (rows * stride, w.shape[1]),
            lambda i, b, stride=stride: ((i * BATCH + b) // stride, 0)))
    tok = lambda width: pl.BlockSpec((1, tm, width), lambda i, b: (b, i, 0))
    tab = pl.BlockSpec((tm, LANES), lambda i, b: (i, 0))
    out = lambda width: jax.ShapeDtypeStruct((BATCH, SEQ, width), BF16)
    pairs = pl.BlockSpec((1, N_HEADS // 2, tm, LANES), lambda i, b: (b, 0, i, 0))
    pairs_shape = jax.ShapeDtypeStruct((BATCH, N_HEADS // 2, SEQ, LANES), BF16)
    return pl.pallas_call(
        _inproj_kernel,
        grid=(SEQ // tm, BATCH),
        in_specs=[tok(D_MODEL),
                  pl.BlockSpec((1, 1, N_MOD * D_MODEL), lambda i, b: (b, 0, 0)),
                  pl.BlockSpec((1, D_MODEL), lambda i, b: (0, 0)),
                  pl.BlockSpec((D_MODEL, IN_COLS), lambda i, b: (0, 0), pipeline_mode=pl.Buffered(1)),
                  tab, tab, tab,
                  pl.BlockSpec((n, n), lambda i, b: (0, 0), pipeline_mode=pl.Buffered(1))]
                 + side_specs + s5_in_specs,
        out_specs=[pairs,
                   pl.BlockSpec((1, tm // KEY_GROUP, ATTN_WIDTH, KEY_GROUP), lambda i, b: (b, i, 0, 0)),
                   pairs, tok(SSM_WIDTH),
                   pl.BlockSpec((SSM_GROUPS, 1, tm // CHUNK, CHUNK_COLS), lambda i, b: (0, b, i, 0)),
                   tok(D_MODEL), tok(D_MODEL)] + side_specs + s5_out_specs,
        out_shape=[pairs_shape,
                   jax.ShapeDtypeStruct((BATCH, SEQ // KEY_GROUP, ATTN_WIDTH, KEY_GROUP), BF16),
                   pairs_shape, out(SSM_WIDTH),
                   jax.ShapeDtypeStruct((SSM_GROUPS, BATCH, N_CHUNKS, CHUNK_COLS), BF16),
                   out(D_MODEL), out(D_MODEL)] + [jax.ShapeDtypeStruct(w.shape, BF16) for w in side_weights]
                  + s5_out_shapes,
        compiler_params=pltpu.CompilerParams(
            dimension_semantics=("arbitrary", "arbitrary"), vmem_limit_bytes=VMEM_LIMIT),
        name="input_projection",
    )(x, mod3, norm_g, w_in_bf16, cos, sin_a, sin_b, perm, *side_weights, *s5_params)


def _ctx_proj_kernel(x_ref, mod_ref, g_ref, wk_ref, wv_ref, wu_ref, perm_ref, k_ref, v_ref, ut_ref):
    x = x_ref[0]
    shift = mod_ref[0, :, 0:D_MODEL]
    scale = mod_ref[0, :, D_MODEL:2 * D_MODEL]
    nb = _rms_modulate(x, g_ref[...], shift, scale).astype(BF16)
    k_ref[0] = jnp.dot(nb, wk_ref[...], preferred_element_type=F32).T.astype(BF16)
    v_ref[0] = jnp.dot(nb, wv_ref[...], preferred_element_type=F32).astype(BF16)
    u16 = jnp.dot(nb, wu_ref[...], preferred_element_type=F32).astype(BF16)
    _store_chunk_layout(u16, perm_ref, ut_ref, 1)


def _context_projection(ctx, mod3, norm_g, w_in_bf16, perm):
    n = CHUNK * CHUNK
    aw = ATTN_WIDTH
    tok = lambda width: pl.BlockSpec((1, CTX_LEN, width), lambda b: (b, 0, 0))
    wcol = lambda j: pl.BlockSpec((D_MODEL, aw), lambda b: (0, j))
    out = jax.ShapeDtypeStruct((BATCH, CTX_LEN, aw), BF16)
    return pl.pallas_call(
        _ctx_proj_kernel,
        grid=(BATCH,),
        in_specs=[tok(D_MODEL),
                  pl.BlockSpec((1, 1, N_MOD * D_MODEL), lambda b: (CTX_MOD_ROW, 0, 0)),
                  pl.BlockSpec((1, D_MODEL), lambda b: (0, 0)),
                  wcol(1), wcol(2), wcol(3),
                  pl.BlockSpec((n, n), lambda b: (0, 0))],
        out_specs=[pl.BlockSpec((1, aw, CTX_LEN), lambda b: (b, 0, 0)), tok(aw),
                   pl.BlockSpec((SSM_GROUPS, 1, N_CTX_CHUNKS, CHUNK_COLS), lambda b: (0, b, 0, 0))],
        out_shape=[jax.ShapeDtypeStruct((BATCH, aw, CTX_LEN), BF16), out,
                   jax.ShapeDtypeStruct((SSM_GROUPS, BATCH, N_CTX_CHUNKS, CHUNK_COLS), BF16)],
        name="context_projection",
    )(ctx, mod3, norm_g, w_in_bf16, w_in_bf16, w_in_bf16, perm)


def _window_start_rows(r):
    return min(max(r - WIN_H // 2, 0), GRID_ROWS - WIN_H)


def _key_block_row(jb):
    return min(max(Q_ROWS_PER_STEP * jb - WIN_H // 2, 0), GRID_ROWS - K_ROWS_PER_STEP)


def _window_geometry(jb):
    key_row0 = _key_block_row(jb)
    offs, deltas = [], []
    for i in range(Q_ROWS_PER_STEP):
        r = Q_ROWS_PER_STEP * jb + i
        rs = _window_start_rows(r)
        offs.append(rs - key_row0)
        deltas.append(r - rs)
    return offs, deltas


ATTN_SAMPLES_PER_STEP = 2


def _attn_kernel(*refs):
    def per_sample(bi, carry):
        _attn_sample(bi, *refs)
        return carry

    lax.fori_loop(0, ATTN_SAMPLES_PER_STEP, per_sample, 0)


def _attn_sample(bi, q_ref, k_ref, v_ref, kc_ref, vc_ref, bias_ref, o_ref,
                 s_ref, sc_ref, p_ref, pc_ref, l_ref, m_ref, acc_ref):
    lane = lax.broadcasted_iota(jnp.int32, (1, LANES), 1)
    left = lane < HEAD_DIM
    n_ktiles = K_BLOCK // LANES
    last = N_QBLOCKS - 1
    max_key_start = (GRID_ROWS - K_ROWS_PER_STEP) * GRID_W

    def scores(e, q_start, key_start):
        q2 = q_ref[bi, 0, pl.ds(q_start, Q_BLOCK), :]
        qm = jnp.where(left if e == 0 else jnp.logical_not(left), q2, jnp.zeros_like(q2))
        group0 = key_start // KEY_GROUP
        for gi in range(K_BLOCK // KEY_GROUP):
            s_ref[e, :, gi * KEY_GROUP:(gi + 1) * KEY_GROUP] = jnp.dot(
                qm, k_ref[bi, group0 + gi], preferred_element_type=F32)
        sc_ref[e] = jnp.dot(qm, kc_ref[bi], preferred_element_type=F32)

    def softmax(e, offs, deltas):
        def geometry(sub):
            i = sub * SOFTMAX_ROWS // GRID_W
            qcols = slice(sub * SOFTMAX_ROWS % GRID_W, sub * SOFTMAX_ROWS % GRID_W + SOFTMAX_ROWS)
            rows = slice(sub * SOFTMAX_ROWS, (sub + 1) * SOFTMAX_ROWS)
            par = offs[i] % 2
            return rows, qcols, par, offs[i] // 2, WIN_H // 2 + par, deltas[i]

        def tile(rows, qcols, par, t0, n_tiles, delta, xt):
            t = t0 + xt
            dr0 = 2 * xt - par - delta
            st = s_ref[e, rows, t * LANES:(t + 1) * LANES] + bias_ref[0, e, dr0 + WIN_H, qcols, :]
            if par and xt == 0:
                st = jnp.where(left, NEG_BIG, st)
            if par and xt == n_tiles - 1:
                st = jnp.where(left, st, NEG_BIG)
            return st

        n_sub = Q_BLOCK // SOFTMAX_ROWS
        for sub in range(n_sub):
            geo = geometry(sub)
            rows, n_tiles = geo[0], geo[4]
            mt = jnp.maximum(sc_ref[e, rows, 0:LANES], sc_ref[e, rows, LANES:2 * LANES])
            for xt in range(n_tiles):
                mt = jnp.maximum(mt, tile(*geo, xt))
            m_ref[e, rows, :] = jnp.broadcast_to(jnp.max(mt, axis=1, keepdims=True), (SOFTMAX_ROWS, LANES))
        for sub in range(n_sub):
            geo = geometry(sub)
            rows, t0, n_tiles = geo[0], geo[3], geo[4]
            m = m_ref[e, rows, :]
            lt = None
            for xt in range(n_tiles):
                t = t0 + xt
                pt = jnp.exp2(tile(*geo, xt) - m)
                lt = pt if lt is None else lt + pt
                p_ref[e, rows, t * LANES:(t + 1) * LANES] = pt.astype(BF16)
            for t in range(n_ktiles):
                if not (t0 <= t < t0 + n_tiles):
                    p_ref[e, rows, t * LANES:(t + 1) * LANES] = jnp.zeros((SOFTMAX_ROWS, LANES), BF16)
            for ci in range(2):
                pt = jnp.exp2(sc_ref[e, rows, ci * LANES:(ci + 1) * LANES] - m)
                lt = lt + pt
                pc_ref[e, rows, ci * LANES:(ci + 1) * LANES] = pt.astype(BF16)
            l = jnp.sum(lt, axis=1, keepdims=True)
            l_ref[e, rows, :] = jnp.broadcast_to(1.0 / l, (SOFTMAX_ROWS, LANES))

    def values(e, key_start):
        vblk = v_ref[bi, 0, pl.ds(key_start, K_BLOCK), :]
        return (jnp.dot(p_ref[e], vblk, preferred_element_type=F32)
                + jnp.dot(pc_ref[e], vc_ref[bi], preferred_element_type=F32)) * l_ref[e]

    def block(jb_static, cur, nxt):
        offs, deltas = _window_geometry(jb_static)
        scores(1, *cur)
        softmax(0, offs, deltas)
        acc_ref[...] = values(0, cur[1])
        if nxt is not None:
            scores(0, *nxt)
        softmax(1, offs, deltas)
        o1 = values(1, cur[1])
        o_ref[bi, 0, pl.ds(cur[0], Q_BLOCK), :] = jnp.where(left, acc_ref[...], o1).astype(BF16)

    def starts(jb):
        q_start = pl.multiple_of(jb * Q_BLOCK, Q_BLOCK)
        key_start = jnp.clip((jb * Q_ROWS_PER_STEP - WIN_H // 2) * GRID_W, 0, max_key_start)
        return q_start, pl.multiple_of(key_start, KEY_GROUP)

    scores(0, 0, 0)
    block(0, (0, 0), (Q_BLOCK, max(Q_ROWS_PER_STEP - WIN_H // 2, 0) * GRID_W))

    blocks_per_trip = 7

    def interior(trip, carry):
        jb = 1 + blocks_per_trip * trip
        for d in range(blocks_per_trip):
            block(1, starts(jb + d), starts(jb + d + 1))
        return carry

    assert (last - 1) % blocks_per_trip == 0
    lax.fori_loop(0, (last - 1) // blocks_per_trip, interior, 0)
    block(last, (last * Q_BLOCK, max_key_start), None)


def _bias_tables(rpb):
    qcol = np.arange(GRID_W)
    kcol = np.arange(GRID_W)
    col_start = np.clip(qcol - WIN_W // 2, 0, GRID_W - WIN_W)
    in_win = (kcol[None, :] >= col_start[:, None]) & (kcol[None, :] < col_start[:, None] + WIN_W)
    dc_idx = np.clip(kcol[None, :] - qcol[:, None], -(WIN_W - 1), WIN_W - 1) + WIN_W - 1
    sel = (np.arange(2 * WIN_W - 1)[:, None, None] == dc_idx[None]).astype(np.float32)
    toe = jnp.einsum('hdt,tck->hdck', rpb * LOG2E, jnp.asarray(sel),
                     precision=lax.Precision.HIGHEST)
    toe = jnp.where(in_win[None, None], toe, NEG_BIG)
    neg = jnp.full((N_HEADS, 1, GRID_W, GRID_W), NEG_BIG, F32)
    ext = jnp.concatenate([neg, toe, neg], axis=1)
    pair = jnp.concatenate([ext[:, 0:16], ext[:, 1:17]], axis=-1)
    return pair.reshape(N_HEADS // 2, 2, 16, GRID_W, LANES)


def _attention(q, k, v, kc, vc, bias):
    n_slots = 2
    nb = ATTN_SAMPLES_PER_STEP
    kspec = pl.BlockSpec((nb, 1, SEQ, LANES), lambda hp, b: (b, hp, 0, 0))
    ktspec = pl.BlockSpec((nb, SEQ // KEY_GROUP, LANES, KEY_GROUP), lambda hp, b: (b, 0, hp, 0))
    cspec = pl.BlockSpec((nb, CTX_LEN, LANES), lambda hp, b: (b, 0, hp))
    ctspec = pl.BlockSpec((nb, LANES, CTX_LEN), lambda hp, b: (b, hp, 0))
    bspec = pl.BlockSpec((1, 2, 16, GRID_W, LANES), lambda hp, b: (hp, 0, 0, 0, 0))
    return pl.pallas_call(
        _attn_kernel,
        grid=(N_HEADS // 2, BATCH // nb),
        in_specs=[kspec, ktspec, kspec, ctspec, cspec, bspec],
        out_specs=kspec,
        out_shape=jax.ShapeDtypeStruct((BATCH, N_HEADS // 2, SEQ, LANES), BF16),
        scratch_shapes=[pltpu.VMEM((n_slots, Q_BLOCK, K_BLOCK), F32),
                        pltpu.VMEM((n_slots, Q_BLOCK, CTX_LEN), F32),
                        pltpu.VMEM((n_slots, Q_BLOCK, K_BLOCK), BF16),
                        pltpu.VMEM((n_slots, Q_BLOCK, CTX_LEN), BF16),
                        pltpu.VMEM((n_slots, Q_BLOCK, LANES), F32),
                        pltpu.VMEM((n_slots, Q_BLOCK, LANES), F32),
                        pltpu.VMEM((Q_BLOCK, LANES), F32)],
        compiler_params=pltpu.CompilerParams(
            dimension_semantics=("arbitrary", "arbitrary"),
            vmem_limit_bytes=VMEM_LIMIT),
        name="attention",
    )(q, k, v, kc, vc, bias)


def _rot256(a, b, s, lane):
    s %= 2 * LANES
    if s >= LANES:
        a, b, s = b, a, s - LANES
    if s == 0:
        return a, b
    ra = pltpu.roll(a, s, 1)
    rb = pltpu.roll(b, s, 1)
    keep = lane >= s
    return jnp.where(keep, ra, rb), jnp.where(keep, rb, ra)


POWER_ROWS = 24


def _s5_prep_group(gi, par_ref, b_ref, c_ref, t16_ref, e_ref, mi_ref, ms_ref, mo_ref, a_ref):
    dot = functools.partial(jnp.dot, preferred_element_type=F32)

    def split2(x):
        hi = x.astype(BF16)
        return hi, (x - hi.astype(F32)).astype(BF16)

    def split3(x):
        hi = x.astype(BF16)
        r1 = x - hi.astype(F32)
        mid = r1.astype(BF16)
        return hi, mid, (r1 - mid.astype(F32)).astype(BF16)

    def pick(x, onehot):
        hi, mid, lo = split3(x)
        return dot(hi, onehot) + dot(mid, onehot) + dot(lo, onehot)

    def dot_f32(a, b):
        ah, al = split2(a)
        bh, bl = split2(b)
        return dot(jnp.concatenate([ah, ah, al], axis=1), jnp.concatenate([bh, bl, bh], axis=0))

    ns = SSM_STATE
    lam_re, lam_im = par_ref[gi, 0:1, :], par_ref[gi, 1:2, :]
    dt = jnp.exp(par_ref[gi, 2:3, :])
    lane = lax.broadcasted_iota(jnp.int32, (1, LANES), 1)
    kf = lax.broadcasted_iota(jnp.int32, (POWER_ROWS, LANES), 0).astype(F32)
    mag = jnp.exp((lam_re * dt) * kf)
    ang = (lam_im * dt) * kf
    pwt_re, pwt_im = mag * jnp.cos(ang), mag * jnp.sin(ang)
    den = lam_re * lam_re + lam_im * lam_im
    nr, ni = pwt_re[1:2, :] - 1.0, pwt_im[1:2, :]
    f_re_row = (nr * lam_re + ni * lam_im) / den
    f_im_row = (ni * lam_re - nr * lam_im) / den
    r8 = lax.broadcasted_iota(jnp.int32, (8, LANES), 0)
    f_rows = jnp.where(r8 == 0, f_re_row, jnp.where(r8 == 1, f_im_row, 0.0))
    pad = jnp.zeros((LANES - POWER_ROWS - 8, LANES), F32)
    pw_re = jnp.concatenate([pwt_re, f_rows, pad], axis=0).T
    pw_im = jnp.concatenate([pwt_im, f_rows, pad], axis=0).T
    f_re, f_im = pw_re[:, POWER_ROWS:POWER_ROWS + 1], pw_re[:, POWER_ROWS + 1:POWER_ROWS + 2]
    b_re, b_im = b_ref[gi, :, 0:SSM_GROUP], b_ref[gi, :, SSM_GROUP:2 * SSM_GROUP]
    bb_re = f_re * b_re - f_im * b_im
    bb_im = f_re * b_im + f_im * b_re
    t16 = t16_ref[...]
    bbt_re, bbt_im = pick(bb_re, t16), pick(bb_im, t16)
    ct_re = pick(c_ref[gi, :, 0:SSM_GROUP], t16)
    ct_im = pick(c_ref[gi, :, SSM_GROUP:2 * SSM_GROUP], t16)
    pw_at = lambda x: (pick(pw_re, e_ref[x]), pick(pw_im, e_ref[x]))
    id_re, id_im = pw_at(0)
    rev_re, rev_im = pw_at(1)
    p1_re, p1_im = pw_at(2)
    r16_re, r16_im = pw_at(3)
    f, b = slice(0, ns), slice(ns, 2 * ns)
    cmul = lambda ar, ai, br, bi: (ar * br - ai * bi, ar * bi + ai * br)

    sf_re, sf_im = cmul(rev_re[f], rev_im[f], bbt_re[f], bbt_im[f])
    sb_re, sb_im = cmul(id_re[b], id_im[b], bbt_re[b], bbt_im[b])
    ms_ref[gi] = jnp.concatenate([sf_re, sb_re, sf_im, sb_im], axis=0).T.astype(BF16)

    of_re, of_im = cmul(p1_re[f], p1_im[f], ct_re[f], ct_im[f])
    ob_re, ob_im = cmul(r16_re[b], r16_im[b], ct_re[b], ct_im[b])
    mo_ref[gi] = jnp.concatenate([of_re, ob_re, -of_im, -ob_im], axis=0).astype(BF16)

    xf_re, xf_im = cmul(id_re[f], id_im[f], ct_re[f], ct_im[f])
    xb_re, xb_im = cmul(rev_re[b], rev_im[b], ct_re[b], ct_im[b])
    btf = jnp.concatenate([bbt_re[f], bbt_im[f]], axis=0).T
    btb = jnp.concatenate([bbt_re[b], bbt_im[b]], axis=0).T
    g_f = dot_f32(btf, jnp.concatenate([xf_re, -xf_im], axis=0))
    g_b = dot_f32(btb, jnp.concatenate([xb_re, -xb_im], axis=0))
    for j in range(CHUNK):
        rows = slice(SSM_GROUP * j, SSM_GROUP * (j + 1))
        lo_col, hi_col = SSM_GROUP * j, SSM_GROUP * (j + 1)
        f_lo, f_hi = _rot256(g_f[rows, :LANES], g_f[rows, LANES:], lo_col, lane)
        b_lo, b_hi = _rot256(g_b[rows, :LANES], g_b[rows, LANES:], -SSM_GROUP * (CHUNK - 1 - j), lane)
        lo = jnp.where(lane >= lo_col, f_lo, 0.0) + jnp.where(lane < hi_col, b_lo, 0.0)
        up = jnp.where(lane + LANES >= lo_col, f_hi, 0.0) + jnp.where(lane + LANES < hi_col, b_hi, 0.0)
        mi_ref[gi, rows, 0:LANES] = lo.astype(BF16)
        mi_ref[gi, rows, LANES:2 * LANES] = up.astype(BF16)

    a16 = jnp.concatenate([pwt_re[CHUNK:CHUNK + 1, :], pwt_im[CHUNK:CHUNK + 1, :]], axis=1)
    a_ref[gi] = jnp.broadcast_to(a16, (8, 2 * LANES))


def _s5_prep_operands(lam_re, lam_im, log_dt, b_re, b_im, c_re, c_im):
    g, p2 = SSM_GROUPS, 2 * SSM_STATE
    both = lambda a: jnp.transpose(a, (1, 0, 2)).reshape(g, p2)
    log_dt_rows = jnp.repeat(jnp.transpose(log_dt), SSM_STATE, axis=1)
    par = jnp.stack([both(lam_re), both(lam_im), log_dt_rows], axis=1)
    par = jnp.concatenate([par, jnp.zeros((g, LANES - 3, p2), F32)], axis=1)
    rows_b = lambda a: jnp.transpose(a, (1, 0, 2, 3)).reshape(g, p2, SSM_GROUP)
    rows_c = lambda a: jnp.transpose(a, (1, 0, 3, 2)).reshape(g, p2, SSM_GROUP)
    b_cat = jnp.concatenate([rows_b(b_re), rows_b(b_im)], axis=-1)
    c_cat = jnp.concatenate([rows_c(c_re), rows_c(c_im)], axis=-1)

    col = np.arange(CHUNK_COLS)
    tile16 = (col[None, :] % SSM_GROUP == np.arange(SSM_GROUP)[:, None]).astype(np.float32)
    pos = col // SSM_GROUP
    k_idx = np.arange(LANES)[:, None]
    expand = np.stack([k_idx == pos[None, :], k_idx == (CHUNK - 1 - pos)[None, :],
                       k_idx == (pos + 1)[None, :], k_idx == (CHUNK - pos)[None, :]]).astype(np.float32)

    return par, b_cat, c_cat, jnp.asarray(tile16, BF16), jnp.asarray(expand, BF16)


def _s5_kernel(ul_ref, uc_ref, ms_ref, mi_ref, mo_ref, a_ref, y_ref, s_ref, sc_ref, hp_ref, *, gb):
    for gi in range(gb):
        for b in range(BATCH):
            sb = jnp.dot(ul_ref[gi, b], ms_ref[gi], preferred_element_type=F32)
            s_ref[gi, 0, pl.ds(b, N_CHUNKS, stride=BATCH), :] = sb[:, :LANES]
            s_ref[gi, 1, pl.ds(b, N_CHUNKS, stride=BATCH), :] = sb[:, LANES:]
            cb = jnp.dot(uc_ref[gi, b], ms_ref[gi], preferred_element_type=F32)
            sc_ref[gi, 0, pl.ds(b, N_CTX_CHUNKS, stride=BATCH), :] = cb[:, :LANES]
            sc_ref[gi, 1, pl.ds(b, N_CTX_CHUNKS, stride=BATCH), :] = cb[:, LANES:]
    lane = lax.broadcasted_iota(jnp.int32, (BATCH, LANES), 1)
    fwd = lane < SSM_STATE
    half = SSM_STATE

    def advance(gi, h_re, h_im, row_f, row_b, src):
        s_re = jnp.where(fwd, src[gi, 0, pl.ds(row_f, BATCH), :], src[gi, 0, pl.ds(row_b, BATCH), :])
        s_im = jnp.where(fwd, src[gi, 1, pl.ds(row_f, BATCH), :], src[gi, 1, pl.ds(row_b, BATCH), :])
        a_re = a_ref[gi, :, 0:LANES]
        a_im = a_ref[gi, :, LANES:2 * LANES]
        n_re = a_re * h_re - a_im * h_im + s_re
        n_im = a_re * h_im + a_im * h_re + s_im
        return n_re, n_im

    def ctx_step(t, carry):
        row_f = pl.multiple_of(t * BATCH, BATCH)
        row_b = pl.multiple_of((N_CTX_CHUNKS - 1 - t) * BATCH, BATCH)
        return tuple(advance(gi, carry[gi][0], carry[gi][1], row_f, row_b, sc_ref) for gi in range(gb))

    def lat_step(t, carry):
        row_f = pl.multiple_of(t * BATCH, BATCH)
        row_b = pl.multiple_of((N_CHUNKS - 1 - t) * BATCH, BATCH)
        out = []
        for gi in range(gb):
            h_re, h_im = carry[gi]
            hp_ref[gi, 0, pl.ds(row_f, BATCH), 0:half] = h_re[:, 0:half]
            hp_ref[gi, 0, pl.ds(row_b, BATCH), half:2 * half] = h_re[:, half:]
            hp_ref[gi, 1, pl.ds(row_f, BATCH), 0:half] = h_im[:, 0:half]
            hp_ref[gi, 1, pl.ds(row_b, BATCH), half:2 * half] = h_im[:, half:]
            out.append(advance(gi, h_re, h_im, row_f, row_b, s_ref))
        return tuple(out)

    zero = jnp.zeros((BATCH, LANES), F32)
    carry = tuple((zero, zero) for _ in range(gb))
    carry = lax.fori_loop(0, N_CTX_CHUNKS, ctx_step, carry)
    lax.fori_loop(0, N_CHUNKS, lat_step, carry)
    for gi in range(gb):
        for b in range(BATCH):
            hb_re = hp_ref[gi, 0, pl.ds(b, N_CHUNKS, stride=BATCH), :].astype(BF16)
            hb_im = hp_ref[gi, 1, pl.ds(b, N_CHUNKS, stride=BATCH), :].astype(BF16)
            y = (jnp.dot(ul_ref[gi, b], mi_ref[gi], preferred_element_type=F32)
                 + jnp.dot(jnp.concatenate([hb_re, hb_im], axis=1), mo_ref[gi], preferred_element_type=F32))
            y_ref[gi, b] = y.astype(BF16)


def _s5_scan(u_lat_t, u_ctx_t, m_intra, m_state, m_out, a16):
    gb = 4
    rows = N_CHUNKS * BATCH
    crows = N_CTX_CHUNKS * BATCH
    grp = lambda r, c: pl.BlockSpec((gb, r, c), lambda g: (g, 0, 0))
    tok = lambda n: pl.BlockSpec((gb, BATCH, n, CHUNK_COLS), lambda g: (g, 0, 0, 0))
    return pl.pallas_call(
        functools.partial(_s5_kernel, gb=gb),
        grid=(SSM_GROUPS // gb,),
        in_specs=[tok(N_CHUNKS), tok(N_CTX_CHUNKS), grp(CHUNK_COLS, CHUNK_COLS),
                  grp(CHUNK_COLS, CHUNK_COLS), grp(CHUNK_COLS, CHUNK_COLS), grp(8, 2 * LANES)],
        out_specs=tok(N_CHUNKS),
        out_shape=jax.ShapeDtypeStruct((SSM_GROUPS, BATCH, N_CHUNKS, CHUNK_COLS), BF16),
        scratch_shapes=[pltpu.VMEM((gb, 2, rows, LANES), F32),
                        pltpu.VMEM((gb, 2, crows, LANES), F32),
                        pltpu.VMEM((gb, 2, rows, LANES), F32)],
        compiler_params=pltpu.CompilerParams(
            dimension_semantics=("arbitrary",), vmem_limit_bytes=VMEM_LIMIT),
        name="s5_scan",
    )(u_lat_t, u_ctx_t, m_state, m_intra, m_out, a16)


FFN_TILE = 256


def _post_kernel(x_ref, a_ref, yt_ref, u_ref, ga_ref, gs_ref, mod_ref, d_ref, fg_ref, og_ref, permt_ref,
                 wglu_ref, wba_ref, wbs_ref, wout_ref, wfi_ref, wfo_ref, o_ref, h1_ref, n2_ref, act_ref, r_ref):
    dm = D_MODEL
    g1 = mod_ref[0, :, 2 * dm:3 * dm]
    sh2 = mod_ref[0, :, 3 * dm:4 * dm]
    sc2 = mod_ref[0, :, 4 * dm:5 * dm]
    g2 = mod_ref[0, :, 5 * dm:6 * dm]
    half = CHUNK * CHUNK
    halves = [slice(h * half, (h + 1) * half) for h in range(x_ref.shape[1] // half)]
    dot = functools.partial(jnp.dot, preferred_element_type=F32)

    sp = []
    for h, rows in enumerate(halves):
        y = _load_chunk_layout(yt_ref, permt_ref, r_ref, h)
        sp.append(jax.nn.gelu(y + d_ref[...] * u_ref[0, rows, :].astype(F32)).astype(BF16))
    s = []
    for h, rows in enumerate(halves):
        vg = dot(sp[h], wglu_ref[...])
        s.append((vg[:, :SSM_WIDTH] * jax.nn.sigmoid(vg[:, SSM_WIDTH:])).astype(BF16))
    merged = []
    for h, rows in enumerate(halves):
        a = jnp.concatenate([a_ref[0, j, rows, :] for j in range(ATTN_WIDTH // LANES)], axis=1)
        m = (ga_ref[0, rows, :].astype(F32) * dot(a, wba_ref[...])
             + gs_ref[0, rows, :].astype(F32) * dot(s[h], wbs_ref[...]))
        merged.append(m.astype(BF16))
    for h, rows in enumerate(halves):
        h1 = x_ref[0, rows, :] + g1 * dot(merged[h], wout_ref[...])
        h1_ref[rows, :] = h1
        n2_ref[rows, :] = _rms_modulate(h1, fg_ref[...], sh2, sc2).astype(BF16)
    for lo in range(0, FFN_HIDDEN, FFN_TILE):
        width = min(FFN_TILE, FFN_HIDDEN - lo)
        for rows in halves:
            n2 = n2_ref[rows, :]
            fa = dot(n2, wfi_ref[:, lo:lo + width])
            fb = dot(n2, wfi_ref[:, FFN_HIDDEN + lo:FFN_HIDDEN + lo + width])
            act_ref[rows, lo:lo + width] = (fa * jax.nn.sigmoid(fa) * fb).astype(BF16)
    for rows in halves:
        h2 = h1_ref[rows, :] + g2 * dot(act_ref[rows, :], wfo_ref[...])
        o_ref[0, rows, :] = (h2 * lax.rsqrt(jnp.mean(h2 * h2, axis=-1, keepdims=True) + NORM_EPS)) * og_ref[...]


def _post(x, a, y_t, u, ga, gs, mod3, d_skip, ffn_g, fin_g, perm_t, wglu, wba, wbs, wout, wfi, wfo):
    tm = 512
    n = CHUNK * CHUNK
    tok = lambda width: pl.BlockSpec((1, tm, width), lambda b, i: (b, i, 0))
    const = lambda r, c: pl.BlockSpec((r, c), lambda b, i: (0, 0), pipeline_mode=pl.Buffered(1))
    return pl.pallas_call(
        _post_kernel,
        grid=(BATCH, SEQ // tm),
        in_specs=[tok(D_MODEL),
                  pl.BlockSpec((1, N_HEADS // 2, tm, LANES), lambda b, i: (b, 0, i, 0)),
                  pl.BlockSpec((SSM_GROUPS, 1, tm // CHUNK, CHUNK_COLS), lambda b, i: (0, b, i, 0)),
                  tok(SSM_WIDTH), tok(D_MODEL), tok(D_MODEL),
                  pl.BlockSpec((1, 1, N_MOD * D_MODEL), lambda b, i: (b, 0, 0)),
                  const(1, SSM_WIDTH), const(1, D_MODEL), const(1, D_MODEL), const(n, n),
                  const(SSM_WIDTH, 2 * SSM_WIDTH), const(ATTN_WIDTH, D_MODEL),
                  const(SSM_WIDTH, D_MODEL), const(D_MODEL, D_MODEL),
                  const(D_MODEL, 2 * FFN_HIDDEN), const(FFN_HIDDEN, D_MODEL)],
        out_specs=tok(D_MODEL),
        out_shape=jax.ShapeDtypeStruct((BATCH, SEQ, D_MODEL), F32),
        scratch_shapes=[pltpu.VMEM((tm, D_MODEL), F32), pltpu.VMEM((tm, D_MODEL), BF16),
                        pltpu.VMEM((tm, FFN_HIDDEN), BF16), pltpu.VMEM((tm // n, n, SSM_WIDTH), F32)],
        compiler_params=pltpu.CompilerParams(
            dimension_semantics=("arbitrary", "arbitrary"), vmem_limit_bytes=VMEM_LIMIT),
        name="post",
    )(x, a, y_t, u, ga, gs, mod3, d_skip, ffn_g, fin_g, perm_t, wglu, wba, wbs, wout, wfi, wfo)


def kernel(x, c, ctx, c_ctx, w_mod, b_mod, attn_norm_g, ffn_norm_g, w_in, rel_pos_bias,
           ssm_lambda_re, ssm_lambda_im, ssm_log_dt, ssm_b_re, ssm_b_im, ssm_c_re, ssm_c_im, ssm_d,
           w_glu, w_branch_attn, w_branch_ssm, w_out, w_ffn_in, w_ffn_out, final_norm_g):
    assert x.shape == (BATCH, SEQ, D_MODEL) and w_mod.shape[0] == 1
    c_rows = jnp.concatenate(
        [c, c_ctx[None, :], jnp.zeros((MOD_ROWS - BATCH - 1, D_MODEL), F32)], axis=0)
    mod3 = _modulation(c_rows, w_mod[0], b_mod[0]).reshape(MOD_ROWS, 1, N_MOD * D_MODEL)

    col_scale = jnp.concatenate([jnp.full((ATTN_WIDTH,), HEAD_DIM ** -0.5 * LOG2E, F32),
                                 jnp.ones((IN_COLS - ATTN_WIDTH,), F32)])
    w_in_bf16 = (w_in[0] * col_scale[None, :]).astype(BF16)
    norm_g = attn_norm_g[0].reshape(1, D_MODEL)

    perm = _chunk_perm()
    post_weights = [w_glu[0], w_branch_attn[0], w_branch_ssm[0], w_out[0], w_ffn_in[0], w_ffn_out[0]]
    s5_operands = _s5_prep_operands(
        ssm_lambda_re[0], ssm_lambda_im[0], ssm_log_dt[0], ssm_b_re[0], ssm_b_im[0],
        ssm_c_re[0], ssm_c_im[0])
    outs = _input_projection(x, mod3, norm_g, w_in_bf16, _rope_tables(), perm, post_weights, s5_operands)
    q, k, v, u, u_t, ga, gs = outs[:7]
    post_weights_bf16 = outs[7:7 + N_SIDE_CASTS]
    m_intra, m_state, m_out, a16 = outs[7 + N_SIDE_CASTS:]
    kc, vc, uc_t = _context_projection(ctx, mod3, norm_g, w_in_bf16, perm)

    attn = _attention(q, k, v, kc, vc, _bias_tables(rel_pos_bias[0]))
    y_t = _s5_scan(u_t, uc_t, m_intra, m_state, m_out, a16)

    return _post(x, attn, y_t, u, ga, gs, mod3,
                 ssm_d[0].reshape(1, SSM_WIDTH), ffn_norm_g[0].reshape(1, D_MODEL),
                 final_norm_g.reshape(1, D_MODEL), perm,
                 *post_weights_bf16)
```
